```python
import jax
import jax.numpy as jnp
from jax import lax
import numpy as np

D_MODEL = 1024
BATCH = 4
SEQ = 4096
DEPTH = 4

GRID_W = 64
CTX_LEN = 256
ROPE_BASE = 10000.0
NORM_EPS = 1e-6
NEG_INF = -1e30
Q_BLOCK = 128
WINDOW = 128
N_MOD = 6

MLA_HEADS = 8
MLA_Q_RANK = 384
MLA_KV_RANK = 256
MLA_NOPE = 64
MLA_ROPE = 32
MLA_V = 64
GA_HEADS = 8
GA_KV_HEADS = 2
GA_HEAD_DIM = 64
GA_GROUP = GA_HEADS // GA_KV_HEADS
WA_HEADS = 8
WA_KV_HEADS = 2
WA_HEAD_DIM = 64
WA_GROUP = WA_HEADS // WA_KV_HEADS
N_BRANCH = 3

PEER_HEADS = 8
PEER_N_KEYS = 128
PEER_N_EXPERTS = PEER_N_KEYS * PEER_N_KEYS
PEER_QUERY_DIM = 256
PEER_HALF = PEER_QUERY_DIM // 2
PEER_TOPK = 16
PEER_CHUNK = 128

IN_SPLITS = (MLA_Q_RANK, MLA_KV_RANK, MLA_ROPE,
             GA_HEADS * GA_HEAD_DIM, GA_KV_HEADS * GA_HEAD_DIM, GA_KV_HEADS * GA_HEAD_DIM,
             WA_HEADS * WA_HEAD_DIM, WA_KV_HEADS * WA_HEAD_DIM, WA_KV_HEADS * WA_HEAD_DIM,
             N_BRANCH * D_MODEL)
IN_COLS = (MLA_Q_RANK + MLA_KV_RANK + MLA_ROPE
           + GA_HEADS * GA_HEAD_DIM + 2 * GA_KV_HEADS * GA_HEAD_DIM
           + WA_HEADS * WA_HEAD_DIM + 2 * WA_KV_HEADS * WA_HEAD_DIM
           + N_BRANCH * D_MODEL)

kernel_name = 'hybrid_mla_gqa_swa_peer_dit'


def rmsnorm(x, g):
    xf = x.astype(jnp.float32)
    y = xf * lax.rsqrt(jnp.mean(xf * xf, axis=-1, keepdims=True) + NORM_EPS)
    return (y * g.astype(jnp.float32)).astype(x.dtype)


def axial_rope_tables(rows, rot_dim, dtype):
    r = jnp.repeat(jnp.arange(rows, dtype=jnp.float32), GRID_W)
    col = jnp.tile(jnp.arange(GRID_W, dtype=jnp.float32), rows)
    n_freq = rot_dim // 4
    inv = ROPE_BASE ** (-jnp.arange(n_freq, dtype=jnp.float32) / n_freq)
    ang = jnp.concatenate([r[:, None] * inv, col[:, None] * inv], axis=-1)
    return jnp.cos(ang).astype(dtype), jnp.sin(ang).astype(dtype)


def apply_rope(x, cos, sin):
    half = x.shape[-1] // 2
    x1, x2 = x[..., :half], x[..., half:]
    c = cos[None, :, None, :]
    s = sin[None, :, None, :]
    return jnp.concatenate([x1 * c - x2 * s, x2 * c + x1 * s], axis=-1)


def split_columns(p):
    bounds = []
    acc = 0
    for w in IN_SPLITS[:-1]:
        acc += w
        bounds.append(acc)
    return jnp.split(p, bounds, axis=-1)


def mixer_heads(p, g_cq, g_ckv, w_uq, w_ukv, g_qn, g_kn, rope_a, rope_h):
    B, T = p.shape[0], p.shape[1]
    cq, ckv, kr, qb, kb, vb, qw, kw, vw, gates = split_columns(p)
    qa = (rmsnorm(cq, g_cq) @ w_uq).reshape(B, T, MLA_HEADS, MLA_NOPE + MLA_ROPE)
    kva = (rmsnorm(ckv, g_ckv) @ w_ukv).reshape(B, T, MLA_HEADS, MLA_NOPE + MLA_V)
    qa_nope, qa_rope = qa[..., :MLA_NOPE], qa[..., MLA_NOPE:]
    ka_nope, va = kva[..., :MLA_NOPE], kva[..., MLA_NOPE:]
    kr = kr[:, :, None, :]
    qb = rmsnorm(qb.reshape(B, T, GA_HEADS, GA_HEAD_DIM), g_qn)
    kb = rmsnorm(kb.reshape(B, T, GA_KV_HEADS, GA_HEAD_DIM), g_kn)
    vb = vb.reshape(B, T, GA_KV_HEADS, GA_HEAD_DIM)
    qw = qw.reshape(B, T, WA_HEADS, WA_HEAD_DIM)
    kw = kw.reshape(B, T, WA_KV_HEADS, WA_HEAD_DIM)
    vw = vw.reshape(B, T, WA_KV_HEADS, WA_HEAD_DIM)
    if rope_a is not None:
        qa_rope = apply_rope(qa_rope, rope_a[0], rope_a[1])
        kr = apply_rope(kr, rope_a[0], rope_a[1])
        qb = apply_rope(qb, rope_h[0], rope_h[1])
        kb = apply_rope(kb, rope_h[0], rope_h[1])
        qw = apply_rope(qw, rope_h[0], rope_h[1])
        kw = apply_rope(kw, rope_h[0], rope_h[1])
    qa = jnp.concatenate([qa_nope, qa_rope], axis=-1)[:, :, :, None, :]
    ka = jnp.concatenate([ka_nope, jnp.broadcast_to(kr, (B, T, MLA_HEADS, MLA_ROPE))], axis=-1)
    qb = qb.reshape(B, T, GA_KV_HEADS, GA_GROUP, GA_HEAD_DIM)
    qw = qw.reshape(B, T, WA_KV_HEADS, WA_GROUP, WA_HEAD_DIM)
    return qa, ka, va, qb, kb, vb, qw, kw, vw, gates


def dense_latent_attention(q, k_lat, v_lat, k_ctx, v_ctx, scale):
    B, T, Hkv, G, dk = q.shape
    nb = T // Q_BLOCK
    k = jnp.concatenate([k_lat, k_ctx], axis=1)
    v = jnp.concatenate([v_lat, v_ctx], axis=1)
    qb = jnp.moveaxis(q.reshape(B, nb, Q_BLOCK, Hkv, G, dk), 1, 0)

    def one_block(q_blk):
        s = jnp.einsum('bqhgd,bkhd->bhgqk', q_blk, k).astype(jnp.float32) * scale
        p = jax.nn.softmax(s, axis=-1).astype(v.dtype)
        return jnp.einsum('bhgqk,bkhd->bqhgd', p, v)

    o = lax.map(one_block, qb)
    return jnp.moveaxis(o, 0, 1).reshape(B, T, Hkv * G * v.shape[-1])


def context_attention(q, k, v, scale, sink=None):
    B, C, Hkv, G, _ = q.shape
    s = jnp.einsum('bqhgd,bkhd->bhgqk', q, k).astype(jnp.float32) * scale
    if sink is not None:
        s_sink = jnp.broadcast_to(sink.astype(jnp.float32).reshape(1, Hkv, G, 1, 1), s.shape[:-1] + (1,))
        s = jnp.concatenate([s, s_sink], axis=-1)
    p = jax.nn.softmax(s, axis=-1)[..., :k.shape[1]].astype(v.dtype)
    o = jnp.einsum('bhgqk,bkhd->bqhgd', p, v)
    return o.reshape(B, C, Hkv * G * v.shape[-1])


def windowed_latent_attention(q, k, v, k_ctx, v_ctx, sink, scale):
    B, T, Hkv, G, d = q.shape
    nb = T // Q_BLOCK
    pad = ((0, 0), (Q_BLOCK, Q_BLOCK), (0, 0), (0, 0))

    def band(a):
        ap = jnp.pad(a, pad).reshape(B, nb + 2, Q_BLOCK, Hkv, a.shape[-1])
        return jnp.concatenate([ap[:, :-2], ap[:, 1:-1], ap[:, 2:]], axis=2)

    k_band, v_band = band(k), band(v)
    qb = q.reshape(B, nb, Q_BLOCK, Hkv, G, d)
    s_loc = jnp.einsum('bnqhgd,bnkhd->bnhgqk', qb, k_band).astype(jnp.float32) * scale
    qpos = jnp.arange(nb)[:, None] * Q_BLOCK + jnp.arange(Q_BLOCK)[None, :]
    kpos = jnp.arange(nb)[:, None] * Q_BLOCK - Q_BLOCK + jnp.arange(3 * Q_BLOCK)[None, :]
    dist = kpos[:, None, :] - qpos[:, :, None]
    valid = (jnp.abs(dist) <= WINDOW) & (kpos[:, None, :] >= 0) & (kpos[:, None, :] < T)
    s_loc = jnp.where(valid[None, :, None, None], s_loc, NEG_INF)
    s_ctx = jnp.einsum('bnqhgd,bkhd->bnhgqk', qb, k_ctx).astype(jnp.float32) * scale
    s_sink = jnp.broadcast_to(sink.astype(jnp.float32).reshape(1, 1, Hkv, G, 1, 1), s_loc.shape[:-1] + (1,))
    p = jax.nn.softmax(jnp.concatenate([s_loc, s_ctx, s_sink], axis=-1), axis=-1).astype(v.dtype)
    n_loc = 3 * Q_BLOCK
    n_ctx = k_ctx.shape[1]
    o = (jnp.einsum('bnhgqk,bnkhd->bnqhgd', p[..., :n_loc], v_band)
         + jnp.einsum('bnhgqk,bkhd->bnqhgd', p[..., n_loc:n_loc + n_ctx], v_ctx))
    return o.reshape(B, T, Hkv * G * v.shape[-1])


def merge_branches(o_a, o_b, o_w, gate_logits, w_oa, w_ob, w_ow, w_o):
    g_a, g_b, g_w = jnp.split(jax.nn.sigmoid(gate_logits), N_BRANCH, axis=-1)
    m = g_a * (o_a @ w_oa) + g_b * (o_b @ w_ob) + g_w * (o_w @ w_ow)
    return m @ w_o


def peer(h, w_pq, sub_keys, u_tab, v_tab):
    lead = h.shape[:-1]
    D = h.shape[-1]
    hf = h.reshape(-1, D)
    n = hf.shape[0]
    q = (hf @ w_pq).reshape(n, PEER_HEADS, 2, PEER_HALF)
    s = jnp.einsum('nhpd,hpkd->nhpk', q, sub_keys).astype(jnp.float32)
    s1, i1 = lax.top_k(s[:, :, 0], PEER_TOPK)
    s2, i2 = lax.top_k(s[:, :, 1], PEER_TOPK)
    cand_s = (s1[..., :, None] + s2[..., None, :]).reshape(n, PEER_HEADS, PEER_TOPK * PEER_TOPK)
    cand_i = (i1[..., :, None] * PEER_N_KEYS + i2[..., None, :]).reshape(n, PEER_HEADS, PEER_TOPK * PEER_TOPK)
    top_s, pos = lax.top_k(cand_s, PEER_TOPK)
    idx = jnp.take_along_axis(cand_i, pos, axis=-1)
    g = jax.nn.softmax(top_s, axis=-1).astype(h.dtype)
    nc = n // PEER_CHUNK

    def chunk(args):
        hc, ic, gc = args
        u = jnp.take(u_tab, ic, axis=0)
        a = jax.nn.gelu(jnp.einsum('cd,chkd->chk', hc, u), approximate=False)
        v = jnp.take(v_tab, ic, axis=0)
        return jnp.einsum('chk,chkd->cd', gc * a, v)

    out = lax.map(chunk, (hf.reshape(nc, PEER_CHUNK, D),
                          idx.reshape(nc, PEER_CHUNK, PEER_HEADS, PEER_TOPK),
                          g.reshape(nc, PEER_CHUNK, PEER_HEADS, PEER_TOPK)))
    return out.reshape(*lead, D)


def setup_inputs(seed: int = 0) -> dict:
    key = jax.random.key(seed)
    ks = jax.random.split(key, 26)
    f32 = jnp.float32
    L, D = DEPTH, D_MODEL

    def nrm(k, shape, scale):
        return jax.random.normal(k, shape, f32) * scale

    def gain(k, shape):
        return 1.0 + 0.02 * jax.random.normal(k, shape, f32)

    return {
        'x': nrm(ks[0], (BATCH, SEQ, D), 1.0),
        'c': nrm(ks[1], (BATCH, D), 1.0),
        'ctx': nrm(ks[2], (BATCH, CTX_LEN, D), 1.0),
        'c_ctx': nrm(ks[3], (D,), 1.0),
        'w_mod': nrm(ks[4], (L, D, N_MOD * D), 0.5 * D ** -0.5),
        'b_mod': nrm(ks[5], (L, N_MOD * D), 0.01),
        'g_attn': gain(ks[6], (L, D)),
        'g_ffn': gain(ks[7], (L, D)),
        'w_in': nrm(ks[8], (L, D, IN_COLS), D ** -0.5),
        'g_cq': gain(ks[9], (L, MLA_Q_RANK)),
        'g_ckv': gain(ks[10], (L, MLA_KV_RANK)),
        'w_uq': nrm(ks[11], (L, MLA_Q_RANK, MLA_HEADS * (MLA_NOPE + MLA_ROPE)), MLA_Q_RANK ** -0.5),
        'w_ukv': nrm(ks[12], (L, MLA_KV_RANK, MLA_HEADS * (MLA_NOPE + MLA_V)), MLA_KV_RANK ** -0.5),
        'g_qn': gain(ks[13], (L, GA_HEAD_DIM)),
        'g_kn': gain(ks[14], (L, GA_HEAD_DIM)),
        'sink': nrm(ks[15], (L, WA_HEADS), 0.5),
        'w_oa': nrm(ks[16], (L, MLA_HEADS * MLA_V, D), (MLA_HEADS * MLA_V) ** -0.5),
        'w_ob': nrm(ks[17], (L, GA_HEADS * GA_HEAD_DIM, D), (GA_HEADS * GA_HEAD_DIM) ** -0.5),
        'w_ow': nrm(ks[18], (L, WA_HEADS * WA_HEAD_DIM, D), (WA_HEADS * WA_HEAD_DIM) ** -0.5),
        'w_o': nrm(ks[19], (L, D, D), D ** -0.5),
        'w_pq': nrm(ks[20], (L, D, PEER_HEADS * PEER_QUERY_DIM), D ** -0.5),
        'sub_keys': nrm(ks[21], (L, PEER_HEADS, 2, PEER_N_KEYS, PEER_HALF), PEER_HALF ** -0.5),
        'peer_u': nrm(ks[22], (L, PEER_N_EXPERTS, D), D ** -0.5),
        'peer_v': nrm(ks[23], (L, PEER_N_EXPERTS, D), 0.25),
        'g_final': gain(ks[24], (D,)),
    }


def reference(x, c, ctx, c_ctx, w_mod, b_mod, g_attn, g_ffn, w_in, g_cq, g_ckv, w_uq, w_ukv,
              g_qn, g_kn, sink, w_oa, w_ob, w_ow, w_o, w_pq, sub_keys, peer_u, peer_v, g_final):
    B, T, _ = x.shape
    rows = T // GRID_W
    rope_a = axial_rope_tables(rows, MLA_ROPE, x.dtype)
    rope_h = axial_rope_tables(rows, GA_HEAD_DIM, x.dtype)
    scale_a = (MLA_NOPE + MLA_ROPE) ** -0.5
    scale_b = GA_HEAD_DIM ** -0.5
    scale_w = WA_HEAD_DIM ** -0.5
    silu_c = jax.nn.silu(c)
    silu_cc = jax.nn.silu(c_ctx)
    for l in range(DEPTH):
        last = l == DEPTH - 1
        mod_x = (silu_c @ w_mod[l] + b_mod[l])[:, None, :]
        mod_c = silu_cc @ w_mod[l] + b_mod[l]
        sh1x, sc1x, gt1x, sh2x, sc2x, gt2x = jnp.split(mod_x, N_MOD, axis=-1)
        sh1c, sc1c, gt1c, sh2c, sc2c, gt2c = jnp.split(mod_c, N_MOD, axis=-1)

        hx = rmsnorm(x, g_attn[l]) * (1 + sc1x) + sh1x
        hc = rmsnorm(ctx, g_attn[l]) * (1 + sc1c) + sh1c
        qa_x, ka_x, va_x, qb_x, kb_x, vb_x, qw_x, kw_x, vw_x, gates_x = mixer_heads(
            hx @ w_in[l], g_cq[l], g_ckv[l], w_uq[l], w_ukv[l], g_qn[l], g_kn[l], rope_a, rope_h)
        qa_c, ka_c, va_c, qb_c, kb_c, vb_c, qw_c, kw_c, vw_c, gates_c = mixer_heads(
            hc @ w_in[l], g_cq[l], g_ckv[l], w_uq[l], w_ukv[l], g_qn[l], g_kn[l], None, None)

        o_a = dense_latent_attention(qa_x, ka_x, va_x, ka_c, va_c, scale_a)
        o_b = dense_latent_attention(qb_x, kb_x, vb_x, kb_c, vb_c, scale_b)
        o_w = windowed_latent_attention(qw_x, kw_x, vw_x, kw_c, vw_c, sink[l], scale_w)
        x = x + gt1x * merge_branches(o_a, o_b, o_w, gates_x, w_oa[l], w_ob[l], w_ow[l], w_o[l])

        if not last:
            oc_a = context_attention(qa_c, ka_c, va_c, scale_a)
            oc_b = context_attention(qb_c, kb_c, vb_c, scale_b)
            oc_w = context_attention(qw_c, kw_c, vw_c, scale_w, sink[l])
            ctx = ctx + gt1c * merge_branches(oc_a, oc_b, oc_w, gates_c, w_oa[l], w_ob[l], w_ow[l], w_o[l])
            hc2 = rmsnorm(ctx, g_ffn[l]) * (1 + sc2c) + sh2c
            ctx = ctx + gt2c * peer(hc2, w_pq[l], sub_keys[l], peer_u[l], peer_v[l])

        hx2 = rmsnorm(x, g_ffn[l]) * (1 + sc2x) + sh2x
        x = x + gt2x * peer(hx2, w_pq[l], sub_keys[l], peer_u[l], peer_v[l])
    return rmsnorm(x, g_final)
```

```python
import functools
import math

import jax
import jax.numpy as jnp
import numpy as np
from jax import lax
from jax.experimental import pallas as pl
from jax.experimental.pallas import tpu as pltpu

GRID_W = 64
ROPE_BASE = 10000.0
NORM_EPS = 1e-6
NEG_INF = -1e30
WINDOW = 128
N_MOD = 6

HEADS = 8
HEAD_PAIRS = HEADS // 2
MLA_Q_RANK = 384
MLA_KV_RANK = 256
MLA_NOPE = 64
MLA_ROPE = 32
HEAD_DIM = 64
KV_HEADS = 2
LANES = 128

PEER_HEADS = 8
PEER_KEYS = 128
PEER_HALF = 128
PEER_TOPK = 16

ROW_TILE = 256
Q_TILE = 256
PEER_TOKENS = 512
PEER_ROWS = 8
PEER_CHUNK = 32
TOPK_TOKENS = 256
VMEM_LIMIT = 56 * 1024 * 1024

OFF_CQ = 0
OFF_CKV = OFF_CQ + MLA_Q_RANK
OFF_KR = OFF_CKV + MLA_KV_RANK
OFF_QB = OFF_KR + LANES
OFF_KB = OFF_QB + HEADS * HEAD_DIM
OFF_VB = OFF_KB + 2 * LANES
OFF_QW = OFF_VB + 2 * LANES
OFF_KW = OFF_QW + HEADS * HEAD_DIM
OFF_VW = OFF_KW + 2 * LANES
OFF_GATES = OFF_VW + 2 * LANES

_NT = (((1,), (1,)), ((), ()))
_F32 = jnp.float32
_BF16 = jnp.bfloat16


def _params(*semantics):
    return pltpu.CompilerParams(dimension_semantics=semantics, vmem_limit_bytes=VMEM_LIMIT)


def _full(shape):
    return pl.BlockSpec(shape, lambda *_: (0,) * len(shape))


def _rms(x, g):
    return x * lax.rsqrt(jnp.mean(x * x, axis=-1, keepdims=True) + NORM_EPS) * g


def _rope(x, tabs, shift):
    cos, sin_lo, sin_hi = tabs
    outs = []
    for c in range(x.shape[1] // LANES):
        xc = x[:, c * LANES:(c + 1) * LANES]
        up = pltpu.roll(xc, LANES - shift, 1)
        dn = pltpu.roll(xc, shift, 1)
        outs.append(xc * cos + up * sin_lo + dn * sin_hi)
    return outs[0] if len(outs) == 1 else jnp.concatenate(outs, axis=1)


def _head_rms(x, bd, g):
    outs = []
    for c in range(x.shape[1] // LANES):
        xc = x[:, c * LANES:(c + 1) * LANES]
        sq = xc * xc
        hi = sq.astype(_BF16)
        lo = (sq - hi.astype(_F32)).astype(_BF16)
        ssq = (jnp.dot(hi, bd, preferred_element_type=_F32)
               + jnp.dot(lo, bd, preferred_element_type=_F32))
        outs.append(xc * lax.rsqrt(ssq * (1.0 / HEAD_DIM) + NORM_EPS))
    y = outs[0] if len(outs) == 1 else jnp.concatenate(outs, axis=1)
    return y * g


def _mod_kernel(c_ref, w_ref, b_ref, o_ref):
    c = c_ref[...]
    a = c / (1.0 + jnp.exp(-c))
    o_ref[0] = jnp.dot(a, w_ref[0], preferred_element_type=_F32,
                       precision=lax.Precision.HIGHEST) + b_ref[0]


def _modulation(cc, w_mod, b_mod):
    depth, d, cols = w_mod.shape
    tn = cols // 4
    return pl.pallas_call(
        _mod_kernel,
        out_shape=jax.ShapeDtypeStruct((depth, 8, cols), _F32),
        grid=(depth, cols // tn),
        in_specs=[
            pl.BlockSpec((8, d), lambda l, j: (0, 0)),
            pl.BlockSpec((1, d, tn), lambda l, j: (l, 0, j)),
            pl.BlockSpec((1, 1, tn), lambda l, j: (l, 0, j)),
        ],
        out_specs=pl.BlockSpec((1, 8, tn), lambda l, j: (l, 0, j)),
        compiler_params=_params("arbitrary", "arbitrary"),
        name="modulation",
    )(cc, w_mod, b_mod.reshape(depth, 1, cols))


def _inproj_kernel(x_ref, mod_ref, gattn_ref, win_ref, gcq_ref, gckv_ref, wuq_ref, wukv_ref,
                   gqn_ref, gkn_ref, rope_ref, bd_ref,
                   qa_ref, ka_ref, va_ref, qb_ref, kb_ref, vb_ref, qw_ref, kw_ref, vw_ref,
                   gates_ref, *, d, scale_a, scale_h):
    x = x_ref[...]
    sh1 = mod_ref[0, :, 0:d]
    sc1 = mod_ref[0, :, d:2 * d]
    h = (_rms(x, gattn_ref[...]) * (1.0 + sc1) + sh1).astype(_BF16)

    def proj(off, width):
        return jnp.dot(h, win_ref[:, off:off + width], preferred_element_type=_F32)

    rope_a = tuple(rope_ref[:, i * LANES:(i + 1) * LANES] for i in range(3))
    rope_h = tuple(rope_ref[:, i * LANES:(i + 1) * LANES] for i in range(3, 6))
    bd = bd_ref[...]

    cq = _rms(proj(OFF_CQ, MLA_Q_RANK), gcq_ref[...]).astype(_BF16)
    qa = jnp.dot(cq, wuq_ref[...], preferred_element_type=_F32)
    qa_ref[...] = (_rope(qa, rope_a, MLA_ROPE // 2) * scale_a).astype(_BF16)
    ckv = _rms(proj(OFF_CKV, MLA_KV_RANK), gckv_ref[...]).astype(_BF16)
    kva = jnp.dot(ckv, wukv_ref[...], preferred_element_type=_F32)
    kr = _rope(proj(OFF_KR, LANES), rope_a, MLA_ROPE // 2)
    ka = kva[:, :HEADS * LANES] + jnp.concatenate([kr] * HEADS, axis=1)
    ka_ref[...] = ka.astype(_BF16)
    va_ref[...] = kva[:, HEADS * LANES:].astype(_BF16)

    qb = _head_rms(proj(OFF_QB, HEADS * HEAD_DIM), bd, gqn_ref[...])
    qb_ref[...] = (_rope(qb, rope_h, HEAD_DIM // 2) * scale_h).astype(_BF16)
    kb = _head_rms(proj(OFF_KB, 2 * LANES), bd, gkn_ref[...])
    kb_ref[...] = _rope(kb, rope_h, HEAD_DIM // 2).astype(_BF16)
    vb_ref[...] = proj(OFF_VB, 2 * LANES).astype(_BF16)

    qw_ref[...] = (_rope(proj(OFF_QW, HEADS * HEAD_DIM), rope_h, HEAD_DIM // 2) * scale_h).astype(_BF16)
    kw_ref[...] = _rope(proj(OFF_KW, 2 * LANES), rope_h, HEAD_DIM // 2).astype(_BF16)
    vw_ref[...] = proj(OFF_VW, 2 * LANES).astype(_BF16)

    for k in range(3):
        g = proj(OFF_GATES + k * d, d)
        gates_ref[:, k * d:(k + 1) * d] = (1.0 / (1.0 + jnp.exp(-g))).astype(_BF16)


def _inproj(xs, mod_l, gattn, win, gcq, gckv, wuq, wukv, gqn_t, gkn_t, rope, bd, *, geo):
    n, d = xs.shape
    tm = ROW_TILE
    lat_tiles, tiles_per_batch, batch = geo
    row = lambda t: (t, 0)

    def mod_idx(t):
        return (jnp.where(t < lat_tiles, t // tiles_per_batch, batch), 0, 0)

    def rope_idx(t):
        return (jnp.where(t < lat_tiles, t % tiles_per_batch, tiles_per_batch), 0)

    widths = (HEADS * LANES, HEADS * LANES, HEADS * HEAD_DIM, HEADS * HEAD_DIM, 2 * LANES, 2 * LANES,
              HEADS * HEAD_DIM, 2 * LANES, 2 * LANES, 3 * d)
    return pl.pallas_call(
        functools.partial(_inproj_kernel, d=d, scale_a=(MLA_NOPE + MLA_ROPE) ** -0.5,
                          scale_h=HEAD_DIM ** -0.5),
        out_shape=[jax.ShapeDtypeStruct((n, w), _BF16) for w in widths],
        grid=(n // tm,),
        in_specs=[
            pl.BlockSpec((tm, d), row),
            pl.BlockSpec((1, 1, N_MOD * d), mod_idx),
            _full(gattn.shape), _full(win.shape), _full(gcq.shape), _full(gckv.shape),
            _full(wuq.shape), _full(wukv.shape), _full(gqn_t.shape), _full(gkn_t.shape),
            pl.BlockSpec((tm, 6 * LANES), rope_idx),
            _full(bd.shape),
        ],
        out_specs=[pl.BlockSpec((tm, w), row) for w in widths],
        compiler_params=_params("arbitrary"),
        name="inproj",
    )(xs, mod_l, gattn, win, gcq, gckv, wuq, wukv, gqn_t, gkn_t, rope, bd)


def _attn_kernel(*refs, nseg, wide, has_sink, band, nq):
    q_ref = refs[0]
    k_refs = refs[1:1 + nseg]
    v_refs = refs[1 + nseg:1 + 2 * nseg]
    sink_ref = refs[1 + 2 * nseg] if has_sink else None
    o_ref = refs[-1]
    pair = pl.program_id(1)
    qi = pl.program_id(2)
    tq = q_ref.shape[0]
    lane = lax.broadcasted_iota(jnp.int32, (tq, LANES), 1)

    masks = [None] * nseg
    if band:
        half = k_refs[0].shape[0]
        r = lax.broadcasted_iota(jnp.int32, (tq, half), 0)
        c = lax.broadcasted_iota(jnp.int32, (tq, half), 1)
        masks[0] = c >= r + jnp.where(qi > 0, 0, half)
        masks[2] = c <= r - (tq - half) - (half - WINDOW) - jnp.where(qi < nq - 1, 0, tq)
        r = lax.broadcasted_iota(jnp.int32, (tq, tq), 0)
        c = lax.broadcasted_iota(jnp.int32, (tq, tq), 1)
        masks[1] = jnp.abs(r - c) <= WINDOW

    outs = []
    for hh in range(2):
        if wide:
            q = q_ref[:, hh * LANES:(hh + 1) * LANES]
            ks = [k[:, hh * LANES:(hh + 1) * LANES] for k in k_refs]
        else:
            own = (lane[:1] < HEAD_DIM) if hh == 0 else (lane[:1] >= HEAD_DIM)
            q = q_ref[...] * own.astype(_F32).astype(_BF16)
            ks = [k[...] for k in k_refs]
        ss = [lax.dot_general(q, k, _NT, preferred_element_type=_F32) for k in ks]
        ss = [s if m is None else jnp.where(m, s, NEG_INF) for s, m in zip(ss, masks)]
        m = functools.reduce(jnp.maximum, [jnp.max(s, axis=-1, keepdims=True) for s in ss])
        if has_sink:
            sink = sink_ref[pl.ds(2 * pair + hh, 1), 0:1]
            m = jnp.maximum(m, sink)
        ps = [jnp.exp(s - m) for s in ss]
        l = functools.reduce(jnp.add, [jnp.sum(p, axis=-1, keepdims=True) for p in ps])
        if has_sink:
            l = l + jnp.exp(sink - m)
        o = functools.reduce(jnp.add, [jnp.dot(p.astype(_BF16), v[...], preferred_element_type=_F32)
                                       for p, v in zip(ps, v_refs)])
        outs.append(o / l)
    o_ref[...] = jnp.where(lane < HEAD_DIM, outs[0], outs[1]).astype(o_ref.dtype)


def _attention(q, k, v, sink, *, wide, segs, nq, q_block0, batch, band=False):
    tq = Q_TILE
    qw = 2 * LANES if wide else LANES
    kdiv = 1 if wide else 2
    in_specs = [pl.BlockSpec((tq, qw), lambda b, p, i: (q_block0 + b * nq + i, p))]
    for rows, fn in segs:
        in_specs.append(pl.BlockSpec((rows, qw), functools.partial(
            lambda b, p, i, fn: (fn(b, i), p // kdiv), fn=fn)))
    for rows, fn in segs:
        in_specs.append(pl.BlockSpec((rows, LANES), functools.partial(
            lambda b, p, i, fn: (fn(b, i), p // kdiv), fn=fn)))
    args = [q] + [k] * len(segs) + [v] * len(segs)
    if sink is not None:
        in_specs.append(_full(sink.shape))
        args.append(sink)
    return pl.pallas_call(
        functools.partial(_attn_kernel, nseg=len(segs), wide=wide, has_sink=sink is not None,
                          band=band, nq=nq),
        out_shape=jax.ShapeDtypeStruct((batch * nq * tq, HEADS * HEAD_DIM), _BF16),
        grid=(batch, HEAD_PAIRS, nq),
        in_specs=in_specs,
        out_specs=pl.BlockSpec((tq, LANES), lambda b, p, i: (b * nq + i, p)),
        compiler_params=_params("arbitrary", "arbitrary", "arbitrary"),
        name="attention",
    )(*args)


def _merge_kernel(x_ref, mod_ref, oa_ref, ob_ref, ow_ref, gates_ref, woa_ref, wob_ref, wow_ref,
                  wo_ref, gffn_ref, wpq_ref, subk_ref,
                  x1_ref, h2_ref, s1_ref, s2_ref, *, d):
    gt1 = mod_ref[0, :, 2 * d:3 * d]
    sh2 = mod_ref[0, :, 3 * d:4 * d]
    sc2 = mod_ref[0, :, 4 * d:5 * d]
    m = None
    for k, (o_ref, w_ref) in enumerate(((oa_ref, woa_ref), (ob_ref, wob_ref), (ow_ref, wow_ref))):
        t = gates_ref[:, k * d:(k + 1) * d].astype(_F32) * jnp.dot(
            o_ref[...], w_ref[...], preferred_element_type=_F32)
        m = t if m is None else m + t
    y = jnp.dot(m.astype(_BF16), wo_ref[...], preferred_element_type=_F32)
    x1 = x_ref[...] + gt1 * y
    x1_ref[...] = x1
    h2 = (_rms(x1, gffn_ref[...]) * (1.0 + sc2) + sh2).astype(_BF16)
    h2_ref[...] = h2
    q = jnp.dot(h2, wpq_ref[...], preferred_element_type=_F32).astype(_BF16)
    for g in range(2 * PEER_HEADS):
        s = lax.dot_general(subk_ref[g], q[:, g * PEER_HALF:(g + 1) * PEER_HALF], _NT,
                            preferred_element_type=_F32)
        if g % 2 == 0:
            s1_ref[g // 2] = s
        else:
            s2_ref[g // 2] = s


def _merge(xs, mod_l, oa, ob, ow, gates, woa, wob, wow, wo, gffn, wpq, subk, *, n_rows, geo):
    d = xs.shape[1]
    tm = ROW_TILE
    lat_tiles, tiles_per_batch, batch = geo
    row = lambda t: (t, 0)

    def mod_idx(t):
        return (jnp.where(t < lat_tiles, t // tiles_per_batch, batch), 0, 0)

    return pl.pallas_call(
        functools.partial(_merge_kernel, d=d),
        out_shape=[
            jax.ShapeDtypeStruct((n_rows, d), _F32),
            jax.ShapeDtypeStruct((n_rows, d), _BF16),
            jax.ShapeDtypeStruct((PEER_HEADS, PEER_KEYS, n_rows), _F32),
            jax.ShapeDtypeStruct((PEER_HEADS, PEER_KEYS, n_rows), _F32),
        ],
        grid=(n_rows // tm,),
        in_specs=[
            pl.BlockSpec((tm, d), row),
            pl.BlockSpec((1, 1, N_MOD * d), mod_idx),
            pl.BlockSpec((tm, HEADS * HEAD_DIM), row),
            pl.BlockSpec((tm, HEADS * HEAD_DIM), row),
            pl.BlockSpec((tm, HEADS * HEAD_DIM), row),
            pl.BlockSpec((tm, 3 * d), row),
            _full(woa.shape), _full(wob.shape), _full(wow.shape), _full(wo.shape),
            _full(gffn.shape), _full(wpq.shape), _full(subk.shape),
        ],
        out_specs=[
            pl.BlockSpec((tm, d), row),
            pl.BlockSpec((tm, d), row),
            pl.BlockSpec((PEER_HEADS, PEER_KEYS, tm), lambda t: (0, 0, t)),
            pl.BlockSpec((PEER_HEADS, PEER_KEYS, tm), lambda t: (0, 0, t)),
        ],
        compiler_params=_params("arbitrary"),
        name="merge",
    )(xs, mod_l, oa, ob, ow, gates, woa, wob, wow, wo, gffn, wpq, subk)


def _top_values(w, k):
    vals = []
    for _ in range(k):
        m = jnp.max(w, axis=0, keepdims=True)
        vals.append(m)
        w = jnp.where(w == m, -jnp.inf, w)
    return vals


def _stack_rows(vals):
    rows = lax.broadcasted_iota(jnp.int32, (len(vals), vals[0].shape[1]), 0)
    out = jnp.broadcast_to(vals[0], rows.shape)
    for k in range(1, len(vals)):
        out = jnp.where(rows == k, vals[k], out)
    return out


def _topk_kernel(s1_ref, s2_ref, e1_ref, e2_ref, tau_ref):
    def head(h, carry):
        s1 = s1_ref[h]
        s2 = s2_ref[h]
        t1 = _top_values(s1, PEER_TOPK)
        t2 = _top_values(s2, PEER_TOPK)
        t2c = _stack_rows(t2)
        cand = [t1[0] + t2c]
        cand += [t1[a] + t2c[:8] for a in range(1, 8)]
        cand += [_stack_rows(t1[8:]) + t2[0]]
        best = _top_values(jnp.concatenate(cand, axis=0), PEER_TOPK)
        z = functools.reduce(jnp.add, [jnp.exp(v - best[0]) for v in best])
        e1_ref[h] = jnp.exp(s1 - t1[0]) / z
        e2_ref[h] = jnp.exp(s2 - t2[0])
        tau_ref[pl.ds(h, 1), :] = best[PEER_TOPK - 1]
        return carry

    lax.fori_loop(0, PEER_HEADS, head, 0)


def _topk(s1t, s2t):
    n = s1t.shape[2]
    tk = TOPK_TOKENS
    blk = pl.BlockSpec((PEER_HEADS, PEER_KEYS, tk), lambda t: (0, 0, t))
    return pl.pallas_call(
        _topk_kernel,
        out_shape=[
            jax.ShapeDtypeStruct(s1t.shape, _F32),
            jax.ShapeDtypeStruct(s1t.shape, _F32),
            jax.ShapeDtypeStruct((PEER_HEADS, n), _F32),
        ],
        grid=(n // tk,),
        in_specs=[blk, blk],
        out_specs=[blk, blk, pl.BlockSpec((PEER_HEADS, tk), lambda t: (0, t))],
        compiler_params=_params("arbitrary"),
        name="peer_topk",
    )(s1t, s2t)


def _peer_kernel(h2_ref, x1_ref, mod_ref, u_ref, vt_ref, s1_ref, e1_ref, s2_ref, e2_ref, tau_ref,
                 o_ref, acc_ref, a_ref, w_ref, *, d, n_exp_tiles):
    e = pl.program_id(1)
    tn = h2_ref.shape[0]

    @pl.when(e == 0)
    def _():
        acc_ref[...] = jnp.zeros_like(acc_ref)

    a_ref[...] = lax.dot_general(u_ref[...], h2_ref[...], _NT, preferred_element_type=_F32)
    shape = (PEER_CHUNK, LANES)
    for ii in range(PEER_ROWS):
        r0 = ii * PEER_KEYS
        for lt in range(tn // LANES):
            ls = slice(lt * LANES, (lt + 1) * LANES)
            s1b = [jnp.broadcast_to(s1_ref[h, ii:ii + 1, ls], shape) for h in range(PEER_HEADS)]
            e1b = [jnp.broadcast_to(e1_ref[h, ii:ii + 1, ls], shape) for h in range(PEER_HEADS)]
            taub = [jnp.broadcast_to(tau_ref[h:h + 1, ls], shape) for h in range(PEER_HEADS)]

            def second_keys(jc, carry, r0=r0, ls=ls, s1b=s1b, e1b=e1b, taub=taub):
                j0 = pl.multiple_of(jc * PEER_CHUNK, PEER_CHUNK)
                g = jnp.zeros(shape, _F32)
                for h in range(PEER_HEADS):
                    s = s1b[h] + s2_ref[h, pl.ds(j0, PEER_CHUNK), ls]
                    g = g + jnp.where(s >= taub[h],
                                      e1b[h] * e2_ref[h, pl.ds(j0, PEER_CHUNK), ls], 0.0)
                a = a_ref[pl.ds(r0 + j0, PEER_CHUNK), ls]
                act = 0.5 * a * (1.0 + lax.erf(a * (1.0 / math.sqrt(2.0))))
                w_ref[pl.ds(r0 + j0, PEER_CHUNK), ls] = (g * act).astype(_BF16)
                return carry

            lax.fori_loop(0, PEER_KEYS // PEER_CHUNK, second_keys, 0)
    acc_ref[...] += jnp.dot(vt_ref[...], w_ref[...], preferred_element_type=_F32)

    @pl.when(e == n_exp_tiles - 1)
    def _():
        gt2 = mod_ref[0, :, 5 * d:6 * d]
        o_ref[...] = x1_ref[...] + gt2 * acc_ref[...].T


def _peer(h2, x1, mod_l, u, vt, s1t, e1t, s2t, e2t, tau, *, geo):
    n, d = h2.shape
    tn = PEER_TOKENS
    te = PEER_ROWS * PEER_KEYS
    n_exp = u.shape[0]
    lat_tiles, tiles_per_batch, batch = geo
    row = lambda t, e: (t, 0)

    def mod_idx(t, e):
        return (jnp.where(t < lat_tiles, t // tiles_per_batch, batch), 0, 0)

    return pl.pallas_call(
        functools.partial(_peer_kernel, d=d, n_exp_tiles=n_exp // te),
        out_shape=jax.ShapeDtypeStruct((n, d), _F32),
        grid=(n // tn, n_exp // te),
        in_specs=[
            pl.BlockSpec((tn, d), row),
            pl.BlockSpec((tn, d), row),
            pl.BlockSpec((1, 1, N_MOD * d), mod_idx),
            pl.BlockSpec((te, d), lambda t, e: (e, 0)),
            pl.BlockSpec((d, te), lambda t, e: (0, e)),
            pl.BlockSpec((PEER_HEADS, PEER_ROWS, tn), lambda t, e: (0, e, t)),
            pl.BlockSpec((PEER_HEADS, PEER_ROWS, tn), lambda t, e: (0, e, t)),
            pl.BlockSpec((PEER_HEADS, PEER_KEYS, tn), lambda t, e: (0, 0, t)),
            pl.BlockSpec((PEER_HEADS, PEER_KEYS, tn), lambda t, e: (0, 0, t)),
            pl.BlockSpec((PEER_HEADS, tn), lambda t, e: (0, t)),
        ],
        out_specs=pl.BlockSpec((tn, d), row),
        scratch_shapes=[
            pltpu.VMEM((d, tn), _F32),
            pltpu.VMEM((te, tn), _F32),
            pltpu.VMEM((te, tn), _BF16),
        ],
        compiler_params=_params("arbitrary", "arbitrary"),
        name="peer_experts",
    )(h2, x1, mod_l, u, vt, s1t, e1t, s2t, e2t, tau)


def _final_kernel(x_ref, g_ref, o_ref):
    o_ref[...] = _rms(x_ref[...], g_ref[...])


def _final_norm(xs, g, n_rows):
    d = xs.shape[1]
    tm = ROW_TILE
    return pl.pallas_call(
        _final_kernel,
        out_shape=jax.ShapeDtypeStruct((n_rows, d), _F32),
        grid=(n_rows // tm,),
        in_specs=[pl.BlockSpec((tm, d), lambda t: (t, 0)), _full(g.shape)],
        out_specs=pl.BlockSpec((tm, d), lambda t: (t, 0)),
        compiler_params=_params("arbitrary"),
        name="final_norm",
    )(xs, g)


def _rope_tables(seq):
    rows = seq // GRID_W
    r = jnp.repeat(jnp.arange(rows, dtype=_F32), GRID_W)
    col = jnp.tile(jnp.arange(GRID_W, dtype=_F32), rows)

    def cos_sin(rot_dim):
        n_freq = rot_dim // 4
        inv = ROPE_BASE ** (-jnp.arange(n_freq, dtype=_F32) / n_freq)
        ang = jnp.concatenate([r[:, None] * inv, col[:, None] * inv], axis=-1)
        return jnp.cos(ang), jnp.sin(ang)

    ca, sa = cos_sin(MLA_ROPE)
    one = jnp.ones((seq, MLA_NOPE), _F32)
    zero = jnp.zeros_like(one)
    z16 = jnp.zeros_like(ca)
    pad1 = jnp.ones((seq, LANES - MLA_NOPE - MLA_ROPE), _F32)
    pad0 = jnp.zeros_like(pad1)
    tab_a = [jnp.concatenate([one, ca, ca, pad1], axis=1),
             jnp.concatenate([zero, -sa, z16, pad0], axis=1),
             jnp.concatenate([zero, z16, sa, pad0], axis=1)]
    ch, sh = cos_sin(HEAD_DIM)
    z32 = jnp.zeros_like(ch)
    tab_h = [jnp.concatenate([ch, ch, ch, ch], axis=1),
             jnp.concatenate([-sh, z32, -sh, z32], axis=1),
             jnp.concatenate([z32, sh, z32, sh], axis=1)]
    lat = jnp.concatenate(tab_a + tab_h, axis=1)
    ident = jnp.concatenate([jnp.ones((ROW_TILE, LANES), _F32), jnp.zeros((ROW_TILE, 2 * LANES), _F32)] * 2,
                            axis=1)
    return jnp.concatenate([lat, ident], axis=0)


def _pack_w_in(w_in, d):
    splits = np.cumsum([MLA_Q_RANK, MLA_KV_RANK, MLA_ROPE, 512, 128, 128, 512, 128, 128])
    cq, ckv, kr, qb, kb, vb, qw, kw, vw, gates = jnp.split(w_in, splits, axis=-1)
    lead = w_in.shape[:-1]
    kr_pad = jnp.concatenate([jnp.zeros(lead + (MLA_NOPE,), w_in.dtype), kr,
                              jnp.zeros(lead + (LANES - MLA_NOPE - MLA_ROPE,), w_in.dtype)], axis=-1)

    def dup(w):
        g0, g1 = w[..., :HEAD_DIM], w[..., HEAD_DIM:]
        return jnp.concatenate([g0, g0, g1, g1], axis=-1)

    return jnp.concatenate([cq, ckv, kr_pad, qb, dup(kb), dup(vb), qw, dup(kw), dup(vw), gates],
                           axis=-1).astype(_BF16)


def _pack_w_uq(w_uq):
    depth, rank, _ = w_uq.shape
    w = w_uq.reshape(depth, rank, HEADS, MLA_NOPE + MLA_ROPE)
    w = jnp.pad(w, ((0, 0), (0, 0), (0, 0), (0, LANES - MLA_NOPE - MLA_ROPE)))
    return w.reshape(depth, rank, HEADS * LANES).astype(_BF16)


def _pack_w_ukv(w_ukv):
    depth, rank, _ = w_ukv.shape
    w = w_ukv.reshape(depth, rank, HEADS, MLA_NOPE + HEAD_DIM)
    k = jnp.pad(w[..., :MLA_NOPE], ((0, 0), (0, 0), (0, 0), (0, LANES - MLA_NOPE)))
    v = w[..., MLA_NOPE:]
    return jnp.concatenate([k.reshape(depth, rank, HEADS * LANES),
                            v.reshape(depth, rank, HEADS * HEAD_DIM)], axis=-1).astype(_BF16)


def kernel(x, c, ctx, c_ctx, w_mod, b_mod, g_attn, g_ffn, w_in, g_cq, g_ckv, w_uq, w_ukv,
           g_qn, g_kn, sink, w_oa, w_ob, w_ow, w_o, w_pq, sub_keys, peer_u, peer_v, g_final):
    batch, seq, d = x.shape
    n_ctx = ctx.shape[1]
    depth = w_mod.shape[0]
    assert n_ctx == ROW_TILE and seq % PEER_TOKENS == 0 and (batch * n_ctx) % PEER_TOKENS == 0
    assert batch < 8 and seq % GRID_W == 0 and seq >= 2 * Q_TILE
    n_lat = batch * seq
    n_all = n_lat + batch * n_ctx
    nq = seq // Q_TILE
    ctx_blk0 = n_lat // ROW_TILE

    xs = jnp.concatenate([x.reshape(n_lat, d), ctx.reshape(batch * n_ctx, d)], axis=0)
    cc = jnp.concatenate([c, c_ctx[None], jnp.zeros((8 - batch - 1, d), _F32)], axis=0)
    mod = _modulation(cc, w_mod, b_mod).reshape(depth, 8, 1, N_MOD * d)

    rope = _rope_tables(seq)
    eye = np.kron(np.eye(2, dtype=np.float32), np.ones((HEAD_DIM, HEAD_DIM), np.float32))
    bd = jnp.asarray(eye, _BF16)
    win = _pack_w_in(w_in, d)
    wuq = _pack_w_uq(w_uq)
    wukv = _pack_w_ukv(w_ukv)
    gqn_t = jnp.tile(g_qn, (1, HEADS))[:, None, :]
    gkn_t = jnp.tile(g_kn, (1, 2 * LANES // HEAD_DIM))[:, None, :]
    sink_t = jnp.broadcast_to(sink[:, :, None], (depth, HEADS, LANES))
    woa, wob, wow, wo, wpq = (w.astype(_BF16) for w in (w_oa, w_ob, w_ow, w_o, w_pq))
    subk = sub_keys.reshape(depth, 2 * PEER_HEADS, PEER_KEYS, PEER_HALF).astype(_BF16)
    u_bf = peer_u.astype(_BF16)
    vt_bf = jnp.swapaxes(peer_v, 1, 2).astype(_BF16)

    geo_row = (n_lat // ROW_TILE, seq // ROW_TILE, batch)
    geo_peer = (n_lat // PEER_TOKENS, seq // PEER_TOKENS, batch)
    lat_seg = (seq, lambda b, i: b)
    ctx_seg = (n_ctx, lambda b, i: ctx_blk0 + b)
    half = Q_TILE // 2
    band_segs = [
        (half, lambda b, i: b * 2 * nq + jnp.maximum(2 * i - 1, 0)),
        (Q_TILE, lambda b, i: b * nq + i),
        (half, lambda b, i: b * 2 * nq + jnp.minimum(2 * i + 2, 2 * nq - 1)),
        ctx_seg,
    ]

    for l in range(depth):
        last = l == depth - 1
        qa, ka, va, qb, kb, vb, qw, kw, vw, gates = _inproj(
            xs, mod[l], g_attn[l][None], win[l], g_cq[l][None], g_ckv[l][None], wuq[l], wukv[l],
            gqn_t[l], gkn_t[l], rope, bd, geo=geo_row)
        lat = dict(nq=nq, q_block0=0, batch=batch)
        o_a = _attention(qa, ka, va, None, wide=True, segs=[lat_seg, ctx_seg], **lat)
        o_b = _attention(qb, kb, vb, None, wide=False, segs=[lat_seg, ctx_seg], **lat)
        o_w = _attention(qw, kw, vw, sink_t[l], wide=False, segs=band_segs, band=True, **lat)
        n_rows = n_lat
        if not last:
            cq = dict(nq=1, q_block0=ctx_blk0, batch=batch, segs=[ctx_seg])
            o_a = jnp.concatenate([o_a, _attention(qa, ka, va, None, wide=True, **cq)], axis=0)
            o_b = jnp.concatenate([o_b, _attention(qb, kb, vb, None, wide=False, **cq)], axis=0)
            o_w = jnp.concatenate([o_w, _attention(qw, kw, vw, sink_t[l], wide=False, **cq)], axis=0)
            n_rows = n_all
        x1, h2, s1t, s2t = _merge(xs, mod[l], o_a, o_b, o_w, gates, woa[l], wob[l], wow[l], wo[l],
                                  g_ffn[l][None], wpq[l], subk[l], n_rows=n_rows, geo=geo_row)
        e1t, e2t, tau = _topk(s1t, s2t)
        xs = _peer(h2, x1, mod[l], u_bf[l], vt_bf[l], s1t, e1t, s2t, e2t, tau, geo=geo_peer)
    out = _final_norm(xs, g_final[None], n_lat)
    return out.reshape(batch, seq, d)
```

```python
import functools
import math

import jax
import jax.numpy as jnp
import numpy as np
from jax import lax
from jax.experimental import pallas as pl
from jax.experimental.pallas import tpu as pltpu

GRID_W = 64
ROPE_BASE = 10000.0
NORM_EPS = 1e-6
NEG_INF = -1e30
WINDOW = 128
N_MOD = 6

HEADS = 8
HEAD_PAIRS = HEADS // 2
MLA_Q_RANK = 384
MLA_KV_RANK = 256
MLA_NOPE = 64
MLA_ROPE = 32
HEAD_DIM = 64
KV_HEADS = 2
LANES = 128

PEER_HEADS = 8
PEER_KEYS = 128
PEER_HALF = 128
PEER_TOPK = 16

ROW_TILE = 256
Q_TILE = 256
PEER_TOKENS = 512
PEER_ROWS = 8
PEER_CHUNK = 32
TOPK_TOKENS = 256
VMEM_LIMIT = 56 * 1024 * 1024

OFF_CQ = 0
OFF_CKV = OFF_CQ + MLA_Q_RANK
OFF_KR = OFF_CKV + MLA_KV_RANK
OFF_QB = OFF_KR + LANES
OFF_KB = OFF_QB + HEADS * HEAD_DIM
OFF_VB = OFF_KB + 2 * LANES
OFF_QW = OFF_VB + 2 * LANES
OFF_KW = OFF_QW + HEADS * HEAD_DIM
OFF_VW = OFF_KW + 2 * LANES
OFF_GATES = OFF_VW + 2 * LANES

_NT = (((1,), (1,)), ((), ()))
_F32 = jnp.float32
_BF16 = jnp.bfloat16


def _params(*semantics):
    return pltpu.CompilerParams(dimension_semantics=semantics, vmem_limit_bytes=VMEM_LIMIT)


def _full(shape):
    return pl.BlockSpec(shape, lambda *_: (0,) * len(shape))


def _rms(x, g):
    return x * lax.rsqrt(jnp.mean(x * x, axis=-1, keepdims=True) + NORM_EPS) * g


def _rope(x, tabs, shift):
    cos, sin_lo, sin_hi = tabs
    outs = []
    for c in range(x.shape[1] // LANES):
        xc = x[:, c * LANES:(c + 1) * LANES]
        up = pltpu.roll(xc, LANES - shift, 1)
        dn = pltpu.roll(xc, shift, 1)
        outs.append(xc * cos + up * sin_lo + dn * sin_hi)
    return outs[0] if len(outs) == 1 else jnp.concatenate(outs, axis=1)


def _head_rms(x, bd, g):
    outs = []
    for c in range(x.shape[1] // LANES):
        xc = x[:, c * LANES:(c + 1) * LANES]
        sq = xc * xc
        hi = sq.astype(_BF16)
        lo = (sq - hi.astype(_F32)).astype(_BF16)
        ssq = (jnp.dot(hi, bd, preferred_element_type=_F32)
               + jnp.dot(lo, bd, preferred_element_type=_F32))
        outs.append(xc * lax.rsqrt(ssq * (1.0 / HEAD_DIM) + NORM_EPS))
    y = outs[0] if len(outs) == 1 else jnp.concatenate(outs, axis=1)
    return y * g


def _mod_kernel(c_ref, w_ref, b_ref, o_ref):
    c = c_ref[...]
    a = c / (1.0 + jnp.exp(-c))
    o_ref[0] = jnp.dot(a, w_ref[0], preferred_element_type=_F32,
                       precision=lax.Precision.HIGHEST) + b_ref[0]


def _modulation(cc, w_mod, b_mod):
    depth, d, cols = w_mod.shape
    tn = cols // 4
    return pl.pallas_call(
        _mod_kernel,
        out_shape=jax.ShapeDtypeStruct((depth, 8, cols), _F32),
        grid=(depth, cols // tn),
        in_specs=[
            pl.BlockSpec((8, d), lambda l, j: (0, 0)),
            pl.BlockSpec((1, d, tn), lambda l, j: (l, 0, j)),
            pl.BlockSpec((1, 1, tn), lambda l, j: (l, 0, j)),
        ],
        out_specs=pl.BlockSpec((1, 8, tn), lambda l, j: (l, 0, j)),
        compiler_params=_params("arbitrary", "arbitrary"),
        name="modulation",
    )(cc, w_mod, b_mod.reshape(depth, 1, cols))


def _inproj_kernel(x_ref, mod_ref, gattn_ref, win_ref, gcq_ref, gckv_ref, wuq_ref, wukv_ref,
                   gqn_ref, gkn_ref, rope_ref, bd_ref,
                   qa_ref, ka_ref, va_ref, qb_ref, kb_ref, vb_ref, qw_ref, kw_ref, vw_ref,
                   gates_ref, *, d, scale_a, scale_h):
    x = x_ref[...]
    sh1 = mod_ref[0, :, 0:d]
    sc1 = mod_ref[0, :, d:2 * d]
    h = (_rms(x, gattn_ref[...]) * (1.0 + sc1) + sh1).astype(_BF16)

    def proj(off, width):
        return jnp.dot(h, win_ref[:, off:off + width], preferred_element_type=_F32)

    rope_a = tuple(rope_ref[:, i * LANES:(i + 1) * LANES] for i in range(3))
    rope_h = tuple(rope_ref[:, i * LANES:(i + 1) * LANES] for i in range(3, 6))
    bd = bd_ref[...]

    cq = _rms(proj(OFF_CQ, MLA_Q_RANK), gcq_ref[...]).astype(_BF16)
    qa = jnp.dot(cq, wuq_ref[...], preferred_element_type=_F32)
    qa_ref[...] = (_rope(qa, rope_a, MLA_ROPE // 2) * scale_a).astype(_BF16)
    ckv = _rms(proj(OFF_CKV, MLA_KV_RANK), gckv_ref[...]).astype(_BF16)
    kva = jnp.dot(ckv, wukv_ref[...], preferred_element_type=_F32)
    kr = _rope(proj(OFF_KR, LANES), rope_a, MLA_ROPE // 2)
    ka = kva[:, :HEADS * LANES] + jnp.concatenate([kr] * HEADS, axis=1)
    ka_ref[...] = ka.astype(_BF16)
    va_ref[...] = kva[:, HEADS * LANES:].astype(_BF16)

    qb = _head_rms(proj(OFF_QB, HEADS * HEAD_DIM), bd, gqn_ref[...])
    qb_ref[...] = (_rope(qb, rope_h, HEAD_DIM // 2) * scale_h).astype(_BF16)
    kb = _head_rms(proj(OFF_KB, 2 * LANES), bd, gkn_ref[...])
    kb_ref[...] = _rope(kb, rope_h, HEAD_DIM // 2).astype(_BF16)
    vb_ref[...] = proj(OFF_VB, 2 * LANES).astype(_BF16)

    qw_ref[...] = (_rope(proj(OFF_QW, HEADS * HEAD_DIM), rope_h, HEAD_DIM // 2) * scale_h).astype(_BF16)
    kw_ref[...] = _rope(proj(OFF_KW, 2 * LANES), rope_h, HEAD_DIM // 2).astype(_BF16)
    vw_ref[...] = proj(OFF_VW, 2 * LANES).astype(_BF16)

    for k in range(3):
        g = proj(OFF_GATES + k * d, d)
        gates_ref[:, k * d:(k + 1) * d] = (1.0 / (1.0 + jnp.exp(-g))).astype(_BF16)


def _inproj(xs, mod_l, gattn, win, gcq, gckv, wuq, wukv, gqn_t, gkn_t, rope, bd, *, geo):
    n, d = xs.shape
    tm = ROW_TILE
    lat_tiles, tiles_per_batch, batch = geo
    row = lambda t: (t, 0)

    def mod_idx(t):
        return (jnp.where(t < lat_tiles, t // tiles_per_batch, batch), 0, 0)

    def rope_idx(t):
        return (jnp.where(t < lat_tiles, t % tiles_per_batch, tiles_per_batch), 0)

    widths = (HEADS * LANES, HEADS * LANES, HEADS * HEAD_DIM, HEADS * HEAD_DIM, 2 * LANES, 2 * LANES,
              HEADS * HEAD_DIM, 2 * LANES, 2 * LANES, 3 * d)
    return pl.pallas_call(
        functools.partial(_inproj_kernel, d=d, scale_a=(MLA_NOPE + MLA_ROPE) ** -0.5,
                          scale_h=HEAD_DIM ** -0.5),
        out_shape=[jax.ShapeDtypeStruct((n, w), _BF16) for w in widths],
        grid=(n // tm,),
        in_specs=[
            pl.BlockSpec((tm, d), row),
            pl.BlockSpec((1, 1, N_MOD * d), mod_idx),
            _full(gattn.shape), _full(win.shape), _full(gcq.shape), _full(gckv.shape),
            _full(wuq.shape), _full(wukv.shape), _full(gqn_t.shape), _full(gkn_t.shape),
            pl.BlockSpec((tm, 6 * LANES), rope_idx),
            _full(bd.shape),
        ],
        out_specs=[pl.BlockSpec((tm, w), row) for w in widths],
        compiler_params=_params("arbitrary"),
        name="inproj",
    )(xs, mod_l, gattn, win, gcq, gckv, wuq, wukv, gqn_t, gkn_t, rope, bd)


def _attn_kernel(*refs, nseg, wide, has_sink, band, nq):
    q_ref = refs[0]
    k_refs = refs[1:1 + nseg]
    v_refs = refs[1 + nseg:1 + 2 * nseg]
    sink_ref = refs[1 + 2 * nseg] if has_sink else None
    o_ref = refs[-1]
    pair = pl.program_id(1)
    qi = pl.program_id(2)
    tq = q_ref.shape[0]
    lane = lax.broadcasted_iota(jnp.int32, (tq, LANES), 1)

    masks = [None] * nseg
    if band:
        half = k_refs[0].shape[0]
        r = lax.broadcasted_iota(jnp.int32, (tq, half), 0)
        c = lax.broadcasted_iota(jnp.int32, (tq, half), 1)
        masks[0] = c >= r + jnp.where(qi > 0, 0, half)
        masks[2] = c <= r - (tq - half) - (half - WINDOW) - jnp.where(qi < nq - 1, 0, tq)
        r = lax.broadcasted_iota(jnp.int32, (tq, tq), 0)
        c = lax.broadcasted_iota(jnp.int32, (tq, tq), 1)
        masks[1] = jnp.abs(r - c) <= WINDOW

    outs = []
    for hh in range(2):
        if wide:
            q = q_ref[:, hh * LANES:(hh + 1) * LANES]
            ks = [k[:, hh * LANES:(hh + 1) * LANES] for k in k_refs]
        else:
            own = (lane[:1] < HEAD_DIM) if hh == 0 else (lane[:1] >= HEAD_DIM)
            q = q_ref[...] * own.astype(_F32).astype(_BF16)
            ks = [k[...] for k in k_refs]
        ss = [lax.dot_general(q, k, _NT, preferred_element_type=_F32) for k in ks]
        ss = [s if m is None else jnp.where(m, s, NEG_INF) for s, m in zip(ss, masks)]
        m = functools.reduce(jnp.maximum, [jnp.max(s, axis=-1, keepdims=True) for s in ss])
        if has_sink:
            sink = sink_ref[pl.ds(2 * pair + hh, 1), 0:1]
            m = jnp.maximum(m, sink)
        ps = [jnp.exp(s - m) for s in ss]
        l = functools.reduce(jnp.add, [jnp.sum(p, axis=-1, keepdims=True) for p in ps])
        if has_sink:
            l = l + jnp.exp(sink - m)
        o = functools.reduce(jnp.add, [jnp.dot(p.astype(_BF16), v[...], preferred_element_type=_F32)
                                       for p, v in zip(ps, v_refs)])
        outs.append(o / l)
    o_ref[...] = jnp.where(lane < HEAD_DIM, outs[0], outs[1]).astype(o_ref.dtype)


def _attention(q, k, v, sink, *, wide, segs, nq, q_block0, batch, band=False):
    tq = Q_TILE
    qw = 2 * LANES if wide else LANES
    kdiv = 1 if wide else 2
    in_specs = [pl.BlockSpec((tq, qw), lambda b, p, i: (q_block0 + b * nq + i, p))]
    for rows, fn in segs:
        in_specs.append(pl.BlockSpec((rows, qw), functools.partial(
            lambda b, p, i, fn: (fn(b, i), p // kdiv), fn=fn)))
    for rows, fn in segs:
        in_specs.append(pl.BlockSpec((rows, LANES), functools.partial(
            lambda b, p, i, fn: (fn(b, i), p // kdiv), fn=fn)))
    args = [q] + [k] * len(segs) + [v] * len(segs)
    if sink is not None:
        in_specs.append(_full(sink.shape))
        args.append(sink)
    return pl.pallas_call(
        functools.partial(_attn_kernel, nseg=len(segs), wide=wide, has_sink=sink is not None,
                          band=band, nq=nq),
        out_shape=jax.ShapeDtypeStruct((batch * nq * tq, HEADS * HEAD_DIM), _BF16),
        grid=(batch, HEAD_PAIRS, nq),
        in_specs=in_specs,
        out_specs=pl.BlockSpec((tq, LANES), lambda b, p, i: (b * nq + i, p)),
        compiler_params=_params("arbitrary", "arbitrary", "arbitrary"),
        name="attention",
    )(*args)


def _merge_kernel(x_ref, mod_ref, oa_ref, ob_ref, ow_ref, gates_ref, woa_ref, wob_ref, wow_ref,
                  wo_ref, gffn_ref, wpq_ref, subk_ref,
                  x1_ref, h2_ref, s1_ref, s2_ref, *, d):
    gt1 = mod_ref[0, :, 2 * d:3 * d]
    sh2 = mod_ref[0, :, 3 * d:4 * d]
    sc2 = mod_ref[0, :, 4 * d:5 * d]
    m = None
    for k, (o_ref, w_ref) in enumerate(((oa_ref, woa_ref), (ob_ref, wob_ref), (ow_ref, wow_ref))):
        t = gates_ref[:, k * d:(k + 1) * d].astype(_F32) * jnp.dot(
            o_ref[...], w_ref[...], preferred_element_type=_F32)
        m = t if m is None else m + t
    y = jnp.dot(m.astype(_BF16), wo_ref[...], preferred_element_type=_F32)
    x1 = x_ref[...] + gt1 * y
    x1_ref[...] = x1
    h2 = (_rms(x1, gffn_ref[...]) * (1.0 + sc2) + sh2).astype(_BF16)
    h2_ref[...] = h2
    q = jnp.dot(h2, wpq_ref[...], preferred_element_type=_F32).astype(_BF16)
    for g in range(2 * PEER_HEADS):
        s = lax.dot_general(subk_ref[g], q[:, g * PEER_HALF:(g + 1) * PEER_HALF], _NT,
                            preferred_element_type=_F32)
        if g % 2 == 0:
            s1_ref[g // 2] = s
        else:
            s2_ref[g // 2] = s


def _merge(xs, mod_l, oa, ob, ow, gates, woa, wob, wow, wo, gffn, wpq, subk, *, n_rows, geo):
    d = xs.shape[1]
    tm = ROW_TILE
    lat_tiles, tiles_per_batch, batch = geo
    row = lambda t: (t, 0)

    def mod_idx(t):
        return (jnp.where(t < lat_tiles, t // tiles_per_batch, batch), 0, 0)

    return pl.pallas_call(
        functools.partial(_merge_kernel, d=d),
        out_shape=[
            jax.ShapeDtypeStruct((n_rows, d), _F32),
            jax.ShapeDtypeStruct((n_rows, d), _BF16),
            jax.ShapeDtypeStruct((PEER_HEADS, PEER_KEYS, n_rows), _F32),
            jax.ShapeDtypeStruct((PEER_HEADS, PEER_KEYS, n_rows), _F32),
        ],
        grid=(n_rows // tm,),
        in_specs=[
            pl.BlockSpec((tm, d), row),
            pl.BlockSpec((1, 1, N_MOD * d), mod_idx),
            pl.BlockSpec((tm, HEADS * HEAD_DIM), row),
            pl.BlockSpec((tm, HEADS * HEAD_DIM), row),
            pl.BlockSpec((tm, HEADS * HEAD_DIM), row),
            pl.BlockSpec((tm, 3 * d), row),
            _full(woa.shape), _full(wob.shape), _full(wow.shape), _full(wo.shape),
            _full(gffn.shape), _full(wpq.shape), _full(subk.shape),
        ],
        out_specs=[
            pl.BlockSpec((tm, d), row),
            pl.BlockSpec((tm, d), row),
            pl.BlockSpec((PEER_HEADS, PEER_KEYS, tm), lambda t: (0, 0, t)),
            pl.BlockSpec((PEER_HEADS, PEER_KEYS, tm), lambda t: (0, 0, t)),
        ],
        compiler_params=_params("arbitrary"),
        name="merge",
    )(xs, mod_l, oa, ob, ow, gates, woa, wob, wow, wo, gffn, wpq, subk)


def _top_values(w, k):
    vals = []
    for _ in range(k):
        m = jnp.max(w, axis=0, keepdims=True)
        vals.append(m)
        w = jnp.where(w == m, -jnp.inf, w)
    return vals


def _stack_rows(vals, n_rows):
    rows = lax.broadcasted_iota(jnp.int32, (n_rows, vals[0].shape[1]), 0)
    out = jnp.full(rows.shape, -jnp.inf, _F32)
    for k, v in enumerate(vals):
        out = jnp.where(rows == k, v, out)
    return out


def _topk_kernel(s1_ref, s2_ref, e1_ref, e2_ref, th_ref):
    n_top = PEER_TOPK + 1

    def head(h, carry):
        s1 = s1_ref[h]
        s2 = s2_ref[h]
        t1 = _top_values(s1, n_top)
        t2 = _top_values(s2, n_top)
        t2c = _stack_rows(t2, 24)
        cand = [t1[0] + t2c]
        cand += [t1[a] + t2c[:8] for a in range(1, 8)]
        cand += [_stack_rows(t1[8:], 16) + t2[0]]
        best = _top_values(jnp.concatenate(cand, axis=0), n_top)
        z = functools.reduce(jnp.add, [jnp.exp(v - best[0]) for v in best[:PEER_TOPK]])
        tau = 0.5 * (best[PEER_TOPK - 1] + best[PEER_TOPK])
        e1_ref[h] = jnp.exp(s1 - t1[0]) / z
        e2_ref[h] = jnp.exp(s2 - t2[0])
        th_ref[h] = jnp.exp((tau - t2[0]) - s1)
        return carry

    lax.fori_loop(0, PEER_HEADS, head, 0)


def _topk(s1t, s2t):
    n = s1t.shape[2]
    tk = TOPK_TOKENS
    blk = pl.BlockSpec((PEER_HEADS, PEER_KEYS, tk), lambda t: (0, 0, t))
    return pl.pallas_call(
        _topk_kernel,
        out_shape=[jax.ShapeDtypeStruct(s1t.shape, _F32)] * 3,
        grid=(n // tk,),
        in_specs=[blk, blk],
        out_specs=[blk, blk, blk],
        compiler_params=_params("arbitrary"),
        name="peer_topk",
    )(s1t, s2t)


def _gate_tile(a_ref, th_ref, e1_ref, e2_ref, w_ref):
    tn = a_ref.shape[1]
    shape = (PEER_CHUNK, LANES)
    for ii in range(PEER_ROWS):
        for lt in range(tn // LANES):
            ls = slice(lt * LANES, (lt + 1) * LANES)
            thb = [jnp.broadcast_to(th_ref[h, ii:ii + 1, ls], shape) for h in range(PEER_HEADS)]
            e1b = [jnp.broadcast_to(e1_ref[h, ii:ii + 1, ls], shape) for h in range(PEER_HEADS)]
            for jc in range(PEER_KEYS // PEER_CHUNK):
                rows = slice(jc * PEER_CHUNK, (jc + 1) * PEER_CHUNK)
                g = None
                for h in range(PEER_HEADS):
                    e2 = e2_ref[h, rows, ls]
                    t = e1b[h] * jnp.where(e2 >= thb[h], e2, 0.0)
                    g = t if g is None else g + t
                erows = slice(ii * PEER_KEYS + rows.start, ii * PEER_KEYS + rows.stop)
                a = a_ref[erows, ls]
                act = (0.5 * a) * (1.0 + lax.erf(a * (1.0 / math.sqrt(2.0))))
                w_ref[erows, ls] = (g * act).astype(_BF16)


def _peer_kernel(h2_ref, x1_ref, mod_ref, u_ref, vt_ref, th_ref, e1_ref, e2_ref,
                 o_ref, acc_ref, a_ref, w_ref, *, d, n_steps):
    k = pl.program_id(1)

    @pl.when(k == 0)
    def _():
        acc_ref[...] = jnp.zeros_like(acc_ref)

    a_ref[...] = lax.dot_general(u_ref[...], h2_ref[...], _NT, preferred_element_type=_F32)
    _gate_tile(a_ref, th_ref, e1_ref, e2_ref, w_ref)
    acc_ref[...] += jnp.dot(vt_ref[...], w_ref[...], preferred_element_type=_F32)

    @pl.when(k == n_steps - 1)
    def _():
        gt2 = mod_ref[0, :, 5 * d:6 * d]
        o_ref[...] = x1_ref[...] + gt2 * acc_ref[...].T


def _peer(h2, x1, mod_l, u, vt, e1t, e2t, tht, *, geo):
    n, d = h2.shape
    tn = PEER_TOKENS
    te = PEER_ROWS * PEER_KEYS
    n_steps = u.shape[0] // te
    lat_tiles, tiles_per_batch, batch = geo
    row = lambda t, k: (t, 0)

    def mod_idx(t, k):
        return (jnp.where(t < lat_tiles, t // tiles_per_batch, batch), 0, 0)

    rows_blk = pl.BlockSpec((PEER_HEADS, PEER_ROWS, tn), lambda t, k: (0, k, t))
    return pl.pallas_call(
        functools.partial(_peer_kernel, d=d, n_steps=n_steps),
        out_shape=jax.ShapeDtypeStruct((n, d), _F32),
        grid=(n // tn, n_steps),
        in_specs=[
            pl.BlockSpec((tn, d), row),
            pl.BlockSpec((tn, d), row),
            pl.BlockSpec((1, 1, N_MOD * d), mod_idx),
            pl.BlockSpec((te, d), lambda t, k: (k, 0)),
            pl.BlockSpec((d, te), lambda t, k: (0, k)),
            rows_blk,
            rows_blk,
            pl.BlockSpec((PEER_HEADS, PEER_KEYS, tn), lambda t, k: (0, 0, t)),
        ],
        out_specs=pl.BlockSpec((tn, d), row),
        scratch_shapes=[
            pltpu.VMEM((d, tn), _F32),
            pltpu.VMEM((te, tn), _F32),
            pltpu.VMEM((te, tn), _BF16),
        ],
        compiler_params=_params("arbitrary", "arbitrary"),
        name="peer_experts",
    )(h2, x1, mod_l, u, vt, tht, e1t, e2t)


def _final_kernel(x_ref, g_ref, o_ref):
    o_ref[...] = _rms(x_ref[...], g_ref[...])


def _final_norm(xs, g, n_rows):
    d = xs.shape[1]
    tm = ROW_TILE
    return pl.pallas_call(
        _final_kernel,
        out_shape=jax.ShapeDtypeStruct((n_rows, d), _F32),
        grid=(n_rows // tm,),
        in_specs=[pl.BlockSpec((tm, d), lambda t: (t, 0)), _full(g.shape)],
        out_specs=pl.BlockSpec((tm, d), lambda t: (t, 0)),
        compiler_params=_params("arbitrary"),
        name="final_norm",
    )(xs, g)


def _rope_tables(seq):
    rows = seq // GRID_W
    r = jnp.repeat(jnp.arange(rows, dtype=_F32), GRID_W)
    col = jnp.tile(jnp.arange(GRID_W, dtype=_F32), rows)

    def cos_sin(rot_dim):
        n_freq = rot_dim // 4
        inv = ROPE_BASE ** (-jnp.arange(n_freq, dtype=_F32) / n_freq)
        ang = jnp.concatenate([r[:, None] * inv, col[:, None] * inv], axis=-1)
        return jnp.cos(ang), jnp.sin(ang)

    ca, sa = cos_sin(MLA_ROPE)
    one = jnp.ones((seq, MLA_NOPE), _F32)
    zero = jnp.zeros_like(one)
    z16 = jnp.zeros_like(ca)
    pad1 = jnp.ones((seq, LANES - MLA_NOPE - MLA_ROPE), _F32)
    pad0 = jnp.zeros_like(pad1)
    tab_a = [jnp.concatenate([one, ca, ca, pad1], axis=1),
             jnp.concatenate([zero, -sa, z16, pad0], axis=1),
             jnp.concatenate([zero, z16, sa, pad0], axis=1)]
    ch, sh = cos_sin(HEAD_DIM)
    z32 = jnp.zeros_like(ch)
    tab_h = [jnp.concatenate([ch, ch, ch, ch], axis=1),
             jnp.concatenate([-sh, z32, -sh, z32], axis=1),
             jnp.concatenate([z32, sh, z32, sh], axis=1)]
    lat = jnp.concatenate(tab_a + tab_h, axis=1)
    ident = jnp.concatenate([jnp.ones((ROW_TILE, LANES), _F32), jnp.zeros((ROW_TILE, 2 * LANES), _F32)] * 2,
                            axis=1)
    return jnp.concatenate([lat, ident], axis=0)


def _pack_w_in(w_in, d):
    splits = np.cumsum([MLA_Q_RANK, MLA_KV_RANK, MLA_ROPE, 512, 128, 128, 512, 128, 128])
    cq, ckv, kr, qb, kb, vb, qw, kw, vw, gates = jnp.split(w_in, splits, axis=-1)
    lead = w_in.shape[:-1]
    kr_pad = jnp.concatenate([jnp.zeros(lead + (MLA_NOPE,), w_in.dtype), kr,
                              jnp.zeros(lead + (LANES - MLA_NOPE - MLA_ROPE,), w_in.dtype)], axis=-1)

    def dup(w):
        g0, g1 = w[..., :HEAD_DIM], w[..., HEAD_DIM:]
        return jnp.concatenate([g0, g0, g1, g1], axis=-1)

    return jnp.concatenate([cq, ckv, kr_pad, qb, dup(kb), dup(vb), qw, dup(kw), dup(vw), gates],
                           axis=-1).astype(_BF16)


def _pack_w_uq(w_uq):
    depth, rank, _ = w_uq.shape
    w = w_uq.reshape(depth, rank, HEADS, MLA_NOPE + MLA_ROPE)
    w = jnp.pad(w, ((0, 0), (0, 0), (0, 0), (0, LANES - MLA_NOPE - MLA_ROPE)))
    return w.reshape(depth, rank, HEADS * LANES).astype(_BF16)


def _pack_w_ukv(w_ukv):
    depth, rank, _ = w_ukv.shape
    w = w_ukv.reshape(depth, rank, HEADS, MLA_NOPE + HEAD_DIM)
    k = jnp.pad(w[..., :MLA_NOPE], ((0, 0), (0, 0), (0, 0), (0, LANES - MLA_NOPE)))
    v = w[..., MLA_NOPE:]
    return jnp.concatenate([k.reshape(depth, rank, HEADS * LANES),
                            v.reshape(depth, rank, HEADS * HEAD_DIM)], axis=-1).astype(_BF16)


def kernel(x, c, ctx, c_ctx, w_mod, b_mod, g_attn, g_ffn, w_in, g_cq, g_ckv, w_uq, w_ukv,
           g_qn, g_kn, sink, w_oa, w_ob, w_ow, w_o, w_pq, sub_keys, peer_u, peer_v, g_final):
    batch, seq, d = x.shape
    n_ctx = ctx.shape[1]
    depth = w_mod.shape[0]
    assert n_ctx == ROW_TILE and seq % PEER_TOKENS == 0 and (batch * n_ctx) % PEER_TOKENS == 0
    assert batch < 8 and seq % GRID_W == 0 and seq >= 2 * Q_TILE
    n_lat = batch * seq
    n_all = n_lat + batch * n_ctx
    nq = seq // Q_TILE
    ctx_blk0 = n_lat // ROW_TILE

    xs = jnp.concatenate([x.reshape(n_lat, d), ctx.reshape(batch * n_ctx, d)], axis=0)
    cc = jnp.concatenate([c, c_ctx[None], jnp.zeros((8 - batch - 1, d), _F32)], axis=0)
    mod = _modulation(cc, w_mod, b_mod).reshape(depth, 8, 1, N_MOD * d)

    rope = _rope_tables(seq)
    eye = np.kron(np.eye(2, dtype=np.float32), np.ones((HEAD_DIM, HEAD_DIM), np.float32))
    bd = jnp.asarray(eye, _BF16)
    win = _pack_w_in(w_in, d)
    wuq = _pack_w_uq(w_uq)
    wukv = _pack_w_ukv(w_ukv)
    gqn_t = jnp.tile(g_qn, (1, HEADS))[:, None, :]
    gkn_t = jnp.tile(g_kn, (1, 2 * LANES // HEAD_DIM))[:, None, :]
    sink_t = jnp.broadcast_to(sink[:, :, None], (depth, HEADS, LANES))
    woa, wob, wow, wo, wpq = (w.astype(_BF16) for w in (w_oa, w_ob, w_ow, w_o, w_pq))
    subk = sub_keys.reshape(depth, 2 * PEER_HEADS, PEER_KEYS, PEER_HALF).astype(_BF16)
    u_bf = peer_u.astype(_BF16)
    vt_bf = jnp.swapaxes(peer_v, 1, 2).astype(_BF16)

    geo_row = (n_lat // ROW_TILE, seq // ROW_TILE, batch)
    geo_peer = (n_lat // PEER_TOKENS, seq // PEER_TOKENS, batch)
    lat_seg = (seq, lambda b, i: b)
    ctx_seg = (n_ctx, lambda b, i: ctx_blk0 + b)
    half = Q_TILE // 2
    band_segs = [
        (half, lambda b, i: b * 2 * nq + jnp.maximum(2 * i - 1, 0)),
        (Q_TILE, lambda b, i: b * nq + i),
        (half, lambda b, i: b * 2 * nq + jnp.minimum(2 * i + 2, 2 * nq - 1)),
        ctx_seg,
    ]

    for l in range(depth):
        last = l == depth - 1
        qa, ka, va, qb, kb, vb, qw, kw, vw, gates = _inproj(
            xs, mod[l], g_attn[l][None], win[l], g_cq[l][None], g_ckv[l][None], wuq[l], wukv[l],
            gqn_t[l], gkn_t[l], rope, bd, geo=geo_row)
        lat = dict(nq=nq, q_block0=0, batch=batch)
        o_a = _attention(qa, ka, va, None, wide=True, segs=[lat_seg, ctx_seg], **lat)
        o_b = _attention(qb, kb, vb, None, wide=False, segs=[lat_seg, ctx_seg], **lat)
        o_w = _attention(qw, kw, vw, sink_t[l], wide=False, segs=band_segs, band=True, **lat)
        n_rows = n_lat
        if not last:
            cq = dict(nq=1, q_block0=ctx_blk0, batch=batch, segs=[ctx_seg])
            o_a = jnp.concatenate([o_a, _attention(qa, ka, va, None, wide=True, **cq)], axis=0)
            o_b = jnp.concatenate([o_b, _attention(qb, kb, vb, None, wide=False, **cq)], axis=0)
            o_w = jnp.concatenate([o_w, _attention(qw, kw, vw, sink_t[l], wide=False, **cq)], axis=0)
            n_rows = n_all
        x1, h2, s1t, s2t = _merge(xs, mod[l], o_a, o_b, o_w, gates, woa[l], wob[l], wow[l], wo[l],
                                  g_ffn[l][None], wpq[l], subk[l], n_rows=n_rows, geo=geo_row)
        e1t, e2t, tht = _topk(s1t, s2t)
        xs = _peer(h2, x1, mod[l], u_bf[l], vt_bf[l], e1t, e2t, tht, geo=geo_peer)
    out = _final_norm(xs, g_final[None], n_lat)
    return out.reshape(batch, seq, d)
```

```python
import functools
import math

import jax
import jax.numpy as jnp
import numpy as np
from jax import lax
from jax.experimental import pallas as pl
from jax.experimental.pallas import tpu as pltpu

GRID_W = 64
ROPE_BASE = 10000.0
NORM_EPS = 1e-6
NEG_INF = -1e30
WINDOW = 128
N_MOD = 6

HEADS = 8
HEAD_PAIRS = HEADS // 2
MLA_Q_RANK = 384
MLA_KV_RANK = 256
MLA_NOPE = 64
MLA_ROPE = 32
HEAD_DIM = 64
KV_HEADS = 2
LANES = 128

PEER_HEADS = 8
PEER_KEYS = 128
PEER_HALF = 128
PEER_TOPK = 16

ROW_TILE = 256
Q_TILE = 256
PEER_TOKENS = 512
PEER_ROWS = 8
PEER_CHUNK = 32
TOPK_TOKENS = 256
VMEM_LIMIT = 56 * 1024 * 1024

OFF_CQ = 0
OFF_CKV = OFF_CQ + MLA_Q_RANK
OFF_KR = OFF_CKV + MLA_KV_RANK
OFF_QB = OFF_KR + LANES
OFF_KB = OFF_QB + HEADS * HEAD_DIM
OFF_VB = OFF_KB + 2 * LANES
OFF_QW = OFF_VB + 2 * LANES
OFF_KW = OFF_QW + HEADS * HEAD_DIM
OFF_VW = OFF_KW + 2 * LANES
OFF_GATES = OFF_VW + 2 * LANES

_NT = (((1,), (1,)), ((), ()))
_F32 = jnp.float32
_BF16 = jnp.bfloat16


def _params(*semantics):
    return pltpu.CompilerParams(dimension_semantics=semantics, vmem_limit_bytes=VMEM_LIMIT)


def _full(shape):
    return pl.BlockSpec(shape, lambda *_: (0,) * len(shape))


def _rms(x, g):
    return x * lax.rsqrt(jnp.mean(x * x, axis=-1, keepdims=True) + NORM_EPS) * g


def _rope(x, tabs, shift):
    cos, sin_lo, sin_hi = tabs
    outs = []
    for c in range(x.shape[1] // LANES):
        xc = x[:, c * LANES:(c + 1) * LANES]
        up = pltpu.roll(xc, LANES - shift, 1)
        dn = pltpu.roll(xc, shift, 1)
        outs.append(xc * cos + up * sin_lo + dn * sin_hi)
    return outs[0] if len(outs) == 1 else jnp.concatenate(outs, axis=1)


def _head_rms(x, bd, g):
    outs = []
    for c in range(x.shape[1] // LANES):
        xc = x[:, c * LANES:(c + 1) * LANES]
        sq = xc * xc
        hi = sq.astype(_BF16)
        lo = (sq - hi.astype(_F32)).astype(_BF16)
        ssq = (jnp.dot(hi, bd, preferred_element_type=_F32)
               + jnp.dot(lo, bd, preferred_element_type=_F32))
        outs.append(xc * lax.rsqrt(ssq * (1.0 / HEAD_DIM) + NORM_EPS))
    y = outs[0] if len(outs) == 1 else jnp.concatenate(outs, axis=1)
    return y * g


def _mod_kernel(c_ref, w_ref, b_ref, o_ref):
    c = c_ref[...]
    a = c / (1.0 + jnp.exp(-c))
    o_ref[0] = jnp.dot(a, w_ref[0], preferred_element_type=_F32,
                       precision=lax.Precision.HIGHEST) + b_ref[0]


def _modulation(cc, w_mod, b_mod):
    depth, d, cols = w_mod.shape
    tn = cols // 4
    return pl.pallas_call(
        _mod_kernel,
        out_shape=jax.ShapeDtypeStruct((depth, 8, cols), _F32),
        grid=(depth, cols // tn),
        in_specs=[
            pl.BlockSpec((8, d), lambda l, j: (0, 0)),
            pl.BlockSpec((1, d, tn), lambda l, j: (l, 0, j)),
            pl.BlockSpec((1, 1, tn), lambda l, j: (l, 0, j)),
        ],
        out_specs=pl.BlockSpec((1, 8, tn), lambda l, j: (l, 0, j)),
        compiler_params=_params("arbitrary", "arbitrary"),
        name="modulation",
    )(cc, w_mod, b_mod.reshape(depth, 1, cols))


def _inproj_kernel(x_ref, mod_ref, gattn_ref, win_ref, gcq_ref, gckv_ref, wuq_ref, wukv_ref,
                   gqn_ref, gkn_ref, rope_ref, bd_ref,
                   qa_ref, ka_ref, va_ref, qb_ref, kb_ref, vb_ref, qw_ref, kw_ref, vw_ref,
                   gates_ref, *, d, scale_a, scale_h):
    x = x_ref[...]
    sh1 = mod_ref[0, :, 0:d]
    sc1 = mod_ref[0, :, d:2 * d]
    h = (_rms(x, gattn_ref[...]) * (1.0 + sc1) + sh1).astype(_BF16)

    def proj(off, width):
        return jnp.dot(h, win_ref[:, off:off + width], preferred_element_type=_F32)

    rope_a = tuple(rope_ref[:, i * LANES:(i + 1) * LANES] for i in range(3))
    rope_h = tuple(rope_ref[:, i * LANES:(i + 1) * LANES] for i in range(3, 6))
    bd = bd_ref[...]

    cq = _rms(proj(OFF_CQ, MLA_Q_RANK), gcq_ref[...]).astype(_BF16)
    qa = jnp.dot(cq, wuq_ref[...], preferred_element_type=_F32)
    qa_ref[...] = (_rope(qa, rope_a, MLA_ROPE // 2) * scale_a).astype(_BF16)
    ckv = _rms(proj(OFF_CKV, MLA_KV_RANK), gckv_ref[...]).astype(_BF16)
    kva = jnp.dot(ckv, wukv_ref[...], preferred_element_type=_F32)
    kr = _rope(proj(OFF_KR, LANES), rope_a, MLA_ROPE // 2)
    ka = kva[:, :HEADS * LANES] + jnp.concatenate([kr] * HEADS, axis=1)
    ka_ref[...] = ka.astype(_BF16)
    va_ref[...] = kva[:, HEADS * LANES:].astype(_BF16)

    qb = _head_rms(proj(OFF_QB, HEADS * HEAD_DIM), bd, gqn_ref[...])
    qb_ref[...] = (_rope(qb, rope_h, HEAD_DIM // 2) * scale_h).astype(_BF16)
    kb = _head_rms(proj(OFF_KB, 2 * LANES), bd, gkn_ref[...])
    kb_ref[...] = _rope(kb, rope_h, HEAD_DIM // 2).astype(_BF16)
    vb_ref[...] = proj(OFF_VB, 2 * LANES).astype(_BF16)

    qw_ref[...] = (_rope(proj(OFF_QW, HEADS * HEAD_DIM), rope_h, HEAD_DIM // 2) * scale_h).astype(_BF16)
    kw_ref[...] = _rope(proj(OFF_KW, 2 * LANES), rope_h, HEAD_DIM // 2).astype(_BF16)
    vw_ref[...] = proj(OFF_VW, 2 * LANES).astype(_BF16)

    for k in range(3):
        g = proj(OFF_GATES + k * d, d)
        gates_ref[:, k * d:(k + 1) * d] = (1.0 / (1.0 + jnp.exp(-g))).astype(_BF16)


def _inproj(xs, mod_l, gattn, win, gcq, gckv, wuq, wukv, gqn_t, gkn_t, rope, bd, *, geo):
    n, d = xs.shape
    tm = ROW_TILE
    lat_tiles, tiles_per_batch, batch = geo
    row = lambda t: (t, 0)

    def mod_idx(t):
        return (jnp.where(t < lat_tiles, t // tiles_per_batch, batch), 0, 0)

    def rope_idx(t):
        return (jnp.where(t < lat_tiles, t % tiles_per_batch, tiles_per_batch), 0)

    widths = (HEADS * LANES, HEADS * LANES, HEADS * HEAD_DIM, HEADS * HEAD_DIM, 2 * LANES, 2 * LANES,
              HEADS * HEAD_DIM, 2 * LANES, 2 * LANES, 3 * d)
    return pl.pallas_call(
        functools.partial(_inproj_kernel, d=d, scale_a=(MLA_NOPE + MLA_ROPE) ** -0.5,
                          scale_h=HEAD_DIM ** -0.5),
        out_shape=[jax.ShapeDtypeStruct((n, w), _BF16) for w in widths],
        grid=(n // tm,),
        in_specs=[
            pl.BlockSpec((tm, d), row),
            pl.BlockSpec((1, 1, N_MOD * d), mod_idx),
            _full(gattn.shape), _full(win.shape), _full(gcq.shape), _full(gckv.shape),
            _full(wuq.shape), _full(wukv.shape), _full(gqn_t.shape), _full(gkn_t.shape),
            pl.BlockSpec((tm, 6 * LANES), rope_idx),
            _full(bd.shape),
        ],
        out_specs=[pl.BlockSpec((tm, w), row) for w in widths],
        compiler_params=_params("arbitrary"),
        name="inproj",
    )(xs, mod_l, gattn, win, gcq, gckv, wuq, wukv, gqn_t, gkn_t, rope, bd)


def _attn_kernel(*refs, nseg, wide, has_sink, band, nq):
    q_ref = refs[0]
    k_refs = refs[1:1 + nseg]
    v_refs = refs[1 + nseg:1 + 2 * nseg]
    sink_ref = refs[1 + 2 * nseg] if has_sink else None
    o_ref = refs[-1]
    pair = pl.program_id(1)
    qi = pl.program_id(2)
    tq = q_ref.shape[0]
    lane = lax.broadcasted_iota(jnp.int32, (tq, LANES), 1)

    masks = [None] * nseg
    if band:
        half = k_refs[0].shape[0]
        r = lax.broadcasted_iota(jnp.int32, (tq, half), 0)
        c = lax.broadcasted_iota(jnp.int32, (tq, half), 1)
        masks[0] = c >= r + jnp.where(qi > 0, 0, half)
        masks[2] = c <= r - (tq - half) - (half - WINDOW) - jnp.where(qi < nq - 1, 0, tq)
        r = lax.broadcasted_iota(jnp.int32, (tq, tq), 0)
        c = lax.broadcasted_iota(jnp.int32, (tq, tq), 1)
        masks[1] = jnp.abs(r - c) <= WINDOW

    outs = []
    for hh in range(2):
        if wide:
            q = q_ref[:, hh * LANES:(hh + 1) * LANES]
            ks = [k[:, hh * LANES:(hh + 1) * LANES] for k in k_refs]
        else:
            own = (lane[:1] < HEAD_DIM) if hh == 0 else (lane[:1] >= HEAD_DIM)
            q = q_ref[...] * own.astype(_F32).astype(_BF16)
            ks = [k[...] for k in k_refs]
        ss = [lax.dot_general(q, k, _NT, preferred_element_type=_F32) for k in ks]
        ss = [s if m is None else jnp.where(m, s, NEG_INF) for s, m in zip(ss, masks)]
        m = functools.reduce(jnp.maximum, [jnp.max(s, axis=-1, keepdims=True) for s in ss])
        if has_sink:
            sink = sink_ref[pl.ds(2 * pair + hh, 1), 0:1]
            m = jnp.maximum(m, sink)
        ps = [jnp.exp(s - m) for s in ss]
        l = functools.reduce(jnp.add, [jnp.sum(p, axis=-1, keepdims=True) for p in ps])
        if has_sink:
            l = l + jnp.exp(sink - m)
        o = functools.reduce(jnp.add, [jnp.dot(p.astype(_BF16), v[...], preferred_element_type=_F32)
                                       for p, v in zip(ps, v_refs)])
        outs.append(o / l)
    o_ref[...] = jnp.where(lane < HEAD_DIM, outs[0], outs[1]).astype(o_ref.dtype)


def _attention(q, k, v, sink, *, wide, segs, nq, q_block0, batch, band=False):
    tq = Q_TILE
    qw = 2 * LANES if wide else LANES
    kdiv = 1 if wide else 2
    in_specs = [pl.BlockSpec((tq, qw), lambda b, p, i: (q_block0 + b * nq + i, p))]
    for rows, fn in segs:
        in_specs.append(pl.BlockSpec((rows, qw), functools.partial(
            lambda b, p, i, fn: (fn(b, i), p // kdiv), fn=fn)))
    for rows, fn in segs:
        in_specs.append(pl.BlockSpec((rows, LANES), functools.partial(
            lambda b, p, i, fn: (fn(b, i), p // kdiv), fn=fn)))
    args = [q] + [k] * len(segs) + [v] * len(segs)
    if sink is not None:
        in_specs.append(_full(sink.shape))
        args.append(sink)
    return pl.pallas_call(
        functools.partial(_attn_kernel, nseg=len(segs), wide=wide, has_sink=sink is not None,
                          band=band, nq=nq),
        out_shape=jax.ShapeDtypeStruct((batch * nq * tq, HEADS * HEAD_DIM), _BF16),
        grid=(batch, HEAD_PAIRS, nq),
        in_specs=in_specs,
        out_specs=pl.BlockSpec((tq, LANES), lambda b, p, i: (b * nq + i, p)),
        compiler_params=_params("arbitrary", "arbitrary", "arbitrary"),
        name="attention",
    )(*args)


def _merge_kernel(x_ref, mod_ref, oa_ref, ob_ref, ow_ref, gates_ref, woa_ref, wob_ref, wow_ref,
                  wo_ref, gffn_ref, wpq_ref, subk_ref,
                  x1_ref, h2_ref, s1_ref, s2_ref, *, d):
    gt1 = mod_ref[0, :, 2 * d:3 * d]
    sh2 = mod_ref[0, :, 3 * d:4 * d]
    sc2 = mod_ref[0, :, 4 * d:5 * d]
    m = None
    for k, (o_ref, w_ref) in enumerate(((oa_ref, woa_ref), (ob_ref, wob_ref), (ow_ref, wow_ref))):
        t = gates_ref[:, k * d:(k + 1) * d].astype(_F32) * jnp.dot(
            o_ref[...], w_ref[...], preferred_element_type=_F32)
        m = t if m is None else m + t
    y = jnp.dot(m.astype(_BF16), wo_ref[...], preferred_element_type=_F32)
    x1 = x_ref[...] + gt1 * y
    x1_ref[...] = x1
    h2 = (_rms(x1, gffn_ref[...]) * (1.0 + sc2) + sh2).astype(_BF16)
    h2_ref[...] = h2
    q = jnp.dot(h2, wpq_ref[...], preferred_element_type=_F32).astype(_BF16)
    for g in range(2 * PEER_HEADS):
        s = lax.dot_general(subk_ref[g], q[:, g * PEER_HALF:(g + 1) * PEER_HALF], _NT,
                            preferred_element_type=_F32)
        if g % 2 == 0:
            s1_ref[g // 2] = s
        else:
            s2_ref[g // 2] = s


def _merge(xs, mod_l, oa, ob, ow, gates, woa, wob, wow, wo, gffn, wpq, subk, *, n_rows, geo):
    d = xs.shape[1]
    tm = ROW_TILE
    lat_tiles, tiles_per_batch, batch = geo
    row = lambda t: (t, 0)

    def mod_idx(t):
        return (jnp.where(t < lat_tiles, t // tiles_per_batch, batch), 0, 0)

    return pl.pallas_call(
        functools.partial(_merge_kernel, d=d),
        out_shape=[
            jax.ShapeDtypeStruct((n_rows, d), _F32),
            jax.ShapeDtypeStruct((n_rows, d), _BF16),
            jax.ShapeDtypeStruct((PEER_HEADS, PEER_KEYS, n_rows), _F32),
            jax.ShapeDtypeStruct((PEER_HEADS, PEER_KEYS, n_rows), _F32),
        ],
        grid=(n_rows // tm,),
        in_specs=[
            pl.BlockSpec((tm, d), row),
            pl.BlockSpec((1, 1, N_MOD * d), mod_idx),
            pl.BlockSpec((tm, HEADS * HEAD_DIM), row),
            pl.BlockSpec((tm, HEADS * HEAD_DIM), row),
            pl.BlockSpec((tm, HEADS * HEAD_DIM), row),
            pl.BlockSpec((tm, 3 * d), row),
            _full(woa.shape), _full(wob.shape), _full(wow.shape), _full(wo.shape),
            _full(gffn.shape), _full(wpq.shape), _full(subk.shape),
        ],
        out_specs=[
            pl.BlockSpec((tm, d), row),
            pl.BlockSpec((tm, d), row),
            pl.BlockSpec((PEER_HEADS, PEER_KEYS, tm), lambda t: (0, 0, t)),
            pl.BlockSpec((PEER_HEADS, PEER_KEYS, tm), lambda t: (0, 0, t)),
        ],
        compiler_params=_params("arbitrary"),
        name="merge",
    )(xs, mod_l, oa, ob, ow, gates, woa, wob, wow, wo, gffn, wpq, subk)


NO_RANK = 64.0


def _top_values(w, k, want_rank=False):
    vals = []
    rank = jnp.full(w.shape, NO_RANK, _F32) if want_rank else None
    for r in range(k):
        m = jnp.max(w, axis=0, keepdims=True)
        vals.append(m)
        hit = w == m
        if want_rank:
            rank = jnp.where(hit, float(r), rank)
        w = jnp.where(hit, -jnp.inf, w)
    return (vals, rank) if want_rank else vals


def _stack_rows(vals, n_rows):
    rows = lax.broadcasted_iota(jnp.int32, (n_rows, vals[0].shape[1]), 0)
    out = jnp.full(rows.shape, -jnp.inf, _F32)
    for k, v in enumerate(vals):
        out = jnp.where(rows == k, v, out)
    return out


def _topk_kernel(s1_ref, s2_ref, e1_ref, cnt_ref, e2_ref, rk_ref):
    n_top = PEER_TOPK + 1

    def head(h, carry):
        s1 = s1_ref[h]
        s2 = s2_ref[h]
        t1 = _top_values(s1, n_top)
        t2, rank2 = _top_values(s2, n_top, want_rank=True)
        t2c = _stack_rows(t2, 24)
        cand = [t1[0] + t2c]
        cand += [t1[a] + t2c[:8] for a in range(1, 8)]
        cand += [_stack_rows(t1[8:], 16) + t2[0]]
        best = _top_values(jnp.concatenate(cand, axis=0), n_top)
        z = functools.reduce(jnp.add, [jnp.exp(v - best[0]) for v in best[:PEER_TOPK]])
        tau = 0.5 * (best[PEER_TOPK - 1] + best[PEER_TOPK])
        cnt = jnp.zeros(s1.shape, _F32)
        for b in range(PEER_TOPK):
            cnt = cnt + jnp.where(s1 + t2[b] >= tau, 1.0, 0.0)
        e1_ref[h] = jnp.exp(s1 - t1[0]) / z
        cnt_ref[h] = cnt
        e2_ref[h] = pltpu.bitcast(jnp.exp(s2 - t2[0]).astype(_BF16), jnp.uint32)
        rk_ref[h] = pltpu.bitcast(rank2.astype(_BF16), jnp.uint32)
        return carry

    lax.fori_loop(0, PEER_HEADS, head, 0)


def _topk(s1t, s2t):
    n = s1t.shape[2]
    tk = TOPK_TOKENS
    blk = pl.BlockSpec((PEER_HEADS, PEER_KEYS, tk), lambda t: (0, 0, t))
    packed = pl.BlockSpec((PEER_HEADS, PEER_KEYS // 2, tk), lambda t: (0, 0, t))
    return pl.pallas_call(
        _topk_kernel,
        out_shape=[jax.ShapeDtypeStruct(s1t.shape, _F32)] * 2
        + [jax.ShapeDtypeStruct((PEER_HEADS, PEER_KEYS // 2, n), jnp.uint32)] * 2,
        grid=(n // tk,),
        in_specs=[blk, blk],
        out_specs=[blk, blk, packed, packed],
        compiler_params=_params("arbitrary"),
        name="peer_topk",
    )(s1t, s2t)


def _gate_tile(a_ref, cnt_ref, e1_ref, e2_ref, rk_ref, w_ref):
    shape = (PEER_CHUNK, LANES)
    for ii in range(PEER_ROWS):
        for lt in range(a_ref.shape[1] // LANES):
            ls = slice(lt * LANES, (lt + 1) * LANES)
            cntb = [jnp.broadcast_to(cnt_ref[h, ii:ii + 1, ls], shape).astype(_BF16)
                    for h in range(PEER_HEADS)]
            e1b = [jnp.broadcast_to(e1_ref[h, ii:ii + 1, ls], shape).astype(_BF16)
                   for h in range(PEER_HEADS)]
            for jc in range(PEER_KEYS // PEER_CHUNK):
                rows = slice(jc * PEER_CHUNK, (jc + 1) * PEER_CHUNK)
                words = slice(rows.start // 2, rows.stop // 2)
                g = None
                for h in range(PEER_HEADS):
                    e2 = pltpu.bitcast(e2_ref[h, words, ls], _BF16)
                    rk = pltpu.bitcast(rk_ref[h, words, ls], _BF16)
                    t = e1b[h] * jnp.where(rk < cntb[h], e2, jnp.zeros_like(e2))
                    g = t if g is None else g + t
                erows = slice(ii * PEER_KEYS + rows.start, ii * PEER_KEYS + rows.stop)
                a = a_ref[erows, ls]
                act = (0.5 * a) * (1.0 + lax.erf(a * (1.0 / math.sqrt(2.0))))
                w_ref[erows, ls] = g * act.astype(_BF16)


def _peer_kernel(h2_ref, x1_ref, mod_ref, u_ref, vt_ref, cnt_ref, e1_ref, e2_ref, rk_ref,
                 o_ref, acc_ref, a_ref, w_ref, *, d, n_steps):
    k = pl.program_id(1)

    @pl.when(k == 0)
    def _():
        acc_ref[...] = jnp.zeros_like(acc_ref)

    a_ref[...] = lax.dot_general(u_ref[...], h2_ref[...], _NT, preferred_element_type=_F32)
    _gate_tile(a_ref, cnt_ref, e1_ref, e2_ref, rk_ref, w_ref)
    acc_ref[...] += jnp.dot(vt_ref[...], w_ref[...], preferred_element_type=_F32)

    @pl.when(k == n_steps - 1)
    def _():
        gt2 = mod_ref[0, :, 5 * d:6 * d]
        o_ref[...] = x1_ref[...] + gt2 * acc_ref[...].T


def _peer(h2, x1, mod_l, u, vt, e1t, cntt, e2t, rkt, *, geo):
    n, d = h2.shape
    tn = PEER_TOKENS
    te = PEER_ROWS * PEER_KEYS
    n_steps = u.shape[0] // te
    lat_tiles, tiles_per_batch, batch = geo
    row = lambda t, k: (t, 0)

    def mod_idx(t, k):
        return (jnp.where(t < lat_tiles, t // tiles_per_batch, batch), 0, 0)

    rows_blk = pl.BlockSpec((PEER_HEADS, PEER_ROWS, tn), lambda t, k: (0, k, t))
    return pl.pallas_call(
        functools.partial(_peer_kernel, d=d, n_steps=n_steps),
        out_shape=jax.ShapeDtypeStruct((n, d), _F32),
        grid=(n // tn, n_steps),
        in_specs=[
            pl.BlockSpec((tn, d), row),
            pl.BlockSpec((tn, d), row),
            pl.BlockSpec((1, 1, N_MOD * d), mod_idx),
            pl.BlockSpec((te, d), lambda t, k: (k, 0)),
            pl.BlockSpec((d, te), lambda t, k: (0, k)),
            rows_blk,
            rows_blk,
            pl.BlockSpec((PEER_HEADS, PEER_KEYS // 2, tn), lambda t, k: (0, 0, t)),
            pl.BlockSpec((PEER_HEADS, PEER_KEYS // 2, tn), lambda t, k: (0, 0, t)),
        ],
        out_specs=pl.BlockSpec((tn, d), row),
        scratch_shapes=[
            pltpu.VMEM((d, tn), _F32),
            pltpu.VMEM((te, tn), _F32),
            pltpu.VMEM((te, tn), _BF16),
        ],
        compiler_params=_params("arbitrary", "arbitrary"),
        name="peer_experts",
    )(h2, x1, mod_l, u, vt, cntt, e1t, e2t, rkt)


def _final_kernel(x_ref, g_ref, o_ref):
    o_ref[...] = _rms(x_ref[...], g_ref[...])


def _final_norm(xs, g, n_rows):
    d = xs.shape[1]
    tm = ROW_TILE
    return pl.pallas_call(
        _final_kernel,
        out_shape=jax.ShapeDtypeStruct((n_rows, d), _F32),
        grid=(n_rows // tm,),
        in_specs=[pl.BlockSpec((tm, d), lambda t: (t, 0)), _full(g.shape)],
        out_specs=pl.BlockSpec((tm, d), lambda t: (t, 0)),
        compiler_params=_params("arbitrary"),
        name="final_norm",
    )(xs, g)


def _rope_tables(seq):
    rows = seq // GRID_W
    r = jnp.repeat(jnp.arange(rows, dtype=_F32), GRID_W)
    col = jnp.tile(jnp.arange(GRID_W, dtype=_F32), rows)

    def cos_sin(rot_dim):
        n_freq = rot_dim // 4
        inv = ROPE_BASE ** (-jnp.arange(n_freq, dtype=_F32) / n_freq)
        ang = jnp.concatenate([r[:, None] * inv, col[:, None] * inv], axis=-1)
        return jnp.cos(ang), jnp.sin(ang)

    ca, sa = cos_sin(MLA_ROPE)
    one = jnp.ones((seq, MLA_NOPE), _F32)
    zero = jnp.zeros_like(one)
    z16 = jnp.zeros_like(ca)
    pad1 = jnp.ones((seq, LANES - MLA_NOPE - MLA_ROPE), _F32)
    pad0 = jnp.zeros_like(pad1)
    tab_a = [jnp.concatenate([one, ca, ca, pad1], axis=1),
             jnp.concatenate([zero, -sa, z16, pad0], axis=1),
             jnp.concatenate([zero, z16, sa, pad0], axis=1)]
    ch, sh = cos_sin(HEAD_DIM)
    z32 = jnp.zeros_like(ch)
    tab_h = [jnp.concatenate([ch, ch, ch, ch], axis=1),
             jnp.concatenate([-sh, z32, -sh, z32], axis=1),
             jnp.concatenate([z32, sh, z32, sh], axis=1)]
    lat = jnp.concatenate(tab_a + tab_h, axis=1)
    ident = jnp.concatenate([jnp.ones((ROW_TILE, LANES), _F32), jnp.zeros((ROW_TILE, 2 * LANES), _F32)] * 2,
                            axis=1)
    return jnp.concatenate([lat, ident], axis=0)


def _pack_w_in(w_in, d):
    splits = np.cumsum([MLA_Q_RANK, MLA_KV_RANK, MLA_ROPE, 512, 128, 128, 512, 128, 128])
    cq, ckv, kr, qb, kb, vb, qw, kw, vw, gates = jnp.split(w_in, splits, axis=-1)
    lead = w_in.shape[:-1]
    kr_pad = jnp.concatenate([jnp.zeros(lead + (MLA_NOPE,), w_in.dtype), kr,
                              jnp.zeros(lead + (LANES - MLA_NOPE - MLA_ROPE,), w_in.dtype)], axis=-1)

    def dup(w):
        g0, g1 = w[..., :HEAD_DIM], w[..., HEAD_DIM:]
        return jnp.concatenate([g0, g0, g1, g1], axis=-1)

    return jnp.concatenate([cq, ckv, kr_pad, qb, dup(kb), dup(vb), qw, dup(kw), dup(vw), gates],
                           axis=-1).astype(_BF16)


def _pack_w_uq(w_uq):
    depth, rank, _ = w_uq.shape
    w = w_uq.reshape(depth, rank, HEADS, MLA_NOPE + MLA_ROPE)
    w = jnp.pad(w, ((0, 0), (0, 0), (0, 0), (0, LANES - MLA_NOPE - MLA_ROPE)))
    return w.reshape(depth, rank, HEADS * LANES).astype(_BF16)


def _pack_w_ukv(w_ukv):
    depth, rank, _ = w_ukv.shape
    w = w_ukv.reshape(depth, rank, HEADS, MLA_NOPE + HEAD_DIM)
    k = jnp.pad(w[..., :MLA_NOPE], ((0, 0), (0, 0), (0, 0), (0, LANES - MLA_NOPE)))
    v = w[..., MLA_NOPE:]
    return jnp.concatenate([k.reshape(depth, rank, HEADS * LANES),
                            v.reshape(depth, rank, HEADS * HEAD_DIM)], axis=-1).astype(_BF16)


def kernel(x, c, ctx, c_ctx, w_mod, b_mod, g_attn, g_ffn, w_in, g_cq, g_ckv, w_uq, w_ukv,
           g_qn, g_kn, sink, w_oa, w_ob, w_ow, w_o, w_pq, sub_keys, peer_u, peer_v, g_final):
    batch, seq, d = x.shape
    n_ctx = ctx.shape[1]
    depth = w_mod.shape[0]
    assert n_ctx == ROW_TILE and seq % PEER_TOKENS == 0 and (batch * n_ctx) % PEER_TOKENS == 0
    assert batch < 8 and seq % GRID_W == 0 and seq >= 2 * Q_TILE
    n_lat = batch * seq
    n_all = n_lat + batch * n_ctx
    nq = seq // Q_TILE
    ctx_blk0 = n_lat // ROW_TILE

    xs = jnp.concatenate([x.reshape(n_lat, d), ctx.reshape(batch * n_ctx, d)], axis=0)
    cc = jnp.concatenate([c, c_ctx[None], jnp.zeros((8 - batch - 1, d), _F32)], axis=0)
    mod = _modulation(cc, w_mod, b_mod).reshape(depth, 8, 1, N_MOD * d)

    rope = _rope_tables(seq)
    eye = np.kron(np.eye(2, dtype=np.float32), np.ones((HEAD_DIM, HEAD_DIM), np.float32))
    bd = jnp.asarray(eye, _BF16)
    win = _pack_w_in(w_in, d)
    wuq = _pack_w_uq(w_uq)
    wukv = _pack_w_ukv(w_ukv)
    gqn_t = jnp.tile(g_qn, (1, HEADS))[:, None, :]
    gkn_t = jnp.tile(g_kn, (1, 2 * LANES // HEAD_DIM))[:, None, :]
    sink_t = jnp.broadcast_to(sink[:, :, None], (depth, HEADS, LANES))
    woa, wob, wow, wo, wpq = (w.astype(_BF16) for w in (w_oa, w_ob, w_ow, w_o, w_pq))
    subk = sub_keys.reshape(depth, 2 * PEER_HEADS, PEER_KEYS, PEER_HALF).astype(_BF16)
    u_bf = peer_u.astype(_BF16)
    vt_bf = jnp.swapaxes(peer_v, 1, 2).astype(_BF16)

    geo_row = (n_lat // ROW_TILE, seq // ROW_TILE, batch)
    geo_peer = (n_lat // PEER_TOKENS, seq // PEER_TOKENS, batch)
    lat_seg = (seq, lambda b, i: b)
    ctx_seg = (n_ctx, lambda b, i: ctx_blk0 + b)
    half = Q_TILE // 2
    band_segs = [
        (half, lambda b, i: b * 2 * nq + jnp.maximum(2 * i - 1, 0)),
        (Q_TILE, lambda b, i: b * nq + i),
        (half, lambda b, i: b * 2 * nq + jnp.minimum(2 * i + 2, 2 * nq - 1)),
        ctx_seg,
    ]

    for l in range(depth):
        last = l == depth - 1
        qa, ka, va, qb, kb, vb, qw, kw, vw, gates = _inproj(
            xs, mod[l], g_attn[l][None], win[l], g_cq[l][None], g_ckv[l][None], wuq[l], wukv[l],
            gqn_t[l], gkn_t[l], rope, bd, geo=geo_row)
        lat = dict(nq=nq, q_block0=0, batch=batch)
        o_a = _attention(qa, ka, va, None, wide=True, segs=[lat_seg, ctx_seg], **lat)
        o_b = _attention(qb, kb, vb, None, wide=False, segs=[lat_seg, ctx_seg], **lat)
        o_w = _attention(qw, kw, vw, sink_t[l], wide=False, segs=band_segs, band=True, **lat)
        n_rows = n_lat
        if not last:
            cq = dict(nq=1, q_block0=ctx_blk0, batch=batch, segs=[ctx_seg])
            o_a = jnp.concatenate([o_a, _attention(qa, ka, va, None, wide=True, **cq)], axis=0)
            o_b = jnp.concatenate([o_b, _attention(qb, kb, vb, None, wide=False, **cq)], axis=0)
            o_w = jnp.concatenate([o_w, _attention(qw, kw, vw, sink_t[l], wide=False, **cq)], axis=0)
            n_rows = n_all
        x1, h2, s1t, s2t = _merge(xs, mod[l], o_a, o_b, o_w, gates, woa[l], wob[l], wow[l], wo[l],
                                  g_ffn[l][None], wpq[l], subk[l], n_rows=n_rows, geo=geo_row)
        e1t, cntt, e2t, rkt = _topk(s1t, s2t)
        xs = _peer(h2, x1, mod[l], u_bf[l], vt_bf[l], e1t, cntt, e2t, rkt, geo=geo_peer)
    out = _final_norm(xs, g_final[None], n_lat)
    return out.reshape(batch, seq, d)
```

```python
import functools
import math

import jax
import jax.numpy as jnp
import numpy as np
from jax import lax
from jax.experimental import pallas as pl
from jax.experimental.pallas import tpu as pltpu

GRID_W = 64
ROPE_BASE = 10000.0
NORM_EPS = 1e-6
NEG_INF = -1e30
WINDOW = 128
N_MOD = 6

HEADS = 8
HEAD_PAIRS = HEADS // 2
MLA_Q_RANK = 384
MLA_KV_RANK = 256
MLA_NOPE = 64
MLA_ROPE = 32
HEAD_DIM = 64
KV_HEADS = 2
LANES = 128

PEER_HEADS = 8
PEER_KEYS = 128
PEER_HALF = 128
PEER_TOPK = 16

ROW_TILE = 256
Q_TILE = 256
FLASH_Q_TILE = 1024
FLASH_K_CHUNK = 1024
PEER_TOKENS = 512
PEER_ROWS = 8
PEER_CHUNK = 32
TOPK_TOKENS = 256
VMEM_LIMIT = 56 * 1024 * 1024

OFF_CQ = 0
OFF_CKV = OFF_CQ + MLA_Q_RANK
OFF_KR = OFF_CKV + MLA_KV_RANK
OFF_QB = OFF_KR + LANES
OFF_KB = OFF_QB + HEADS * HEAD_DIM
OFF_VB = OFF_KB + 2 * LANES
OFF_QW = OFF_VB + 2 * LANES
OFF_KW = OFF_QW + HEADS * HEAD_DIM
OFF_VW = OFF_KW + 2 * LANES
OFF_GATES = OFF_VW + 2 * LANES

_NT = (((1,), (1,)), ((), ()))
_F32 = jnp.float32
_BF16 = jnp.bfloat16


def _params(*semantics):
    return pltpu.CompilerParams(dimension_semantics=semantics, vmem_limit_bytes=VMEM_LIMIT)


def _full(shape):
    return pl.BlockSpec(shape, lambda *_: (0,) * len(shape))


def _rms(x, g):
    return x * lax.rsqrt(jnp.mean(x * x, axis=-1, keepdims=True) + NORM_EPS) * g


def _rope(x, tabs, shift):
    cos, sin_lo, sin_hi = tabs
    outs = []
    for c in range(x.shape[1] // LANES):
        xc = x[:, c * LANES:(c + 1) * LANES]
        up = pltpu.roll(xc, LANES - shift, 1)
        dn = pltpu.roll(xc, shift, 1)
        outs.append(xc * cos + up * sin_lo + dn * sin_hi)
    return outs[0] if len(outs) == 1 else jnp.concatenate(outs, axis=1)


def _head_rms(x, bd, g):
    outs = []
    for c in range(x.shape[1] // LANES):
        xc = x[:, c * LANES:(c + 1) * LANES]
        sq = xc * xc
        hi = sq.astype(_BF16)
        lo = (sq - hi.astype(_F32)).astype(_BF16)
        ssq = (jnp.dot(hi, bd, preferred_element_type=_F32)
               + jnp.dot(lo, bd, preferred_element_type=_F32))
        outs.append(xc * lax.rsqrt(ssq * (1.0 / HEAD_DIM) + NORM_EPS))
    y = outs[0] if len(outs) == 1 else jnp.concatenate(outs, axis=1)
    return y * g


def _mod_kernel(c_ref, w_ref, b_ref, o_ref):
    c = c_ref[...]
    a = c / (1.0 + jnp.exp(-c))
    o_ref[0] = jnp.dot(a, w_ref[0], preferred_element_type=_F32,
                       precision=lax.Precision.HIGHEST) + b_ref[0]


def _modulation(cc, w_mod, b_mod):
    depth, d, cols = w_mod.shape
    tn = cols // 4
    return pl.pallas_call(
        _mod_kernel,
        out_shape=jax.ShapeDtypeStruct((depth, 8, cols), _F32),
        grid=(depth, cols // tn),
        in_specs=[
            pl.BlockSpec((8, d), lambda l, j: (0, 0)),
            pl.BlockSpec((1, d, tn), lambda l, j: (l, 0, j)),
            pl.BlockSpec((1, 1, tn), lambda l, j: (l, 0, j)),
        ],
        out_specs=pl.BlockSpec((1, 8, tn), lambda l, j: (l, 0, j)),
        compiler_params=_params("arbitrary", "arbitrary"),
        name="modulation",
    )(cc, w_mod, b_mod.reshape(depth, 1, cols))


def _inproj_kernel(x_ref, mod_ref, gattn_ref, win_ref, gcq_ref, gckv_ref, wuq_ref, wukv_ref,
                   gqn_ref, gkn_ref, rope_ref, bd_ref,
                   qa_ref, ka_ref, va_ref, qb_ref, kb_ref, vb_ref, qw_ref, kw_ref, vw_ref,
                   gates_ref, *, d, scale_a, scale_h):
    x = x_ref[...]
    sh1 = mod_ref[0, :, 0:d]
    sc1 = mod_ref[0, :, d:2 * d]
    h = (_rms(x, gattn_ref[...]) * (1.0 + sc1) + sh1).astype(_BF16)

    def proj(off, width):
        return jnp.dot(h, win_ref[:, off:off + width], preferred_element_type=_F32)

    rope_a = tuple(rope_ref[:, i * LANES:(i + 1) * LANES] for i in range(3))
    rope_h = tuple(rope_ref[:, i * LANES:(i + 1) * LANES] for i in range(3, 6))
    bd = bd_ref[...]

    cq = _rms(proj(OFF_CQ, MLA_Q_RANK), gcq_ref[...]).astype(_BF16)
    qa = jnp.dot(cq, wuq_ref[...], preferred_element_type=_F32)
    qa_ref[...] = (_rope(qa, rope_a, MLA_ROPE // 2) * scale_a).astype(_BF16)
    ckv = _rms(proj(OFF_CKV, MLA_KV_RANK), gckv_ref[...]).astype(_BF16)
    kva = jnp.dot(ckv, wukv_ref[...], preferred_element_type=_F32)
    kr = _rope(proj(OFF_KR, LANES), rope_a, MLA_ROPE // 2)
    ka = kva[:, :HEADS * LANES] + jnp.concatenate([kr] * HEADS, axis=1)
    ka_ref[...] = ka.astype(_BF16)
    va_ref[...] = kva[:, HEADS * LANES:].astype(_BF16)

    qb = _head_rms(proj(OFF_QB, HEADS * HEAD_DIM), bd, gqn_ref[...])
    qb_ref[...] = (_rope(qb, rope_h, HEAD_DIM // 2) * scale_h).astype(_BF16)
    kb = _head_rms(proj(OFF_KB, 2 * LANES), bd, gkn_ref[...])
    kb_ref[...] = _rope(kb, rope_h, HEAD_DIM // 2).astype(_BF16)
    vb_ref[...] = proj(OFF_VB, 2 * LANES).astype(_BF16)

    qw_ref[...] = (_rope(proj(OFF_QW, HEADS * HEAD_DIM), rope_h, HEAD_DIM // 2) * scale_h).astype(_BF16)
    kw_ref[...] = _rope(proj(OFF_KW, 2 * LANES), rope_h, HEAD_DIM // 2).astype(_BF16)
    vw_ref[...] = proj(OFF_VW, 2 * LANES).astype(_BF16)

    for k in range(3):
        g = proj(OFF_GATES + k * d, d)
        gates_ref[:, k * d:(k + 1) * d] = (1.0 / (1.0 + jnp.exp(-g))).astype(_BF16)


def _inproj(xs, mod_l, gattn, win, gcq, gckv, wuq, wukv, gqn_t, gkn_t, rope, bd, *, geo):
    n, d = xs.shape
    tm = ROW_TILE
    lat_tiles, tiles_per_batch, batch = geo
    row = lambda t: (t, 0)

    def mod_idx(t):
        return (jnp.where(t < lat_tiles, t // tiles_per_batch, batch), 0, 0)

    def rope_idx(t):
        return (jnp.where(t < lat_tiles, t % tiles_per_batch, tiles_per_batch), 0)

    widths = (HEADS * LANES, HEADS * LANES, HEADS * HEAD_DIM, HEADS * HEAD_DIM, 2 * LANES, 2 * LANES,
              HEADS * HEAD_DIM, 2 * LANES, 2 * LANES, 3 * d)
    return pl.pallas_call(
        functools.partial(_inproj_kernel, d=d, scale_a=(MLA_NOPE + MLA_ROPE) ** -0.5,
                          scale_h=HEAD_DIM ** -0.5),
        out_shape=[jax.ShapeDtypeStruct((n, w), _BF16) for w in widths],
        grid=(n // tm,),
        in_specs=[
            pl.BlockSpec((tm, d), row),
            pl.BlockSpec((1, 1, N_MOD * d), mod_idx),
            _full(gattn.shape), _full(win.shape), _full(gcq.shape), _full(gckv.shape),
            _full(wuq.shape), _full(wukv.shape), _full(gqn_t.shape), _full(gkn_t.shape),
            pl.BlockSpec((tm, 6 * LANES), rope_idx),
            _full(bd.shape),
        ],
        out_specs=[pl.BlockSpec((tm, w), row) for w in widths],
        compiler_params=_params("arbitrary"),
        name="inproj",
    )(xs, mod_l, gattn, win, gcq, gckv, wuq, wukv, gqn_t, gkn_t, rope, bd)


def _attn_kernel(*refs, nseg, wide, has_sink, band, nq):
    q_ref = refs[0]
    k_refs = refs[1:1 + nseg]
    v_refs = refs[1 + nseg:1 + 2 * nseg]
    sink_ref = refs[1 + 2 * nseg] if has_sink else None
    o_ref = refs[-1]
    pair = pl.program_id(1)
    qi = pl.program_id(2)
    tq = q_ref.shape[0]
    lane = lax.broadcasted_iota(jnp.int32, (tq, LANES), 1)

    masks = [None] * nseg
    if band:
        half = k_refs[0].shape[0]
        r = lax.broadcasted_iota(jnp.int32, (tq, half), 0)
        c = lax.broadcasted_iota(jnp.int32, (tq, half), 1)
        masks[0] = c >= r + jnp.where(qi > 0, 0, half)
        masks[2] = c <= r - (tq - half) - (half - WINDOW) - jnp.where(qi < nq - 1, 0, tq)
        r = lax.broadcasted_iota(jnp.int32, (tq, tq), 0)
        c = lax.broadcasted_iota(jnp.int32, (tq, tq), 1)
        masks[1] = jnp.abs(r - c) <= WINDOW

    outs = []
    for hh in range(2):
        if wide:
            q = q_ref[:, hh * LANES:(hh + 1) * LANES]
            ks = [k[:, hh * LANES:(hh + 1) * LANES] for k in k_refs]
        else:
            own = (lane[:1] < HEAD_DIM) if hh == 0 else (lane[:1] >= HEAD_DIM)
            q = q_ref[...] * own.astype(_F32).astype(_BF16)
            ks = [k[...] for k in k_refs]
        ss = [lax.dot_general(q, k, _NT, preferred_element_type=_F32) for k in ks]
        ss = [s if m is None else jnp.where(m, s, NEG_INF) for s, m in zip(ss, masks)]
        m = functools.reduce(jnp.maximum, [jnp.max(s, axis=-1, keepdims=True) for s in ss])
        if has_sink:
            sink = sink_ref[pl.ds(2 * pair + hh, 1), 0:1]
            m = jnp.maximum(m, sink)
        ps = [jnp.exp(s - m) for s in ss]
        l = functools.reduce(jnp.add, [jnp.sum(p, axis=-1, keepdims=True) for p in ps])
        if has_sink:
            l = l + jnp.exp(sink - m)
        o = functools.reduce(jnp.add, [jnp.dot(p.astype(_BF16), v[...], preferred_element_type=_F32)
                                       for p, v in zip(ps, v_refs)])
        outs.append(o / l)
    o_ref[...] = jnp.where(lane < HEAD_DIM, outs[0], outs[1]).astype(o_ref.dtype)


def _flash_kernel(q_ref, k_ref, kc_ref, v_ref, vc_ref, o_ref, *, wide, chunk):
    tq = q_ref.shape[0]
    lane = lax.broadcasted_iota(jnp.int32, (1, LANES), 1)
    qs, kcols = [], []
    for hh in range(2):
        if wide:
            qs.append(q_ref[:, hh * LANES:(hh + 1) * LANES])
            kcols.append(slice(hh * LANES, (hh + 1) * LANES))
        else:
            own = (lane < HEAD_DIM) if hh == 0 else (lane >= HEAD_DIM)
            qs.append(q_ref[...] * own.astype(_F32).astype(_BF16))
            kcols.append(slice(0, LANES))
    m = [jnp.full((tq, 1), -jnp.inf, _F32) for _ in range(2)]
    l = [jnp.zeros((tq, 1), _F32) for _ in range(2)]
    acc = [jnp.zeros((tq, LANES), _F32) for _ in range(2)]
    pieces = [(k_ref, v_ref, r0, chunk) for r0 in range(0, k_ref.shape[0], chunk)]
    pieces.append((kc_ref, vc_ref, 0, kc_ref.shape[0]))
    for kr, vr, r0, rows in pieces:
        for hh in range(2):
            s = lax.dot_general(qs[hh], kr[r0:r0 + rows, kcols[hh]], _NT,
                                preferred_element_type=_F32)
            m_new = jnp.maximum(m[hh], jnp.max(s, axis=-1, keepdims=True))
            alpha = jnp.exp(m[hh] - m_new)
            p = jnp.exp(s - m_new)
            l[hh] = alpha * l[hh] + jnp.sum(p, axis=-1, keepdims=True)
            acc[hh] = alpha * acc[hh] + jnp.dot(p.astype(_BF16), vr[r0:r0 + rows, :],
                                                preferred_element_type=_F32)
            m[hh] = m_new
    o_ref[...] = jnp.where(lane < HEAD_DIM, acc[0] / l[0], acc[1] / l[1]).astype(o_ref.dtype)


def _flash_attention(q, k, v, *, wide, batch, seq, n_ctx):
    tq = FLASH_Q_TILE
    nq = seq // tq
    qw = 2 * LANES if wide else LANES
    kdiv = 1 if wide else 2
    ctx_blk0 = batch * seq // n_ctx
    return pl.pallas_call(
        functools.partial(_flash_kernel, wide=wide, chunk=FLASH_K_CHUNK),
        out_shape=jax.ShapeDtypeStruct((batch * seq, HEADS * HEAD_DIM), _BF16),
        grid=(batch, HEAD_PAIRS, nq),
        in_specs=[
            pl.BlockSpec((tq, qw), lambda b, p, i: (b * nq + i, p)),
            pl.BlockSpec((seq, qw), lambda b, p, i: (b, p // kdiv)),
            pl.BlockSpec((n_ctx, qw), lambda b, p, i: (ctx_blk0 + b, p // kdiv)),
            pl.BlockSpec((seq, LANES), lambda b, p, i: (b, p // kdiv)),
            pl.BlockSpec((n_ctx, LANES), lambda b, p, i: (ctx_blk0 + b, p // kdiv)),
        ],
        out_specs=pl.BlockSpec((tq, LANES), lambda b, p, i: (b * nq + i, p)),
        compiler_params=_params("arbitrary", "arbitrary", "arbitrary"),
        name="flash_attention",
    )(q, k, k, v, v)


def _attention(q, k, v, sink, *, wide, segs, nq, q_block0, batch, band=False):
    tq = Q_TILE
    qw = 2 * LANES if wide else LANES
    kdiv = 1 if wide else 2
    in_specs = [pl.BlockSpec((tq, qw), lambda b, p, i: (q_block0 + b * nq + i, p))]
    for rows, fn in segs:
        in_specs.append(pl.BlockSpec((rows, qw), functools.partial(
            lambda b, p, i, fn: (fn(b, i), p // kdiv), fn=fn)))
    for rows, fn in segs:
        in_specs.append(pl.BlockSpec((rows, LANES), functools.partial(
            lambda b, p, i, fn: (fn(b, i), p // kdiv), fn=fn)))
    args = [q] + [k] * len(segs) + [v] * len(segs)
    if sink is not None:
        in_specs.append(_full(sink.shape))
        args.append(sink)
    return pl.pallas_call(
        functools.partial(_attn_kernel, nseg=len(segs), wide=wide, has_sink=sink is not None,
                          band=band, nq=nq),
        out_shape=jax.ShapeDtypeStruct((batch * nq * tq, HEADS * HEAD_DIM), _BF16),
        grid=(batch, HEAD_PAIRS, nq),
        in_specs=in_specs,
        out_specs=pl.BlockSpec((tq, LANES), lambda b, p, i: (b * nq + i, p)),
        compiler_params=_params("arbitrary", "arbitrary", "arbitrary"),
        name="attention",
    )(*args)


def _merge_kernel(x_ref, mod_ref, oa_ref, ob_ref, ow_ref, gates_ref, woa_ref, wob_ref, wow_ref,
                  wo_ref, gffn_ref, wpq_ref, subk_ref,
                  x1_ref, h2_ref, s1_ref, s2_ref, *, d):
    gt1 = mod_ref[0, :, 2 * d:3 * d]
    sh2 = mod_ref[0, :, 3 * d:4 * d]
    sc2 = mod_ref[0, :, 4 * d:5 * d]
    m = None
    for k, (o_ref, w_ref) in enumerate(((oa_ref, woa_ref), (ob_ref, wob_ref), (ow_ref, wow_ref))):
        t = gates_ref[:, k * d:(k + 1) * d].astype(_F32) * jnp.dot(
            o_ref[...], w_ref[...], preferred_element_type=_F32)
        m = t if m is None else m + t
    y = jnp.dot(m.astype(_BF16), wo_ref[...], preferred_element_type=_F32)
    x1 = x_ref[...] + gt1 * y
    x1_ref[...] = x1
    h2 = (_rms(x1, gffn_ref[...]) * (1.0 + sc2) + sh2).astype(_BF16)
    h2_ref[...] = h2
    q = jnp.dot(h2, wpq_ref[...], preferred_element_type=_F32).astype(_BF16)
    for g in range(2 * PEER_HEADS):
        s = lax.dot_general(subk_ref[g], q[:, g * PEER_HALF:(g + 1) * PEER_HALF], _NT,
                            preferred_element_type=_F32)
        if g % 2 == 0:
            s1_ref[g // 2] = s
        else:
            s2_ref[g // 2] = s


def _merge(xs, mod_l, oa, ob, ow, gates, woa, wob, wow, wo, gffn, wpq, subk, *, n_rows, geo):
    d = xs.shape[1]
    tm = ROW_TILE
    lat_tiles, tiles_per_batch, batch = geo
    row = lambda t: (t, 0)

    def mod_idx(t):
        return (jnp.where(t < lat_tiles, t // tiles_per_batch, batch), 0, 0)

    return pl.pallas_call(
        functools.partial(_merge_kernel, d=d),
        out_shape=[
            jax.ShapeDtypeStruct((n_rows, d), _F32),
            jax.ShapeDtypeStruct((n_rows, d), _BF16),
            jax.ShapeDtypeStruct((PEER_HEADS, PEER_KEYS, n_rows), _F32),
            jax.ShapeDtypeStruct((PEER_HEADS, PEER_KEYS, n_rows), _F32),
        ],
        grid=(n_rows // tm,),
        in_specs=[
            pl.BlockSpec((tm, d), row),
            pl.BlockSpec((1, 1, N_MOD * d), mod_idx),
            pl.BlockSpec((tm, HEADS * HEAD_DIM), row),
            pl.BlockSpec((tm, HEADS * HEAD_DIM), row),
            pl.BlockSpec((tm, HEADS * HEAD_DIM), row),
            pl.BlockSpec((tm, 3 * d), row),
            _full(woa.shape), _full(wob.shape), _full(wow.shape), _full(wo.shape),
            _full(gffn.shape), _full(wpq.shape), _full(subk.shape),
        ],
        out_specs=[
            pl.BlockSpec((tm, d), row),
            pl.BlockSpec((tm, d), row),
            pl.BlockSpec((PEER_HEADS, PEER_KEYS, tm), lambda t: (0, 0, t)),
            pl.BlockSpec((PEER_HEADS, PEER_KEYS, tm), lambda t: (0, 0, t)),
        ],
        compiler_params=_params("arbitrary"),
        name="merge",
    )(xs, mod_l, oa, ob, ow, gates, woa, wob, wow, wo, gffn, wpq, subk)


NO_RANK = 64.0


def _top_values(w, k, want_rank=False):
    vals = []
    rank = jnp.full(w.shape, NO_RANK, _F32) if want_rank else None
    for r in range(k):
        m = jnp.max(w, axis=0, keepdims=True)
        vals.append(m)
        hit = w == m
        if want_rank:
            rank = jnp.where(hit, float(r), rank)
        w = jnp.where(hit, -jnp.inf, w)
    return (vals, rank) if want_rank else vals


def _stack_rows(vals, n_rows):
    rows = lax.broadcasted_iota(jnp.int32, (n_rows, vals[0].shape[1]), 0)
    out = jnp.full(rows.shape, -jnp.inf, _F32)
    for k, v in enumerate(vals):
        out = jnp.where(rows == k, v, out)
    return out


def _topk_kernel(s1_ref, s2_ref, e1_ref, cnt_ref, e2_ref, rk_ref):
    n_top = PEER_TOPK + 1

    def head(h, carry):
        s1 = s1_ref[h]
        s2 = s2_ref[h]
        t1 = _top_values(s1, n_top)
        t2, rank2 = _top_values(s2, n_top, want_rank=True)
        t2c = _stack_rows(t2, 24)
        cand = [t1[0] + t2c]
        cand += [t1[a] + t2c[:8] for a in range(1, 8)]
        cand += [_stack_rows(t1[8:], 16) + t2[0]]
        best = _top_values(jnp.concatenate(cand, axis=0), n_top)
        z = functools.reduce(jnp.add, [jnp.exp(v - best[0]) for v in best[:PEER_TOPK]])
        tau = 0.5 * (best[PEER_TOPK - 1] + best[PEER_TOPK])
        cnt = jnp.zeros(s1.shape, _F32)
        for b in range(PEER_TOPK):
            cnt = cnt + jnp.where(s1 + t2[b] >= tau, 1.0, 0.0)
        e1_ref[h] = jnp.exp(s1 - t1[0]) / z
        cnt_ref[h] = cnt
        e2_ref[h] = pltpu.bitcast(jnp.exp(s2 - t2[0]).astype(_BF16), jnp.uint32)
        rk_ref[h] = pltpu.bitcast(rank2.astype(_BF16), jnp.uint32)
        return carry

    lax.fori_loop(0, PEER_HEADS, head, 0)


def _topk(s1t, s2t):
    n = s1t.shape[2]
    tk = TOPK_TOKENS
    blk = pl.BlockSpec((PEER_HEADS, PEER_KEYS, tk), lambda t: (0, 0, t))
    packed = pl.BlockSpec((PEER_HEADS, PEER_KEYS // 2, tk), lambda t: (0, 0, t))
    return pl.pallas_call(
        _topk_kernel,
        out_shape=[jax.ShapeDtypeStruct(s1t.shape, _F32)] * 2
        + [jax.ShapeDtypeStruct((PEER_HEADS, PEER_KEYS // 2, n), jnp.uint32)] * 2,
        grid=(n // tk,),
        in_specs=[blk, blk],
        out_specs=[blk, blk, packed, packed],
        compiler_params=_params("arbitrary"),
        name="peer_topk",
    )(s1t, s2t)


def _gate_tile(a_ref, cnt_ref, e1_ref, e2_ref, rk_ref, w_ref):
    shape = (PEER_CHUNK, LANES)
    for ii in range(PEER_ROWS):
        for lt in range(a_ref.shape[1] // LANES):
            ls = slice(lt * LANES, (lt + 1) * LANES)
            cntb = [jnp.broadcast_to(cnt_ref[h, ii:ii + 1, ls], shape).astype(_BF16)
                    for h in range(PEER_HEADS)]
            e1b = [jnp.broadcast_to(e1_ref[h, ii:ii + 1, ls], shape).astype(_BF16)
                   for h in range(PEER_HEADS)]
            for jc in range(PEER_KEYS // PEER_CHUNK):
                rows = slice(jc * PEER_CHUNK, (jc + 1) * PEER_CHUNK)
                words = slice(rows.start // 2, rows.stop // 2)
                g = None
                for h in range(PEER_HEADS):
                    e2 = pltpu.bitcast(e2_ref[h, words, ls], _BF16)
                    rk = pltpu.bitcast(rk_ref[h, words, ls], _BF16)
                    t = e1b[h] * jnp.where(rk < cntb[h], e2, jnp.zeros_like(e2))
                    g = t if g is None else g + t
                erows = slice(ii * PEER_KEYS + rows.start, ii * PEER_KEYS + rows.stop)
                a = a_ref[erows, ls]
                act = (0.5 * a) * (1.0 + lax.erf(a * (1.0 / math.sqrt(2.0))))
                w_ref[erows, ls] = g * act.astype(_BF16)


def _peer_kernel(h2_ref, x1_ref, mod_ref, u_ref, vt_ref, cnt_ref, e1_ref, e2_ref, rk_ref,
                 o_ref, acc_ref, a_ref, w_ref, *, d, n_steps):
    k = pl.program_id(1)

    @pl.when(k == 0)
    def _():
        acc_ref[...] = jnp.zeros_like(acc_ref)

    a_ref[...] = lax.dot_general(u_ref[...], h2_ref[...], _NT, preferred_element_type=_F32)
    _gate_tile(a_ref, cnt_ref, e1_ref, e2_ref, rk_ref, w_ref)
    acc_ref[...] += jnp.dot(vt_ref[...], w_ref[...], preferred_element_type=_F32)

    @pl.when(k == n_steps - 1)
    def _():
        gt2 = mod_ref[0, :, 5 * d:6 * d]
        o_ref[...] = x1_ref[...] + gt2 * acc_ref[...].T


def _peer(h2, x1, mod_l, u, vt, e1t, cntt, e2t, rkt, *, geo):
    n, d = h2.shape
    tn = PEER_TOKENS
    te = PEER_ROWS * PEER_KEYS
    n_steps = u.shape[0] // te
    lat_tiles, tiles_per_batch, batch = geo
    row = lambda t, k: (t, 0)

    def mod_idx(t, k):
        return (jnp.where(t < lat_tiles, t // tiles_per_batch, batch), 0, 0)

    rows_blk = pl.BlockSpec((PEER_HEADS, PEER_ROWS, tn), lambda t, k: (0, k, t))
    return pl.pallas_call(
        functools.partial(_peer_kernel, d=d, n_steps=n_steps),
        out_shape=jax.ShapeDtypeStruct((n, d), _F32),
        grid=(n // tn, n_steps),
        in_specs=[
            pl.BlockSpec((tn, d), row),
            pl.BlockSpec((tn, d), row),
            pl.BlockSpec((1, 1, N_MOD * d), mod_idx),
            pl.BlockSpec((te, d), lambda t, k: (k, 0)),
            pl.BlockSpec((d, te), lambda t, k: (0, k)),
            rows_blk,
            rows_blk,
            pl.BlockSpec((PEER_HEADS, PEER_KEYS // 2, tn), lambda t, k: (0, 0, t)),
            pl.BlockSpec((PEER_HEADS, PEER_KEYS // 2, tn), lambda t, k: (0, 0, t)),
        ],
        out_specs=pl.BlockSpec((tn, d), row),
        scratch_shapes=[
            pltpu.VMEM((d, tn), _F32),
            pltpu.VMEM((te, tn), _F32),
            pltpu.VMEM((te, tn), _BF16),
        ],
        compiler_params=_params("arbitrary", "arbitrary"),
        name="peer_experts",
    )(h2, x1, mod_l, u, vt, cntt, e1t, e2t, rkt)


def _final_kernel(x_ref, g_ref, o_ref):
    o_ref[...] = _rms(x_ref[...], g_ref[...])


def _final_norm(xs, g, n_rows):
    d = xs.shape[1]
    tm = ROW_TILE
    return pl.pallas_call(
        _final_kernel,
        out_shape=jax.ShapeDtypeStruct((n_rows, d), _F32),
        grid=(n_rows // tm,),
        in_specs=[pl.BlockSpec((tm, d), lambda t: (t, 0)), _full(g.shape)],
        out_specs=pl.BlockSpec((tm, d), lambda t: (t, 0)),
        compiler_params=_params("arbitrary"),
        name="final_norm",
    )(xs, g)


def _rope_tables(seq):
    rows = seq // GRID_W
    r = jnp.repeat(jnp.arange(rows, dtype=_F32), GRID_W)
    col = jnp.tile(jnp.arange(GRID_W, dtype=_F32), rows)

    def cos_sin(rot_dim):
        n_freq = rot_dim // 4
        inv = ROPE_BASE ** (-jnp.arange(n_freq, dtype=_F32) / n_freq)
        ang = jnp.concatenate([r[:, None] * inv, col[:, None] * inv], axis=-1)
        return jnp.cos(ang), jnp.sin(ang)

    ca, sa = cos_sin(MLA_ROPE)
    one = jnp.ones((seq, MLA_NOPE), _F32)
    zero = jnp.zeros_like(one)
    z16 = jnp.zeros_like(ca)
    pad1 = jnp.ones((seq, LANES - MLA_NOPE - MLA_ROPE), _F32)
    pad0 = jnp.zeros_like(pad1)
    tab_a = [jnp.concatenate([one, ca, ca, pad1], axis=1),
             jnp.concatenate([zero, -sa, z16, pad0], axis=1),
             jnp.concatenate([zero, z16, sa, pad0], axis=1)]
    ch, sh = cos_sin(HEAD_DIM)
    z32 = jnp.zeros_like(ch)
    tab_h = [jnp.concatenate([ch, ch, ch, ch], axis=1),
             jnp.concatenate([-sh, z32, -sh, z32], axis=1),
             jnp.concatenate([z32, sh, z32, sh], axis=1)]
    lat = jnp.concatenate(tab_a + tab_h, axis=1)
    ident = jnp.concatenate([jnp.ones((ROW_TILE, LANES), _F32), jnp.zeros((ROW_TILE, 2 * LANES), _F32)] * 2,
                            axis=1)
    return jnp.concatenate([lat, ident], axis=0)


def _pack_w_in(w_in, d):
    splits = np.cumsum([MLA_Q_RANK, MLA_KV_RANK, MLA_ROPE, 512, 128, 128, 512, 128, 128])
    cq, ckv, kr, qb, kb, vb, qw, kw, vw, gates = jnp.split(w_in, splits, axis=-1)
    lead = w_in.shape[:-1]
    kr_pad = jnp.concatenate([jnp.zeros(lead + (MLA_NOPE,), w_in.dtype), kr,
                              jnp.zeros(lead + (LANES - MLA_NOPE - MLA_ROPE,), w_in.dtype)], axis=-1)

    def dup(w):
        g0, g1 = w[..., :HEAD_DIM], w[..., HEAD_DIM:]
        return jnp.concatenate([g0, g0, g1, g1], axis=-1)

    return jnp.concatenate([cq, ckv, kr_pad, qb, dup(kb), dup(vb), qw, dup(kw), dup(vw), gates],
                           axis=-1).astype(_BF16)


def _pack_w_uq(w_uq):
    depth, rank, _ = w_uq.shape
    w = w_uq.reshape(depth, rank, HEADS, MLA_NOPE + MLA_ROPE)
    w = jnp.pad(w, ((0, 0), (0, 0), (0, 0), (0, LANES - MLA_NOPE - MLA_ROPE)))
    return w.reshape(depth, rank, HEADS * LANES).astype(_BF16)


def _pack_w_ukv(w_ukv):
    depth, rank, _ = w_ukv.shape
    w = w_ukv.reshape(depth, rank, HEADS, MLA_NOPE + HEAD_DIM)
    k = jnp.pad(w[..., :MLA_NOPE], ((0, 0), (0, 0), (0, 0), (0, LANES - MLA_NOPE)))
    v = w[..., MLA_NOPE:]
    return jnp.concatenate([k.reshape(depth, rank, HEADS * LANES),
                            v.reshape(depth, rank, HEADS * HEAD_DIM)], axis=-1).astype(_BF16)


def kernel(x, c, ctx, c_ctx, w_mod, b_mod, g_attn, g_ffn, w_in, g_cq, g_ckv, w_uq, w_ukv,
           g_qn, g_kn, sink, w_oa, w_ob, w_ow, w_o, w_pq, sub_keys, peer_u, peer_v, g_final):
    batch, seq, d = x.shape
    n_ctx = ctx.shape[1]
    depth = w_mod.shape[0]
    assert n_ctx == ROW_TILE and seq % PEER_TOKENS == 0 and (batch * n_ctx) % PEER_TOKENS == 0
    assert batch < 8 and seq % GRID_W == 0 and seq >= 2 * Q_TILE
    assert seq % FLASH_Q_TILE == 0 and seq % FLASH_K_CHUNK == 0
    n_lat = batch * seq
    n_all = n_lat + batch * n_ctx
    nq = seq // Q_TILE
    ctx_blk0 = n_lat // ROW_TILE

    xs = jnp.concatenate([x.reshape(n_lat, d), ctx.reshape(batch * n_ctx, d)], axis=0)
    cc = jnp.concatenate([c, c_ctx[None], jnp.zeros((8 - batch - 1, d), _F32)], axis=0)
    mod = _modulation(cc, w_mod, b_mod).reshape(depth, 8, 1, N_MOD * d)

    rope = _rope_tables(seq)
    eye = np.kron(np.eye(2, dtype=np.float32), np.ones((HEAD_DIM, HEAD_DIM), np.float32))
    bd = jnp.asarray(eye, _BF16)
    win = _pack_w_in(w_in, d)
    wuq = _pack_w_uq(w_uq)
    wukv = _pack_w_ukv(w_ukv)
    gqn_t = jnp.tile(g_qn, (1, HEADS))[:, None, :]
    gkn_t = jnp.tile(g_kn, (1, 2 * LANES // HEAD_DIM))[:, None, :]
    sink_t = jnp.broadcast_to(sink[:, :, None], (depth, HEADS, LANES))
    woa, wob, wow, wo, wpq = (w.astype(_BF16) for w in (w_oa, w_ob, w_ow, w_o, w_pq))
    subk = sub_keys.reshape(depth, 2 * PEER_HEADS, PEER_KEYS, PEER_HALF).astype(_BF16)
    u_bf = peer_u.astype(_BF16)
    vt_bf = jnp.swapaxes(peer_v, 1, 2).astype(_BF16)

    geo_row = (n_lat // ROW_TILE, seq // ROW_TILE, batch)
    geo_peer = (n_lat // PEER_TOKENS, seq // PEER_TOKENS, batch)
    lat_seg = (seq, lambda b, i: b)
    ctx_seg = (n_ctx, lambda b, i: ctx_blk0 + b)
    half = Q_TILE // 2
    band_segs = [
        (half, lambda b, i: b * 2 * nq + jnp.maximum(2 * i - 1, 0)),
        (Q_TILE, lambda b, i: b * nq + i),
        (half, lambda b, i: b * 2 * nq + jnp.minimum(2 * i + 2, 2 * nq - 1)),
        ctx_seg,
    ]

    for l in range(depth):
        last = l == depth - 1
        qa, ka, va, qb, kb, vb, qw, kw, vw, gates = _inproj(
            xs, mod[l], g_attn[l][None], win[l], g_cq[l][None], g_ckv[l][None], wuq[l], wukv[l],
            gqn_t[l], gkn_t[l], rope, bd, geo=geo_row)
        lat = dict(nq=nq, q_block0=0, batch=batch)
        dense = dict(batch=batch, seq=seq, n_ctx=n_ctx)
        o_a = _flash_attention(qa, ka, va, wide=True, **dense)
        o_b = _flash_attention(qb, kb, vb, wide=False, **dense)
        o_w = _attention(qw, kw, vw, sink_t[l], wide=False, segs=band_segs, band=True, **lat)
        n_rows = n_lat
        if not last:
            cq = dict(nq=1, q_block0=ctx_blk0, batch=batch, segs=[ctx_seg])
            o_a = jnp.concatenate([o_a, _attention(qa, ka, va, None, wide=True, **cq)], axis=0)
            o_b = jnp.concatenate([o_b, _attention(qb, kb, vb, None, wide=False, **cq)], axis=0)
            o_w = jnp.concatenate([o_w, _attention(qw, kw, vw, sink_t[l], wide=False, **cq)], axis=0)
            n_rows = n_all
        x1, h2, s1t, s2t = _merge(xs, mod[l], o_a, o_b, o_w, gates, woa[l], wob[l], wow[l], wo[l],
                                  g_ffn[l][None], wpq[l], subk[l], n_rows=n_rows, geo=geo_row)
        e1t, cntt, e2t, rkt = _topk(s1t, s2t)
        xs = _peer(h2, x1, mod[l], u_bf[l], vt_bf[l], e1t, cntt, e2t, rkt, geo=geo_peer)
    out = _final_norm(xs, g_final[None], n_lat)
    return out.reshape(batch, seq, d)
```

```python
import functools
import math

import jax
import jax.numpy as jnp
import numpy as np
from jax import lax
from jax.experimental import pallas as pl
from jax.experimental.pallas import tpu as pltpu

GRID_W = 64
ROPE_BASE = 10000.0
NORM_EPS = 1e-6
NEG_INF = -1e30
WINDOW = 128
N_MOD = 6

HEADS = 8
HEAD_PAIRS = HEADS // 2
MLA_Q_RANK = 384
MLA_KV_RANK = 256
MLA_NOPE = 64
MLA_ROPE = 32
HEAD_DIM = 64
KV_HEADS = 2
LANES = 128

PEER_HEADS = 8
PEER_KEYS = 128
PEER_HALF = 128
PEER_TOPK = 16

ROW_TILE = 256
Q_TILE = 256
FLASH_Q_TILE = 1024
FLASH_K_CHUNK = 1024
PEER_TOKENS = 512
PEER_ROWS = 8
PEER_CHUNK = 32
TOPK_TOKENS = 256
VMEM_LIMIT = 56 * 1024 * 1024

OFF_CQ = 0
OFF_CKV = OFF_CQ + MLA_Q_RANK
OFF_KR = OFF_CKV + MLA_KV_RANK
OFF_QB = OFF_KR + LANES
OFF_KB = OFF_QB + HEADS * HEAD_DIM
OFF_VB = OFF_KB + 2 * LANES
OFF_QW = OFF_VB + 2 * LANES
OFF_KW = OFF_QW + HEADS * HEAD_DIM
OFF_VW = OFF_KW + 2 * LANES
OFF_GATES = OFF_VW + 2 * LANES

_NT = (((1,), (1,)), ((), ()))
_F32 = jnp.float32
_BF16 = jnp.bfloat16


def _params(*semantics):
    return pltpu.CompilerParams(dimension_semantics=semantics, vmem_limit_bytes=VMEM_LIMIT)


def _full(shape):
    return pl.BlockSpec(shape, lambda *_: (0,) * len(shape))


def _rms(x, g):
    return x * lax.rsqrt(jnp.mean(x * x, axis=-1, keepdims=True) + NORM_EPS) * g


def _rope(x, tabs, shift):
    cos, sin_lo, sin_hi = tabs
    outs = []
    for c in range(x.shape[1] // LANES):
        xc = x[:, c * LANES:(c + 1) * LANES]
        up = pltpu.roll(xc, LANES - shift, 1)
        dn = pltpu.roll(xc, shift, 1)
        outs.append(xc * cos + up * sin_lo + dn * sin_hi)
    return outs[0] if len(outs) == 1 else jnp.concatenate(outs, axis=1)


def _head_rms(x, bd, g):
    outs = []
    for c in range(x.shape[1] // LANES):
        xc = x[:, c * LANES:(c + 1) * LANES]
        sq = xc * xc
        hi = sq.astype(_BF16)
        lo = (sq - hi.astype(_F32)).astype(_BF16)
        ssq = (jnp.dot(hi, bd, preferred_element_type=_F32)
               + jnp.dot(lo, bd, preferred_element_type=_F32))
        outs.append(xc * lax.rsqrt(ssq * (1.0 / HEAD_DIM) + NORM_EPS))
    y = outs[0] if len(outs) == 1 else jnp.concatenate(outs, axis=1)
    return y * g


def _mod_kernel(c_ref, w_ref, b_ref, o_ref):
    c = c_ref[...]
    a = c / (1.0 + jnp.exp(-c))
    o_ref[0] = jnp.dot(a, w_ref[0], preferred_element_type=_F32,
                       precision=lax.Precision.HIGHEST) + b_ref[0]


def _modulation(cc, w_mod, b_mod):
    depth, d, cols = w_mod.shape
    tn = cols // 4
    return pl.pallas_call(
        _mod_kernel,
        out_shape=jax.ShapeDtypeStruct((depth, 8, cols), _F32),
        grid=(depth, cols // tn),
        in_specs=[
            pl.BlockSpec((8, d), lambda l, j: (0, 0)),
            pl.BlockSpec((1, d, tn), lambda l, j: (l, 0, j)),
            pl.BlockSpec((1, 1, tn), lambda l, j: (l, 0, j)),
        ],
        out_specs=pl.BlockSpec((1, 8, tn), lambda l, j: (l, 0, j)),
        compiler_params=_params("arbitrary", "arbitrary"),
        name="modulation",
    )(cc, w_mod, b_mod.reshape(depth, 1, cols))


def _inproj_kernel(x_ref, mod_ref, gattn_ref, win_ref, gcq_ref, gckv_ref, wuq_ref, wukv_ref,
                   gqn_ref, gkn_ref, rope_ref, bd_ref,
                   qa_ref, ka_ref, va_ref, qb_ref, kb_ref, vb_ref, qw_ref, kw_ref, vw_ref,
                   gates_ref, *, d, scale_a, scale_h):
    x = x_ref[...]
    sh1 = mod_ref[0, :, 0:d]
    sc1 = mod_ref[0, :, d:2 * d]
    h = (_rms(x, gattn_ref[...]) * (1.0 + sc1) + sh1).astype(_BF16)

    def proj(off, width):
        return jnp.dot(h, win_ref[:, off:off + width], preferred_element_type=_F32)

    rope_a = tuple(rope_ref[:, i * LANES:(i + 1) * LANES] for i in range(3))
    rope_h = tuple(rope_ref[:, i * LANES:(i + 1) * LANES] for i in range(3, 6))
    bd = bd_ref[...]

    cq = _rms(proj(OFF_CQ, MLA_Q_RANK), gcq_ref[...]).astype(_BF16)
    qa = jnp.dot(cq, wuq_ref[...], preferred_element_type=_F32)
    qa_ref[...] = (_rope(qa, rope_a, MLA_ROPE // 2) * scale_a).astype(_BF16)
    ckv = _rms(proj(OFF_CKV, MLA_KV_RANK), gckv_ref[...]).astype(_BF16)
    kva = jnp.dot(ckv, wukv_ref[...], preferred_element_type=_F32)
    kr = _rope(proj(OFF_KR, LANES), rope_a, MLA_ROPE // 2)
    ka = kva[:, :HEADS * LANES] + jnp.concatenate([kr] * HEADS, axis=1)
    ka_ref[...] = ka.astype(_BF16)
    va_ref[...] = kva[:, HEADS * LANES:].astype(_BF16)

    qb = _head_rms(proj(OFF_QB, HEADS * HEAD_DIM), bd, gqn_ref[...])
    qb_ref[...] = (_rope(qb, rope_h, HEAD_DIM // 2) * scale_h).astype(_BF16)
    kb = _head_rms(proj(OFF_KB, 2 * LANES), bd, gkn_ref[...])
    kb_ref[...] = _rope(kb, rope_h, HEAD_DIM // 2).astype(_BF16)
    vb_ref[...] = proj(OFF_VB, 2 * LANES).astype(_BF16)

    qw_ref[...] = (_rope(proj(OFF_QW, HEADS * HEAD_DIM), rope_h, HEAD_DIM // 2) * scale_h).astype(_BF16)
    kw_ref[...] = _rope(proj(OFF_KW, 2 * LANES), rope_h, HEAD_DIM // 2).astype(_BF16)
    vw_ref[...] = proj(OFF_VW, 2 * LANES).astype(_BF16)

    for k in range(3):
        g = proj(OFF_GATES + k * d, d)
        gates_ref[:, k * d:(k + 1) * d] = (1.0 / (1.0 + jnp.exp(-g))).astype(_BF16)


def _inproj(xs, mod_l, gattn, win, gcq, gckv, wuq, wukv, gqn_t, gkn_t, rope, bd, *, geo):
    n, d = xs.shape
    tm = ROW_TILE
    lat_tiles, tiles_per_batch, batch = geo
    row = lambda t: (t, 0)

    def mod_idx(t):
        return (jnp.where(t < lat_tiles, t // tiles_per_batch, batch), 0, 0)

    def rope_idx(t):
        return (jnp.where(t < lat_tiles, t % tiles_per_batch, tiles_per_batch), 0)

    widths = (HEADS * LANES, HEADS * LANES, HEADS * HEAD_DIM, HEADS * HEAD_DIM, 2 * LANES, 2 * LANES,
              HEADS * HEAD_DIM, 2 * LANES, 2 * LANES, 3 * d)
    return pl.pallas_call(
        functools.partial(_inproj_kernel, d=d, scale_a=(MLA_NOPE + MLA_ROPE) ** -0.5,
                          scale_h=HEAD_DIM ** -0.5),
        out_shape=[jax.ShapeDtypeStruct((n, w), _BF16) for w in widths],
        grid=(n // tm,),
        in_specs=[
            pl.BlockSpec((tm, d), row),
            pl.BlockSpec((1, 1, N_MOD * d), mod_idx),
            _full(gattn.shape), _full(win.shape), _full(gcq.shape), _full(gckv.shape),
            _full(wuq.shape), _full(wukv.shape), _full(gqn_t.shape), _full(gkn_t.shape),
            pl.BlockSpec((tm, 6 * LANES), rope_idx),
            _full(bd.shape),
        ],
        out_specs=[pl.BlockSpec((tm, w), row) for w in widths],
        compiler_params=_params("arbitrary"),
        name="inproj",
    )(xs, mod_l, gattn, win, gcq, gckv, wuq, wukv, gqn_t, gkn_t, rope, bd)


def _attn_kernel(*refs, nseg, wide, has_sink, band, nq):
    q_ref = refs[0]
    k_refs = refs[1:1 + nseg]
    v_refs = refs[1 + nseg:1 + 2 * nseg]
    sink_ref = refs[1 + 2 * nseg] if has_sink else None
    o_ref = refs[-1]
    pair = pl.program_id(1)
    qi = pl.program_id(2)
    tq = q_ref.shape[0]
    lane = lax.broadcasted_iota(jnp.int32, (tq, LANES), 1)

    masks = [None] * nseg
    if band:
        half = k_refs[0].shape[0]
        r = lax.broadcasted_iota(jnp.int32, (tq, half), 0)
        c = lax.broadcasted_iota(jnp.int32, (tq, half), 1)
        masks[0] = c >= r + jnp.where(qi > 0, 0, half)
        masks[2] = c <= r - (tq - half) - (half - WINDOW) - jnp.where(qi < nq - 1, 0, tq)
        r = lax.broadcasted_iota(jnp.int32, (tq, tq), 0)
        c = lax.broadcasted_iota(jnp.int32, (tq, tq), 1)
        masks[1] = jnp.abs(r - c) <= WINDOW

    outs = []
    for hh in range(2):
        if wide:
            q = q_ref[:, hh * LANES:(hh + 1) * LANES]
            ks = [k[:, hh * LANES:(hh + 1) * LANES] for k in k_refs]
        else:
            own = (lane[:1] < HEAD_DIM) if hh == 0 else (lane[:1] >= HEAD_DIM)
            q = q_ref[...] * own.astype(_F32).astype(_BF16)
            ks = [k[...] for k in k_refs]
        ss = [lax.dot_general(q, k, _NT, preferred_element_type=_F32) for k in ks]
        ss = [s if m is None else jnp.where(m, s, NEG_INF) for s, m in zip(ss, masks)]
        m = functools.reduce(jnp.maximum, [jnp.max(s, axis=-1, keepdims=True) for s in ss])
        if has_sink:
            sink = sink_ref[pl.ds(2 * pair + hh, 1), 0:1]
            m = jnp.maximum(m, sink)
        ps = [jnp.exp(s - m) for s in ss]
        l = functools.reduce(jnp.add, [jnp.sum(p, axis=-1, keepdims=True) for p in ps])
        if has_sink:
            l = l + jnp.exp(sink - m)
        o = functools.reduce(jnp.add, [jnp.dot(p.astype(_BF16), v[...], preferred_element_type=_F32)
                                       for p, v in zip(ps, v_refs)])
        outs.append(o / l)
    o_ref[...] = jnp.where(lane < HEAD_DIM, outs[0], outs[1]).astype(o_ref.dtype)


def _flash_kernel(q_ref, k_ref, kc_ref, v_ref, vc_ref, o_ref, *, wide, chunk):
    tq = q_ref.shape[0]
    lane = lax.broadcasted_iota(jnp.int32, (1, LANES), 1)
    qs, kcols = [], []
    for hh in range(2):
        if wide:
            qs.append(q_ref[:, hh * LANES:(hh + 1) * LANES])
            kcols.append(slice(hh * LANES, (hh + 1) * LANES))
        else:
            own = (lane < HEAD_DIM) if hh == 0 else (lane >= HEAD_DIM)
            qs.append(q_ref[...] * own.astype(_F32).astype(_BF16))
            kcols.append(slice(0, LANES))
    m = [jnp.full((tq, 1), -jnp.inf, _F32) for _ in range(2)]
    l = [jnp.zeros((tq, 1), _F32) for _ in range(2)]
    acc = [jnp.zeros((tq, LANES), _F32) for _ in range(2)]
    pieces = [(k_ref, v_ref, r0, chunk) for r0 in range(0, k_ref.shape[0], chunk)]
    pieces.append((kc_ref, vc_ref, 0, kc_ref.shape[0]))
    for kr, vr, r0, rows in pieces:
        for hh in range(2):
            s = lax.dot_general(qs[hh], kr[r0:r0 + rows, kcols[hh]], _NT,
                                preferred_element_type=_F32)
            m_new = jnp.maximum(m[hh], jnp.max(s, axis=-1, keepdims=True))
            alpha = jnp.exp(m[hh] - m_new)
            p = jnp.exp(s - m_new)
            l[hh] = alpha * l[hh] + jnp.sum(p, axis=-1, keepdims=True)
            acc[hh] = alpha * acc[hh] + jnp.dot(p.astype(_BF16), vr[r0:r0 + rows, :],
                                                preferred_element_type=_F32)
            m[hh] = m_new
    o_ref[...] = jnp.where(lane < HEAD_DIM, acc[0] / l[0], acc[1] / l[1]).astype(o_ref.dtype)


def _flash_attention(q, k, v, *, wide, batch, seq, n_ctx):
    tq = FLASH_Q_TILE
    nq = seq // tq
    qw = 2 * LANES if wide else LANES
    kdiv = 1 if wide else 2
    ctx_blk0 = batch * seq // n_ctx
    return pl.pallas_call(
        functools.partial(_flash_kernel, wide=wide, chunk=FLASH_K_CHUNK),
        out_shape=jax.ShapeDtypeStruct((batch * seq, HEADS * HEAD_DIM), _BF16),
        grid=(batch, HEAD_PAIRS, nq),
        in_specs=[
            pl.BlockSpec((tq, qw), lambda b, p, i: (b * nq + i, p)),
            pl.BlockSpec((seq, qw), lambda b, p, i: (b, p // kdiv)),
            pl.BlockSpec((n_ctx, qw), lambda b, p, i: (ctx_blk0 + b, p // kdiv)),
            pl.BlockSpec((seq, LANES), lambda b, p, i: (b, p // kdiv)),
            pl.BlockSpec((n_ctx, LANES), lambda b, p, i: (ctx_blk0 + b, p // kdiv)),
        ],
        out_specs=pl.BlockSpec((tq, LANES), lambda b, p, i: (b * nq + i, p)),
        compiler_params=_params("arbitrary", "arbitrary", "arbitrary"),
        name="flash_attention",
    )(q, k, k, v, v)


def _attention(q, k, v, sink, *, wide, segs, nq, q_block0, batch, band=False):
    tq = Q_TILE
    qw = 2 * LANES if wide else LANES
    kdiv = 1 if wide else 2
    in_specs = [pl.BlockSpec((tq, qw), lambda b, p, i: (q_block0 + b * nq + i, p))]
    for rows, fn in segs:
        in_specs.append(pl.BlockSpec((rows, qw), functools.partial(
            lambda b, p, i, fn: (fn(b, i), p // kdiv), fn=fn)))
    for rows, fn in segs:
        in_specs.append(pl.BlockSpec((rows, LANES), functools.partial(
            lambda b, p, i, fn: (fn(b, i), p // kdiv), fn=fn)))
    args = [q] + [k] * len(segs) + [v] * len(segs)
    if sink is not None:
        in_specs.append(_full(sink.shape))
        args.append(sink)
    return pl.pallas_call(
        functools.partial(_attn_kernel, nseg=len(segs), wide=wide, has_sink=sink is not None,
                          band=band, nq=nq),
        out_shape=jax.ShapeDtypeStruct((batch * nq * tq, HEADS * HEAD_DIM), _BF16),
        grid=(batch, HEAD_PAIRS, nq),
        in_specs=in_specs,
        out_specs=pl.BlockSpec((tq, LANES), lambda b, p, i: (b * nq + i, p)),
        compiler_params=_params("arbitrary", "arbitrary", "arbitrary"),
        name="attention",
    )(*args)


def _merge_kernel(x_ref, mod_ref, oa_ref, ob_ref, ow_ref, gates_ref, woa_ref, wob_ref, wow_ref,
                  wo_ref, gffn_ref, wpq_ref, subk_ref,
                  x1_ref, h2_ref, s1_ref, s2_ref, *, d):
    gt1 = mod_ref[0, :, 2 * d:3 * d]
    sh2 = mod_ref[0, :, 3 * d:4 * d]
    sc2 = mod_ref[0, :, 4 * d:5 * d]
    m = None
    for k, (o_ref, w_ref) in enumerate(((oa_ref, woa_ref), (ob_ref, wob_ref), (ow_ref, wow_ref))):
        t = gates_ref[:, k * d:(k + 1) * d].astype(_F32) * jnp.dot(
            o_ref[...], w_ref[...], preferred_element_type=_F32)
        m = t if m is None else m + t
    y = jnp.dot(m.astype(_BF16), wo_ref[...], preferred_element_type=_F32)
    x1 = x_ref[...] + gt1 * y
    x1_ref[...] = x1
    h2 = (_rms(x1, gffn_ref[...]) * (1.0 + sc2) + sh2).astype(_BF16)
    h2_ref[...] = h2
    q = jnp.dot(h2, wpq_ref[...], preferred_element_type=_F32).astype(_BF16)
    for g in range(2 * PEER_HEADS):
        s = lax.dot_general(subk_ref[g], q[:, g * PEER_HALF:(g + 1) * PEER_HALF], _NT,
                            preferred_element_type=_F32)
        if g % 2 == 0:
            s1_ref[g // 2] = s
        else:
            s2_ref[g // 2] = s


def _merge(xs, mod_l, oa, ob, ow, gates, woa, wob, wow, wo, gffn, wpq, subk, *, n_rows, geo):
    d = xs.shape[1]
    tm = ROW_TILE
    lat_tiles, tiles_per_batch, batch = geo
    row = lambda t: (t, 0)

    def mod_idx(t):
        return (jnp.where(t < lat_tiles, t // tiles_per_batch, batch), 0, 0)

    return pl.pallas_call(
        functools.partial(_merge_kernel, d=d),
        out_shape=[
            jax.ShapeDtypeStruct((n_rows, d), _F32),
            jax.ShapeDtypeStruct((n_rows, d), _BF16),
            jax.ShapeDtypeStruct((PEER_HEADS, PEER_KEYS, n_rows), _F32),
            jax.ShapeDtypeStruct((PEER_HEADS, PEER_KEYS, n_rows), _F32),
        ],
        grid=(n_rows // tm,),
        in_specs=[
            pl.BlockSpec((tm, d), row),
            pl.BlockSpec((1, 1, N_MOD * d), mod_idx),
            pl.BlockSpec((tm, HEADS * HEAD_DIM), row),
            pl.BlockSpec((tm, HEADS * HEAD_DIM), row),
            pl.BlockSpec((tm, HEADS * HEAD_DIM), row),
            pl.BlockSpec((tm, 3 * d), row),
            _full(woa.shape), _full(wob.shape), _full(wow.shape), _full(wo.shape),
            _full(gffn.shape), _full(wpq.shape), _full(subk.shape),
        ],
        out_specs=[
            pl.BlockSpec((tm, d), row),
            pl.BlockSpec((tm, d), row),
            pl.BlockSpec((PEER_HEADS, PEER_KEYS, tm), lambda t: (0, 0, t)),
            pl.BlockSpec((PEER_HEADS, PEER_KEYS, tm), lambda t: (0, 0, t)),
        ],
        compiler_params=_params("arbitrary"),
        name="merge",
    )(xs, mod_l, oa, ob, ow, gates, woa, wob, wow, wo, gffn, wpq, subk)


NO_RANK = 64.0


def _top_values(w, k, want_rank=False):
    vals = []
    rank = jnp.full(w.shape, NO_RANK, _F32) if want_rank else None
    for r in range(k):
        m = jnp.max(w, axis=0, keepdims=True)
        vals.append(m)
        hit = w == m
        if want_rank:
            rank = jnp.where(hit, float(r), rank)
        w = jnp.where(hit, -jnp.inf, w)
    return (vals, rank) if want_rank else vals


def _stack_rows(vals, n_rows):
    rows = lax.broadcasted_iota(jnp.int32, (n_rows, vals[0].shape[1]), 0)
    out = jnp.full(rows.shape, -jnp.inf, _F32)
    for k, v in enumerate(vals):
        out = jnp.where(rows == k, v, out)
    return out


def _topk_kernel(s1_ref, s2_ref, e1_ref, cnt_ref, e2_ref, rk_ref):
    n_top = PEER_TOPK + 1

    def head(h, carry):
        s1 = s1_ref[h]
        s2 = s2_ref[h]
        t1 = _top_values(s1, n_top)
        t2, rank2 = _top_values(s2, n_top, want_rank=True)
        t2c = _stack_rows(t2, 24)
        cand = [t1[0] + t2c]
        cand += [t1[a] + t2c[:8] for a in range(1, 8)]
        cand += [_stack_rows(t1[8:], 16) + t2[0]]
        best = _top_values(jnp.concatenate(cand, axis=0), n_top)
        z = functools.reduce(jnp.add, [jnp.exp(v - best[0]) for v in best[:PEER_TOPK]])
        tau = 0.5 * (best[PEER_TOPK - 1] + best[PEER_TOPK])
        cnt = jnp.zeros(s1.shape, _F32)
        for b in range(PEER_TOPK):
            cnt = cnt + jnp.where(s1 + t2[b] >= tau, 1.0, 0.0)
        e1_ref[h] = jnp.exp(s1 - t1[0]) / z
        cnt_ref[h] = cnt
        e2_ref[h] = pltpu.bitcast(jnp.exp(s2 - t2[0]).astype(_BF16), jnp.uint32)
        rk_ref[h] = pltpu.bitcast(rank2.astype(_BF16), jnp.uint32)
        return carry

    lax.fori_loop(0, PEER_HEADS, head, 0)


def _topk(s1t, s2t):
    n = s1t.shape[2]
    tk = TOPK_TOKENS
    blk = pl.BlockSpec((PEER_HEADS, PEER_KEYS, tk), lambda t: (0, 0, t))
    packed = pl.BlockSpec((PEER_HEADS, PEER_KEYS // 2, tk), lambda t: (0, 0, t))
    return pl.pallas_call(
        _topk_kernel,
        out_shape=[jax.ShapeDtypeStruct(s1t.shape, _F32)] * 2
        + [jax.ShapeDtypeStruct((PEER_HEADS, PEER_KEYS // 2, n), jnp.uint32)] * 2,
        grid=(n // tk,),
        in_specs=[blk, blk],
        out_specs=[blk, blk, packed, packed],
        compiler_params=_params("arbitrary"),
        name="peer_topk",
    )(s1t, s2t)


def _gate_tile(a_ref, cnt_ref, e1_ref, e2_ref, rk_ref, w_ref):
    shape = (PEER_CHUNK, LANES)
    for ii in range(PEER_ROWS):
        for lt in range(a_ref.shape[1] // LANES):
            ls = slice(lt * LANES, (lt + 1) * LANES)
            cntb = [jnp.broadcast_to(cnt_ref[h, ii:ii + 1, ls], shape).astype(_BF16)
                    for h in range(PEER_HEADS)]
            e1b = [jnp.broadcast_to(e1_ref[h, ii:ii + 1, ls], shape).astype(_BF16)
                   for h in range(PEER_HEADS)]
            for jc in range(PEER_KEYS // PEER_CHUNK):
                rows = slice(jc * PEER_CHUNK, (jc + 1) * PEER_CHUNK)
                words = slice(rows.start // 2, rows.stop // 2)
                g = None
                for h in range(PEER_HEADS):
                    e2 = pltpu.bitcast(e2_ref[h, words, ls], _BF16)
                    rk = pltpu.bitcast(rk_ref[h, words, ls], _BF16)
                    t = e1b[h] * jnp.where(rk < cntb[h], e2, jnp.zeros_like(e2))
                    g = t if g is None else g + t
                erows = slice(ii * PEER_KEYS + rows.start, ii * PEER_KEYS + rows.stop)
                a = a_ref[erows, ls]
                act = (0.5 * a) * (1.0 + lax.erf(a * (1.0 / math.sqrt(2.0))))
                w_ref[erows, ls] = g * act.astype(_BF16)


def _peer_kernel(h2_ref, x1_ref, mod_ref, u0_ref, un_ref, vt_ref, cnt_ref, e1_ref, e2_ref, rk_ref,
                 o_ref, acc_ref, a_ref, w_ref, *, d, n_steps):
    k = pl.program_id(1)

    @pl.when(k == 0)
    def _():
        acc_ref[...] = jnp.zeros_like(acc_ref)
        a_ref[...] = lax.dot_general(u0_ref[...], h2_ref[...], _NT, preferred_element_type=_F32)

    _gate_tile(a_ref, cnt_ref, e1_ref, e2_ref, rk_ref, w_ref)
    acc_ref[...] += jnp.dot(vt_ref[...], w_ref[...], preferred_element_type=_F32)
    a_ref[...] = lax.dot_general(un_ref[...], h2_ref[...], _NT, preferred_element_type=_F32)

    @pl.when(k == n_steps - 1)
    def _():
        gt2 = mod_ref[0, :, 5 * d:6 * d]
        o_ref[...] = x1_ref[...] + gt2 * acc_ref[...].T


def _peer(h2, x1, mod_l, u, vt, e1t, cntt, e2t, rkt, *, geo):
    n, d = h2.shape
    tn = PEER_TOKENS
    te = PEER_ROWS * PEER_KEYS
    n_steps = u.shape[0] // te
    lat_tiles, tiles_per_batch, batch = geo
    row = lambda t, k: (t, 0)

    def mod_idx(t, k):
        return (jnp.where(t < lat_tiles, t // tiles_per_batch, batch), 0, 0)

    rows_blk = pl.BlockSpec((PEER_HEADS, PEER_ROWS, tn), lambda t, k: (0, k, t))
    return pl.pallas_call(
        functools.partial(_peer_kernel, d=d, n_steps=n_steps),
        out_shape=jax.ShapeDtypeStruct((n, d), _F32),
        grid=(n // tn, n_steps),
        in_specs=[
            pl.BlockSpec((tn, d), row),
            pl.BlockSpec((tn, d), row),
            pl.BlockSpec((1, 1, N_MOD * d), mod_idx),
            pl.BlockSpec((te, d), lambda t, k: (0, 0)),
            pl.BlockSpec((te, d), lambda t, k: (jnp.minimum(k + 1, n_steps - 1), 0)),
            pl.BlockSpec((d, te), lambda t, k: (0, k)),
            rows_blk,
            rows_blk,
            pl.BlockSpec((PEER_HEADS, PEER_KEYS // 2, tn), lambda t, k: (0, 0, t)),
            pl.BlockSpec((PEER_HEADS, PEER_KEYS // 2, tn), lambda t, k: (0, 0, t)),
        ],
        out_specs=pl.BlockSpec((tn, d), row),
        scratch_shapes=[
            pltpu.VMEM((d, tn), _F32),
            pltpu.VMEM((te, tn), _F32),
            pltpu.VMEM((te, tn), _BF16),
        ],
        compiler_params=_params("arbitrary", "arbitrary"),
        name="peer_experts",
    )(h2, x1, mod_l, u, u, vt, cntt, e1t, e2t, rkt)


def _final_kernel(x_ref, g_ref, o_ref):
    o_ref[...] = _rms(x_ref[...], g_ref[...])


def _final_norm(xs, g, n_rows):
    d = xs.shape[1]
    tm = ROW_TILE
    return pl.pallas_call(
        _final_kernel,
        out_shape=jax.ShapeDtypeStruct((n_rows, d), _F32),
        grid=(n_rows // tm,),
        in_specs=[pl.BlockSpec((tm, d), lambda t: (t, 0)), _full(g.shape)],
        out_specs=pl.BlockSpec((tm, d), lambda t: (t, 0)),
        compiler_params=_params("arbitrary"),
        name="final_norm",
    )(xs, g)


def _rope_tables(seq):
    rows = seq // GRID_W
    r = jnp.repeat(jnp.arange(rows, dtype=_F32), GRID_W)
    col = jnp.tile(jnp.arange(GRID_W, dtype=_F32), rows)

    def cos_sin(rot_dim):
        n_freq = rot_dim // 4
        inv = ROPE_BASE ** (-jnp.arange(n_freq, dtype=_F32) / n_freq)
        ang = jnp.concatenate([r[:, None] * inv, col[:, None] * inv], axis=-1)
        return jnp.cos(ang), jnp.sin(ang)

    ca, sa = cos_sin(MLA_ROPE)
    one = jnp.ones((seq, MLA_NOPE), _F32)
    zero = jnp.zeros_like(one)
    z16 = jnp.zeros_like(ca)
    pad1 = jnp.ones((seq, LANES - MLA_NOPE - MLA_ROPE), _F32)
    pad0 = jnp.zeros_like(pad1)
    tab_a = [jnp.concatenate([one, ca, ca, pad1], axis=1),
             jnp.concatenate([zero, -sa, z16, pad0], axis=1),
             jnp.concatenate([zero, z16, sa, pad0], axis=1)]
    ch, sh = cos_sin(HEAD_DIM)
    z32 = jnp.zeros_like(ch)
    tab_h = [jnp.concatenate([ch, ch, ch, ch], axis=1),
             jnp.concatenate([-sh, z32, -sh, z32], axis=1),
             jnp.concatenate([z32, sh, z32, sh], axis=1)]
    lat = jnp.concatenate(tab_a + tab_h, axis=1)
    ident = jnp.concatenate([jnp.ones((ROW_TILE, LANES), _F32), jnp.zeros((ROW_TILE, 2 * LANES), _F32)] * 2,
                            axis=1)
    return jnp.concatenate([lat, ident], axis=0)


def _pack_w_in(w_in, d):
    splits = np.cumsum([MLA_Q_RANK, MLA_KV_RANK, MLA_ROPE, 512, 128, 128, 512, 128, 128])
    cq, ckv, kr, qb, kb, vb, qw, kw, vw, gates = jnp.split(w_in, splits, axis=-1)
    lead = w_in.shape[:-1]
    kr_pad = jnp.concatenate([jnp.zeros(lead + (MLA_NOPE,), w_in.dtype), kr,
                              jnp.zeros(lead + (LANES - MLA_NOPE - MLA_ROPE,), w_in.dtype)], axis=-1)

    def dup(w):
        g0, g1 = w[..., :HEAD_DIM], w[..., HEAD_DIM:]
        return jnp.concatenate([g0, g0, g1, g1], axis=-1)

    return jnp.concatenate([cq, ckv, kr_pad, qb, dup(kb), dup(vb), qw, dup(kw), dup(vw), gates],
                           axis=-1).astype(_BF16)


def _pack_w_uq(w_uq):
    depth, rank, _ = w_uq.shape
    w = w_uq.reshape(depth, rank, HEADS, MLA_NOPE + MLA_ROPE)
    w = jnp.pad(w, ((0, 0), (0, 0), (0, 0), (0, LANES - MLA_NOPE - MLA_ROPE)))
    return w.reshape(depth, rank, HEADS * LANES).astype(_BF16)


def _pack_w_ukv(w_ukv):
    depth, rank, _ = w_ukv.shape
    w = w_ukv.reshape(depth, rank, HEADS, MLA_NOPE + HEAD_DIM)
    k = jnp.pad(w[..., :MLA_NOPE], ((0, 0), (0, 0), (0, 0), (0, LANES - MLA_NOPE)))
    v = w[..., MLA_NOPE:]
    return jnp.concatenate([k.reshape(depth, rank, HEADS * LANES),
                            v.reshape(depth, rank, HEADS * HEAD_DIM)], axis=-1).astype(_BF16)


def kernel(x, c, ctx, c_ctx, w_mod, b_mod, g_attn, g_ffn, w_in, g_cq, g_ckv, w_uq, w_ukv,
           g_qn, g_kn, sink, w_oa, w_ob, w_ow, w_o, w_pq, sub_keys, peer_u, peer_v, g_final):
    batch, seq, d = x.shape
    n_ctx = ctx.shape[1]
    depth = w_mod.shape[0]
    assert n_ctx == ROW_TILE and seq % PEER_TOKENS == 0 and (batch * n_ctx) % PEER_TOKENS == 0
    assert batch < 8 and seq % GRID_W == 0 and seq >= 2 * Q_TILE
    assert seq % FLASH_Q_TILE == 0 and seq % FLASH_K_CHUNK == 0
    n_lat = batch * seq
    n_all = n_lat + batch * n_ctx
    nq = seq // Q_TILE
    ctx_blk0 = n_lat // ROW_TILE

    xs = jnp.concatenate([x.reshape(n_lat, d), ctx.reshape(batch * n_ctx, d)], axis=0)
    cc = jnp.concatenate([c, c_ctx[None], jnp.zeros((8 - batch - 1, d), _F32)], axis=0)
    mod = _modulation(cc, w_mod, b_mod).reshape(depth, 8, 1, N_MOD * d)

    rope = _rope_tables(seq)
    eye = np.kron(np.eye(2, dtype=np.float32), np.ones((HEAD_DIM, HEAD_DIM), np.float32))
    bd = jnp.asarray(eye, _BF16)
    win = _pack_w_in(w_in, d)
    wuq = _pack_w_uq(w_uq)
    wukv = _pack_w_ukv(w_ukv)
    gqn_t = jnp.tile(g_qn, (1, HEADS))[:, None, :]
    gkn_t = jnp.tile(g_kn, (1, 2 * LANES // HEAD_DIM))[:, None, :]
    sink_t = jnp.broadcast_to(sink[:, :, None], (depth, HEADS, LANES))
    woa, wob, wow, wo, wpq = (w.astype(_BF16) for w in (w_oa, w_ob, w_ow, w_o, w_pq))
    subk = sub_keys.reshape(depth, 2 * PEER_HEADS, PEER_KEYS, PEER_HALF).astype(_BF16)
    u_bf = peer_u.astype(_BF16)
    vt_bf = jnp.swapaxes(peer_v, 1, 2).astype(_BF16)

    geo_row = (n_lat // ROW_TILE, seq // ROW_TILE, batch)
    geo_peer = (n_lat // PEER_TOKENS, seq // PEER_TOKENS, batch)
    lat_seg = (seq, lambda b, i: b)
    ctx_seg = (n_ctx, lambda b, i: ctx_blk0 + b)
    half = Q_TILE // 2
    band_segs = [
        (half, lambda b, i: b * 2 * nq + jnp.maximum(2 * i - 1, 0)),
        (Q_TILE, lambda b, i: b * nq + i),
        (half, lambda b, i: b * 2 * nq + jnp.minimum(2 * i + 2, 2 * nq - 1)),
        ctx_seg,
    ]

    for l in range(depth):
        last = l == depth - 1
        qa, ka, va, qb, kb, vb, qw, kw, vw, gates = _inproj(
            xs, mod[l], g_attn[l][None], win[l], g_cq[l][None], g_ckv[l][None], wuq[l], wukv[l],
            gqn_t[l], gkn_t[l], rope, bd, geo=geo_row)
        lat = dict(nq=nq, q_block0=0, batch=batch)
        dense = dict(batch=batch, seq=seq, n_ctx=n_ctx)
        o_a = _flash_attention(qa, ka, va, wide=True, **dense)
        o_b = _flash_attention(qb, kb, vb, wide=False, **dense)
        o_w = _attention(qw, kw, vw, sink_t[l], wide=False, segs=band_segs, band=True, **lat)
        n_rows = n_lat
        if not last:
            cq = dict(nq=1, q_block0=ctx_blk0, batch=batch, segs=[ctx_seg])
            o_a = jnp.concatenate([o_a, _attention(qa, ka, va, None, wide=True, **cq)], axis=0)
            o_b = jnp.concatenate([o_b, _attention(qb, kb, vb, None, wide=False, **cq)], axis=0)
            o_w = jnp.concatenate([o_w, _attention(qw, kw, vw, sink_t[l], wide=False, **cq)], axis=0)
            n_rows = n_all
        x1, h2, s1t, s2t = _merge(xs, mod[l], o_a, o_b, o_w, gates, woa[l], wob[l], wow[l], wo[l],
                                  g_ffn[l][None], wpq[l], subk[l], n_rows=n_rows, geo=geo_row)
        e1t, cntt, e2t, rkt = _topk(s1t, s2t)
        xs = _peer(h2, x1, mod[l], u_bf[l], vt_bf[l], e1t, cntt, e2t, rkt, geo=geo_peer)
    out = _final_norm(xs, g_final[None], n_lat)
    return out.reshape(batch, seq, d)
```

```python
import functools
import math

import jax
import jax.numpy as jnp
import numpy as np
from jax import lax
from jax.experimental import pallas as pl
from jax.experimental.pallas import tpu as pltpu

GRID_W = 64
ROPE_BASE = 10000.0
NORM_EPS = 1e-6
NEG_INF = -1e30
WINDOW = 128
N_MOD = 6

HEADS = 8
HEAD_PAIRS = HEADS // 2
MLA_Q_RANK = 384
MLA_KV_RANK = 256
MLA_NOPE = 64
MLA_ROPE = 32
HEAD_DIM = 64
KV_HEADS = 2
LANES = 128

PEER_HEADS = 8
PEER_KEYS = 128
PEER_HALF = 128
PEER_TOPK = 16

ROW_TILE = 256
PROJ_TILE = 512
Q_TILE = 256
FLASH_Q_TILE = 1024
FLASH_K_CHUNK = 1024
PEER_TOKENS = 512
PEER_ROWS = 16
PEER_CHUNK = 32
TOPK_TOKENS = 256
VMEM_LIMIT = 56 * 1024 * 1024

OFF_CQ = 0
OFF_CKV = OFF_CQ + MLA_Q_RANK
OFF_KR = OFF_CKV + MLA_KV_RANK
OFF_QB = OFF_KR + LANES
OFF_KB = OFF_QB + HEADS * HEAD_DIM
OFF_VB = OFF_KB + 2 * LANES
OFF_QW = OFF_VB + 2 * LANES
OFF_KW = OFF_QW + HEADS * HEAD_DIM
OFF_VW = OFF_KW + 2 * LANES
OFF_GATES = OFF_VW + 2 * LANES

_NT = (((1,), (1,)), ((), ()))
_F32 = jnp.float32
_BF16 = jnp.bfloat16


def _params(*semantics):
    return pltpu.CompilerParams(dimension_semantics=semantics, vmem_limit_bytes=VMEM_LIMIT)


def _full(shape):
    return pl.BlockSpec(shape, lambda *_: (0,) * len(shape), pipeline_mode=pl.Buffered(1))


def _rms(x, g):
    return x * lax.rsqrt(jnp.mean(x * x, axis=-1, keepdims=True) + NORM_EPS) * g


def _rope(x, tabs, shift):
    cos, sin_lo, sin_hi = tabs
    outs = []
    for c in range(x.shape[1] // LANES):
        xc = x[:, c * LANES:(c + 1) * LANES]
        up = pltpu.roll(xc, LANES - shift, 1)
        dn = pltpu.roll(xc, shift, 1)
        outs.append(xc * cos + up * sin_lo + dn * sin_hi)
    return outs[0] if len(outs) == 1 else jnp.concatenate(outs, axis=1)


def _head_rms(x, bd, g):
    outs = []
    for c in range(x.shape[1] // LANES):
        xc = x[:, c * LANES:(c + 1) * LANES]
        sq = xc * xc
        hi = sq.astype(_BF16)
        lo = (sq - hi.astype(_F32)).astype(_BF16)
        ssq = (jnp.dot(hi, bd, preferred_element_type=_F32)
               + jnp.dot(lo, bd, preferred_element_type=_F32))
        outs.append(xc * lax.rsqrt(ssq * (1.0 / HEAD_DIM) + NORM_EPS))
    y = outs[0] if len(outs) == 1 else jnp.concatenate(outs, axis=1)
    return y * g


def _mod_kernel(c_ref, w_ref, b_ref, o_ref):
    c = c_ref[...]
    a = c / (1.0 + jnp.exp(-c))
    o_ref[0] = jnp.dot(a, w_ref[0], preferred_element_type=_F32,
                       precision=lax.Precision.HIGHEST) + b_ref[0]


def _modulation(cc, w_mod, b_mod):
    depth, d, cols = w_mod.shape
    tn = cols // 4
    return pl.pallas_call(
        _mod_kernel,
        out_shape=jax.ShapeDtypeStruct((depth, 8, cols), _F32),
        grid=(depth, cols // tn),
        in_specs=[
            pl.BlockSpec((8, d), lambda l, j: (0, 0)),
            pl.BlockSpec((1, d, tn), lambda l, j: (l, 0, j)),
            pl.BlockSpec((1, 1, tn), lambda l, j: (l, 0, j)),
        ],
        out_specs=pl.BlockSpec((1, 8, tn), lambda l, j: (l, 0, j)),
        compiler_params=_params("arbitrary", "arbitrary"),
        name="modulation",
    )(cc, w_mod, b_mod.reshape(depth, 1, cols))


def _inproj_kernel(x_ref, mod_ref, gattn_ref, win_ref, gcq_ref, gckv_ref, wuq_ref, wukv_ref,
                   gqn_ref, gkn_ref, rope_ref, bd_ref,
                   qa_ref, ka_ref, va_ref, qb_ref, kb_ref, vb_ref, qw_ref, kw_ref, vw_ref,
                   gates_ref, *, d, scale_a, scale_h):
    x = x_ref[...]
    sh1 = mod_ref[0, :, 0:d]
    sc1 = mod_ref[0, :, d:2 * d]
    h = (_rms(x, gattn_ref[...]) * (1.0 + sc1) + sh1).astype(_BF16)

    def proj(off, width):
        return jnp.dot(h, win_ref[:, off:off + width], preferred_element_type=_F32)

    rope_a = tuple(rope_ref[:, i * LANES:(i + 1) * LANES] for i in range(3))
    rope_h = tuple(rope_ref[:, i * LANES:(i + 1) * LANES] for i in range(3, 6))
    bd = bd_ref[...]

    cq = _rms(proj(OFF_CQ, MLA_Q_RANK), gcq_ref[...]).astype(_BF16)
    qa = jnp.dot(cq, wuq_ref[...], preferred_element_type=_F32)
    qa_ref[...] = (_rope(qa, rope_a, MLA_ROPE // 2) * scale_a).astype(_BF16)
    ckv = _rms(proj(OFF_CKV, MLA_KV_RANK), gckv_ref[...]).astype(_BF16)
    kva = jnp.dot(ckv, wukv_ref[...], preferred_element_type=_F32)
    kr = _rope(proj(OFF_KR, LANES), rope_a, MLA_ROPE // 2)
    ka = kva[:, :HEADS * LANES] + jnp.concatenate([kr] * HEADS, axis=1)
    ka_ref[...] = ka.astype(_BF16)
    va_ref[...] = kva[:, HEADS * LANES:].astype(_BF16)

    qb = _head_rms(proj(OFF_QB, HEADS * HEAD_DIM), bd, gqn_ref[...])
    qb_ref[...] = (_rope(qb, rope_h, HEAD_DIM // 2) * scale_h).astype(_BF16)
    kb = _head_rms(proj(OFF_KB, 2 * LANES), bd, gkn_ref[...])
    kb_ref[...] = _rope(kb, rope_h, HEAD_DIM // 2).astype(_BF16)
    vb_ref[...] = proj(OFF_VB, 2 * LANES).astype(_BF16)

    qw_ref[...] = (_rope(proj(OFF_QW, HEADS * HEAD_DIM), rope_h, HEAD_DIM // 2) * scale_h).astype(_BF16)
    kw_ref[...] = _rope(proj(OFF_KW, 2 * LANES), rope_h, HEAD_DIM // 2).astype(_BF16)
    vw_ref[...] = proj(OFF_VW, 2 * LANES).astype(_BF16)

    for k in range(3):
        g = proj(OFF_GATES + k * d, d)
        gates_ref[:, k * d:(k + 1) * d] = (1.0 / (1.0 + jnp.exp(-g))).astype(_BF16)


def _inproj(xs, mod_l, gattn, win, gcq, gckv, wuq, wukv, gqn_t, gkn_t, rope, bd, *, geo):
    n, d = xs.shape
    tm = PROJ_TILE
    lat_tiles, tiles_per_batch, batch = geo
    row = lambda t: (t, 0)

    def mod_idx(t):
        return (jnp.where(t < lat_tiles, t // tiles_per_batch, batch), 0, 0)

    def rope_idx(t):
        return (jnp.where(t < lat_tiles, t % tiles_per_batch, tiles_per_batch), 0)

    widths = (HEADS * LANES, HEADS * LANES, HEADS * HEAD_DIM, HEADS * HEAD_DIM, 2 * LANES, 2 * LANES,
              HEADS * HEAD_DIM, 2 * LANES, 2 * LANES, 3 * d)
    return pl.pallas_call(
        functools.partial(_inproj_kernel, d=d, scale_a=(MLA_NOPE + MLA_ROPE) ** -0.5,
                          scale_h=HEAD_DIM ** -0.5),
        out_shape=[jax.ShapeDtypeStruct((n, w), _BF16) for w in widths],
        grid=(n // tm,),
        in_specs=[
            pl.BlockSpec((tm, d), row),
            pl.BlockSpec((1, 1, N_MOD * d), mod_idx),
            _full(gattn.shape), _full(win.shape), _full(gcq.shape), _full(gckv.shape),
            _full(wuq.shape), _full(wukv.shape), _full(gqn_t.shape), _full(gkn_t.shape),
            pl.BlockSpec((tm, 6 * LANES), rope_idx),
            _full(bd.shape),
        ],
        out_specs=[pl.BlockSpec((tm, w), row) for w in widths],
        compiler_params=_params("arbitrary"),
        name="inproj",
    )(xs, mod_l, gattn, win, gcq, gckv, wuq, wukv, gqn_t, gkn_t, rope, bd)


def _attn_kernel(*refs, nseg, wide, has_sink, band, nq):
    q_ref = refs[0]
    k_refs = refs[1:1 + nseg]
    v_refs = refs[1 + nseg:1 + 2 * nseg]
    sink_ref = refs[1 + 2 * nseg] if has_sink else None
    o_ref = refs[-1]
    pair = pl.program_id(1)
    qi = pl.program_id(2)
    tq = q_ref.shape[0]
    lane = lax.broadcasted_iota(jnp.int32, (tq, LANES), 1)

    masks = [None] * nseg
    if band:
        half = k_refs[0].shape[0]
        r = lax.broadcasted_iota(jnp.int32, (tq, half), 0)
        c = lax.broadcasted_iota(jnp.int32, (tq, half), 1)
        masks[0] = c >= r + jnp.where(qi > 0, 0, half)
        masks[2] = c <= r - (tq - half) - (half - WINDOW) - jnp.where(qi < nq - 1, 0, tq)
        r = lax.broadcasted_iota(jnp.int32, (tq, tq), 0)
        c = lax.broadcasted_iota(jnp.int32, (tq, tq), 1)
        masks[1] = jnp.abs(r - c) <= WINDOW

    outs = []
    for hh in range(2):
        if wide:
            q = q_ref[:, hh * LANES:(hh + 1) * LANES]
            ks = [k[:, hh * LANES:(hh + 1) * LANES] for k in k_refs]
        else:
            own = (lane[:1] < HEAD_DIM) if hh == 0 else (lane[:1] >= HEAD_DIM)
            q = q_ref[...] * own.astype(_F32).astype(_BF16)
            ks = [k[...] for k in k_refs]
        ss = [lax.dot_general(q, k, _NT, preferred_element_type=_F32) for k in ks]
        ss = [s if m is None else jnp.where(m, s, NEG_INF) for s, m in zip(ss, masks)]
        m = functools.reduce(jnp.maximum, [jnp.max(s, axis=-1, keepdims=True) for s in ss])
        if has_sink:
            sink = sink_ref[pl.ds(2 * pair + hh, 1), 0:1]
            m = jnp.maximum(m, sink)
        ps = [jnp.exp(s - m) for s in ss]
        l = functools.reduce(jnp.add, [jnp.sum(p, axis=-1, keepdims=True) for p in ps])
        if has_sink:
            l = l + jnp.exp(sink - m)
        o = functools.reduce(jnp.add, [jnp.dot(p.astype(_BF16), v[...], preferred_element_type=_F32)
                                       for p, v in zip(ps, v_refs)])
        outs.append(o / l)
    o_ref[...] = jnp.where(lane < HEAD_DIM, outs[0], outs[1]).astype(o_ref.dtype)


def _flash_kernel(q_ref, k_ref, kc_ref, v_ref, vc_ref, o_ref, *, wide, chunk):
    tq = q_ref.shape[0]
    lane = lax.broadcasted_iota(jnp.int32, (1, LANES), 1)
    qs, kcols = [], []
    for hh in range(2):
        if wide:
            qs.append(q_ref[:, hh * LANES:(hh + 1) * LANES])
            kcols.append(slice(hh * LANES, (hh + 1) * LANES))
        else:
            own = (lane < HEAD_DIM) if hh == 0 else (lane >= HEAD_DIM)
            qs.append(q_ref[...] * own.astype(_F32).astype(_BF16))
            kcols.append(slice(0, LANES))
    m = [jnp.full((tq, 1), -jnp.inf, _F32) for _ in range(2)]
    l = [jnp.zeros((tq, 1), _F32) for _ in range(2)]
    acc = [jnp.zeros((tq, LANES), _F32) for _ in range(2)]
    pieces = [(k_ref, v_ref, r0, chunk) for r0 in range(0, k_ref.shape[0], chunk)]
    pieces.append((kc_ref, vc_ref, 0, kc_ref.shape[0]))
    for kr, vr, r0, rows in pieces:
        for hh in range(2):
            s = lax.dot_general(qs[hh], kr[r0:r0 + rows, kcols[hh]], _NT,
                                preferred_element_type=_F32)
            m_new = jnp.maximum(m[hh], jnp.max(s, axis=-1, keepdims=True))
            alpha = jnp.exp(m[hh] - m_new)
            p = jnp.exp(s - m_new)
            l[hh] = alpha * l[hh] + jnp.sum(p, axis=-1, keepdims=True)
            acc[hh] = alpha * acc[hh] + jnp.dot(p.astype(_BF16), vr[r0:r0 + rows, :],
                                                preferred_element_type=_F32)
            m[hh] = m_new
    o_ref[...] = jnp.where(lane < HEAD_DIM, acc[0] / l[0], acc[1] / l[1]).astype(o_ref.dtype)


def _flash_attention(q, k, v, *, wide, batch, seq, n_ctx):
    tq = FLASH_Q_TILE
    nq = seq // tq
    qw = 2 * LANES if wide else LANES
    kdiv = 1 if wide else 2
    ctx_blk0 = batch * seq // n_ctx
    return pl.pallas_call(
        functools.partial(_flash_kernel, wide=wide, chunk=FLASH_K_CHUNK),
        out_shape=jax.ShapeDtypeStruct((batch * seq, HEADS * HEAD_DIM), _BF16),
        grid=(batch, HEAD_PAIRS, nq),
        in_specs=[
            pl.BlockSpec((tq, qw), lambda b, p, i: (b * nq + i, p)),
            pl.BlockSpec((seq, qw), lambda b, p, i: (b, p // kdiv)),
            pl.BlockSpec((n_ctx, qw), lambda b, p, i: (ctx_blk0 + b, p // kdiv)),
            pl.BlockSpec((seq, LANES), lambda b, p, i: (b, p // kdiv)),
            pl.BlockSpec((n_ctx, LANES), lambda b, p, i: (ctx_blk0 + b, p // kdiv)),
        ],
        out_specs=pl.BlockSpec((tq, LANES), lambda b, p, i: (b * nq + i, p)),
        compiler_params=_params("arbitrary", "arbitrary", "arbitrary"),
        name="flash_attention",
    )(q, k, k, v, v)


def _attention(q, k, v, sink, *, wide, segs, nq, q_block0, batch, band=False):
    tq = Q_TILE
    qw = 2 * LANES if wide else LANES
    kdiv = 1 if wide else 2
    in_specs = [pl.BlockSpec((tq, qw), lambda b, p, i: (q_block0 + b * nq + i, p))]
    for rows, fn in segs:
        in_specs.append(pl.BlockSpec((rows, qw), functools.partial(
            lambda b, p, i, fn: (fn(b, i), p // kdiv), fn=fn)))
    for rows, fn in segs:
        in_specs.append(pl.BlockSpec((rows, LANES), functools.partial(
            lambda b, p, i, fn: (fn(b, i), p // kdiv), fn=fn)))
    args = [q] + [k] * len(segs) + [v] * len(segs)
    if sink is not None:
        in_specs.append(_full(sink.shape))
        args.append(sink)
    return pl.pallas_call(
        functools.partial(_attn_kernel, nseg=len(segs), wide=wide, has_sink=sink is not None,
                          band=band, nq=nq),
        out_shape=jax.ShapeDtypeStruct((batch * nq * tq, HEADS * HEAD_DIM), _BF16),
        grid=(batch, HEAD_PAIRS, nq),
        in_specs=in_specs,
        out_specs=pl.BlockSpec((tq, LANES), lambda b, p, i: (b * nq + i, p)),
        compiler_params=_params("arbitrary", "arbitrary", "arbitrary"),
        name="attention",
    )(*args)


def _merge_kernel(x_ref, mod_ref, oa_ref, ob_ref, ow_ref, gates_ref, woa_ref, wob_ref, wow_ref,
                  wo_ref, gffn_ref, wpq_ref, subk_ref,
                  x1_ref, h2_ref, s1_ref, s2_ref, *, d):
    gt1 = mod_ref[0, :, 2 * d:3 * d]
    sh2 = mod_ref[0, :, 3 * d:4 * d]
    sc2 = mod_ref[0, :, 4 * d:5 * d]
    m = None
    for k, (o_ref, w_ref) in enumerate(((oa_ref, woa_ref), (ob_ref, wob_ref), (ow_ref, wow_ref))):
        t = gates_ref[:, k * d:(k + 1) * d].astype(_F32) * jnp.dot(
            o_ref[...], w_ref[...], preferred_element_type=_F32)
        m = t if m is None else m + t
    y = jnp.dot(m.astype(_BF16), wo_ref[...], preferred_element_type=_F32)
    x1 = x_ref[...] + gt1 * y
    x1_ref[...] = x1
    h2 = (_rms(x1, gffn_ref[...]) * (1.0 + sc2) + sh2).astype(_BF16)
    h2_ref[...] = h2
    q = jnp.dot(h2, wpq_ref[...], preferred_element_type=_F32).astype(_BF16)
    for g in range(2 * PEER_HEADS):
        s = lax.dot_general(subk_ref[g], q[:, g * PEER_HALF:(g + 1) * PEER_HALF], _NT,
                            preferred_element_type=_F32)
        if g % 2 == 0:
            s1_ref[g // 2] = s
        else:
            s2_ref[g // 2] = s


def _merge(xs, mod_l, oa, ob, ow, gates, woa, wob, wow, wo, gffn, wpq, subk, *, n_rows, geo):
    d = xs.shape[1]
    tm = PROJ_TILE
    lat_tiles, tiles_per_batch, batch = geo
    row = lambda t: (t, 0)

    def mod_idx(t):
        return (jnp.where(t < lat_tiles, t // tiles_per_batch, batch), 0, 0)

    return pl.pallas_call(
        functools.partial(_merge_kernel, d=d),
        out_shape=[
            jax.ShapeDtypeStruct((n_rows, d), _F32),
            jax.ShapeDtypeStruct((n_rows, d), _BF16),
            jax.ShapeDtypeStruct((PEER_HEADS, PEER_KEYS, n_rows), _F32),
            jax.ShapeDtypeStruct((PEER_HEADS, PEER_KEYS, n_rows), _F32),
        ],
        grid=(n_rows // tm,),
        in_specs=[
            pl.BlockSpec((tm, d), row),
            pl.BlockSpec((1, 1, N_MOD * d), mod_idx),
            pl.BlockSpec((tm, HEADS * HEAD_DIM), row),
            pl.BlockSpec((tm, HEADS * HEAD_DIM), row),
            pl.BlockSpec((tm, HEADS * HEAD_DIM), row),
            pl.BlockSpec((tm, 3 * d), row),
            _full(woa.shape), _full(wob.shape), _full(wow.shape), _full(wo.shape),
            _full(gffn.shape), _full(wpq.shape), _full(subk.shape),
        ],
        out_specs=[
            pl.BlockSpec((tm, d), row),
            pl.BlockSpec((tm, d), row),
            pl.BlockSpec((PEER_HEADS, PEER_KEYS, tm), lambda t: (0, 0, t)),
            pl.BlockSpec((PEER_HEADS, PEER_KEYS, tm), lambda t: (0, 0, t)),
        ],
        compiler_params=_params("arbitrary"),
        name="merge",
    )(xs, mod_l, oa, ob, ow, gates, woa, wob, wow, wo, gffn, wpq, subk)


NO_RANK = 64.0


def _top_values(w, k, want_rank=False):
    vals = []
    rank = jnp.full(w.shape, NO_RANK, _F32) if want_rank else None
    for r in range(k):
        m = jnp.max(w, axis=0, keepdims=True)
        vals.append(m)
        hit = w == m
        if want_rank:
            rank = jnp.where(hit, float(r), rank)
        w = jnp.where(hit, -jnp.inf, w)
    return (vals, rank) if want_rank else vals


def _stack_rows(vals, n_rows):
    rows = lax.broadcasted_iota(jnp.int32, (n_rows, vals[0].shape[1]), 0)
    out = jnp.full(rows.shape, -jnp.inf, _F32)
    for k, v in enumerate(vals):
        out = jnp.where(rows == k, v, out)
    return out


def _topk_kernel(s1_ref, s2_ref, e1_ref, cnt_ref, e2_ref, rk_ref):
    n_top = PEER_TOPK + 1

    def head(h, carry):
        s1 = s1_ref[h]
        s2 = s2_ref[h]
        t1 = _top_values(s1, n_top)
        t2, rank2 = _top_values(s2, n_top, want_rank=True)
        t2c = _stack_rows(t2, 24)
        cand = [t1[0] + t2c]
        cand += [t1[a] + t2c[:8] for a in range(1, 8)]
        cand += [_stack_rows(t1[8:], 16) + t2[0]]
        best = _top_values(jnp.concatenate(cand, axis=0), n_top)
        z = functools.reduce(jnp.add, [jnp.exp(v - best[0]) for v in best[:PEER_TOPK]])
        tau = 0.5 * (best[PEER_TOPK - 1] + best[PEER_TOPK])
        cnt = jnp.zeros(s1.shape, _F32)
        for b in range(PEER_TOPK):
            cnt = cnt + jnp.where(s1 + t2[b] >= tau, 1.0, 0.0)
        e1_ref[h] = jnp.exp(s1 - t1[0]) / z
        cnt_ref[h] = cnt
        e2_ref[h] = pltpu.bitcast(jnp.exp(s2 - t2[0]).astype(_BF16), jnp.uint32)
        rk_ref[h] = pltpu.bitcast(rank2.astype(_BF16), jnp.uint32)
        return carry

    lax.fori_loop(0, PEER_HEADS, head, 0)


def _topk(s1t, s2t):
    n = s1t.shape[2]
    tk = TOPK_TOKENS
    blk = pl.BlockSpec((PEER_HEADS, PEER_KEYS, tk), lambda t: (0, 0, t))
    packed = pl.BlockSpec((PEER_HEADS, PEER_KEYS // 2, tk), lambda t: (0, 0, t))
    return pl.pallas_call(
        _topk_kernel,
        out_shape=[jax.ShapeDtypeStruct(s1t.shape, _F32)] * 2
        + [jax.ShapeDtypeStruct((PEER_HEADS, PEER_KEYS // 2, n), jnp.uint32)] * 2,
        grid=(n // tk,),
        in_specs=[blk, blk],
        out_specs=[blk, blk, packed, packed],
        compiler_params=_params("arbitrary"),
        name="peer_topk",
    )(s1t, s2t)


def _gate_tile(a_ref, cnt_ref, e1_ref, e2_ref, rk_ref, w_ref):
    shape = (PEER_CHUNK, LANES)
    for ii in range(PEER_ROWS):
        for lt in range(a_ref.shape[1] // LANES):
            ls = slice(lt * LANES, (lt + 1) * LANES)
            cntb = [jnp.broadcast_to(cnt_ref[h, ii:ii + 1, ls], shape).astype(_BF16)
                    for h in range(PEER_HEADS)]
            e1b = [jnp.broadcast_to(e1_ref[h, ii:ii + 1, ls], shape).astype(_BF16)
                   for h in range(PEER_HEADS)]
            for jc in range(PEER_KEYS // PEER_CHUNK):
                rows = slice(jc * PEER_CHUNK, (jc + 1) * PEER_CHUNK)
                words = slice(rows.start // 2, rows.stop // 2)
                g = None
                for h in range(PEER_HEADS):
                    e2 = pltpu.bitcast(e2_ref[h, words, ls], _BF16)
                    rk = pltpu.bitcast(rk_ref[h, words, ls], _BF16)
                    t = e1b[h] * jnp.where(rk < cntb[h], e2, jnp.zeros_like(e2))
                    g = t if g is None else g + t
                erows = slice(ii * PEER_KEYS + rows.start, ii * PEER_KEYS + rows.stop)
                a = a_ref[erows, ls]
                act = (0.5 * a) * (1.0 + lax.erf(a * (1.0 / math.sqrt(2.0))))
                w_ref[erows, ls] = g * act.astype(_BF16)


def _peer_kernel(h2_ref, x1_ref, mod_ref, u_ref, vt_ref, cnt_ref, e1_ref, e2_ref, rk_ref,
                 o_ref, acc_ref, a_ref, w_ref, *, d, n_steps):
    k = pl.program_id(1)

    @pl.when(k == 0)
    def _():
        acc_ref[...] = jnp.zeros_like(acc_ref)

    a_ref[...] = lax.dot_general(u_ref[...], h2_ref[...], _NT, preferred_element_type=_F32)
    _gate_tile(a_ref, cnt_ref, e1_ref, e2_ref, rk_ref, w_ref)
    acc_ref[...] += jnp.dot(vt_ref[...], w_ref[...], preferred_element_type=_F32)

    @pl.when(k == n_steps - 1)
    def _():
        gt2 = mod_ref[0, :, 5 * d:6 * d]
        o_ref[...] = x1_ref[...] + gt2 * acc_ref[...].T


def _peer(h2, x1, mod_l, u, vt, e1t, cntt, e2t, rkt, *, geo):
    n, d = h2.shape
    tn = PEER_TOKENS
    te = PEER_ROWS * PEER_KEYS
    n_steps = u.shape[0] // te
    lat_tiles, tiles_per_batch, batch = geo
    row = lambda t, k: (t, 0)

    def mod_idx(t, k):
        return (jnp.where(t < lat_tiles, t // tiles_per_batch, batch), 0, 0)

    rows_blk = pl.BlockSpec((PEER_HEADS, PEER_ROWS, tn), lambda t, k: (0, k, t))
    return pl.pallas_call(
        functools.partial(_peer_kernel, d=d, n_steps=n_steps),
        out_shape=jax.ShapeDtypeStruct((n, d), _F32),
        grid=(n // tn, n_steps),
        in_specs=[
            pl.BlockSpec((tn, d), row),
            pl.BlockSpec((tn, d), row),
            pl.BlockSpec((1, 1, N_MOD * d), mod_idx),
            pl.BlockSpec((te, d), lambda t, k: (k, 0)),
            pl.BlockSpec((d, te), lambda t, k: (0, k)),
            rows_blk,
            rows_blk,
            pl.BlockSpec((PEER_HEADS, PEER_KEYS // 2, tn), lambda t, k: (0, 0, t)),
            pl.BlockSpec((PEER_HEADS, PEER_KEYS // 2, tn), lambda t, k: (0, 0, t)),
        ],
        out_specs=pl.BlockSpec((tn, d), row),
        scratch_shapes=[
            pltpu.VMEM((d, tn), _F32),
            pltpu.VMEM((te, tn), _F32),
            pltpu.VMEM((te, tn), _BF16),
        ],
        compiler_params=_params("arbitrary", "arbitrary"),
        name="peer_experts",
    )(h2, x1, mod_l, u, vt, cntt, e1t, e2t, rkt)


def _final_kernel(x_ref, g_ref, o_ref):
    o_ref[...] = _rms(x_ref[...], g_ref[...])


def _final_norm(xs, g, n_rows):
    d = xs.shape[1]
    tm = PROJ_TILE
    return pl.pallas_call(
        _final_kernel,
        out_shape=jax.ShapeDtypeStruct((n_rows, d), _F32),
        grid=(n_rows // tm,),
        in_specs=[pl.BlockSpec((tm, d), lambda t: (t, 0)), _full(g.shape)],
        out_specs=pl.BlockSpec((tm, d), lambda t: (t, 0)),
        compiler_params=_params("arbitrary"),
        name="final_norm",
    )(xs, g)


def _rope_tables(seq):
    rows = seq // GRID_W
    r = jnp.repeat(jnp.arange(rows, dtype=_F32), GRID_W)
    col = jnp.tile(jnp.arange(GRID_W, dtype=_F32), rows)

    def cos_sin(rot_dim):
        n_freq = rot_dim // 4
        inv = ROPE_BASE ** (-jnp.arange(n_freq, dtype=_F32) / n_freq)
        ang = jnp.concatenate([r[:, None] * inv, col[:, None] * inv], axis=-1)
        return jnp.cos(ang), jnp.sin(ang)

    ca, sa = cos_sin(MLA_ROPE)
    one = jnp.ones((seq, MLA_NOPE), _F32)
    zero = jnp.zeros_like(one)
    z16 = jnp.zeros_like(ca)
    pad1 = jnp.ones((seq, LANES - MLA_NOPE - MLA_ROPE), _F32)
    pad0 = jnp.zeros_like(pad1)
    tab_a = [jnp.concatenate([one, ca, ca, pad1], axis=1),
             jnp.concatenate([zero, -sa, z16, pad0], axis=1),
             jnp.concatenate([zero, z16, sa, pad0], axis=1)]
    ch, sh = cos_sin(HEAD_DIM)
    z32 = jnp.zeros_like(ch)
    tab_h = [jnp.concatenate([ch, ch, ch, ch], axis=1),
             jnp.concatenate([-sh, z32, -sh, z32], axis=1),
             jnp.concatenate([z32, sh, z32, sh], axis=1)]
    lat = jnp.concatenate(tab_a + tab_h, axis=1)
    ident = jnp.concatenate([jnp.ones((PROJ_TILE, LANES), _F32),
                             jnp.zeros((PROJ_TILE, 2 * LANES), _F32)] * 2, axis=1)
    return jnp.concatenate([lat, ident], axis=0)


def _pack_w_in(w_in, d):
    splits = np.cumsum([MLA_Q_RANK, MLA_KV_RANK, MLA_ROPE, 512, 128, 128, 512, 128, 128])
    cq, ckv, kr, qb, kb, vb, qw, kw, vw, gates = jnp.split(w_in, splits, axis=-1)
    lead = w_in.shape[:-1]
    kr_pad = jnp.concatenate([jnp.zeros(lead + (MLA_NOPE,), w_in.dtype), kr,
                              jnp.zeros(lead + (LANES - MLA_NOPE - MLA_ROPE,), w_in.dtype)], axis=-1)

    def dup(w):
        g0, g1 = w[..., :HEAD_DIM], w[..., HEAD_DIM:]
        return jnp.concatenate([g0, g0, g1, g1], axis=-1)

    return jnp.concatenate([cq, ckv, kr_pad, qb, dup(kb), dup(vb), qw, dup(kw), dup(vw), gates],
                           axis=-1).astype(_BF16)


def _pack_w_uq(w_uq):
    depth, rank, _ = w_uq.shape
    w = w_uq.reshape(depth, rank, HEADS, MLA_NOPE + MLA_ROPE)
    w = jnp.pad(w, ((0, 0), (0, 0), (0, 0), (0, LANES - MLA_NOPE - MLA_ROPE)))
    return w.reshape(depth, rank, HEADS * LANES).astype(_BF16)


def _pack_w_ukv(w_ukv):
    depth, rank, _ = w_ukv.shape
    w = w_ukv.reshape(depth, rank, HEADS, MLA_NOPE + HEAD_DIM)
    k = jnp.pad(w[..., :MLA_NOPE], ((0, 0), (0, 0), (0, 0), (0, LANES - MLA_NOPE)))
    v = w[..., MLA_NOPE:]
    return jnp.concatenate([k.reshape(depth, rank, HEADS * LANES),
                            v.reshape(depth, rank, HEADS * HEAD_DIM)], axis=-1).astype(_BF16)


def kernel(x, c, ctx, c_ctx, w_mod, b_mod, g_attn, g_ffn, w_in, g_cq, g_ckv, w_uq, w_ukv,
           g_qn, g_kn, sink, w_oa, w_ob, w_ow, w_o, w_pq, sub_keys, peer_u, peer_v, g_final):
    batch, seq, d = x.shape
    n_ctx = ctx.shape[1]
    depth = w_mod.shape[0]
    assert n_ctx == ROW_TILE and seq % PEER_TOKENS == 0 and (batch * n_ctx) % PEER_TOKENS == 0
    assert batch < 8 and seq % GRID_W == 0 and seq >= 2 * Q_TILE
    assert seq % FLASH_Q_TILE == 0 and seq % FLASH_K_CHUNK == 0
    n_lat = batch * seq
    n_all = n_lat + batch * n_ctx
    nq = seq // Q_TILE
    ctx_blk0 = n_lat // ROW_TILE

    xs = jnp.concatenate([x.reshape(n_lat, d), ctx.reshape(batch * n_ctx, d)], axis=0)
    cc = jnp.concatenate([c, c_ctx[None], jnp.zeros((8 - batch - 1, d), _F32)], axis=0)
    mod = _modulation(cc, w_mod, b_mod).reshape(depth, 8, 1, N_MOD * d)

    rope = _rope_tables(seq)
    eye = np.kron(np.eye(2, dtype=np.float32), np.ones((HEAD_DIM, HEAD_DIM), np.float32))
    bd = jnp.asarray(eye, _BF16)
    win = _pack_w_in(w_in, d)
    wuq = _pack_w_uq(w_uq)
    wukv = _pack_w_ukv(w_ukv)
    gqn_t = jnp.tile(g_qn, (1, HEADS))[:, None, :]
    gkn_t = jnp.tile(g_kn, (1, 2 * LANES // HEAD_DIM))[:, None, :]
    sink_t = jnp.broadcast_to(sink[:, :, None], (depth, HEADS, LANES))
    woa, wob, wow, wo, wpq = (w.astype(_BF16) for w in (w_oa, w_ob, w_ow, w_o, w_pq))
    subk = sub_keys.reshape(depth, 2 * PEER_HEADS, PEER_KEYS, PEER_HALF).astype(_BF16)
    u_bf = peer_u.astype(_BF16)
    vt_bf = jnp.swapaxes(peer_v, 1, 2).astype(_BF16)

    geo_row = (n_lat // PROJ_TILE, seq // PROJ_TILE, batch)
    geo_peer = (n_lat // PEER_TOKENS, seq // PEER_TOKENS, batch)
    lat_seg = (seq, lambda b, i: b)
    ctx_seg = (n_ctx, lambda b, i: ctx_blk0 + b)
    half = Q_TILE // 2
    band_segs = [
        (half, lambda b, i: b * 2 * nq + jnp.maximum(2 * i - 1, 0)),
        (Q_TILE, lambda b, i: b * nq + i),
        (half, lambda b, i: b * 2 * nq + jnp.minimum(2 * i + 2, 2 * nq - 1)),
        ctx_seg,
    ]

    for l in range(depth):
        last = l == depth - 1
        qa, ka, va, qb, kb, vb, qw, kw, vw, gates = _inproj(
            xs, mod[l], g_attn[l][None], win[l], g_cq[l][None], g_ckv[l][None], wuq[l], wukv[l],
            gqn_t[l], gkn_t[l], rope, bd, geo=geo_row)
        lat = dict(nq=nq, q_block0=0, batch=batch)
        dense = dict(batch=batch, seq=seq, n_ctx=n_ctx)
        o_a = _flash_attention(qa, ka, va, wide=True, **dense)
        o_b = _flash_attention(qb, kb, vb, wide=False, **dense)
        o_w = _attention(qw, kw, vw, sink_t[l], wide=False, segs=band_segs, band=True, **lat)
        n_rows = n_lat
        if not last:
            cq = dict(nq=1, q_block0=ctx_blk0, batch=batch, segs=[ctx_seg])
            o_a = jnp.concatenate([o_a, _attention(qa, ka, va, None, wide=True, **cq)], axis=0)
            o_b = jnp.concatenate([o_b, _attention(qb, kb, vb, None, wide=False, **cq)], axis=0)
            o_w = jnp.concatenate([o_w, _attention(qw, kw, vw, sink_t[l], wide=False, **cq)], axis=0)
            n_rows = n_all
        x1, h2, s1t, s2t = _merge(xs, mod[l], o_a, o_b, o_w, gates, woa[l], wob[l], wow[l], wo[l],
                                  g_ffn[l][None], wpq[l], subk[l], n_rows=n_rows, geo=geo_row)
        e1t, cntt, e2t, rkt = _topk(s1t, s2t)
        xs = _peer(h2, x1, mod[l], u_bf[l], vt_bf[l], e1t, cntt, e2t, rkt, geo=geo_peer)
    out = _final_norm(xs, g_final[None], n_lat)
    return out.reshape(batch, seq, d)
```

```python
import functools
import math

import jax
import jax.numpy as jnp
import numpy as np
from jax import lax
from jax.experimental import pallas as pl
from jax.experimental.pallas import tpu as pltpu

GRID_W = 64
ROPE_BASE = 10000.0
NORM_EPS = 1e-6
NEG_INF = -1e30
LOG2E = math.log2(math.e)
WINDOW = 128
N_MOD = 6

HEADS = 8
HEAD_PAIRS = HEADS // 2
MLA_Q_RANK = 384
MLA_KV_RANK = 256
MLA_NOPE = 64
MLA_ROPE = 32
HEAD_DIM = 64
KV_HEADS = 2
LANES = 128

PEER_HEADS = 8
PEER_KEYS = 128
PEER_HALF = 128
PEER_TOPK = 16

ROW_TILE = 256
PROJ_TILE = 512
WIN_Q_TILE = 256
FLASH_Q_TILE = 1024
FLASH_K_CHUNK = 1024
PEER_TOKENS = 512
PEER_ROWS = 16
PEER_CHUNK = 32
TOPK_TOKENS = 256
VMEM_LIMIT = 56 * 1024 * 1024

OFF_CQ = 0
OFF_CKV = OFF_CQ + MLA_Q_RANK
OFF_KR = OFF_CKV + MLA_KV_RANK
OFF_QB = OFF_KR + LANES
OFF_KB = OFF_QB + HEADS * HEAD_DIM
OFF_VB = OFF_KB + 2 * LANES
OFF_QW = OFF_VB + 2 * LANES
OFF_KW = OFF_QW + HEADS * HEAD_DIM
OFF_VW = OFF_KW + 2 * LANES
OFF_GATES = OFF_VW + 2 * LANES

_NT = (((1,), (1,)), ((), ()))
_F32 = jnp.float32
_BF16 = jnp.bfloat16


def _params(*semantics):
    return pltpu.CompilerParams(dimension_semantics=semantics, vmem_limit_bytes=VMEM_LIMIT)


def _full(shape):
    return pl.BlockSpec(shape, lambda *_: (0,) * len(shape), pipeline_mode=pl.Buffered(1))


def _rms(x, g):
    return x * lax.rsqrt(jnp.mean(x * x, axis=-1, keepdims=True) + NORM_EPS) * g


def _rope(x, tabs, shift):
    cos, sin_lo, sin_hi = tabs
    outs = []
    for c in range(x.shape[1] // LANES):
        xc = x[:, c * LANES:(c + 1) * LANES]
        up = pltpu.roll(xc, LANES - shift, 1)
        dn = pltpu.roll(xc, shift, 1)
        outs.append(xc * cos + up * sin_lo + dn * sin_hi)
    return outs[0] if len(outs) == 1 else jnp.concatenate(outs, axis=1)


def _head_rms(x, bd, g):
    outs = []
    for c in range(x.shape[1] // LANES):
        xc = x[:, c * LANES:(c + 1) * LANES]
        sq = xc * xc
        hi = sq.astype(_BF16)
        lo = (sq - hi.astype(_F32)).astype(_BF16)
        ssq = (jnp.dot(hi, bd, preferred_element_type=_F32)
               + jnp.dot(lo, bd, preferred_element_type=_F32))
        outs.append(xc * lax.rsqrt(ssq * (1.0 / HEAD_DIM) + NORM_EPS))
    y = outs[0] if len(outs) == 1 else jnp.concatenate(outs, axis=1)
    return y * g


def _mod_kernel(c_ref, w_ref, b_ref, o_ref):
    c = c_ref[...]
    a = c / (1.0 + jnp.exp(-c))
    o_ref[0] = jnp.dot(a, w_ref[0], preferred_element_type=_F32,
                       precision=lax.Precision.HIGHEST) + b_ref[0]


def _modulation(cc, w_mod, b_mod):
    depth, d, cols = w_mod.shape
    tn = cols // 4
    return pl.pallas_call(
        _mod_kernel,
        out_shape=jax.ShapeDtypeStruct((depth, 8, cols), _F32),
        grid=(depth, cols // tn),
        in_specs=[
            pl.BlockSpec((8, d), lambda l, j: (0, 0)),
            pl.BlockSpec((1, d, tn), lambda l, j: (l, 0, j)),
            pl.BlockSpec((1, 1, tn), lambda l, j: (l, 0, j)),
        ],
        out_specs=pl.BlockSpec((1, 8, tn), lambda l, j: (l, 0, j)),
        compiler_params=_params("arbitrary", "arbitrary"),
        name="modulation",
    )(cc, w_mod, b_mod.reshape(depth, 1, cols))


def _inproj_kernel(x_ref, mod_ref, gattn_ref, win_ref, gcq_ref, gckv_ref, wuq_ref, wukv_ref,
                   gqn_ref, gkn_ref, rope_ref, bd_ref,
                   qa_ref, ka_ref, va_ref, qb_ref, kb_ref, vb_ref, qw_ref, kw_ref, vw_ref,
                   gates_ref, *, d, scale_a, scale_h):
    x = x_ref[...]
    sh1 = mod_ref[0, :, 0:d]
    sc1 = mod_ref[0, :, d:2 * d]
    h = (_rms(x, gattn_ref[...]) * (1.0 + sc1) + sh1).astype(_BF16)

    def proj(off, width):
        return jnp.dot(h, win_ref[:, off:off + width], preferred_element_type=_F32)

    rope_a = tuple(rope_ref[:, i * LANES:(i + 1) * LANES] for i in range(3))
    rope_h = tuple(rope_ref[:, i * LANES:(i + 1) * LANES] for i in range(3, 6))
    bd = bd_ref[...]

    cq = _rms(proj(OFF_CQ, MLA_Q_RANK), gcq_ref[...]).astype(_BF16)
    qa = jnp.dot(cq, wuq_ref[...], preferred_element_type=_F32)
    qa_ref[...] = (_rope(qa, rope_a, MLA_ROPE // 2) * (scale_a * LOG2E)).astype(_BF16)
    ckv = _rms(proj(OFF_CKV, MLA_KV_RANK), gckv_ref[...]).astype(_BF16)
    kva = jnp.dot(ckv, wukv_ref[...], preferred_element_type=_F32)
    kr = _rope(proj(OFF_KR, LANES), rope_a, MLA_ROPE // 2)
    ka = kva[:, :HEADS * LANES] + jnp.concatenate([kr] * HEADS, axis=1)
    ka_ref[...] = ka.astype(_BF16)
    va_ref[...] = kva[:, HEADS * LANES:].astype(_BF16)

    qb = _head_rms(proj(OFF_QB, HEADS * HEAD_DIM), bd, gqn_ref[...])
    qb_ref[...] = (_rope(qb, rope_h, HEAD_DIM // 2) * (scale_h * LOG2E)).astype(_BF16)
    kb = _head_rms(proj(OFF_KB, 2 * LANES), bd, gkn_ref[...])
    kb_ref[...] = _rope(kb, rope_h, HEAD_DIM // 2).astype(_BF16)
    vb_ref[...] = proj(OFF_VB, 2 * LANES).astype(_BF16)

    qw_ref[...] = (_rope(proj(OFF_QW, HEADS * HEAD_DIM), rope_h, HEAD_DIM // 2) * scale_h).astype(_BF16)
    kw_ref[...] = _rope(proj(OFF_KW, 2 * LANES), rope_h, HEAD_DIM // 2).astype(_BF16)
    vw_ref[...] = proj(OFF_VW, 2 * LANES).astype(_BF16)

    for k in range(3):
        g = proj(OFF_GATES + k * d, d)
        gates_ref[:, k * d:(k + 1) * d] = (1.0 / (1.0 + jnp.exp(-g))).astype(_BF16)


def _inproj(xs, mod_l, gattn, win, gcq, gckv, wuq, wukv, gqn_t, gkn_t, rope, bd, *, geo):
    n, d = xs.shape
    tm = PROJ_TILE
    lat_tiles, tiles_per_batch, batch = geo
    row = lambda t: (t, 0)

    def mod_idx(t):
        return (jnp.where(t < lat_tiles, t // tiles_per_batch, batch), 0, 0)

    def rope_idx(t):
        return (jnp.where(t < lat_tiles, t % tiles_per_batch, tiles_per_batch), 0)

    widths = (HEADS * LANES, HEADS * LANES, HEADS * HEAD_DIM, HEADS * HEAD_DIM, 2 * LANES, 2 * LANES,
              HEADS * HEAD_DIM, 2 * LANES, 2 * LANES, 3 * d)
    return pl.pallas_call(
        functools.partial(_inproj_kernel, d=d, scale_a=(MLA_NOPE + MLA_ROPE) ** -0.5,
                          scale_h=HEAD_DIM ** -0.5),
        out_shape=[jax.ShapeDtypeStruct((n, w), _BF16) for w in widths],
        grid=(n // tm,),
        in_specs=[
            pl.BlockSpec((tm, d), row),
            pl.BlockSpec((1, 1, N_MOD * d), mod_idx),
            _full(gattn.shape), _full(win.shape), _full(gcq.shape), _full(gckv.shape),
            _full(wuq.shape), _full(wukv.shape), _full(gqn_t.shape), _full(gkn_t.shape),
            pl.BlockSpec((tm, 6 * LANES), rope_idx),
            _full(bd.shape),
        ],
        out_specs=[pl.BlockSpec((tm, w), row) for w in widths],
        compiler_params=_params("arbitrary"),
        name="inproj",
    )(xs, mod_l, gattn, win, gcq, gckv, wuq, wukv, gqn_t, gkn_t, rope, bd)


def _attn_kernel(*refs, nseg, wide, has_sink, band, nq, base2):
    q_ref = refs[0]
    k_refs = refs[1:1 + nseg]
    v_refs = refs[1 + nseg:1 + 2 * nseg]
    sink_ref = refs[1 + 2 * nseg] if has_sink else None
    o_ref = refs[-1]
    pair = pl.program_id(1)
    qi = pl.program_id(2)
    tq = q_ref.shape[0]
    lane = lax.broadcasted_iota(jnp.int32, (tq, LANES), 1)

    masks = [None] * nseg
    if band:
        r = lax.broadcasted_iota(jnp.int32, (tq, WINDOW), 0)
        c = lax.broadcasted_iota(jnp.int32, (tq, WINDOW), 1)
        masks[0] = c >= r + jnp.where(qi > 0, 0, WINDOW)
        masks[2] = c <= r - (tq - WINDOW) - jnp.where(qi < nq - 1, 0, tq)
        r = lax.broadcasted_iota(jnp.int32, (tq, tq), 0)
        c = lax.broadcasted_iota(jnp.int32, (tq, tq), 1)
        masks[1] = jnp.abs(r - c) <= WINDOW

    for pp in range(1 if wide else HEAD_PAIRS):
        qcols = slice(pp * LANES, (pp + 1) * LANES)
        kvcols = slice((pp // 2) * LANES, (pp // 2 + 1) * LANES)
        outs = []
        for hh in range(2):
            if wide:
                q = q_ref[:, hh * LANES:(hh + 1) * LANES]
                ks = [k[:, hh * LANES:(hh + 1) * LANES] for k in k_refs]
                vs = [v[...] for v in v_refs]
                head = 2 * pair + hh
            else:
                own = (lane[:1] < HEAD_DIM) if hh == 0 else (lane[:1] >= HEAD_DIM)
                q = q_ref[:, qcols] * own.astype(_F32).astype(_BF16)
                ks = [k[:, kvcols] for k in k_refs]
                vs = [v[:, kvcols] for v in v_refs]
                head = 2 * pp + hh
            ss = [lax.dot_general(q, k, _NT, preferred_element_type=_F32) for k in ks]
            ss = [s if m is None else jnp.where(m, s, NEG_INF) for s, m in zip(ss, masks)]
            m = functools.reduce(jnp.maximum, [jnp.max(s, axis=-1, keepdims=True) for s in ss])
            if has_sink:
                sink = sink_ref[pl.ds(head, 1), 0:1]
                m = jnp.maximum(m, sink)
            ps = [(jnp.exp2 if base2 else jnp.exp)(s - m) for s in ss]
            l = functools.reduce(jnp.add, [jnp.sum(p, axis=-1, keepdims=True) for p in ps])
            if has_sink:
                l = l + jnp.exp(sink - m)
            o = functools.reduce(jnp.add, [jnp.dot(p.astype(_BF16), v, preferred_element_type=_F32)
                                           for p, v in zip(ps, vs)])
            outs.append(o / l)
        o_ref[:, qcols] = jnp.where(lane < HEAD_DIM, outs[0], outs[1]).astype(o_ref.dtype)


def _flash_kernel(q_ref, k_ref, kc_ref, v_ref, vc_ref, o_ref, *, wide, chunk):
    tq = q_ref.shape[0]
    lane = lax.broadcasted_iota(jnp.int32, (1, LANES), 1)
    qs, kcols = [], []
    for hh in range(2):
        if wide:
            qs.append(q_ref[:, hh * LANES:(hh + 1) * LANES])
            kcols.append(slice(hh * LANES, (hh + 1) * LANES))
        else:
            own = (lane < HEAD_DIM) if hh == 0 else (lane >= HEAD_DIM)
            qs.append(q_ref[...] * own.astype(_F32).astype(_BF16))
            kcols.append(slice(0, LANES))
    m = [jnp.full((tq, 1), -jnp.inf, _F32) for _ in range(2)]
    l = [jnp.zeros((tq, 1), _F32) for _ in range(2)]
    acc = [jnp.zeros((tq, LANES), _F32) for _ in range(2)]
    pieces = [(k_ref, v_ref, r0, chunk) for r0 in range(0, k_ref.shape[0], chunk)]
    pieces.append((kc_ref, vc_ref, 0, kc_ref.shape[0]))
    for kr, vr, r0, rows in pieces:
        for hh in range(2):
            s = lax.dot_general(qs[hh], kr[r0:r0 + rows, kcols[hh]], _NT,
                                preferred_element_type=_F32)
            m_new = jnp.maximum(m[hh], jnp.max(s, axis=-1, keepdims=True))
            alpha = jnp.exp2(m[hh] - m_new)
            p = jnp.exp2(s - m_new)
            l[hh] = alpha * l[hh] + jnp.sum(p, axis=-1, keepdims=True)
            acc[hh] = alpha * acc[hh] + jnp.dot(p.astype(_BF16), vr[r0:r0 + rows, :],
                                                preferred_element_type=_F32)
            m[hh] = m_new
    o_ref[...] = jnp.where(lane < HEAD_DIM, acc[0] / l[0], acc[1] / l[1]).astype(o_ref.dtype)


def _flash_attention(q, k, v, *, wide, batch, seq, n_ctx):
    tq = FLASH_Q_TILE
    nq = seq // tq
    qw = 2 * LANES if wide else LANES
    kdiv = 1 if wide else 2
    ctx_blk0 = batch * seq // n_ctx
    return pl.pallas_call(
        functools.partial(_flash_kernel, wide=wide, chunk=FLASH_K_CHUNK),
        out_shape=jax.ShapeDtypeStruct((batch * seq, HEADS * HEAD_DIM), _BF16),
        grid=(batch, HEAD_PAIRS, nq),
        in_specs=[
            pl.BlockSpec((tq, qw), lambda b, p, i: (b * nq + i, p)),
            pl.BlockSpec((seq, qw), lambda b, p, i: (b, p // kdiv)),
            pl.BlockSpec((n_ctx, qw), lambda b, p, i: (ctx_blk0 + b, p // kdiv)),
            pl.BlockSpec((seq, LANES), lambda b, p, i: (b, p // kdiv)),
            pl.BlockSpec((n_ctx, LANES), lambda b, p, i: (ctx_blk0 + b, p // kdiv)),
        ],
        out_specs=pl.BlockSpec((tq, LANES), lambda b, p, i: (b * nq + i, p)),
        compiler_params=_params("arbitrary", "arbitrary", "arbitrary"),
        name="flash_attention",
    )(q, k, k, v, v)


def _attention(q, k, v, sink, *, wide, segs, nq, q_block0, batch, tq, band=False, base2=False):
    pairs = HEAD_PAIRS if wide else 1
    qw, kw, vw, ow = ((2 * LANES, 2 * LANES, LANES, LANES) if wide
                      else (HEADS * HEAD_DIM, KV_HEADS * LANES, KV_HEADS * LANES, HEADS * HEAD_DIM))
    in_specs = [pl.BlockSpec((tq, qw), lambda b, p, i: (q_block0 + b * nq + i, p))]
    for width in (kw, vw):
        for rows, fn in segs:
            in_specs.append(pl.BlockSpec((rows, width), functools.partial(
                lambda b, p, i, fn: (fn(b, i), p), fn=fn)))
    args = [q] + [k] * len(segs) + [v] * len(segs)
    if sink is not None:
        in_specs.append(_full(sink.shape))
        args.append(sink)
    return pl.pallas_call(
        functools.partial(_attn_kernel, nseg=len(segs), wide=wide, has_sink=sink is not None,
                          band=band, nq=nq, base2=base2),
        out_shape=jax.ShapeDtypeStruct((batch * nq * tq, HEADS * HEAD_DIM), _BF16),
        grid=(batch, pairs, nq),
        in_specs=in_specs,
        out_specs=pl.BlockSpec((tq, ow), lambda b, p, i: (b * nq + i, p)),
        compiler_params=_params("arbitrary", "arbitrary", "arbitrary"),
        name="attention",
    )(*args)


def _merge_kernel(x_ref, mod_ref, oa_ref, ob_ref, ow_ref, gates_ref, woa_ref, wob_ref, wow_ref,
                  wo_ref, gffn_ref, wpq_ref, subk_ref,
                  x1_ref, h2_ref, s1_ref, s2_ref, *, d):
    gt1 = mod_ref[0, :, 2 * d:3 * d]
    sh2 = mod_ref[0, :, 3 * d:4 * d]
    sc2 = mod_ref[0, :, 4 * d:5 * d]
    m = None
    for k, (o_ref, w_ref) in enumerate(((oa_ref, woa_ref), (ob_ref, wob_ref), (ow_ref, wow_ref))):
        t = gates_ref[:, k * d:(k + 1) * d].astype(_F32) * jnp.dot(
            o_ref[...], w_ref[...], preferred_element_type=_F32)
        m = t if m is None else m + t
    y = jnp.dot(m.astype(_BF16), wo_ref[...], preferred_element_type=_F32)
    x1 = x_ref[...] + gt1 * y
    x1_ref[...] = x1
    h2 = (_rms(x1, gffn_ref[...]) * (1.0 + sc2) + sh2).astype(_BF16)
    h2_ref[...] = h2
    q = jnp.dot(h2, wpq_ref[...], preferred_element_type=_F32).astype(_BF16)
    for g in range(2 * PEER_HEADS):
        s = lax.dot_general(subk_ref[g], q[:, g * PEER_HALF:(g + 1) * PEER_HALF], _NT,
                            preferred_element_type=_F32)
        if g % 2 == 0:
            s1_ref[g // 2] = s
        else:
            s2_ref[g // 2] = s


def _merge(xs, mod_l, oa, ob, ow, gates, woa, wob, wow, wo, gffn, wpq, subk, *, n_rows, geo):
    d = xs.shape[1]
    tm = PROJ_TILE
    lat_tiles, tiles_per_batch, batch = geo
    row = lambda t: (t, 0)

    def mod_idx(t):
        return (jnp.where(t < lat_tiles, t // tiles_per_batch, batch), 0, 0)

    return pl.pallas_call(
        functools.partial(_merge_kernel, d=d),
        out_shape=[
            jax.ShapeDtypeStruct((n_rows, d), _F32),
            jax.ShapeDtypeStruct((n_rows, d), _BF16),
            jax.ShapeDtypeStruct((PEER_HEADS, PEER_KEYS, n_rows), _F32),
            jax.ShapeDtypeStruct((PEER_HEADS, PEER_KEYS, n_rows), _F32),
        ],
        grid=(n_rows // tm,),
        in_specs=[
            pl.BlockSpec((tm, d), row),
            pl.BlockSpec((1, 1, N_MOD * d), mod_idx),
            pl.BlockSpec((tm, HEADS * HEAD_DIM), row),
            pl.BlockSpec((tm, HEADS * HEAD_DIM), row),
            pl.BlockSpec((tm, HEADS * HEAD_DIM), row),
            pl.BlockSpec((tm, 3 * d), row),
            _full(woa.shape), _full(wob.shape), _full(wow.shape), _full(wo.shape),
            _full(gffn.shape), _full(wpq.shape), _full(subk.shape),
        ],
        out_specs=[
            pl.BlockSpec((tm, d), row),
            pl.BlockSpec((tm, d), row),
            pl.BlockSpec((PEER_HEADS, PEER_KEYS, tm), lambda t: (0, 0, t)),
            pl.BlockSpec((PEER_HEADS, PEER_KEYS, tm), lambda t: (0, 0, t)),
        ],
        compiler_params=_params("arbitrary"),
        name="merge",
    )(xs, mod_l, oa, ob, ow, gates, woa, wob, wow, wo, gffn, wpq, subk)


NO_RANK = 64.0


def _top_values(w, k, want_rank=False):
    vals = []
    rank = jnp.full(w.shape, NO_RANK, _F32) if want_rank else None
    for r in range(k):
        m = jnp.max(w, axis=0, keepdims=True)
        vals.append(m)
        hit = w == m
        if want_rank:
            rank = jnp.where(hit, float(r), rank)
        w = jnp.where(hit, -jnp.inf, w)
    return (vals, rank) if want_rank else vals


def _stack_rows(vals, n_rows):
    rows = lax.broadcasted_iota(jnp.int32, (n_rows, vals[0].shape[1]), 0)
    out = jnp.full(rows.shape, -jnp.inf, _F32)
    for k, v in enumerate(vals):
        out = jnp.where(rows == k, v, out)
    return out


def _topk_kernel(s1_ref, s2_ref, e1_ref, cnt_ref, e2_ref, rk_ref):
    n_top = PEER_TOPK + 1

    def head(h, carry):
        s1 = s1_ref[h]
        s2 = s2_ref[h]
        t1 = _top_values(s1, n_top)
        t2, rank2 = _top_values(s2, n_top, want_rank=True)
        t2c = _stack_rows(t2, 24)
        cand = [t1[0] + t2c]
        cand += [t1[a] + t2c[:8] for a in range(1, 8)]
        cand += [_stack_rows(t1[8:], 16) + t2[0]]
        best = _top_values(jnp.concatenate(cand, axis=0), n_top)
        z = functools.reduce(jnp.add, [jnp.exp(v - best[0]) for v in best[:PEER_TOPK]])
        tau = 0.5 * (best[PEER_TOPK - 1] + best[PEER_TOPK])
        cnt = jnp.zeros(s1.shape, _F32)
        for b in range(PEER_TOPK):
            cnt = cnt + jnp.where(s1 + t2[b] >= tau, 1.0, 0.0)
        e1_ref[h] = jnp.exp(s1 - t1[0]) / z
        cnt_ref[h] = cnt
        e2_ref[h] = pltpu.bitcast(jnp.exp(s2 - t2[0]).astype(_BF16), jnp.uint32)
        rk_ref[h] = pltpu.bitcast(rank2.astype(_BF16), jnp.uint32)
        return carry

    lax.fori_loop(0, PEER_HEADS, head, 0)


def _topk(s1t, s2t):
    n = s1t.shape[2]
    tk = TOPK_TOKENS
    blk = pl.BlockSpec((PEER_HEADS, PEER_KEYS, tk), lambda t: (0, 0, t))
    packed = pl.BlockSpec((PEER_HEADS, PEER_KEYS // 2, tk), lambda t: (0, 0, t))
    return pl.pallas_call(
        _topk_kernel,
        out_shape=[jax.ShapeDtypeStruct(s1t.shape, _F32)] * 2
        + [jax.ShapeDtypeStruct((PEER_HEADS, PEER_KEYS // 2, n), jnp.uint32)] * 2,
        grid=(n // tk,),
        in_specs=[blk, blk],
        out_specs=[blk, blk, packed, packed],
        compiler_params=_params("arbitrary"),
        name="peer_topk",
    )(s1t, s2t)


def _gate_tile(a_ref, cnt_ref, e1_ref, e2_ref, rk_ref, w_ref):
    shape = (PEER_CHUNK, LANES)
    for ii in range(PEER_ROWS):
        for lt in range(a_ref.shape[1] // LANES):
            ls = slice(lt * LANES, (lt + 1) * LANES)
            cntb = [jnp.broadcast_to(cnt_ref[h, ii:ii + 1, ls], shape).astype(_BF16)
                    for h in range(PEER_HEADS)]
            e1b = [jnp.broadcast_to(e1_ref[h, ii:ii + 1, ls], shape).astype(_BF16)
                   for h in range(PEER_HEADS)]
            for jc in range(PEER_KEYS // PEER_CHUNK):
                rows = slice(jc * PEER_CHUNK, (jc + 1) * PEER_CHUNK)
                words = slice(rows.start // 2, rows.stop // 2)
                g = None
                for h in range(PEER_HEADS):
                    e2 = pltpu.bitcast(e2_ref[h, words, ls], _BF16)
                    rk = pltpu.bitcast(rk_ref[h, words, ls], _BF16)
                    t = e1b[h] * jnp.where(rk < cntb[h], e2, jnp.zeros_like(e2))
                    g = t if g is None else g + t
                erows = slice(ii * PEER_KEYS + rows.start, ii * PEER_KEYS + rows.stop)
                a = a_ref[erows, ls]
                act = (0.5 * a) * (1.0 + lax.erf(a * (1.0 / math.sqrt(2.0))))
                w_ref[erows, ls] = g * act.astype(_BF16)


def _peer_kernel(h2_ref, x1_ref, mod_ref, u_ref, vt_ref, cnt_ref, e1_ref, e2_ref, rk_ref,
                 o_ref, acc_ref, a_ref, w_ref, *, d, n_steps):
    k = pl.program_id(1)

    @pl.when(k == 0)
    def _():
        acc_ref[...] = jnp.zeros_like(acc_ref)

    a_ref[...] = lax.dot_general(u_ref[...], h2_ref[...], _NT, preferred_element_type=_F32)
    _gate_tile(a_ref, cnt_ref, e1_ref, e2_ref, rk_ref, w_ref)
    acc_ref[...] += jnp.dot(vt_ref[...], w_ref[...], preferred_element_type=_F32)

    @pl.when(k == n_steps - 1)
    def _():
        gt2 = mod_ref[0, :, 5 * d:6 * d]
        o_ref[...] = x1_ref[...] + gt2 * acc_ref[...].T


def _peer(h2, x1, mod_l, u, vt, e1t, cntt, e2t, rkt, *, geo):
    n, d = h2.shape
    tn = PEER_TOKENS
    te = PEER_ROWS * PEER_KEYS
    n_steps = u.shape[0] // te
    lat_tiles, tiles_per_batch, batch = geo
    row = lambda t, k: (t, 0)

    def mod_idx(t, k):
        return (jnp.where(t < lat_tiles, t // tiles_per_batch, batch), 0, 0)

    rows_blk = pl.BlockSpec((PEER_HEADS, PEER_ROWS, tn), lambda t, k: (0, k, t))
    return pl.pallas_call(
        functools.partial(_peer_kernel, d=d, n_steps=n_steps),
        out_shape=jax.ShapeDtypeStruct((n, d), _F32),
        grid=(n // tn, n_steps),
        in_specs=[
            pl.BlockSpec((tn, d), row),
            pl.BlockSpec((tn, d), row),
            pl.BlockSpec((1, 1, N_MOD * d), mod_idx),
            pl.BlockSpec((te, d), lambda t, k: (k, 0)),
            pl.BlockSpec((d, te), lambda t, k: (0, k)),
            rows_blk,
            rows_blk,
            pl.BlockSpec((PEER_HEADS, PEER_KEYS // 2, tn), lambda t, k: (0, 0, t)),
            pl.BlockSpec((PEER_HEADS, PEER_KEYS // 2, tn), lambda t, k: (0, 0, t)),
        ],
        out_specs=pl.BlockSpec((tn, d), row),
        scratch_shapes=[
            pltpu.VMEM((d, tn), _F32),
            pltpu.VMEM((te, tn), _F32),
            pltpu.VMEM((te, tn), _BF16),
        ],
        compiler_params=_params("arbitrary", "arbitrary"),
        name="peer_experts",
    )(h2, x1, mod_l, u, vt, cntt, e1t, e2t, rkt)


def _final_kernel(x_ref, g_ref, o_ref):
    o_ref[...] = _rms(x_ref[...], g_ref[...])


def _final_norm(xs, g, n_rows):
    d = xs.shape[1]
    tm = PROJ_TILE
    return pl.pallas_call(
        _final_kernel,
        out_shape=jax.ShapeDtypeStruct((n_rows, d), _F32),
        grid=(n_rows // tm,),
        in_specs=[pl.BlockSpec((tm, d), lambda t: (t, 0)), _full(g.shape)],
        out_specs=pl.BlockSpec((tm, d), lambda t: (t, 0)),
        compiler_params=_params("arbitrary"),
        name="final_norm",
    )(xs, g)


def _rope_tables(seq):
    rows = seq // GRID_W
    r = jnp.repeat(jnp.arange(rows, dtype=_F32), GRID_W)
    col = jnp.tile(jnp.arange(GRID_W, dtype=_F32), rows)

    def cos_sin(rot_dim):
        n_freq = rot_dim // 4
        inv = ROPE_BASE ** (-jnp.arange(n_freq, dtype=_F32) / n_freq)
        ang = jnp.concatenate([r[:, None] * inv, col[:, None] * inv], axis=-1)
        return jnp.cos(ang), jnp.sin(ang)

    ca, sa = cos_sin(MLA_ROPE)
    one = jnp.ones((seq, MLA_NOPE), _F32)
    zero = jnp.zeros_like(one)
    z16 = jnp.zeros_like(ca)
    pad1 = jnp.ones((seq, LANES - MLA_NOPE - MLA_ROPE), _F32)
    pad0 = jnp.zeros_like(pad1)
    tab_a = [jnp.concatenate([one, ca, ca, pad1], axis=1),
             jnp.concatenate([zero, -sa, z16, pad0], axis=1),
             jnp.concatenate([zero, z16, sa, pad0], axis=1)]
    ch, sh = cos_sin(HEAD_DIM)
    z32 = jnp.zeros_like(ch)
    tab_h = [jnp.concatenate([ch, ch, ch, ch], axis=1),
             jnp.concatenate([-sh, z32, -sh, z32], axis=1),
             jnp.concatenate([z32, sh, z32, sh], axis=1)]
    lat = jnp.concatenate(tab_a + tab_h, axis=1)
    ident = jnp.concatenate([jnp.ones((PROJ_TILE, LANES), _F32),
                             jnp.zeros((PROJ_TILE, 2 * LANES), _F32)] * 2, axis=1)
    return jnp.concatenate([lat, ident], axis=0)


def _pack_w_in(w_in, d):
    splits = np.cumsum([MLA_Q_RANK, MLA_KV_RANK, MLA_ROPE, 512, 128, 128, 512, 128, 128])
    cq, ckv, kr, qb, kb, vb, qw, kw, vw, gates = jnp.split(w_in, splits, axis=-1)
    lead = w_in.shape[:-1]
    kr_pad = jnp.concatenate([jnp.zeros(lead + (MLA_NOPE,), w_in.dtype), kr,
                              jnp.zeros(lead + (LANES - MLA_NOPE - MLA_ROPE,), w_in.dtype)], axis=-1)

    def dup(w):
        g0, g1 = w[..., :HEAD_DIM], w[..., HEAD_DIM:]
        return jnp.concatenate([g0, g0, g1, g1], axis=-1)

    return jnp.concatenate([cq, ckv, kr_pad, qb, dup(kb), dup(vb), qw, dup(kw), dup(vw), gates],
                           axis=-1).astype(_BF16)


def _pack_w_uq(w_uq):
    depth, rank, _ = w_uq.shape
    w = w_uq.reshape(depth, rank, HEADS, MLA_NOPE + MLA_ROPE)
    w = jnp.pad(w, ((0, 0), (0, 0), (0, 0), (0, LANES - MLA_NOPE - MLA_ROPE)))
    return w.reshape(depth, rank, HEADS * LANES).astype(_BF16)


def _pack_w_ukv(w_ukv):
    depth, rank, _ = w_ukv.shape
    w = w_ukv.reshape(depth, rank, HEADS, MLA_NOPE + HEAD_DIM)
    k = jnp.pad(w[..., :MLA_NOPE], ((0, 0), (0, 0), (0, 0), (0, LANES - MLA_NOPE)))
    v = w[..., MLA_NOPE:]
    return jnp.concatenate([k.reshape(depth, rank, HEADS * LANES),
                            v.reshape(depth, rank, HEADS * HEAD_DIM)], axis=-1).astype(_BF16)


def kernel(x, c, ctx, c_ctx, w_mod, b_mod, g_attn, g_ffn, w_in, g_cq, g_ckv, w_uq, w_ukv,
           g_qn, g_kn, sink, w_oa, w_ob, w_ow, w_o, w_pq, sub_keys, peer_u, peer_v, g_final):
    batch, seq, d = x.shape
    n_ctx = ctx.shape[1]
    depth = w_mod.shape[0]
    assert n_ctx == ROW_TILE and seq % PEER_TOKENS == 0 and (batch * n_ctx) % PEER_TOKENS == 0
    assert batch < 8 and seq % GRID_W == 0 and seq >= 2 * WIN_Q_TILE and seq % WIN_Q_TILE == 0
    assert seq % FLASH_Q_TILE == 0 and seq % FLASH_K_CHUNK == 0 and seq % PROJ_TILE == 0
    n_lat = batch * seq
    n_all = n_lat + batch * n_ctx
    ctx_blk0 = n_lat // ROW_TILE

    xs = jnp.concatenate([x.reshape(n_lat, d), ctx.reshape(batch * n_ctx, d)], axis=0)
    cc = jnp.concatenate([c, c_ctx[None], jnp.zeros((8 - batch - 1, d), _F32)], axis=0)
    mod = _modulation(cc, w_mod, b_mod).reshape(depth, 8, 1, N_MOD * d)

    rope = _rope_tables(seq)
    eye = np.kron(np.eye(2, dtype=np.float32), np.ones((HEAD_DIM, HEAD_DIM), np.float32))
    bd = jnp.asarray(eye, _BF16)
    win = _pack_w_in(w_in, d)
    wuq = _pack_w_uq(w_uq)
    wukv = _pack_w_ukv(w_ukv)
    gqn_t = jnp.tile(g_qn, (1, HEADS))[:, None, :]
    gkn_t = jnp.tile(g_kn, (1, 2 * LANES // HEAD_DIM))[:, None, :]
    sink_t = jnp.broadcast_to(sink[:, :, None], (depth, HEADS, LANES))
    woa, wob, wow, wo, wpq = (w.astype(_BF16) for w in (w_oa, w_ob, w_ow, w_o, w_pq))
    subk = sub_keys.reshape(depth, 2 * PEER_HEADS, PEER_KEYS, PEER_HALF).astype(_BF16)
    u_bf = peer_u.astype(_BF16)
    vt_bf = jnp.swapaxes(peer_v, 1, 2).astype(_BF16)

    geo_row = (n_lat // PROJ_TILE, seq // PROJ_TILE, batch)
    geo_peer = (n_lat // PEER_TOKENS, seq // PEER_TOKENS, batch)
    lat_seg = (seq, lambda b, i: b)
    ctx_seg = (n_ctx, lambda b, i: ctx_blk0 + b)
    nqw = seq // WIN_Q_TILE
    per_tile = WIN_Q_TILE // WINDOW
    edge_blocks = seq // WINDOW
    band_segs = [
        (WINDOW, lambda b, i: b * edge_blocks + jnp.maximum(per_tile * i - 1, 0)),
        (WIN_Q_TILE, lambda b, i: b * nqw + i),
        (WINDOW, lambda b, i: b * edge_blocks + jnp.minimum(per_tile * (i + 1), edge_blocks - 1)),
        ctx_seg,
    ]

    for l in range(depth):
        last = l == depth - 1
        qa, ka, va, qb, kb, vb, qw, kw, vw, gates = _inproj(
            xs, mod[l], g_attn[l][None], win[l], g_cq[l][None], g_ckv[l][None], wuq[l], wukv[l],
            gqn_t[l], gkn_t[l], rope, bd, geo=geo_row)
        dense = dict(batch=batch, seq=seq, n_ctx=n_ctx)
        o_a = _flash_attention(qa, ka, va, wide=True, **dense)
        o_b = _flash_attention(qb, kb, vb, wide=False, **dense)
        o_w = _attention(qw, kw, vw, sink_t[l], wide=False, segs=band_segs, band=True,
                         nq=nqw, q_block0=0, batch=batch, tq=WIN_Q_TILE)
        n_rows = n_lat
        if not last:
            cq = dict(nq=1, q_block0=ctx_blk0, batch=batch, segs=[ctx_seg], tq=n_ctx)
            o_a = jnp.concatenate(
                [o_a, _attention(qa, ka, va, None, wide=True, base2=True, **cq)], axis=0)
            o_b = jnp.concatenate(
                [o_b, _attention(qb, kb, vb, None, wide=False, base2=True, **cq)], axis=0)
            o_w = jnp.concatenate([o_w, _attention(qw, kw, vw, sink_t[l], wide=False, **cq)], axis=0)
            n_rows = n_all
        x1, h2, s1t, s2t = _merge(xs, mod[l], o_a, o_b, o_w, gates, woa[l], wob[l], wow[l], wo[l],
                                  g_ffn[l][None], wpq[l], subk[l], n_rows=n_rows, geo=geo_row)
        e1t, cntt, e2t, rkt = _topk(s1t, s2t)
        xs = _peer(h2, x1, mod[l], u_bf[l], vt_bf[l], e1t, cntt, e2t, rkt, geo=geo_peer)
    out = _final_norm(xs, g_final[None], n_lat)
    return out.reshape(batch, seq, d)
```

```python
import functools
import math

import jax
import jax.numpy as jnp
import numpy as np
from jax import lax
from jax.experimental import pallas as pl
from jax.experimental.pallas import tpu as pltpu

GRID_W = 64
ROPE_BASE = 10000.0
NORM_EPS = 1e-6
NEG_INF = -1e30
LOG2E = math.log2(math.e)
WINDOW = 128
N_MOD = 6

HEADS = 8
HEAD_PAIRS = HEADS // 2
MLA_Q_RANK = 384
MLA_KV_RANK = 256
MLA_NOPE = 64
MLA_ROPE = 32
HEAD_DIM = 64
KV_HEADS = 2
LANES = 128

PEER_HEADS = 8
PEER_KEYS = 128
PEER_HALF = 128
PEER_TOPK = 16

ROW_TILE = 256
PROJ_TILE = 512
WIN_Q_TILE = 256
FLASH_Q_TILE = 1024
FLASH_K_CHUNK = 1024
PEER_TOKENS = 512
PEER_ROWS = 16
PEER_CHUNK = 32
PEER_A_PIECE = 512
TOPK_TOKENS = 256
VMEM_LIMIT = 56 * 1024 * 1024

OFF_CQ = 0
OFF_CKV = OFF_CQ + MLA_Q_RANK
OFF_KR = OFF_CKV + MLA_KV_RANK
OFF_QB = OFF_KR + LANES
OFF_KB = OFF_QB + HEADS * HEAD_DIM
OFF_VB = OFF_KB + 2 * LANES
OFF_QW = OFF_VB + 2 * LANES
OFF_KW = OFF_QW + HEADS * HEAD_DIM
OFF_VW = OFF_KW + 2 * LANES
OFF_GATES = OFF_VW + 2 * LANES

_NT = (((1,), (1,)), ((), ()))
_F32 = jnp.float32
_BF16 = jnp.bfloat16


def _params(*semantics):
    return pltpu.CompilerParams(dimension_semantics=semantics, vmem_limit_bytes=VMEM_LIMIT)


def _full(shape):
    return pl.BlockSpec(shape, lambda *_: (0,) * len(shape), pipeline_mode=pl.Buffered(1))


def _rms(x, g):
    return x * lax.rsqrt(jnp.mean(x * x, axis=-1, keepdims=True) + NORM_EPS) * g


def _rope(x, tabs, shift):
    cos, sin_lo, sin_hi = tabs
    outs = []
    for c in range(x.shape[1] // LANES):
        xc = x[:, c * LANES:(c + 1) * LANES]
        up = pltpu.roll(xc, LANES - shift, 1)
        dn = pltpu.roll(xc, shift, 1)
        outs.append(xc * cos + up * sin_lo + dn * sin_hi)
    return outs[0] if len(outs) == 1 else jnp.concatenate(outs, axis=1)


def _head_rms(x, bd, g):
    outs = []
    for c in range(x.shape[1] // LANES):
        xc = x[:, c * LANES:(c + 1) * LANES]
        sq = xc * xc
        hi = sq.astype(_BF16)
        lo = (sq - hi.astype(_F32)).astype(_BF16)
        ssq = (jnp.dot(hi, bd, preferred_element_type=_F32)
               + jnp.dot(lo, bd, preferred_element_type=_F32))
        outs.append(xc * lax.rsqrt(ssq * (1.0 / HEAD_DIM) + NORM_EPS))
    y = outs[0] if len(outs) == 1 else jnp.concatenate(outs, axis=1)
    return y * g


def _mod_kernel(c_ref, w_ref, b_ref, o_ref):
    c = c_ref[...]
    a = c / (1.0 + jnp.exp(-c))
    o_ref[0] = jnp.dot(a, w_ref[0], preferred_element_type=_F32,
                       precision=lax.Precision.HIGHEST) + b_ref[0]


def _modulation(cc, w_mod, b_mod):
    depth, d, cols = w_mod.shape
    tn = cols // 4
    return pl.pallas_call(
        _mod_kernel,
        out_shape=jax.ShapeDtypeStruct((depth, 8, cols), _F32),
        grid=(depth, cols // tn),
        in_specs=[
            pl.BlockSpec((8, d), lambda l, j: (0, 0)),
            pl.BlockSpec((1, d, tn), lambda l, j: (l, 0, j)),
            pl.BlockSpec((1, 1, tn), lambda l, j: (l, 0, j)),
        ],
        out_specs=pl.BlockSpec((1, 8, tn), lambda l, j: (l, 0, j)),
        compiler_params=_params("arbitrary", "arbitrary"),
        name="modulation",
    )(cc, w_mod, b_mod.reshape(depth, 1, cols))


def _inproj_kernel(x_ref, mod_ref, gattn_ref, win_ref, gcq_ref, gckv_ref, wuq_ref, wukv_ref,
                   gqn_ref, gkn_ref, rope_ref, bd_ref,
                   qa_ref, ka_ref, va_ref, qb_ref, kb_ref, vb_ref, qw_ref, kw_ref, vw_ref,
                   gates_ref, *, d, scale_a, scale_h):
    x = x_ref[...]
    sh1 = mod_ref[0, :, 0:d]
    sc1 = mod_ref[0, :, d:2 * d]
    h = (_rms(x, gattn_ref[...]) * (1.0 + sc1) + sh1).astype(_BF16)

    def proj(off, width):
        return jnp.dot(h, win_ref[:, off:off + width], preferred_element_type=_F32)

    rope_a = tuple(rope_ref[:, i * LANES:(i + 1) * LANES] for i in range(3))
    rope_h = tuple(rope_ref[:, i * LANES:(i + 1) * LANES] for i in range(3, 6))
    bd = bd_ref[...]

    cq = _rms(proj(OFF_CQ, MLA_Q_RANK), gcq_ref[...]).astype(_BF16)
    qa = jnp.dot(cq, wuq_ref[...], preferred_element_type=_F32)
    qa_ref[...] = (_rope(qa, rope_a, MLA_ROPE // 2) * (scale_a * LOG2E)).astype(_BF16)
    ckv = _rms(proj(OFF_CKV, MLA_KV_RANK), gckv_ref[...]).astype(_BF16)
    kva = jnp.dot(ckv, wukv_ref[...], preferred_element_type=_F32)
    kr = _rope(proj(OFF_KR, LANES), rope_a, MLA_ROPE // 2)
    ka = kva[:, :HEADS * LANES] + jnp.concatenate([kr] * HEADS, axis=1)
    ka_ref[...] = ka.astype(_BF16)
    va_ref[...] = kva[:, HEADS * LANES:].astype(_BF16)

    qb = _head_rms(proj(OFF_QB, HEADS * HEAD_DIM), bd, gqn_ref[...])
    qb_ref[...] = (_rope(qb, rope_h, HEAD_DIM // 2) * (scale_h * LOG2E)).astype(_BF16)
    kb = _head_rms(proj(OFF_KB, 2 * LANES), bd, gkn_ref[...])
    kb_ref[...] = _rope(kb, rope_h, HEAD_DIM // 2).astype(_BF16)
    vb_ref[...] = proj(OFF_VB, 2 * LANES).astype(_BF16)

    qw_ref[...] = (_rope(proj(OFF_QW, HEADS * HEAD_DIM), rope_h, HEAD_DIM // 2) * scale_h).astype(_BF16)
    kw_ref[...] = _rope(proj(OFF_KW, 2 * LANES), rope_h, HEAD_DIM // 2).astype(_BF16)
    vw_ref[...] = proj(OFF_VW, 2 * LANES).astype(_BF16)

    for k in range(3):
        g = proj(OFF_GATES + k * d, d)
        gates_ref[:, k * d:(k + 1) * d] = (1.0 / (1.0 + jnp.exp(-g))).astype(_BF16)


def _inproj(xs, mod_l, gattn, win, gcq, gckv, wuq, wukv, gqn_t, gkn_t, rope, bd, *, geo):
    n, d = xs.shape
    tm = PROJ_TILE
    lat_tiles, tiles_per_batch, batch = geo
    row = lambda t: (t, 0)

    def mod_idx(t):
        return (jnp.where(t < lat_tiles, t // tiles_per_batch, batch), 0, 0)

    def rope_idx(t):
        return (jnp.where(t < lat_tiles, t % tiles_per_batch, tiles_per_batch), 0)

    widths = (HEADS * LANES, HEADS * LANES, HEADS * HEAD_DIM, HEADS * HEAD_DIM, 2 * LANES, 2 * LANES,
              HEADS * HEAD_DIM, 2 * LANES, 2 * LANES, 3 * d)
    return pl.pallas_call(
        functools.partial(_inproj_kernel, d=d, scale_a=(MLA_NOPE + MLA_ROPE) ** -0.5,
                          scale_h=HEAD_DIM ** -0.5),
        out_shape=[jax.ShapeDtypeStruct((n, w), _BF16) for w in widths],
        grid=(n // tm,),
        in_specs=[
            pl.BlockSpec((tm, d), row),
            pl.BlockSpec((1, 1, N_MOD * d), mod_idx),
            _full(gattn.shape), _full(win.shape), _full(gcq.shape), _full(gckv.shape),
            _full(wuq.shape), _full(wukv.shape), _full(gqn_t.shape), _full(gkn_t.shape),
            pl.BlockSpec((tm, 6 * LANES), rope_idx),
            _full(bd.shape),
        ],
        out_specs=[pl.BlockSpec((tm, w), row) for w in widths],
        compiler_params=_params("arbitrary"),
        name="inproj",
    )(xs, mod_l, gattn, win, gcq, gckv, wuq, wukv, gqn_t, gkn_t, rope, bd)


def _attn_kernel(*refs, nseg, wide, has_sink, band, nq, base2):
    q_ref = refs[0]
    k_refs = refs[1:1 + nseg]
    v_refs = refs[1 + nseg:1 + 2 * nseg]
    sink_ref = refs[1 + 2 * nseg] if has_sink else None
    o_ref = refs[-1]
    pair = pl.program_id(1)
    qi = pl.program_id(2)
    tq = q_ref.shape[0]
    lane = lax.broadcasted_iota(jnp.int32, (tq, LANES), 1)

    masks = [None] * nseg
    if band:
        r = lax.broadcasted_iota(jnp.int32, (tq, WINDOW), 0)
        c = lax.broadcasted_iota(jnp.int32, (tq, WINDOW), 1)
        masks[0] = c >= r + jnp.where(qi > 0, 0, WINDOW)
        masks[2] = c <= r - (tq - WINDOW) - jnp.where(qi < nq - 1, 0, tq)
        r = lax.broadcasted_iota(jnp.int32, (tq, tq), 0)
        c = lax.broadcasted_iota(jnp.int32, (tq, tq), 1)
        masks[1] = jnp.abs(r - c) <= WINDOW

    for pp in range(1 if wide else HEAD_PAIRS):
        qcols = slice(pp * LANES, (pp + 1) * LANES)
        kvcols = slice((pp // 2) * LANES, (pp // 2 + 1) * LANES)
        outs = []
        for hh in range(2):
            if wide:
                q = q_ref[:, hh * LANES:(hh + 1) * LANES]
                ks = [k[:, hh * LANES:(hh + 1) * LANES] for k in k_refs]
                vs = [v[...] for v in v_refs]
                head = 2 * pair + hh
            else:
                own = (lane[:1] < HEAD_DIM) if hh == 0 else (lane[:1] >= HEAD_DIM)
                q = q_ref[:, qcols] * own.astype(_F32).astype(_BF16)
                ks = [k[:, kvcols] for k in k_refs]
                vs = [v[:, kvcols] for v in v_refs]
                head = 2 * pp + hh
            ss = [lax.dot_general(q, k, _NT, preferred_element_type=_F32) for k in ks]
            ss = [s if m is None else jnp.where(m, s, NEG_INF) for s, m in zip(ss, masks)]
            m = functools.reduce(jnp.maximum, [jnp.max(s, axis=-1, keepdims=True) for s in ss])
            if has_sink:
                sink = sink_ref[pl.ds(head, 1), 0:1]
                m = jnp.maximum(m, sink)
            ps = [(jnp.exp2 if base2 else jnp.exp)(s - m) for s in ss]
            l = functools.reduce(jnp.add, [jnp.sum(p, axis=-1, keepdims=True) for p in ps])
            if has_sink:
                l = l + jnp.exp(sink - m)
            o = functools.reduce(jnp.add, [jnp.dot(p.astype(_BF16), v, preferred_element_type=_F32)
                                           for p, v in zip(ps, vs)])
            outs.append(o / l)
        o_ref[:, qcols] = jnp.where(lane < HEAD_DIM, outs[0], outs[1]).astype(o_ref.dtype)


def _flash_kernel(q_ref, k_ref, kc_ref, v_ref, vc_ref, o_ref, *, wide, chunk):
    tq = q_ref.shape[0]
    lane = lax.broadcasted_iota(jnp.int32, (1, LANES), 1)
    qs, kcols = [], []
    for hh in range(2):
        if wide:
            qs.append(q_ref[:, hh * LANES:(hh + 1) * LANES])
            kcols.append(slice(hh * LANES, (hh + 1) * LANES))
        else:
            own = (lane < HEAD_DIM) if hh == 0 else (lane >= HEAD_DIM)
            qs.append(q_ref[...] * own.astype(_F32).astype(_BF16))
            kcols.append(slice(0, LANES))
    m = [jnp.full((tq, 1), -jnp.inf, _F32) for _ in range(2)]
    l = [jnp.zeros((tq, 1), _F32) for _ in range(2)]
    acc = [jnp.zeros((tq, LANES), _F32) for _ in range(2)]
    pieces = [(k_ref, v_ref, r0, chunk) for r0 in range(0, k_ref.shape[0], chunk)]
    pieces.append((kc_ref, vc_ref, 0, kc_ref.shape[0]))
    for kr, vr, r0, rows in pieces:
        for hh in range(2):
            s = lax.dot_general(qs[hh], kr[r0:r0 + rows, kcols[hh]], _NT,
                                preferred_element_type=_F32)
            m_new = jnp.maximum(m[hh], jnp.max(s, axis=-1, keepdims=True))
            alpha = jnp.exp2(m[hh] - m_new)
            p = jnp.exp2(s - m_new)
            l[hh] = alpha * l[hh] + jnp.sum(p, axis=-1, keepdims=True)
            acc[hh] = alpha * acc[hh] + jnp.dot(p.astype(_BF16), vr[r0:r0 + rows, :],
                                                preferred_element_type=_F32)
            m[hh] = m_new
    o_ref[...] = jnp.where(lane < HEAD_DIM, acc[0] / l[0], acc[1] / l[1]).astype(o_ref.dtype)


def _flash_attention(q, k, v, *, wide, batch, seq, n_ctx):
    tq = FLASH_Q_TILE
    nq = seq // tq
    qw = 2 * LANES if wide else LANES
    kdiv = 1 if wide else 2
    ctx_blk0 = batch * seq // n_ctx
    return pl.pallas_call(
        functools.partial(_flash_kernel, wide=wide, chunk=FLASH_K_CHUNK),
        out_shape=jax.ShapeDtypeStruct((batch * seq, HEADS * HEAD_DIM), _BF16),
        grid=(batch, HEAD_PAIRS, nq),
        in_specs=[
            pl.BlockSpec((tq, qw), lambda b, p, i: (b * nq + i, p)),
            pl.BlockSpec((seq, qw), lambda b, p, i: (b, p // kdiv)),
            pl.BlockSpec((n_ctx, qw), lambda b, p, i: (ctx_blk0 + b, p // kdiv)),
            pl.BlockSpec((seq, LANES), lambda b, p, i: (b, p // kdiv)),
            pl.BlockSpec((n_ctx, LANES), lambda b, p, i: (ctx_blk0 + b, p // kdiv)),
        ],
        out_specs=pl.BlockSpec((tq, LANES), lambda b, p, i: (b * nq + i, p)),
        compiler_params=_params("arbitrary", "arbitrary", "arbitrary"),
        name="flash_attention",
    )(q, k, k, v, v)


def _attention(q, k, v, sink, *, wide, segs, nq, q_block0, batch, tq, band=False, base2=False):
    pairs = HEAD_PAIRS if wide else 1
    qw, kw, vw, ow = ((2 * LANES, 2 * LANES, LANES, LANES) if wide
                      else (HEADS * HEAD_DIM, KV_HEADS * LANES, KV_HEADS * LANES, HEADS * HEAD_DIM))
    in_specs = [pl.BlockSpec((tq, qw), lambda b, p, i: (q_block0 + b * nq + i, p))]
    for width in (kw, vw):
        for rows, fn in segs:
            in_specs.append(pl.BlockSpec((rows, width), functools.partial(
                lambda b, p, i, fn: (fn(b, i), p), fn=fn)))
    args = [q] + [k] * len(segs) + [v] * len(segs)
    if sink is not None:
        in_specs.append(_full(sink.shape))
        args.append(sink)
    return pl.pallas_call(
        functools.partial(_attn_kernel, nseg=len(segs), wide=wide, has_sink=sink is not None,
                          band=band, nq=nq, base2=base2),
        out_shape=jax.ShapeDtypeStruct((batch * nq * tq, HEADS * HEAD_DIM), _BF16),
        grid=(batch, pairs, nq),
        in_specs=in_specs,
        out_specs=pl.BlockSpec((tq, ow), lambda b, p, i: (b * nq + i, p)),
        compiler_params=_params("arbitrary", "arbitrary", "arbitrary"),
        name="attention",
    )(*args)


def _merge_kernel(x_ref, mod_ref, oa_ref, ob_ref, ow_ref, gates_ref, woa_ref, wob_ref, wow_ref,
                  wo_ref, gffn_ref, wpq_ref, subk_ref,
                  x1_ref, h2_ref, s1_ref, s2_ref, *, d):
    gt1 = mod_ref[0, :, 2 * d:3 * d]
    sh2 = mod_ref[0, :, 3 * d:4 * d]
    sc2 = mod_ref[0, :, 4 * d:5 * d]
    m = None
    for k, (o_ref, w_ref) in enumerate(((oa_ref, woa_ref), (ob_ref, wob_ref), (ow_ref, wow_ref))):
        t = gates_ref[:, k * d:(k + 1) * d].astype(_F32) * jnp.dot(
            o_ref[...], w_ref[...], preferred_element_type=_F32)
        m = t if m is None else m + t
    y = jnp.dot(m.astype(_BF16), wo_ref[...], preferred_element_type=_F32)
    x1 = x_ref[...] + gt1 * y
    x1_ref[...] = x1
    h2 = (_rms(x1, gffn_ref[...]) * (1.0 + sc2) + sh2).astype(_BF16)
    h2_ref[...] = h2
    q = jnp.dot(h2, wpq_ref[...], preferred_element_type=_F32).astype(_BF16)
    for g in range(2 * PEER_HEADS):
        s = lax.dot_general(subk_ref[g], q[:, g * PEER_HALF:(g + 1) * PEER_HALF], _NT,
                            preferred_element_type=_F32)
        if g % 2 == 0:
            s1_ref[g // 2] = s
        else:
            s2_ref[g // 2] = s


def _merge(xs, mod_l, oa, ob, ow, gates, woa, wob, wow, wo, gffn, wpq, subk, *, n_rows, geo):
    d = xs.shape[1]
    tm = PROJ_TILE
    lat_tiles, tiles_per_batch, batch = geo
    row = lambda t: (t, 0)

    def mod_idx(t):
        return (jnp.where(t < lat_tiles, t // tiles_per_batch, batch), 0, 0)

    return pl.pallas_call(
        functools.partial(_merge_kernel, d=d),
        out_shape=[
            jax.ShapeDtypeStruct((n_rows, d), _F32),
            jax.ShapeDtypeStruct((n_rows, d), _BF16),
            jax.ShapeDtypeStruct((PEER_HEADS, PEER_KEYS, n_rows), _F32),
            jax.ShapeDtypeStruct((PEER_HEADS, PEER_KEYS, n_rows), _F32),
        ],
        grid=(n_rows // tm,),
        in_specs=[
            pl.BlockSpec((tm, d), row),
            pl.BlockSpec((1, 1, N_MOD * d), mod_idx),
            pl.BlockSpec((tm, HEADS * HEAD_DIM), row),
            pl.BlockSpec((tm, HEADS * HEAD_DIM), row),
            pl.BlockSpec((tm, HEADS * HEAD_DIM), row),
            pl.BlockSpec((tm, 3 * d), row),
            _full(woa.shape), _full(wob.shape), _full(wow.shape), _full(wo.shape),
            _full(gffn.shape), _full(wpq.shape), _full(subk.shape),
        ],
        out_specs=[
            pl.BlockSpec((tm, d), row),
            pl.BlockSpec((tm, d), row),
            pl.BlockSpec((PEER_HEADS, PEER_KEYS, tm), lambda t: (0, 0, t)),
            pl.BlockSpec((PEER_HEADS, PEER_KEYS, tm), lambda t: (0, 0, t)),
        ],
        compiler_params=_params("arbitrary"),
        name="merge",
    )(xs, mod_l, oa, ob, ow, gates, woa, wob, wow, wo, gffn, wpq, subk)


NO_RANK = 64.0


def _top_values(w, k, want_rank=False):
    vals = []
    rank = jnp.full(w.shape, NO_RANK, _F32) if want_rank else None
    for r in range(k):
        m = jnp.max(w, axis=0, keepdims=True)
        vals.append(m)
        hit = w == m
        if want_rank:
            rank = jnp.where(hit, float(r), rank)
        w = jnp.where(hit, -jnp.inf, w)
    return (vals, rank) if want_rank else vals


def _stack_rows(vals, n_rows):
    rows = lax.broadcasted_iota(jnp.int32, (n_rows, vals[0].shape[1]), 0)
    out = jnp.full(rows.shape, -jnp.inf, _F32)
    for k, v in enumerate(vals):
        out = jnp.where(rows == k, v, out)
    return out


def _topk_kernel(s1_ref, s2_ref, e1_ref, cnt_ref, e2_ref, rk_ref):
    n_top = PEER_TOPK + 1

    def head(h, carry):
        s1 = s1_ref[h]
        s2 = s2_ref[h]
        t1 = _top_values(s1, n_top)
        t2, rank2 = _top_values(s2, n_top, want_rank=True)
        t2c = _stack_rows(t2, 24)
        cand = [t1[0] + t2c]
        cand += [t1[a] + t2c[:8] for a in range(1, 8)]
        cand += [_stack_rows(t1[8:], 16) + t2[0]]
        best = _top_values(jnp.concatenate(cand, axis=0), n_top)
        z = functools.reduce(jnp.add, [jnp.exp(v - best[0]) for v in best[:PEER_TOPK]])
        tau = 0.5 * (best[PEER_TOPK - 1] + best[PEER_TOPK])
        cnt = jnp.zeros(s1.shape, _F32)
        for b in range(PEER_TOPK):
            cnt = cnt + jnp.where(s1 + t2[b] >= tau, 1.0, 0.0)
        e1_ref[h] = jnp.exp(s1 - t1[0]) / z
        cnt_ref[h] = cnt
        e2_ref[h] = pltpu.bitcast(jnp.exp(s2 - t2[0]).astype(_BF16), jnp.uint32)
        rk_ref[h] = pltpu.bitcast(rank2.astype(_BF16), jnp.uint32)
        return carry

    lax.fori_loop(0, PEER_HEADS, head, 0)


def _topk(s1t, s2t):
    n = s1t.shape[2]
    tk = TOPK_TOKENS
    blk = pl.BlockSpec((PEER_HEADS, PEER_KEYS, tk), lambda t: (0, 0, t))
    packed = pl.BlockSpec((PEER_HEADS, PEER_KEYS // 2, tk), lambda t: (0, 0, t))
    return pl.pallas_call(
        _topk_kernel,
        out_shape=[jax.ShapeDtypeStruct(s1t.shape, _F32)] * 2
        + [jax.ShapeDtypeStruct((PEER_HEADS, PEER_KEYS // 2, n), jnp.uint32)] * 2,
        grid=(n // tk,),
        in_specs=[blk, blk],
        out_specs=[blk, blk, packed, packed],
        compiler_params=_params("arbitrary"),
        name="peer_topk",
    )(s1t, s2t)


def _gate_rows(a_ref, cnt_ref, e1_ref, e2_ref, rk_ref, w_ref, first_keys):
    shape = (PEER_CHUNK, LANES)
    for ii in first_keys:
        for lt in range(a_ref.shape[1] // LANES):
            ls = slice(lt * LANES, (lt + 1) * LANES)
            cntb = [jnp.broadcast_to(cnt_ref[h, ii:ii + 1, ls], shape).astype(_BF16)
                    for h in range(PEER_HEADS)]
            e1b = [jnp.broadcast_to(e1_ref[h, ii:ii + 1, ls], shape).astype(_BF16)
                   for h in range(PEER_HEADS)]
            for jc in range(PEER_KEYS // PEER_CHUNK):
                rows = slice(jc * PEER_CHUNK, (jc + 1) * PEER_CHUNK)
                words = slice(rows.start // 2, rows.stop // 2)
                g = None
                for h in range(PEER_HEADS):
                    e2 = pltpu.bitcast(e2_ref[h, words, ls], _BF16)
                    rk = pltpu.bitcast(rk_ref[h, words, ls], _BF16)
                    t = e1b[h] * jnp.where(rk < cntb[h], e2, jnp.zeros_like(e2))
                    g = t if g is None else g + t
                erows = slice(ii * PEER_KEYS + rows.start, ii * PEER_KEYS + rows.stop)
                a = a_ref[erows, ls]
                act = (0.5 * a) * (1.0 + lax.erf(a * (1.0 / math.sqrt(2.0))))
                w_ref[erows, ls] = g * act.astype(_BF16)


def _peer_kernel(h2_ref, x1_ref, mod_ref, u_ref, vt_ref, cnt_ref, e1_ref, e2_ref, rk_ref,
                 o_ref, acc_ref, a_ref, w_ref, *, d, n_steps):
    k = pl.program_id(1)

    @pl.when(k == 0)
    def _():
        acc_ref[...] = jnp.zeros_like(acc_ref)

    h2 = h2_ref[...]
    for r0 in range(0, a_ref.shape[0], PEER_A_PIECE):
        a_ref[r0:r0 + PEER_A_PIECE, :] = lax.dot_general(
            u_ref[r0:r0 + PEER_A_PIECE, :], h2, _NT, preferred_element_type=_F32)
        _gate_rows(a_ref, cnt_ref, e1_ref, e2_ref, rk_ref, w_ref,
                   range(r0 // PEER_KEYS, (r0 + PEER_A_PIECE) // PEER_KEYS))
    acc_ref[...] += jnp.dot(vt_ref[...], w_ref[...], preferred_element_type=_F32)

    @pl.when(k == n_steps - 1)
    def _():
        gt2 = mod_ref[0, :, 5 * d:6 * d]
        o_ref[...] = x1_ref[...] + gt2 * acc_ref[...].T


def _peer(h2, x1, mod_l, u, vt, e1t, cntt, e2t, rkt, *, geo):
    n, d = h2.shape
    tn = PEER_TOKENS
    te = PEER_ROWS * PEER_KEYS
    n_steps = u.shape[0] // te
    lat_tiles, tiles_per_batch, batch = geo
    row = lambda t, k: (t, 0)

    def mod_idx(t, k):
        return (jnp.where(t < lat_tiles, t // tiles_per_batch, batch), 0, 0)

    rows_blk = pl.BlockSpec((PEER_HEADS, PEER_ROWS, tn), lambda t, k: (0, k, t))
    return pl.pallas_call(
        functools.partial(_peer_kernel, d=d, n_steps=n_steps),
        out_shape=jax.ShapeDtypeStruct((n, d), _F32),
        grid=(n // tn, n_steps),
        in_specs=[
            pl.BlockSpec((tn, d), row),
            pl.BlockSpec((tn, d), row),
            pl.BlockSpec((1, 1, N_MOD * d), mod_idx),
            pl.BlockSpec((te, d), lambda t, k: (k, 0)),
            pl.BlockSpec((d, te), lambda t, k: (0, k)),
            rows_blk,
            rows_blk,
            pl.BlockSpec((PEER_HEADS, PEER_KEYS // 2, tn), lambda t, k: (0, 0, t)),
            pl.BlockSpec((PEER_HEADS, PEER_KEYS // 2, tn), lambda t, k: (0, 0, t)),
        ],
        out_specs=pl.BlockSpec((tn, d), row),
        scratch_shapes=[
            pltpu.VMEM((d, tn), _F32),
            pltpu.VMEM((te, tn), _F32),
            pltpu.VMEM((te, tn), _BF16),
        ],
        compiler_params=_params("arbitrary", "arbitrary"),
        name="peer_experts",
    )(h2, x1, mod_l, u, vt, cntt, e1t, e2t, rkt)


def _final_kernel(x_ref, g_ref, o_ref):
    o_ref[...] = _rms(x_ref[...], g_ref[...])


def _final_norm(xs, g, n_rows):
    d = xs.shape[1]
    tm = PROJ_TILE
    return pl.pallas_call(
        _final_kernel,
        out_shape=jax.ShapeDtypeStruct((n_rows, d), _F32),
        grid=(n_rows // tm,),
        in_specs=[pl.BlockSpec((tm, d), lambda t: (t, 0)), _full(g.shape)],
        out_specs=pl.BlockSpec((tm, d), lambda t: (t, 0)),
        compiler_params=_params("arbitrary"),
        name="final_norm",
    )(xs, g)


def _rope_tables(seq):
    rows = seq // GRID_W
    r = jnp.repeat(jnp.arange(rows, dtype=_F32), GRID_W)
    col = jnp.tile(jnp.arange(GRID_W, dtype=_F32), rows)

    def cos_sin(rot_dim):
        n_freq = rot_dim // 4
        inv = ROPE_BASE ** (-jnp.arange(n_freq, dtype=_F32) / n_freq)
        ang = jnp.concatenate([r[:, None] * inv, col[:, None] * inv], axis=-1)
        return jnp.cos(ang), jnp.sin(ang)

    ca, sa = cos_sin(MLA_ROPE)
    one = jnp.ones((seq, MLA_NOPE), _F32)
    zero = jnp.zeros_like(one)
    z16 = jnp.zeros_like(ca)
    pad1 = jnp.ones((seq, LANES - MLA_NOPE - MLA_ROPE), _F32)
    pad0 = jnp.zeros_like(pad1)
    tab_a = [jnp.concatenate([one, ca, ca, pad1], axis=1),
             jnp.concatenate([zero, -sa, z16, pad0], axis=1),
             jnp.concatenate([zero, z16, sa, pad0], axis=1)]
    ch, sh = cos_sin(HEAD_DIM)
    z32 = jnp.zeros_like(ch)
    tab_h = [jnp.concatenate([ch, ch, ch, ch], axis=1),
             jnp.concatenate([-sh, z32, -sh, z32], axis=1),
             jnp.concatenate([z32, sh, z32, sh], axis=1)]
    lat = jnp.concatenate(tab_a + tab_h, axis=1)
    ident = jnp.concatenate([jnp.ones((PROJ_TILE, LANES), _F32),
                             jnp.zeros((PROJ_TILE, 2 * LANES), _F32)] * 2, axis=1)
    return jnp.concatenate([lat, ident], axis=0)


def _pack_w_in(w_in, d):
    splits = np.cumsum([MLA_Q_RANK, MLA_KV_RANK, MLA_ROPE, 512, 128, 128, 512, 128, 128])
    cq, ckv, kr, qb, kb, vb, qw, kw, vw, gates = jnp.split(w_in, splits, axis=-1)
    lead = w_in.shape[:-1]
    kr_pad = jnp.concatenate([jnp.zeros(lead + (MLA_NOPE,), w_in.dtype), kr,
                              jnp.zeros(lead + (LANES - MLA_NOPE - MLA_ROPE,), w_in.dtype)], axis=-1)

    def dup(w):
        g0, g1 = w[..., :HEAD_DIM], w[..., HEAD_DIM:]
        return jnp.concatenate([g0, g0, g1, g1], axis=-1)

    return jnp.concatenate([cq, ckv, kr_pad, qb, dup(kb), dup(vb), qw, dup(kw), dup(vw), gates],
                           axis=-1).astype(_BF16)


def _pack_w_uq(w_uq):
    depth, rank, _ = w_uq.shape
    w = w_uq.reshape(depth, rank, HEADS, MLA_NOPE + MLA_ROPE)
    w = jnp.pad(w, ((0, 0), (0, 0), (0, 0), (0, LANES - MLA_NOPE - MLA_ROPE)))
    return w.reshape(depth, rank, HEADS * LANES).astype(_BF16)


def _pack_w_ukv(w_ukv):
    depth, rank, _ = w_ukv.shape
    w = w_ukv.reshape(depth, rank, HEADS, MLA_NOPE + HEAD_DIM)
    k = jnp.pad(w[..., :MLA_NOPE], ((0, 0), (0, 0), (0, 0), (0, LANES - MLA_NOPE)))
    v = w[..., MLA_NOPE:]
    return jnp.concatenate([k.reshape(depth, rank, HEADS * LANES),
                            v.reshape(depth, rank, HEADS * HEAD_DIM)], axis=-1).astype(_BF16)


def kernel(x, c, ctx, c_ctx, w_mod, b_mod, g_attn, g_ffn, w_in, g_cq, g_ckv, w_uq, w_ukv,
           g_qn, g_kn, sink, w_oa, w_ob, w_ow, w_o, w_pq, sub_keys, peer_u, peer_v, g_final):
    batch, seq, d = x.shape
    n_ctx = ctx.shape[1]
    depth = w_mod.shape[0]
    assert n_ctx == ROW_TILE and seq % PEER_TOKENS == 0 and (batch * n_ctx) % PEER_TOKENS == 0
    assert batch < 8 and seq % GRID_W == 0 and seq >= 2 * WIN_Q_TILE and seq % WIN_Q_TILE == 0
    assert seq % FLASH_Q_TILE == 0 and seq % FLASH_K_CHUNK == 0 and seq % PROJ_TILE == 0
    n_lat = batch * seq
    n_all = n_lat + batch * n_ctx
    ctx_blk0 = n_lat // ROW_TILE

    xs = jnp.concatenate([x.reshape(n_lat, d), ctx.reshape(batch * n_ctx, d)], axis=0)
    cc = jnp.concatenate([c, c_ctx[None], jnp.zeros((8 - batch - 1, d), _F32)], axis=0)
    mod = _modulation(cc, w_mod, b_mod).reshape(depth, 8, 1, N_MOD * d)

    rope = _rope_tables(seq)
    eye = np.kron(np.eye(2, dtype=np.float32), np.ones((HEAD_DIM, HEAD_DIM), np.float32))
    bd = jnp.asarray(eye, _BF16)
    win = _pack_w_in(w_in, d)
    wuq = _pack_w_uq(w_uq)
    wukv = _pack_w_ukv(w_ukv)
    gqn_t = jnp.tile(g_qn, (1, HEADS))[:, None, :]
    gkn_t = jnp.tile(g_kn, (1, 2 * LANES // HEAD_DIM))[:, None, :]
    sink_t = jnp.broadcast_to(sink[:, :, None], (depth, HEADS, LANES))
    woa, wob, wow, wo, wpq = (w.astype(_BF16) for w in (w_oa, w_ob, w_ow, w_o, w_pq))
    subk = sub_keys.reshape(depth, 2 * PEER_HEADS, PEER_KEYS, PEER_HALF).astype(_BF16)
    u_bf = peer_u.astype(_BF16)
    vt_bf = jnp.swapaxes(peer_v, 1, 2).astype(_BF16)

    geo_row = (n_lat // PROJ_TILE, seq // PROJ_TILE, batch)
    geo_peer = (n_lat // PEER_TOKENS, seq // PEER_TOKENS, batch)
    lat_seg = (seq, lambda b, i: b)
    ctx_seg = (n_ctx, lambda b, i: ctx_blk0 + b)
    nqw = seq // WIN_Q_TILE
    per_tile = WIN_Q_TILE // WINDOW
    edge_blocks = seq // WINDOW
    band_segs = [
        (WINDOW, lambda b, i: b * edge_blocks + jnp.maximum(per_tile * i - 1, 0)),
        (WIN_Q_TILE, lambda b, i: b * nqw + i),
        (WINDOW, lambda b, i: b * edge_blocks + jnp.minimum(per_tile * (i + 1), edge_blocks - 1)),
        ctx_seg,
    ]

    for l in range(depth):
        last = l == depth - 1
        qa, ka, va, qb, kb, vb, qw, kw, vw, gates = _inproj(
            xs, mod[l], g_attn[l][None], win[l], g_cq[l][None], g_ckv[l][None], wuq[l], wukv[l],
            gqn_t[l], gkn_t[l], rope, bd, geo=geo_row)
        dense = dict(batch=batch, seq=seq, n_ctx=n_ctx)
        o_a = _flash_attention(qa, ka, va, wide=True, **dense)
        o_b = _flash_attention(qb, kb, vb, wide=False, **dense)
        o_w = _attention(qw, kw, vw, sink_t[l], wide=False, segs=band_segs, band=True,
                         nq=nqw, q_block0=0, batch=batch, tq=WIN_Q_TILE)
        n_rows = n_lat
        if not last:
            cq = dict(nq=1, q_block0=ctx_blk0, batch=batch, segs=[ctx_seg], tq=n_ctx)
            o_a = jnp.concatenate(
                [o_a, _attention(qa, ka, va, None, wide=True, base2=True, **cq)], axis=0)
            o_b = jnp.concatenate(
                [o_b, _attention(qb, kb, vb, None, wide=False, base2=True, **cq)], axis=0)
            o_w = jnp.concatenate([o_w, _attention(qw, kw, vw, sink_t[l], wide=False, **cq)], axis=0)
            n_rows = n_all
        x1, h2, s1t, s2t = _merge(xs, mod[l], o_a, o_b, o_w, gates, woa[l], wob[l], wow[l], wo[l],
                                  g_ffn[l][None], wpq[l], subk[l], n_rows=n_rows, geo=geo_row)
        e1t, cntt, e2t, rkt = _topk(s1t, s2t)
        xs = _peer(h2, x1, mod[l], u_bf[l], vt_bf[l], e1t, cntt, e2t, rkt, geo=geo_peer)
    out = _final_norm(xs, g_final[None], n_lat)
    return out.reshape(batch, seq, d)
```

```python
import functools
import math

import jax
import jax.numpy as jnp
import numpy as np
from jax import lax
from jax.experimental import pallas as pl
from jax.experimental.pallas import tpu as pltpu

GRID_W = 64
ROPE_BASE = 10000.0
NORM_EPS = 1e-6
NEG_INF = -1e30
LOG2E = math.log2(math.e)
WINDOW = 128
N_MOD = 6

HEADS = 8
HEAD_PAIRS = HEADS // 2
MLA_Q_RANK = 384
MLA_KV_RANK = 256
MLA_NOPE = 64
MLA_ROPE = 32
HEAD_DIM = 64
KV_HEADS = 2
LANES = 128

PEER_HEADS = 8
PEER_KEYS = 128
PEER_HALF = 128
PEER_TOPK = 16

ROW_TILE = 256
PROJ_TILE = 512
WIN_Q_TILE = 256
FLASH_Q_TILE = 1024
FLASH_K_CHUNK = 1024
PEER_TOKENS = 512
PEER_ROWS = 16
PEER_CHUNK = 32
PEER_A_PIECE = 512
TOPK_TOKENS = 256
VMEM_LIMIT = 56 * 1024 * 1024

OFF_CQ = 0
OFF_CKV = OFF_CQ + MLA_Q_RANK
OFF_KR = OFF_CKV + MLA_KV_RANK
OFF_QB = OFF_KR + LANES
OFF_KB = OFF_QB + HEADS * HEAD_DIM
OFF_VB = OFF_KB + 2 * LANES
OFF_QW = OFF_VB + 2 * LANES
OFF_KW = OFF_QW + HEADS * HEAD_DIM
OFF_VW = OFF_KW + 2 * LANES
OFF_GATES = OFF_VW + 2 * LANES

_NT = (((1,), (1,)), ((), ()))
_F32 = jnp.float32
_BF16 = jnp.bfloat16


def _params(*semantics):
    return pltpu.CompilerParams(dimension_semantics=semantics, vmem_limit_bytes=VMEM_LIMIT)


def _full(shape):
    return pl.BlockSpec(shape, lambda *_: (0,) * len(shape), pipeline_mode=pl.Buffered(1))


def _rms(x, g):
    return x * lax.rsqrt(jnp.mean(x * x, axis=-1, keepdims=True) + NORM_EPS) * g


def _rope(x, tabs, shift):
    cos, sin_lo, sin_hi = tabs
    outs = []
    for c in range(x.shape[1] // LANES):
        xc = x[:, c * LANES:(c + 1) * LANES]
        up = pltpu.roll(xc, LANES - shift, 1)
        dn = pltpu.roll(xc, shift, 1)
        outs.append(xc * cos + up * sin_lo + dn * sin_hi)
    return outs[0] if len(outs) == 1 else jnp.concatenate(outs, axis=1)


def _head_rms(x, bd, g):
    outs = []
    for c in range(x.shape[1] // LANES):
        xc = x[:, c * LANES:(c + 1) * LANES]
        sq = xc * xc
        hi = sq.astype(_BF16)
        lo = (sq - hi.astype(_F32)).astype(_BF16)
        ssq = (jnp.dot(hi, bd, preferred_element_type=_F32)
               + jnp.dot(lo, bd, preferred_element_type=_F32))
        outs.append(xc * lax.rsqrt(ssq * (1.0 / HEAD_DIM) + NORM_EPS))
    y = outs[0] if len(outs) == 1 else jnp.concatenate(outs, axis=1)
    return y * g


def _mod_kernel(c_ref, w_ref, b_ref, o_ref):
    c = c_ref[...]
    a = c / (1.0 + jnp.exp(-c))
    o_ref[0] = jnp.dot(a, w_ref[0], preferred_element_type=_F32,
                       precision=lax.Precision.HIGHEST) + b_ref[0]


def _modulation(cc, w_mod, b_mod):
    depth, d, cols = w_mod.shape
    tn = cols // 4
    return pl.pallas_call(
        _mod_kernel,
        out_shape=jax.ShapeDtypeStruct((depth, 8, cols), _F32),
        grid=(depth, cols // tn),
        in_specs=[
            pl.BlockSpec((8, d), lambda l, j: (0, 0)),
            pl.BlockSpec((1, d, tn), lambda l, j: (l, 0, j)),
            pl.BlockSpec((1, 1, tn), lambda l, j: (l, 0, j)),
        ],
        out_specs=pl.BlockSpec((1, 8, tn), lambda l, j: (l, 0, j)),
        compiler_params=_params("arbitrary", "arbitrary"),
        name="modulation",
    )(cc, w_mod, b_mod.reshape(depth, 1, cols))


def _inproj_kernel(x_ref, mod_ref, gattn_ref, win_ref, gcq_ref, gckv_ref, wuq_ref, wukv_ref,
                   gqn_ref, gkn_ref, rope_ref, bd_ref,
                   qa_ref, ka_ref, va_ref, qb_ref, kb_ref, vb_ref, qw_ref, kw_ref, vw_ref,
                   gates_ref, *, d, scale_a, scale_h):
    x = x_ref[...]
    sh1 = mod_ref[0, :, 0:d]
    sc1 = mod_ref[0, :, d:2 * d]
    h = (_rms(x, gattn_ref[...]) * (1.0 + sc1) + sh1).astype(_BF16)

    def proj(off, width):
        return jnp.dot(h, win_ref[:, off:off + width], preferred_element_type=_F32)

    rope_a = tuple(rope_ref[:, i * LANES:(i + 1) * LANES] for i in range(3))
    rope_h = tuple(rope_ref[:, i * LANES:(i + 1) * LANES] for i in range(3, 6))
    bd = bd_ref[...]

    cq = _rms(proj(OFF_CQ, MLA_Q_RANK), gcq_ref[...]).astype(_BF16)
    qa = jnp.dot(cq, wuq_ref[...], preferred_element_type=_F32)
    qa_ref[...] = (_rope(qa, rope_a, MLA_ROPE // 2) * (scale_a * LOG2E)).astype(_BF16)
    ckv = _rms(proj(OFF_CKV, MLA_KV_RANK), gckv_ref[...]).astype(_BF16)
    kva = jnp.dot(ckv, wukv_ref[...], preferred_element_type=_F32)
    kr = _rope(proj(OFF_KR, LANES), rope_a, MLA_ROPE // 2)
    ka = kva[:, :HEADS * LANES] + jnp.concatenate([kr] * HEADS, axis=1)
    ka_ref[...] = ka.astype(_BF16)
    va_ref[...] = kva[:, HEADS * LANES:].astype(_BF16)

    qb = _head_rms(proj(OFF_QB, HEADS * HEAD_DIM), bd, gqn_ref[...])
    qb_ref[...] = (_rope(qb, rope_h, HEAD_DIM // 2) * (scale_h * LOG2E)).astype(_BF16)
    kb = _head_rms(proj(OFF_KB, 2 * LANES), bd, gkn_ref[...])
    kb_ref[...] = _rope(kb, rope_h, HEAD_DIM // 2).astype(_BF16)
    vb_ref[...] = proj(OFF_VB, 2 * LANES).astype(_BF16)

    qw_ref[...] = (_rope(proj(OFF_QW, HEADS * HEAD_DIM), rope_h, HEAD_DIM // 2) * scale_h).astype(_BF16)
    kw_ref[...] = _rope(proj(OFF_KW, 2 * LANES), rope_h, HEAD_DIM // 2).astype(_BF16)
    vw_ref[...] = proj(OFF_VW, 2 * LANES).astype(_BF16)

    for k in range(3):
        g = proj(OFF_GATES + k * d, d)
        gates_ref[:, k * d:(k + 1) * d] = (1.0 / (1.0 + jnp.exp(-g))).astype(_BF16)


def _inproj(xs, mod_l, gattn, win, gcq, gckv, wuq, wukv, gqn_t, gkn_t, rope, bd, *, geo):
    n, d = xs.shape
    tm = PROJ_TILE
    lat_tiles, tiles_per_batch, batch = geo
    row = lambda t: (t, 0)

    def mod_idx(t):
        return (jnp.where(t < lat_tiles, t // tiles_per_batch, batch), 0, 0)

    def rope_idx(t):
        return (jnp.where(t < lat_tiles, t % tiles_per_batch, tiles_per_batch), 0)

    widths = (HEADS * LANES, HEADS * LANES, HEADS * HEAD_DIM, HEADS * HEAD_DIM, 2 * LANES, 2 * LANES,
              HEADS * HEAD_DIM, 2 * LANES, 2 * LANES, 3 * d)
    return pl.pallas_call(
        functools.partial(_inproj_kernel, d=d, scale_a=(MLA_NOPE + MLA_ROPE) ** -0.5,
                          scale_h=HEAD_DIM ** -0.5),
        out_shape=[jax.ShapeDtypeStruct((n, w), _BF16) for w in widths],
        grid=(n // tm,),
        in_specs=[
            pl.BlockSpec((tm, d), row),
            pl.BlockSpec((1, 1, N_MOD * d), mod_idx),
            _full(gattn.shape), _full(win.shape), _full(gcq.shape), _full(gckv.shape),
            _full(wuq.shape), _full(wukv.shape), _full(gqn_t.shape), _full(gkn_t.shape),
            pl.BlockSpec((tm, 6 * LANES), rope_idx),
            _full(bd.shape),
        ],
        out_specs=[pl.BlockSpec((tm, w), row) for w in widths],
        compiler_params=_params("arbitrary"),
        name="inproj",
    )(xs, mod_l, gattn, win, gcq, gckv, wuq, wukv, gqn_t, gkn_t, rope, bd)


def _attn_kernel(*refs, nseg, wide, has_sink, band, nq, base2, has_into):
    q_ref = refs[0]
    k_refs = refs[1:1 + nseg]
    v_refs = refs[1 + nseg:1 + 2 * nseg]
    sink_ref = refs[1 + 2 * nseg] if has_sink else None
    o_ref = refs[-1]
    pair = pl.program_id(1)
    qi = pl.program_id(2)
    tq = q_ref.shape[0]
    lane = lax.broadcasted_iota(jnp.int32, (tq, LANES), 1)

    masks = [None] * nseg
    if band:
        r = lax.broadcasted_iota(jnp.int32, (tq, WINDOW), 0)
        c = lax.broadcasted_iota(jnp.int32, (tq, WINDOW), 1)
        masks[0] = c >= r + jnp.where(qi > 0, 0, WINDOW)
        masks[2] = c <= r - (tq - WINDOW) - jnp.where(qi < nq - 1, 0, tq)
        r = lax.broadcasted_iota(jnp.int32, (tq, tq), 0)
        c = lax.broadcasted_iota(jnp.int32, (tq, tq), 1)
        masks[1] = jnp.abs(r - c) <= WINDOW

    for pp in range(1 if wide else HEAD_PAIRS):
        qcols = slice(pp * LANES, (pp + 1) * LANES)
        kvcols = slice((pp // 2) * LANES, (pp // 2 + 1) * LANES)
        outs = []
        for hh in range(2):
            if wide:
                q = q_ref[:, hh * LANES:(hh + 1) * LANES]
                ks = [k[:, hh * LANES:(hh + 1) * LANES] for k in k_refs]
                vs = [v[...] for v in v_refs]
                head = 2 * pair + hh
            else:
                own = (lane[:1] < HEAD_DIM) if hh == 0 else (lane[:1] >= HEAD_DIM)
                q = q_ref[:, qcols] * own.astype(_F32).astype(_BF16)
                ks = [k[:, kvcols] for k in k_refs]
                vs = [v[:, kvcols] for v in v_refs]
                head = 2 * pp + hh
            ss = [lax.dot_general(q, k, _NT, preferred_element_type=_F32) for k in ks]
            ss = [s if m is None else jnp.where(m, s, NEG_INF) for s, m in zip(ss, masks)]
            m = functools.reduce(jnp.maximum, [jnp.max(s, axis=-1, keepdims=True) for s in ss])
            if has_sink:
                sink = sink_ref[pl.ds(head, 1), 0:1]
                m = jnp.maximum(m, sink)
            ps = [(jnp.exp2 if base2 else jnp.exp)(s - m) for s in ss]
            l = functools.reduce(jnp.add, [jnp.sum(p, axis=-1, keepdims=True) for p in ps])
            if has_sink:
                l = l + jnp.exp(sink - m)
            o = functools.reduce(jnp.add, [jnp.dot(p.astype(_BF16), v, preferred_element_type=_F32)
                                           for p, v in zip(ps, vs)])
            outs.append(o / l)
        o_ref[:, qcols] = jnp.where(lane < HEAD_DIM, outs[0], outs[1]).astype(o_ref.dtype)


def _flash_kernel(q_ref, k_ref, kc_ref, v_ref, vc_ref, o_ref, *, wide, chunk):
    tq = q_ref.shape[0]
    lane = lax.broadcasted_iota(jnp.int32, (1, LANES), 1)
    qs, kcols = [], []
    for hh in range(2):
        if wide:
            qs.append(q_ref[:, hh * LANES:(hh + 1) * LANES])
            kcols.append(slice(hh * LANES, (hh + 1) * LANES))
        else:
            own = (lane < HEAD_DIM) if hh == 0 else (lane >= HEAD_DIM)
            qs.append(q_ref[...] * own.astype(_F32).astype(_BF16))
            kcols.append(slice(0, LANES))
    m = [jnp.full((tq, 1), -jnp.inf, _F32) for _ in range(2)]
    l = [jnp.zeros((tq, 1), _F32) for _ in range(2)]
    acc = [jnp.zeros((tq, LANES), _F32) for _ in range(2)]
    pieces = [(k_ref, v_ref, r0, chunk) for r0 in range(0, k_ref.shape[0], chunk)]
    pieces.append((kc_ref, vc_ref, 0, kc_ref.shape[0]))
    for kr, vr, r0, rows in pieces:
        for hh in range(2):
            s = lax.dot_general(qs[hh], kr[r0:r0 + rows, kcols[hh]], _NT,
                                preferred_element_type=_F32)
            m_new = jnp.maximum(m[hh], jnp.max(s, axis=-1, keepdims=True))
            alpha = jnp.exp2(m[hh] - m_new)
            p = jnp.exp2(s - m_new)
            l[hh] = alpha * l[hh] + jnp.sum(p, axis=-1, keepdims=True)
            acc[hh] = alpha * acc[hh] + jnp.dot(p.astype(_BF16), vr[r0:r0 + rows, :],
                                                preferred_element_type=_F32)
            m[hh] = m_new
    o_ref[...] = jnp.where(lane < HEAD_DIM, acc[0] / l[0], acc[1] / l[1]).astype(o_ref.dtype)


def _flash_attention(q, k, v, *, wide, batch, seq, n_ctx, out_rows):
    tq = FLASH_Q_TILE
    nq = seq // tq
    qw = 2 * LANES if wide else LANES
    kdiv = 1 if wide else 2
    ctx_blk0 = batch * seq // n_ctx
    return pl.pallas_call(
        functools.partial(_flash_kernel, wide=wide, chunk=FLASH_K_CHUNK),
        out_shape=jax.ShapeDtypeStruct((out_rows, HEADS * HEAD_DIM), _BF16),
        grid=(batch, HEAD_PAIRS, nq),
        in_specs=[
            pl.BlockSpec((tq, qw), lambda b, p, i: (b * nq + i, p)),
            pl.BlockSpec((seq, qw), lambda b, p, i: (b, p // kdiv)),
            pl.BlockSpec((n_ctx, qw), lambda b, p, i: (ctx_blk0 + b, p // kdiv)),
            pl.BlockSpec((seq, LANES), lambda b, p, i: (b, p // kdiv)),
            pl.BlockSpec((n_ctx, LANES), lambda b, p, i: (ctx_blk0 + b, p // kdiv)),
        ],
        out_specs=pl.BlockSpec((tq, LANES), lambda b, p, i: (b * nq + i, p)),
        compiler_params=_params("arbitrary", "arbitrary", "arbitrary"),
        name="flash_attention",
    )(q, k, k, v, v)


def _attention(q, k, v, sink, *, wide, segs, nq, q_block0, batch, tq, band=False, base2=False,
               out_rows=None, into=None):
    pairs = HEAD_PAIRS if wide else 1
    qw, kw, vw, ow = ((2 * LANES, 2 * LANES, LANES, LANES) if wide
                      else (HEADS * HEAD_DIM, KV_HEADS * LANES, KV_HEADS * LANES, HEADS * HEAD_DIM))
    in_specs = [pl.BlockSpec((tq, qw), lambda b, p, i: (q_block0 + b * nq + i, p))]
    for width in (kw, vw):
        for rows, fn in segs:
            in_specs.append(pl.BlockSpec((rows, width), functools.partial(
                lambda b, p, i, fn: (fn(b, i), p), fn=fn)))
    args = [q] + [k] * len(segs) + [v] * len(segs)
    if sink is not None:
        in_specs.append(_full(sink.shape))
        args.append(sink)
    aliases = {}
    out0 = 0
    if into is not None:
        in_specs.append(pl.BlockSpec(memory_space=pl.ANY))
        args.append(into)
        aliases = {len(args) - 1: 0}
        out_rows = into.shape[0]
        out0 = q_block0
    return pl.pallas_call(
        functools.partial(_attn_kernel, nseg=len(segs), wide=wide, has_sink=sink is not None,
                          band=band, nq=nq, base2=base2, has_into=into is not None),
        out_shape=jax.ShapeDtypeStruct((out_rows or batch * nq * tq, HEADS * HEAD_DIM), _BF16),
        grid=(batch, pairs, nq),
        in_specs=in_specs,
        out_specs=pl.BlockSpec((tq, ow), lambda b, p, i: (out0 + b * nq + i, p)),
        input_output_aliases=aliases,
        compiler_params=_params("arbitrary", "arbitrary", "arbitrary"),
        name="attention",
    )(*args)


def _merge_kernel(x_ref, mod_ref, oa_ref, ob_ref, ow_ref, gates_ref, woa_ref, wob_ref, wow_ref,
                  wo_ref, gffn_ref, wpq_ref, subk_ref,
                  x1_ref, h2_ref, s1_ref, s2_ref, *, d):
    gt1 = mod_ref[0, :, 2 * d:3 * d]
    sh2 = mod_ref[0, :, 3 * d:4 * d]
    sc2 = mod_ref[0, :, 4 * d:5 * d]
    m = None
    for k, (o_ref, w_ref) in enumerate(((oa_ref, woa_ref), (ob_ref, wob_ref), (ow_ref, wow_ref))):
        t = gates_ref[:, k * d:(k + 1) * d].astype(_F32) * jnp.dot(
            o_ref[...], w_ref[...], preferred_element_type=_F32)
        m = t if m is None else m + t
    y = jnp.dot(m.astype(_BF16), wo_ref[...], preferred_element_type=_F32)
    x1 = x_ref[...] + gt1 * y
    x1_ref[...] = x1
    h2 = (_rms(x1, gffn_ref[...]) * (1.0 + sc2) + sh2).astype(_BF16)
    h2_ref[...] = h2
    q = jnp.dot(h2, wpq_ref[...], preferred_element_type=_F32).astype(_BF16)
    for g in range(2 * PEER_HEADS):
        s = lax.dot_general(subk_ref[g], q[:, g * PEER_HALF:(g + 1) * PEER_HALF], _NT,
                            preferred_element_type=_F32)
        if g % 2 == 0:
            s1_ref[g // 2] = s
        else:
            s2_ref[g // 2] = s


def _merge(xs, mod_l, oa, ob, ow, gates, woa, wob, wow, wo, gffn, wpq, subk, *, n_rows, geo):
    d = xs.shape[1]
    tm = PROJ_TILE
    lat_tiles, tiles_per_batch, batch = geo
    row = lambda t: (t, 0)

    def mod_idx(t):
        return (jnp.where(t < lat_tiles, t // tiles_per_batch, batch), 0, 0)

    return pl.pallas_call(
        functools.partial(_merge_kernel, d=d),
        out_shape=[
            jax.ShapeDtypeStruct((n_rows, d), _F32),
            jax.ShapeDtypeStruct((n_rows, d), _BF16),
            jax.ShapeDtypeStruct((PEER_HEADS, PEER_KEYS, n_rows), _F32),
            jax.ShapeDtypeStruct((PEER_HEADS, PEER_KEYS, n_rows), _F32),
        ],
        grid=(n_rows // tm,),
        in_specs=[
            pl.BlockSpec((tm, d), row),
            pl.BlockSpec((1, 1, N_MOD * d), mod_idx),
            pl.BlockSpec((tm, HEADS * HEAD_DIM), row),
            pl.BlockSpec((tm, HEADS * HEAD_DIM), row),
            pl.BlockSpec((tm, HEADS * HEAD_DIM), row),
            pl.BlockSpec((tm, 3 * d), row),
            _full(woa.shape), _full(wob.shape), _full(wow.shape), _full(wo.shape),
            _full(gffn.shape), _full(wpq.shape), _full(subk.shape),
        ],
        out_specs=[
            pl.BlockSpec((tm, d), row),
            pl.BlockSpec((tm, d), row),
            pl.BlockSpec((PEER_HEADS, PEER_KEYS, tm), lambda t: (0, 0, t)),
            pl.BlockSpec((PEER_HEADS, PEER_KEYS, tm), lambda t: (0, 0, t)),
        ],
        compiler_params=_params("arbitrary"),
        name="merge",
    )(xs, mod_l, oa, ob, ow, gates, woa, wob, wow, wo, gffn, wpq, subk)


NO_RANK = 64.0


def _top_values(w, k, want_rank=False):
    vals = []
    rank = jnp.full(w.shape, NO_RANK, _F32) if want_rank else None
    for r in range(k):
        m = jnp.max(w, axis=0, keepdims=True)
        vals.append(m)
        hit = w == m
        if want_rank:
            rank = jnp.where(hit, float(r), rank)
        w = jnp.where(hit, -jnp.inf, w)
    return (vals, rank) if want_rank else vals


def _stack_rows(vals, n_rows):
    rows = lax.broadcasted_iota(jnp.int32, (n_rows, vals[0].shape[1]), 0)
    out = jnp.full(rows.shape, -jnp.inf, _F32)
    for k, v in enumerate(vals):
        out = jnp.where(rows == k, v, out)
    return out


def _topk_kernel(s1_ref, s2_ref, e1_ref, cnt_ref, e2_ref, rk_ref):
    n_top = PEER_TOPK + 1

    def head(h, carry):
        s1 = s1_ref[h]
        s2 = s2_ref[h]
        t1 = _top_values(s1, n_top)
        t2, rank2 = _top_values(s2, n_top, want_rank=True)
        t2c = _stack_rows(t2, 24)
        cand = [t1[0] + t2c]
        cand += [t1[a] + t2c[:8] for a in range(1, 8)]
        cand += [_stack_rows(t1[8:], 16) + t2[0]]
        best = _top_values(jnp.concatenate(cand, axis=0), n_top)
        z = functools.reduce(jnp.add, [jnp.exp(v - best[0]) for v in best[:PEER_TOPK]])
        tau = 0.5 * (best[PEER_TOPK - 1] + best[PEER_TOPK])
        need = tau - s1
        cnt = jnp.zeros(s1.shape, _F32)
        for b in range(PEER_TOPK):
            cnt = jnp.where(t2[b] >= need, float(b + 1), cnt)
        e1_ref[h] = jnp.exp(s1 - t1[0]) / z
        cnt_ref[h] = cnt
        e2_ref[h] = pltpu.bitcast(jnp.exp(s2 - t2[0]).astype(_BF16), jnp.uint32)
        rk_ref[h] = pltpu.bitcast(rank2.astype(_BF16), jnp.uint32)
        return carry

    lax.fori_loop(0, PEER_HEADS, head, 0)


def _topk(s1t, s2t):
    n = s1t.shape[2]
    tk = TOPK_TOKENS
    blk = pl.BlockSpec((PEER_HEADS, PEER_KEYS, tk), lambda t: (0, 0, t))
    packed = pl.BlockSpec((PEER_HEADS, PEER_KEYS // 2, tk), lambda t: (0, 0, t))
    return pl.pallas_call(
        _topk_kernel,
        out_shape=[jax.ShapeDtypeStruct(s1t.shape, _F32)] * 2
        + [jax.ShapeDtypeStruct((PEER_HEADS, PEER_KEYS // 2, n), jnp.uint32)] * 2,
        grid=(n // tk,),
        in_specs=[blk, blk],
        out_specs=[blk, blk, packed, packed],
        compiler_params=_params("arbitrary"),
        name="peer_topk",
    )(s1t, s2t)


def _gate_rows(a_ref, cnt_ref, e1_ref, e2_ref, rk_ref, w_ref, first_keys):
    shape = (PEER_CHUNK, LANES)
    for ii in first_keys:
        for lt in range(a_ref.shape[1] // LANES):
            ls = slice(lt * LANES, (lt + 1) * LANES)
            cntb = [jnp.broadcast_to(cnt_ref[h, ii:ii + 1, ls], shape).astype(_BF16)
                    for h in range(PEER_HEADS)]
            e1b = [jnp.broadcast_to(e1_ref[h, ii:ii + 1, ls], shape).astype(_BF16)
                   for h in range(PEER_HEADS)]
            for jc in range(PEER_KEYS // PEER_CHUNK):
                rows = slice(jc * PEER_CHUNK, (jc + 1) * PEER_CHUNK)
                words = slice(rows.start // 2, rows.stop // 2)
                g = None
                for h in range(PEER_HEADS):
                    e2 = pltpu.bitcast(e2_ref[h, words, ls], _BF16)
                    rk = pltpu.bitcast(rk_ref[h, words, ls], _BF16)
                    t = e1b[h] * jnp.where(rk < cntb[h], e2, jnp.zeros_like(e2))
                    g = t if g is None else g + t
                erows = slice(ii * PEER_KEYS + rows.start, ii * PEER_KEYS + rows.stop)
                a = a_ref[erows, ls]
                act = (0.5 * a) * (1.0 + lax.erf(a * (1.0 / math.sqrt(2.0))))
                w_ref[erows, ls] = g * act.astype(_BF16)


def _peer_kernel(h2_ref, x1_ref, mod_ref, u_ref, vt_ref, cnt_ref, e1_ref, e2_ref, rk_ref,
                 o_ref, acc_ref, a_ref, w_ref, *, d, n_steps):
    k = pl.program_id(1)

    @pl.when(k == 0)
    def _():
        acc_ref[...] = jnp.zeros_like(acc_ref)

    h2 = h2_ref[...]
    for r0 in range(0, a_ref.shape[0], PEER_A_PIECE):
        a_ref[r0:r0 + PEER_A_PIECE, :] = lax.dot_general(
            u_ref[r0:r0 + PEER_A_PIECE, :], h2, _NT, preferred_element_type=_F32)
        _gate_rows(a_ref, cnt_ref, e1_ref, e2_ref, rk_ref, w_ref,
                   range(r0 // PEER_KEYS, (r0 + PEER_A_PIECE) // PEER_KEYS))
    acc_ref[...] += jnp.dot(vt_ref[...], w_ref[...], preferred_element_type=_F32)

    @pl.when(k == n_steps - 1)
    def _():
        gt2 = mod_ref[0, :, 5 * d:6 * d]
        o_ref[...] = x1_ref[...] + gt2 * acc_ref[...].T


def _peer(h2, x1, mod_l, u, vt, e1t, cntt, e2t, rkt, *, geo):
    n, d = h2.shape
    tn = PEER_TOKENS
    te = PEER_ROWS * PEER_KEYS
    n_steps = u.shape[0] // te
    lat_tiles, tiles_per_batch, batch = geo
    row = lambda t, k: (t, 0)

    def mod_idx(t, k):
        return (jnp.where(t < lat_tiles, t // tiles_per_batch, batch), 0, 0)

    rows_blk = pl.BlockSpec((PEER_HEADS, PEER_ROWS, tn), lambda t, k: (0, k, t))
    return pl.pallas_call(
        functools.partial(_peer_kernel, d=d, n_steps=n_steps),
        out_shape=jax.ShapeDtypeStruct((n, d), _F32),
        grid=(n // tn, n_steps),
        in_specs=[
            pl.BlockSpec((tn, d), row),
            pl.BlockSpec((tn, d), row),
            pl.BlockSpec((1, 1, N_MOD * d), mod_idx),
            pl.BlockSpec((te, d), lambda t, k: (k, 0)),
            pl.BlockSpec((d, te), lambda t, k: (0, k)),
            rows_blk,
            rows_blk,
            pl.BlockSpec((PEER_HEADS, PEER_KEYS // 2, tn), lambda t, k: (0, 0, t)),
            pl.BlockSpec((PEER_HEADS, PEER_KEYS // 2, tn), lambda t, k: (0, 0, t)),
        ],
        out_specs=pl.BlockSpec((tn, d), row),
        scratch_shapes=[
            pltpu.VMEM((d, tn), _F32),
            pltpu.VMEM((te, tn), _F32),
            pltpu.VMEM((te, tn), _BF16),
        ],
        compiler_params=_params("arbitrary", "arbitrary"),
        name="peer_experts",
    )(h2, x1, mod_l, u, vt, cntt, e1t, e2t, rkt)


def _final_kernel(x_ref, g_ref, o_ref):
    o_ref[...] = _rms(x_ref[...], g_ref[...])


def _final_norm(xs, g, n_rows):
    d = xs.shape[1]
    tm = PROJ_TILE
    return pl.pallas_call(
        _final_kernel,
        out_shape=jax.ShapeDtypeStruct((n_rows, d), _F32),
        grid=(n_rows // tm,),
        in_specs=[pl.BlockSpec((tm, d), lambda t: (t, 0)), _full(g.shape)],
        out_specs=pl.BlockSpec((tm, d), lambda t: (t, 0)),
        compiler_params=_params("arbitrary"),
        name="final_norm",
    )(xs, g)


def _rope_tables(seq):
    rows = seq // GRID_W
    r = jnp.repeat(jnp.arange(rows, dtype=_F32), GRID_W)
    col = jnp.tile(jnp.arange(GRID_W, dtype=_F32), rows)

    def cos_sin(rot_dim):
        n_freq = rot_dim // 4
        inv = ROPE_BASE ** (-jnp.arange(n_freq, dtype=_F32) / n_freq)
        ang = jnp.concatenate([r[:, None] * inv, col[:, None] * inv], axis=-1)
        return jnp.cos(ang), jnp.sin(ang)

    ca, sa = cos_sin(MLA_ROPE)
    one = jnp.ones((seq, MLA_NOPE), _F32)
    zero = jnp.zeros_like(one)
    z16 = jnp.zeros_like(ca)
    pad1 = jnp.ones((seq, LANES - MLA_NOPE - MLA_ROPE), _F32)
    pad0 = jnp.zeros_like(pad1)
    tab_a = [jnp.concatenate([one, ca, ca, pad1], axis=1),
             jnp.concatenate([zero, -sa, z16, pad0], axis=1),
             jnp.concatenate([zero, z16, sa, pad0], axis=1)]
    ch, sh = cos_sin(HEAD_DIM)
    z32 = jnp.zeros_like(ch)
    tab_h = [jnp.concatenate([ch, ch, ch, ch], axis=1),
             jnp.concatenate([-sh, z32, -sh, z32], axis=1),
             jnp.concatenate([z32, sh, z32, sh], axis=1)]
    lat = jnp.concatenate(tab_a + tab_h, axis=1)
    ident = jnp.concatenate([jnp.ones((PROJ_TILE, LANES), _F32),
                             jnp.zeros((PROJ_TILE, 2 * LANES), _F32)] * 2, axis=1)
    return jnp.concatenate([lat, ident], axis=0)


def _pack_w_in(w_in, d):
    splits = np.cumsum([MLA_Q_RANK, MLA_KV_RANK, MLA_ROPE, 512, 128, 128, 512, 128, 128])
    cq, ckv, kr, qb, kb, vb, qw, kw, vw, gates = jnp.split(w_in, splits, axis=-1)
    lead = w_in.shape[:-1]
    kr_pad = jnp.concatenate([jnp.zeros(lead + (MLA_NOPE,), w_in.dtype), kr,
                              jnp.zeros(lead + (LANES - MLA_NOPE - MLA_ROPE,), w_in.dtype)], axis=-1)

    def dup(w):
        g0, g1 = w[..., :HEAD_DIM], w[..., HEAD_DIM:]
        return jnp.concatenate([g0, g0, g1, g1], axis=-1)

    return jnp.concatenate([cq, ckv, kr_pad, qb, dup(kb), dup(vb), qw, dup(kw), dup(vw), gates],
                           axis=-1).astype(_BF16)


def _pack_w_uq(w_uq):
    depth, rank, _ = w_uq.shape
    w = w_uq.reshape(depth, rank, HEADS, MLA_NOPE + MLA_ROPE)
    w = jnp.pad(w, ((0, 0), (0, 0), (0, 0), (0, LANES - MLA_NOPE - MLA_ROPE)))
    return w.reshape(depth, rank, HEADS * LANES).astype(_BF16)


def _pack_w_ukv(w_ukv):
    depth, rank, _ = w_ukv.shape
    w = w_ukv.reshape(depth, rank, HEADS, MLA_NOPE + HEAD_DIM)
    k = jnp.pad(w[..., :MLA_NOPE], ((0, 0), (0, 0), (0, 0), (0, LANES - MLA_NOPE)))
    v = w[..., MLA_NOPE:]
    return jnp.concatenate([k.reshape(depth, rank, HEADS * LANES),
                            v.reshape(depth, rank, HEADS * HEAD_DIM)], axis=-1).astype(_BF16)


def kernel(x, c, ctx, c_ctx, w_mod, b_mod, g_attn, g_ffn, w_in, g_cq, g_ckv, w_uq, w_ukv,
           g_qn, g_kn, sink, w_oa, w_ob, w_ow, w_o, w_pq, sub_keys, peer_u, peer_v, g_final):
    batch, seq, d = x.shape
    n_ctx = ctx.shape[1]
    depth = w_mod.shape[0]
    assert n_ctx == ROW_TILE and seq % PEER_TOKENS == 0 and (batch * n_ctx) % PEER_TOKENS == 0
    assert batch < 8 and seq % GRID_W == 0 and seq >= 2 * WIN_Q_TILE and seq % WIN_Q_TILE == 0
    assert seq % FLASH_Q_TILE == 0 and seq % FLASH_K_CHUNK == 0 and seq % PROJ_TILE == 0
    n_lat = batch * seq
    n_all = n_lat + batch * n_ctx
    ctx_blk0 = n_lat // ROW_TILE

    xs = jnp.concatenate([x.reshape(n_lat, d), ctx.reshape(batch * n_ctx, d)], axis=0)
    cc = jnp.concatenate([c, c_ctx[None], jnp.zeros((8 - batch - 1, d), _F32)], axis=0)
    mod = _modulation(cc, w_mod, b_mod).reshape(depth, 8, 1, N_MOD * d)

    rope = _rope_tables(seq)
    eye = np.kron(np.eye(2, dtype=np.float32), np.ones((HEAD_DIM, HEAD_DIM), np.float32))
    bd = jnp.asarray(eye, _BF16)
    win = _pack_w_in(w_in, d)
    wuq = _pack_w_uq(w_uq)
    wukv = _pack_w_ukv(w_ukv)
    gqn_t = jnp.tile(g_qn, (1, HEADS))[:, None, :]
    gkn_t = jnp.tile(g_kn, (1, 2 * LANES // HEAD_DIM))[:, None, :]
    sink_t = jnp.broadcast_to(sink[:, :, None], (depth, HEADS, LANES))
    woa, wob, wow, wo, wpq = (w.astype(_BF16) for w in (w_oa, w_ob, w_ow, w_o, w_pq))
    subk = sub_keys.reshape(depth, 2 * PEER_HEADS, PEER_KEYS, PEER_HALF).astype(_BF16)
    u_bf = peer_u.astype(_BF16)
    vt_bf = jnp.swapaxes(peer_v, 1, 2).astype(_BF16)

    geo_row = (n_lat // PROJ_TILE, seq // PROJ_TILE, batch)
    geo_peer = (n_lat // PEER_TOKENS, seq // PEER_TOKENS, batch)
    lat_seg = (seq, lambda b, i: b)
    ctx_seg = (n_ctx, lambda b, i: ctx_blk0 + b)
    nqw = seq // WIN_Q_TILE
    per_tile = WIN_Q_TILE // WINDOW
    edge_blocks = seq // WINDOW
    band_segs = [
        (WINDOW, lambda b, i: b * edge_blocks + jnp.maximum(per_tile * i - 1, 0)),
        (WIN_Q_TILE, lambda b, i: b * nqw + i),
        (WINDOW, lambda b, i: b * edge_blocks + jnp.minimum(per_tile * (i + 1), edge_blocks - 1)),
        ctx_seg,
    ]

    for l in range(depth):
        last = l == depth - 1
        qa, ka, va, qb, kb, vb, qw, kw, vw, gates = _inproj(
            xs, mod[l], g_attn[l][None], win[l], g_cq[l][None], g_ckv[l][None], wuq[l], wukv[l],
            gqn_t[l], gkn_t[l], rope, bd, geo=geo_row)
        dense = dict(batch=batch, seq=seq, n_ctx=n_ctx)
        n_rows = n_lat if last else n_all
        o_a = _flash_attention(qa, ka, va, wide=True, out_rows=n_rows, **dense)
        o_b = _flash_attention(qb, kb, vb, wide=False, out_rows=n_rows, **dense)
        o_w = _attention(qw, kw, vw, sink_t[l], wide=False, segs=band_segs, band=True,
                         nq=nqw, q_block0=0, batch=batch, tq=WIN_Q_TILE, out_rows=n_rows)
        if not last:
            cq = dict(nq=1, q_block0=ctx_blk0, batch=batch, segs=[ctx_seg], tq=n_ctx)
            o_a = _attention(qa, ka, va, None, wide=True, base2=True, into=o_a, **cq)
            o_b = _attention(qb, kb, vb, None, wide=False, base2=True, into=o_b, **cq)
            o_w = _attention(qw, kw, vw, sink_t[l], wide=False, into=o_w, **cq)
        x1, h2, s1t, s2t = _merge(xs, mod[l], o_a, o_b, o_w, gates, woa[l], wob[l], wow[l], wo[l],
                                  g_ffn[l][None], wpq[l], subk[l], n_rows=n_rows, geo=geo_row)
        e1t, cntt, e2t, rkt = _topk(s1t, s2t)
        xs = _peer(h2, x1, mod[l], u_bf[l], vt_bf[l], e1t, cntt, e2t, rkt, geo=geo_peer)
    out = _final_norm(xs, g_final[None], n_lat)
    return out.reshape(batch, seq, d)
```

```python
import functools
import math

import jax
import jax.numpy as jnp
import numpy as np
from jax import lax
from jax.experimental import pallas as pl
from jax.experimental.pallas import tpu as pltpu

GRID_W = 64
ROPE_BASE = 10000.0
NORM_EPS = 1e-6
NEG_INF = -1e30
LOG2E = math.log2(math.e)
WINDOW = 128
N_MOD = 6

HEADS = 8
HEAD_PAIRS = HEADS // 2
MLA_Q_RANK = 384
MLA_KV_RANK = 256
MLA_NOPE = 64
MLA_ROPE = 32
HEAD_DIM = 64
KV_HEADS = 2
LANES = 128

PEER_HEADS = 8
PEER_KEYS = 128
PEER_HALF = 128
PEER_TOPK = 16

ROW_TILE = 256
PROJ_TILE = 512
WIN_Q_TILE = 256
FLASH_Q_TILE = 1024
FLASH_K_CHUNK = 1024
PEER_TOKENS = 512
PEER_ROWS = 16
PEER_CHUNK = 32
PEER_A_PIECE = 512
TOPK_TOKENS = 256
VMEM_LIMIT = 56 * 1024 * 1024

OFF_CQ = 0
OFF_CKV = OFF_CQ + MLA_Q_RANK
OFF_KR = OFF_CKV + MLA_KV_RANK
OFF_QB = OFF_KR + LANES
OFF_KB = OFF_QB + HEADS * HEAD_DIM
OFF_VB = OFF_KB + 2 * LANES
OFF_QW = OFF_VB + 2 * LANES
OFF_KW = OFF_QW + HEADS * HEAD_DIM
OFF_VW = OFF_KW + 2 * LANES
OFF_GATES = OFF_VW + 2 * LANES

_NT = (((1,), (1,)), ((), ()))
_F32 = jnp.float32
_BF16 = jnp.bfloat16


def _params(*semantics):
    return pltpu.CompilerParams(dimension_semantics=semantics, vmem_limit_bytes=VMEM_LIMIT)


def _full(shape):
    return pl.BlockSpec(shape, lambda *_: (0,) * len(shape), pipeline_mode=pl.Buffered(1))


def _rms(x, g):
    return x * lax.rsqrt(jnp.mean(x * x, axis=-1, keepdims=True) + NORM_EPS) * g


def _rope(x, tabs, shift):
    cos, sin_lo, sin_hi = tabs
    outs = []
    for c in range(x.shape[1] // LANES):
        xc = x[:, c * LANES:(c + 1) * LANES]
        up = pltpu.roll(xc, LANES - shift, 1)
        dn = pltpu.roll(xc, shift, 1)
        outs.append(xc * cos + up * sin_lo + dn * sin_hi)
    return outs[0] if len(outs) == 1 else jnp.concatenate(outs, axis=1)


def _head_rms(x, bd, g):
    outs = []
    for c in range(x.shape[1] // LANES):
        xc = x[:, c * LANES:(c + 1) * LANES]
        sq = xc * xc
        hi = sq.astype(_BF16)
        lo = (sq - hi.astype(_F32)).astype(_BF16)
        ssq = (jnp.dot(hi, bd, preferred_element_type=_F32)
               + jnp.dot(lo, bd, preferred_element_type=_F32))
        outs.append(xc * lax.rsqrt(ssq * (1.0 / HEAD_DIM) + NORM_EPS))
    y = outs[0] if len(outs) == 1 else jnp.concatenate(outs, axis=1)
    return y * g


def _mod_kernel(c_ref, w_ref, b_ref, o_ref):
    c = c_ref[...]
    a = c / (1.0 + jnp.exp(-c))
    o_ref[0] = jnp.dot(a, w_ref[0], preferred_element_type=_F32,
                       precision=lax.Precision.HIGHEST) + b_ref[0]


def _modulation(cc, w_mod, b_mod):
    depth, d, cols = w_mod.shape
    tn = cols // 4
    return pl.pallas_call(
        _mod_kernel,
        out_shape=jax.ShapeDtypeStruct((depth, 8, cols), _F32),
        grid=(depth, cols // tn),
        in_specs=[
            pl.BlockSpec((8, d), lambda l, j: (0, 0)),
            pl.BlockSpec((1, d, tn), lambda l, j: (l, 0, j)),
            pl.BlockSpec((1, 1, tn), lambda l, j: (l, 0, j)),
        ],
        out_specs=pl.BlockSpec((1, 8, tn), lambda l, j: (l, 0, j)),
        compiler_params=_params("arbitrary", "arbitrary"),
        name="modulation",
    )(cc, w_mod, b_mod.reshape(depth, 1, cols))


def _inproj_kernel(x_ref, mod_ref, gattn_ref, win_ref, gcq_ref, gckv_ref, wuq_ref, wukv_ref,
                   gqn_ref, gkn_ref, rope_ref, bd_ref,
                   qa_ref, ka_ref, va_ref, qb_ref, kb_ref, vb_ref, qw_ref, kw_ref, vw_ref,
                   gates_ref, *, d, scale_a, scale_h):
    x = x_ref[...]
    sh1 = mod_ref[0, :, 0:d]
    sc1 = mod_ref[0, :, d:2 * d]
    h = (_rms(x, gattn_ref[...]) * (1.0 + sc1) + sh1).astype(_BF16)

    def proj(off, width):
        return jnp.dot(h, win_ref[:, off:off + width], preferred_element_type=_F32)

    rope_a = tuple(rope_ref[:, i * LANES:(i + 1) * LANES] for i in range(3))
    rope_h = tuple(rope_ref[:, i * LANES:(i + 1) * LANES] for i in range(3, 6))
    bd = bd_ref[...]

    cq = _rms(proj(OFF_CQ, MLA_Q_RANK), gcq_ref[...]).astype(_BF16)
    qa = jnp.dot(cq, wuq_ref[...], preferred_element_type=_F32)
    qa_ref[...] = (_rope(qa, rope_a, MLA_ROPE // 2) * (scale_a * LOG2E)).astype(_BF16)
    ckv = _rms(proj(OFF_CKV, MLA_KV_RANK), gckv_ref[...]).astype(_BF16)
    kva = jnp.dot(ckv, wukv_ref[...], preferred_element_type=_F32)
    kr = _rope(proj(OFF_KR, LANES), rope_a, MLA_ROPE // 2)
    ka = kva[:, :HEADS * LANES] + jnp.concatenate([kr] * HEADS, axis=1)
    ka_ref[...] = ka.astype(_BF16)
    va_ref[...] = kva[:, HEADS * LANES:].astype(_BF16)

    qb = _head_rms(proj(OFF_QB, HEADS * HEAD_DIM), bd, gqn_ref[...])
    qb_ref[...] = (_rope(qb, rope_h, HEAD_DIM // 2) * (scale_h * LOG2E)).astype(_BF16)
    kb = _head_rms(proj(OFF_KB, 2 * LANES), bd, gkn_ref[...])
    kb_ref[...] = _rope(kb, rope_h, HEAD_DIM // 2).astype(_BF16)
    vb_ref[...] = proj(OFF_VB, 2 * LANES).astype(_BF16)

    qw_ref[...] = (_rope(proj(OFF_QW, HEADS * HEAD_DIM), rope_h, HEAD_DIM // 2) * scale_h).astype(_BF16)
    kw_ref[...] = _rope(proj(OFF_KW, 2 * LANES), rope_h, HEAD_DIM // 2).astype(_BF16)
    vw_ref[...] = proj(OFF_VW, 2 * LANES).astype(_BF16)

    for k in range(3):
        g = proj(OFF_GATES + k * d, d)
        gates_ref[:, k * d:(k + 1) * d] = (1.0 / (1.0 + jnp.exp(-g))).astype(_BF16)


def _inproj(xs, mod_l, gattn, win, gcq, gckv, wuq, wukv, gqn_t, gkn_t, rope, bd, *, geo):
    n, d = xs.shape
    tm = PROJ_TILE
    lat_tiles, tiles_per_batch, batch = geo
    row = lambda t: (t, 0)

    def mod_idx(t):
        return (jnp.where(t < lat_tiles, t // tiles_per_batch, batch), 0, 0)

    def rope_idx(t):
        return (jnp.where(t < lat_tiles, t % tiles_per_batch, tiles_per_batch), 0)

    widths = (HEADS * LANES, HEADS * LANES, HEADS * HEAD_DIM, HEADS * HEAD_DIM, 2 * LANES, 2 * LANES,
              HEADS * HEAD_DIM, 2 * LANES, 2 * LANES, 3 * d)
    return pl.pallas_call(
        functools.partial(_inproj_kernel, d=d, scale_a=(MLA_NOPE + MLA_ROPE) ** -0.5,
                          scale_h=HEAD_DIM ** -0.5),
        out_shape=[jax.ShapeDtypeStruct((n, w), _BF16) for w in widths],
        grid=(n // tm,),
        in_specs=[
            pl.BlockSpec((tm, d), row),
            pl.BlockSpec((1, 1, N_MOD * d), mod_idx),
            _full(gattn.shape), _full(win.shape), _full(gcq.shape), _full(gckv.shape),
            _full(wuq.shape), _full(wukv.shape), _full(gqn_t.shape), _full(gkn_t.shape),
            pl.BlockSpec((tm, 6 * LANES), rope_idx),
            _full(bd.shape),
        ],
        out_specs=[pl.BlockSpec((tm, w), row) for w in widths],
        compiler_params=_params("arbitrary"),
        name="inproj",
    )(xs, mod_l, gattn, win, gcq, gckv, wuq, wukv, gqn_t, gkn_t, rope, bd)


def _attn_kernel(*refs, nseg, wide, has_sink, band, nq, base2, has_into):
    q_ref = refs[0]
    k_refs = refs[1:1 + nseg]
    v_refs = refs[1 + nseg:1 + 2 * nseg]
    sink_ref = refs[1 + 2 * nseg] if has_sink else None
    o_ref = refs[-1]
    pair = pl.program_id(1)
    qi = pl.program_id(2)
    tq = q_ref.shape[0]
    lane = lax.broadcasted_iota(jnp.int32, (tq, LANES), 1)

    masks = [None] * nseg
    if band:
        r = lax.broadcasted_iota(jnp.int32, (tq, WINDOW), 0)
        c = lax.broadcasted_iota(jnp.int32, (tq, WINDOW), 1)
        masks[0] = c >= r + jnp.where(qi > 0, 0, WINDOW)
        masks[2] = c <= r - (tq - WINDOW) - jnp.where(qi < nq - 1, 0, tq)
        r = lax.broadcasted_iota(jnp.int32, (tq, tq), 0)
        c = lax.broadcasted_iota(jnp.int32, (tq, tq), 1)
        masks[1] = jnp.abs(r - c) <= WINDOW

    for pp in range(1 if wide else HEAD_PAIRS):
        qcols = slice(pp * LANES, (pp + 1) * LANES)
        kvcols = slice((pp // 2) * LANES, (pp // 2 + 1) * LANES)
        outs = []
        for hh in range(2):
            if wide:
                q = q_ref[:, hh * LANES:(hh + 1) * LANES]
                ks = [k[:, hh * LANES:(hh + 1) * LANES] for k in k_refs]
                vs = [v[...] for v in v_refs]
                head = 2 * pair + hh
            else:
                own = (lane[:1] < HEAD_DIM) if hh == 0 else (lane[:1] >= HEAD_DIM)
                q = q_ref[:, qcols] * own.astype(_F32).astype(_BF16)
                ks = [k[:, kvcols] for k in k_refs]
                vs = [v[:, kvcols] for v in v_refs]
                head = 2 * pp + hh
            ss = [lax.dot_general(q, k, _NT, preferred_element_type=_F32) for k in ks]
            ss = [s if m is None else jnp.where(m, s, NEG_INF) for s, m in zip(ss, masks)]
            m = functools.reduce(jnp.maximum, [jnp.max(s, axis=-1, keepdims=True) for s in ss])
            if has_sink:
                sink = sink_ref[pl.ds(head, 1), 0:1]
                m = jnp.maximum(m, sink)
            ps = [(jnp.exp2 if base2 else jnp.exp)(s - m) for s in ss]
            l = functools.reduce(jnp.add, [jnp.sum(p, axis=-1, keepdims=True) for p in ps])
            if has_sink:
                l = l + jnp.exp(sink - m)
            o = functools.reduce(jnp.add, [jnp.dot(p.astype(_BF16), v, preferred_element_type=_F32)
                                           for p, v in zip(ps, vs)])
            outs.append(o / l)
        o_ref[:, qcols] = jnp.where(lane < HEAD_DIM, outs[0], outs[1]).astype(o_ref.dtype)


def _flash_kernel(q_ref, k_ref, kc_ref, v_ref, vc_ref, o_ref, *, wide, chunk):
    tq = q_ref.shape[0]
    lane = lax.broadcasted_iota(jnp.int32, (1, LANES), 1)
    qs, kcols = [], []
    for hh in range(2):
        if wide:
            qs.append(q_ref[:, hh * LANES:(hh + 1) * LANES])
            kcols.append(slice(hh * LANES, (hh + 1) * LANES))
        else:
            own = (lane < HEAD_DIM) if hh == 0 else (lane >= HEAD_DIM)
            qs.append(q_ref[...] * own.astype(_F32).astype(_BF16))
            kcols.append(slice(0, LANES))
    m = [jnp.full((tq, 1), -jnp.inf, _F32) for _ in range(2)]
    l = [jnp.zeros((tq, 1), _F32) for _ in range(2)]
    acc = [jnp.zeros((tq, LANES), _F32) for _ in range(2)]
    pieces = [(k_ref, v_ref, r0, chunk) for r0 in range(0, k_ref.shape[0], chunk)]
    pieces.append((kc_ref, vc_ref, 0, kc_ref.shape[0]))
    for kr, vr, r0, rows in pieces:
        for hh in range(2):
            s = lax.dot_general(qs[hh], kr[r0:r0 + rows, kcols[hh]], _NT,
                                preferred_element_type=_F32)
            m_new = jnp.maximum(m[hh], jnp.max(s, axis=-1, keepdims=True))
            alpha = jnp.exp2(m[hh] - m_new)
            p = jnp.exp2(s - m_new)
            l[hh] = alpha * l[hh] + jnp.sum(p, axis=-1, keepdims=True)
            acc[hh] = alpha * acc[hh] + jnp.dot(p.astype(_BF16), vr[r0:r0 + rows, :],
                                                preferred_element_type=_F32)
            m[hh] = m_new
    o_ref[...] = jnp.where(lane < HEAD_DIM, acc[0] / l[0], acc[1] / l[1]).astype(o_ref.dtype)


def _flash_attention(q, k, v, *, wide, batch, seq, n_ctx, out_rows):
    tq = FLASH_Q_TILE
    nq = seq // tq
    qw = 2 * LANES if wide else LANES
    kdiv = 1 if wide else 2
    ctx_blk0 = batch * seq // n_ctx
    return pl.pallas_call(
        functools.partial(_flash_kernel, wide=wide, chunk=FLASH_K_CHUNK),
        out_shape=jax.ShapeDtypeStruct((out_rows, HEADS * HEAD_DIM), _BF16),
        grid=(batch, HEAD_PAIRS, nq),
        in_specs=[
            pl.BlockSpec((tq, qw), lambda b, p, i: (b * nq + i, p)),
            pl.BlockSpec((seq, qw), lambda b, p, i: (b, p // kdiv)),
            pl.BlockSpec((n_ctx, qw), lambda b, p, i: (ctx_blk0 + b, p // kdiv)),
            pl.BlockSpec((seq, LANES), lambda b, p, i: (b, p // kdiv)),
            pl.BlockSpec((n_ctx, LANES), lambda b, p, i: (ctx_blk0 + b, p // kdiv)),
        ],
        out_specs=pl.BlockSpec((tq, LANES), lambda b, p, i: (b * nq + i, p)),
        compiler_params=_params("arbitrary", "arbitrary", "arbitrary"),
        name="flash_attention",
    )(q, k, k, v, v)


def _attention(q, k, v, sink, *, wide, segs, nq, q_block0, batch, tq, band=False, base2=False,
               out_rows=None, into=None):
    pairs = HEAD_PAIRS if wide else 1
    qw, kw, vw, ow = ((2 * LANES, 2 * LANES, LANES, LANES) if wide
                      else (HEADS * HEAD_DIM, KV_HEADS * LANES, KV_HEADS * LANES, HEADS * HEAD_DIM))
    in_specs = [pl.BlockSpec((tq, qw), lambda b, p, i: (q_block0 + b * nq + i, p))]
    for width in (kw, vw):
        for rows, fn in segs:
            in_specs.append(pl.BlockSpec((rows, width), functools.partial(
                lambda b, p, i, fn: (fn(b, i), p), fn=fn)))
    args = [q] + [k] * len(segs) + [v] * len(segs)
    if sink is not None:
        in_specs.append(_full(sink.shape))
        args.append(sink)
    aliases = {}
    out0 = 0
    if into is not None:
        in_specs.append(pl.BlockSpec(memory_space=pl.ANY))
        args.append(into)
        aliases = {len(args) - 1: 0}
        out_rows = into.shape[0]
        out0 = q_block0
    return pl.pallas_call(
        functools.partial(_attn_kernel, nseg=len(segs), wide=wide, has_sink=sink is not None,
                          band=band, nq=nq, base2=base2, has_into=into is not None),
        out_shape=jax.ShapeDtypeStruct((out_rows or batch * nq * tq, HEADS * HEAD_DIM), _BF16),
        grid=(batch, pairs, nq),
        in_specs=in_specs,
        out_specs=pl.BlockSpec((tq, ow), lambda b, p, i: (out0 + b * nq + i, p)),
        input_output_aliases=aliases,
        compiler_params=_params("arbitrary", "arbitrary", "arbitrary"),
        name="attention",
    )(*args)


def _merge_kernel(x_ref, mod_ref, oa_ref, ob_ref, ow_ref, gates_ref, woa_ref, wob_ref, wow_ref,
                  wo_ref, gffn_ref, wpq_ref, subk_ref,
                  x1_ref, h2_ref, s1_ref, s2_ref, *, d):
    gt1 = mod_ref[0, :, 2 * d:3 * d]
    sh2 = mod_ref[0, :, 3 * d:4 * d]
    sc2 = mod_ref[0, :, 4 * d:5 * d]
    m = None
    for k, (o_ref, w_ref) in enumerate(((oa_ref, woa_ref), (ob_ref, wob_ref), (ow_ref, wow_ref))):
        t = gates_ref[:, k * d:(k + 1) * d].astype(_F32) * jnp.dot(
            o_ref[...], w_ref[...], preferred_element_type=_F32)
        m = t if m is None else m + t
    y = jnp.dot(m.astype(_BF16), wo_ref[...], preferred_element_type=_F32)
    x1 = x_ref[...] + gt1 * y
    x1_ref[...] = x1
    h2 = (_rms(x1, gffn_ref[...]) * (1.0 + sc2) + sh2).astype(_BF16)
    h2_ref[...] = h2
    q = jnp.dot(h2, wpq_ref[...], preferred_element_type=_F32).astype(_BF16)
    for g in range(2 * PEER_HEADS):
        s = lax.dot_general(subk_ref[g], q[:, g * PEER_HALF:(g + 1) * PEER_HALF], _NT,
                            preferred_element_type=_F32)
        if g % 2 == 0:
            s1_ref[g // 2] = s
        else:
            s2_ref[g // 2] = s


def _merge(xs, mod_l, oa, ob, ow, gates, woa, wob, wow, wo, gffn, wpq, subk, *, n_rows, geo):
    d = xs.shape[1]
    tm = PROJ_TILE
    lat_tiles, tiles_per_batch, batch = geo
    row = lambda t: (t, 0)

    def mod_idx(t):
        return (jnp.where(t < lat_tiles, t // tiles_per_batch, batch), 0, 0)

    return pl.pallas_call(
        functools.partial(_merge_kernel, d=d),
        out_shape=[
            jax.ShapeDtypeStruct((n_rows, d), _F32),
            jax.ShapeDtypeStruct((n_rows, d), _BF16),
            jax.ShapeDtypeStruct((PEER_HEADS, PEER_KEYS, n_rows), _F32),
            jax.ShapeDtypeStruct((PEER_HEADS, PEER_KEYS, n_rows), _F32),
        ],
        grid=(n_rows // tm,),
        in_specs=[
            pl.BlockSpec((tm, d), row),
            pl.BlockSpec((1, 1, N_MOD * d), mod_idx),
            pl.BlockSpec((tm, HEADS * HEAD_DIM), row),
            pl.BlockSpec((tm, HEADS * HEAD_DIM), row),
            pl.BlockSpec((tm, HEADS * HEAD_DIM), row),
            pl.BlockSpec((tm, 3 * d), row),
            _full(woa.shape), _full(wob.shape), _full(wow.shape), _full(wo.shape),
            _full(gffn.shape), _full(wpq.shape), _full(subk.shape),
        ],
        out_specs=[
            pl.BlockSpec((tm, d), row),
            pl.BlockSpec((tm, d), row),
            pl.BlockSpec((PEER_HEADS, PEER_KEYS, tm), lambda t: (0, 0, t)),
            pl.BlockSpec((PEER_HEADS, PEER_KEYS, tm), lambda t: (0, 0, t)),
        ],
        compiler_params=_params("arbitrary"),
        name="merge",
    )(xs, mod_l, oa, ob, ow, gates, woa, wob, wow, wo, gffn, wpq, subk)


NO_RANK = 64.0


def _top_values(w, k, want_rank=False):
    vals = []
    rank = jnp.full(w.shape, NO_RANK, _F32) if want_rank else None
    for r in range(k):
        m = jnp.max(w, axis=0, keepdims=True)
        vals.append(m)
        hit = w == m
        if want_rank:
            rank = jnp.where(hit, float(r), rank)
        w = jnp.where(hit, -jnp.inf, w)
    return (vals, rank) if want_rank else vals


def _stack_rows(vals, n_rows):
    rows = lax.broadcasted_iota(jnp.int32, (n_rows, vals[0].shape[1]), 0)
    out = jnp.full(rows.shape, -jnp.inf, _F32)
    for k, v in enumerate(vals):
        out = jnp.where(rows == k, v, out)
    return out


def _topk_kernel(s1_ref, s2_ref, e1_ref, cnt_ref, e2_ref, rk_ref):
    n_top = PEER_TOPK + 1

    def head(h, carry):
        s1 = s1_ref[h]
        s2 = s2_ref[h]
        t1 = _top_values(s1, n_top)
        t2, rank2 = _top_values(s2, n_top, want_rank=True)
        t2c = _stack_rows(t2, 24)
        cand = [t1[0] + t2c]
        cand += [t1[a] + t2c[:8] for a in range(1, 8)]
        cand += [_stack_rows(t1[8:], 16) + t2[0]]
        best = _top_values(jnp.concatenate(cand, axis=0), n_top)
        z = functools.reduce(jnp.add, [jnp.exp(v - best[0]) for v in best[:PEER_TOPK]])
        tau = 0.5 * (best[PEER_TOPK - 1] + best[PEER_TOPK])
        need = tau - s1
        cnt = jnp.zeros(s1.shape, _F32)
        for b in range(PEER_TOPK):
            cnt = jnp.where(t2[b] >= need, float(b + 1), cnt)
        e1_ref[h] = jnp.exp(s1 - t1[0]) / z
        cnt_ref[h] = cnt
        e2_ref[h] = pltpu.bitcast(jnp.exp(s2 - t2[0]).astype(_BF16), jnp.uint32)
        rk_ref[h] = pltpu.bitcast(rank2.astype(_BF16), jnp.uint32)
        return carry

    lax.fori_loop(0, PEER_HEADS, head, 0)


def _topk(s1t, s2t):
    n = s1t.shape[2]
    tk = TOPK_TOKENS
    blk = pl.BlockSpec((PEER_HEADS, PEER_KEYS, tk), lambda t: (0, 0, t))
    packed = pl.BlockSpec((PEER_HEADS, PEER_KEYS // 2, tk), lambda t: (0, 0, t))
    return pl.pallas_call(
        _topk_kernel,
        out_shape=[jax.ShapeDtypeStruct(s1t.shape, _F32)] * 2
        + [jax.ShapeDtypeStruct((PEER_HEADS, PEER_KEYS // 2, n), jnp.uint32)] * 2,
        grid=(n // tk,),
        in_specs=[blk, blk],
        out_specs=[blk, blk, packed, packed],
        compiler_params=_params("arbitrary"),
        name="peer_topk",
    )(s1t, s2t)


def _gate_rows(a_ref, cnt_ref, e1_ref, e2_ref, rk_ref, w_ref, first_keys, lane_tiles):
    shape = (PEER_CHUNK, LANES)
    for ii in first_keys:
        for lt in lane_tiles:
            ls = slice(lt * LANES, (lt + 1) * LANES)
            cntb = [jnp.broadcast_to(cnt_ref[h, ii:ii + 1, ls], shape).astype(_BF16)
                    for h in range(PEER_HEADS)]
            e1b = [jnp.broadcast_to(e1_ref[h, ii:ii + 1, ls], shape).astype(_BF16)
                   for h in range(PEER_HEADS)]
            for jc in range(PEER_KEYS // PEER_CHUNK):
                rows = slice(jc * PEER_CHUNK, (jc + 1) * PEER_CHUNK)
                words = slice(rows.start // 2, rows.stop // 2)
                g = None
                for h in range(PEER_HEADS):
                    e2 = pltpu.bitcast(e2_ref[h, words, ls], _BF16)
                    rk = pltpu.bitcast(rk_ref[h, words, ls], _BF16)
                    t = e1b[h] * jnp.where(rk < cntb[h], e2, jnp.zeros_like(e2))
                    g = t if g is None else g + t
                erows = slice(ii * PEER_KEYS + rows.start, ii * PEER_KEYS + rows.stop)
                a = a_ref[erows, ls]
                act = (0.5 * a) * (1.0 + lax.erf(a * (1.0 / math.sqrt(2.0))))
                w_ref[erows, ls] = g * act.astype(_BF16)


def _peer_kernel(h2_ref, x1_ref, mod_ref, u0_ref, un_ref, vt_ref, cnt_ref, e1_ref, e2_ref, rk_ref,
                 o_ref, acc_ref, a_ref, w_ref, *, d, n_steps):
    k = pl.program_id(1)

    @pl.when(k == 0)
    def _():
        acc_ref[...] = jnp.zeros_like(acc_ref)
        a_ref[...] = lax.dot_general(u0_ref[...], h2_ref[...], _NT, preferred_element_type=_F32)

    half = a_ref.shape[1] // 2
    for hv in range(2):
        lanes = slice(hv * half, (hv + 1) * half)
        _gate_rows(a_ref, cnt_ref, e1_ref, e2_ref, rk_ref, w_ref, range(PEER_ROWS),
                   range(hv * half // LANES, (hv + 1) * half // LANES))
        acc_ref[:, lanes] += jnp.dot(vt_ref[...], w_ref[:, lanes], preferred_element_type=_F32)
        a_ref[:, lanes] = lax.dot_general(un_ref[...], h2_ref[lanes, :], _NT,
                                          preferred_element_type=_F32)

    @pl.when(k == n_steps - 1)
    def _():
        gt2 = mod_ref[0, :, 5 * d:6 * d]
        o_ref[...] = x1_ref[...] + gt2 * acc_ref[...].T


def _peer(h2, x1, mod_l, u, vt, e1t, cntt, e2t, rkt, *, geo):
    n, d = h2.shape
    tn = PEER_TOKENS
    te = PEER_ROWS * PEER_KEYS
    n_steps = u.shape[0] // te
    lat_tiles, tiles_per_batch, batch = geo
    row = lambda t, k: (t, 0)

    def mod_idx(t, k):
        return (jnp.where(t < lat_tiles, t // tiles_per_batch, batch), 0, 0)

    rows_blk = pl.BlockSpec((PEER_HEADS, PEER_ROWS, tn), lambda t, k: (0, k, t))
    return pl.pallas_call(
        functools.partial(_peer_kernel, d=d, n_steps=n_steps),
        out_shape=jax.ShapeDtypeStruct((n, d), _F32),
        grid=(n // tn, n_steps),
        in_specs=[
            pl.BlockSpec((tn, d), row),
            pl.BlockSpec((tn, d), row),
            pl.BlockSpec((1, 1, N_MOD * d), mod_idx),
            pl.BlockSpec((te, d), lambda t, k: (0, 0)),
            pl.BlockSpec((te, d), lambda t, k: (jnp.minimum(k + 1, n_steps - 1), 0)),
            pl.BlockSpec((d, te), lambda t, k: (0, k)),
            rows_blk,
            rows_blk,
            pl.BlockSpec((PEER_HEADS, PEER_KEYS // 2, tn), lambda t, k: (0, 0, t)),
            pl.BlockSpec((PEER_HEADS, PEER_KEYS // 2, tn), lambda t, k: (0, 0, t)),
        ],
        out_specs=pl.BlockSpec((tn, d), row),
        scratch_shapes=[
            pltpu.VMEM((d, tn), _F32),
            pltpu.VMEM((te, tn), _F32),
            pltpu.VMEM((te, tn), _BF16),
        ],
        compiler_params=_params("arbitrary", "arbitrary"),
        name="peer_experts",
    )(h2, x1, mod_l, u, u, vt, cntt, e1t, e2t, rkt)


def _final_kernel(x_ref, g_ref, o_ref):
    o_ref[...] = _rms(x_ref[...], g_ref[...])


def _final_norm(xs, g, n_rows):
    d = xs.shape[1]
    tm = PROJ_TILE
    return pl.pallas_call(
        _final_kernel,
        out_shape=jax.ShapeDtypeStruct((n_rows, d), _F32),
        grid=(n_rows // tm,),
        in_specs=[pl.BlockSpec((tm, d), lambda t: (t, 0)), _full(g.shape)],
        out_specs=pl.BlockSpec((tm, d), lambda t: (t, 0)),
        compiler_params=_params("arbitrary"),
        name="final_norm",
    )(xs, g)


def _rope_tables(seq):
    rows = seq // GRID_W
    r = jnp.repeat(jnp.arange(rows, dtype=_F32), GRID_W)
    col = jnp.tile(jnp.arange(GRID_W, dtype=_F32), rows)

    def cos_sin(rot_dim):
        n_freq = rot_dim // 4
        inv = ROPE_BASE ** (-jnp.arange(n_freq, dtype=_F32) / n_freq)
        ang = jnp.concatenate([r[:, None] * inv, col[:, None] * inv], axis=-1)
        return jnp.cos(ang), jnp.sin(ang)

    ca, sa = cos_sin(MLA_ROPE)
    one = jnp.ones((seq, MLA_NOPE), _F32)
    zero = jnp.zeros_like(one)
    z16 = jnp.zeros_like(ca)
    pad1 = jnp.ones((seq, LANES - MLA_NOPE - MLA_ROPE), _F32)
    pad0 = jnp.zeros_like(pad1)
    tab_a = [jnp.concatenate([one, ca, ca, pad1], axis=1),
             jnp.concatenate([zero, -sa, z16, pad0], axis=1),
             jnp.concatenate([zero, z16, sa, pad0], axis=1)]
    ch, sh = cos_sin(HEAD_DIM)
    z32 = jnp.zeros_like(ch)
    tab_h = [jnp.concatenate([ch, ch, ch, ch], axis=1),
             jnp.concatenate([-sh, z32, -sh, z32], axis=1),
             jnp.concatenate([z32, sh, z32, sh], axis=1)]
    lat = jnp.concatenate(tab_a + tab_h, axis=1)
    ident = jnp.concatenate([jnp.ones((PROJ_TILE, LANES), _F32),
                             jnp.zeros((PROJ_TILE, 2 * LANES), _F32)] * 2, axis=1)
    return jnp.concatenate([lat, ident], axis=0)


def _pack_w_in(w_in, d):
    splits = np.cumsum([MLA_Q_RANK, MLA_KV_RANK, MLA_ROPE, 512, 128, 128, 512, 128, 128])
    cq, ckv, kr, qb, kb, vb, qw, kw, vw, gates = jnp.split(w_in, splits, axis=-1)
    lead = w_in.shape[:-1]
    kr_pad = jnp.concatenate([jnp.zeros(lead + (MLA_NOPE,), w_in.dtype), kr,
                              jnp.zeros(lead + (LANES - MLA_NOPE - MLA_ROPE,), w_in.dtype)], axis=-1)

    def dup(w):
        g0, g1 = w[..., :HEAD_DIM], w[..., HEAD_DIM:]
        return jnp.concatenate([g0, g0, g1, g1], axis=-1)

    return jnp.concatenate([cq, ckv, kr_pad, qb, dup(kb), dup(vb), qw, dup(kw), dup(vw), gates],
                           axis=-1).astype(_BF16)


def _pack_w_uq(w_uq):
    depth, rank, _ = w_uq.shape
    w = w_uq.reshape(depth, rank, HEADS, MLA_NOPE + MLA_ROPE)
    w = jnp.pad(w, ((0, 0), (0, 0), (0, 0), (0, LANES - MLA_NOPE - MLA_ROPE)))
    return w.reshape(depth, rank, HEADS * LANES).astype(_BF16)


def _pack_w_ukv(w_ukv):
    depth, rank, _ = w_ukv.shape
    w = w_ukv.reshape(depth, rank, HEADS, MLA_NOPE + HEAD_DIM)
    k = jnp.pad(w[..., :MLA_NOPE], ((0, 0), (0, 0), (0, 0), (0, LANES - MLA_NOPE)))
    v = w[..., MLA_NOPE:]
    return jnp.concatenate([k.reshape(depth, rank, HEADS * LANES),
                            v.reshape(depth, rank, HEADS * HEAD_DIM)], axis=-1).astype(_BF16)


def kernel(x, c, ctx, c_ctx, w_mod, b_mod, g_attn, g_ffn, w_in, g_cq, g_ckv, w_uq, w_ukv,
           g_qn, g_kn, sink, w_oa, w_ob, w_ow, w_o, w_pq, sub_keys, peer_u, peer_v, g_final):
    batch, seq, d = x.shape
    n_ctx = ctx.shape[1]
    depth = w_mod.shape[0]
    assert n_ctx == ROW_TILE and seq % PEER_TOKENS == 0 and (batch * n_ctx) % PEER_TOKENS == 0
    assert batch < 8 and seq % GRID_W == 0 and seq >= 2 * WIN_Q_TILE and seq % WIN_Q_TILE == 0
    assert seq % FLASH_Q_TILE == 0 and seq % FLASH_K_CHUNK == 0 and seq % PROJ_TILE == 0
    n_lat = batch * seq
    n_all = n_lat + batch * n_ctx
    ctx_blk0 = n_lat // ROW_TILE

    xs = jnp.concatenate([x.reshape(n_lat, d), ctx.reshape(batch * n_ctx, d)], axis=0)
    cc = jnp.concatenate([c, c_ctx[None], jnp.zeros((8 - batch - 1, d), _F32)], axis=0)
    mod = _modulation(cc, w_mod, b_mod).reshape(depth, 8, 1, N_MOD * d)

    rope = _rope_tables(seq)
    eye = np.kron(np.eye(2, dtype=np.float32), np.ones((HEAD_DIM, HEAD_DIM), np.float32))
    bd = jnp.asarray(eye, _BF16)
    win = _pack_w_in(w_in, d)
    wuq = _pack_w_uq(w_uq)
    wukv = _pack_w_ukv(w_ukv)
    gqn_t = jnp.tile(g_qn, (1, HEADS))[:, None, :]
    gkn_t = jnp.tile(g_kn, (1, 2 * LANES // HEAD_DIM))[:, None, :]
    sink_t = jnp.broadcast_to(sink[:, :, None], (depth, HEADS, LANES))
    woa, wob, wow, wo, wpq = (w.astype(_BF16) for w in (w_oa, w_ob, w_ow, w_o, w_pq))
    subk = sub_keys.reshape(depth, 2 * PEER_HEADS, PEER_KEYS, PEER_HALF).astype(_BF16)
    u_bf = peer_u.astype(_BF16)
    vt_bf = jnp.swapaxes(peer_v, 1, 2).astype(_BF16)

    geo_row = (n_lat // PROJ_TILE, seq // PROJ_TILE, batch)
    geo_peer = (n_lat // PEER_TOKENS, seq // PEER_TOKENS, batch)
    lat_seg = (seq, lambda b, i: b)
    ctx_seg = (n_ctx, lambda b, i: ctx_blk0 + b)
    nqw = seq // WIN_Q_TILE
    per_tile = WIN_Q_TILE // WINDOW
    edge_blocks = seq // WINDOW
    band_segs = [
        (WINDOW, lambda b, i: b * edge_blocks + jnp.maximum(per_tile * i - 1, 0)),
        (WIN_Q_TILE, lambda b, i: b * nqw + i),
        (WINDOW, lambda b, i: b * edge_blocks + jnp.minimum(per_tile * (i + 1), edge_blocks - 1)),
        ctx_seg,
    ]

    for l in range(depth):
        last = l == depth - 1
        qa, ka, va, qb, kb, vb, qw, kw, vw, gates = _inproj(
            xs, mod[l], g_attn[l][None], win[l], g_cq[l][None], g_ckv[l][None], wuq[l], wukv[l],
            gqn_t[l], gkn_t[l], rope, bd, geo=geo_row)
        dense = dict(batch=batch, seq=seq, n_ctx=n_ctx)
        n_rows = n_lat if last else n_all
        o_a = _flash_attention(qa, ka, va, wide=True, out_rows=n_rows, **dense)
        o_b = _flash_attention(qb, kb, vb, wide=False, out_rows=n_rows, **dense)
        o_w = _attention(qw, kw, vw, sink_t[l], wide=False, segs=band_segs, band=True,
                         nq=nqw, q_block0=0, batch=batch, tq=WIN_Q_TILE, out_rows=n_rows)
        if not last:
            cq = dict(nq=1, q_block0=ctx_blk0, batch=batch, segs=[ctx_seg], tq=n_ctx)
            o_a = _attention(qa, ka, va, None, wide=True, base2=True, into=o_a, **cq)
            o_b = _attention(qb, kb, vb, None, wide=False, base2=True, into=o_b, **cq)
            o_w = _attention(qw, kw, vw, sink_t[l], wide=False, into=o_w, **cq)
        x1, h2, s1t, s2t = _merge(xs, mod[l], o_a, o_b, o_w, gates, woa[l], wob[l], wow[l], wo[l],
                                  g_ffn[l][None], wpq[l], subk[l], n_rows=n_rows, geo=geo_row)
        e1t, cntt, e2t, rkt = _topk(s1t, s2t)
        xs = _peer(h2, x1, mod[l], u_bf[l], vt_bf[l], e1t, cntt, e2t, rkt, geo=geo_peer)
    out = _final_norm(xs, g_final[None], n_lat)
    return out.reshape(batch, seq, d)
```

```python
import functools
import math

import jax
import jax.numpy as jnp
import numpy as np
from jax import lax
from jax.experimental import pallas as pl
from jax.experimental.pallas import tpu as pltpu

GRID_W = 64
ROPE_BASE = 10000.0
NORM_EPS = 1e-6
NEG_INF = -1e30
LOG2E = math.log2(math.e)
WINDOW = 128
N_MOD = 6

HEADS = 8
HEAD_PAIRS = HEADS // 2
MLA_Q_RANK = 384
MLA_KV_RANK = 256
MLA_NOPE = 64
MLA_ROPE = 32
HEAD_DIM = 64
KV_HEADS = 2
LANES = 128

PEER_HEADS = 8
PEER_KEYS = 128
PEER_HALF = 128
PEER_TOPK = 16

ROW_TILE = 256
PROJ_TILE = 512
WIN_Q_TILE = 256
FLASH_Q_TILE = 1024
FLASH_K_CHUNK = 1024
PEER_TOKENS = 512
PEER_ROWS = 16
PEER_CHUNK = 32
PEER_A_PIECE = 512
TOPK_TOKENS = 256
VMEM_LIMIT = 56 * 1024 * 1024

OFF_CQ = 0
OFF_CKV = OFF_CQ + MLA_Q_RANK
OFF_KR = OFF_CKV + MLA_KV_RANK
OFF_QB = OFF_KR + LANES
OFF_KB = OFF_QB + HEADS * HEAD_DIM
OFF_VB = OFF_KB + 2 * LANES
OFF_QW = OFF_VB + 2 * LANES
OFF_KW = OFF_QW + HEADS * HEAD_DIM
OFF_VW = OFF_KW + 2 * LANES
OFF_GATES = OFF_VW + 2 * LANES

_NT = (((1,), (1,)), ((), ()))
_F32 = jnp.float32
_BF16 = jnp.bfloat16


def _params(*semantics):
    return pltpu.CompilerParams(dimension_semantics=semantics, vmem_limit_bytes=VMEM_LIMIT)


def _full(shape):
    return pl.BlockSpec(shape, lambda *_: (0,) * len(shape), pipeline_mode=pl.Buffered(1))


def _rms(x, g):
    return x * lax.rsqrt(jnp.mean(x * x, axis=-1, keepdims=True) + NORM_EPS) * g


def _rope(x, tabs, shift):
    cos, sin_lo, sin_hi = tabs
    outs = []
    for c in range(x.shape[1] // LANES):
        xc = x[:, c * LANES:(c + 1) * LANES]
        up = pltpu.roll(xc, LANES - shift, 1)
        dn = pltpu.roll(xc, shift, 1)
        outs.append(xc * cos + up * sin_lo + dn * sin_hi)
    return outs[0] if len(outs) == 1 else jnp.concatenate(outs, axis=1)


def _head_rms(x, bd, g):
    outs = []
    for c in range(x.shape[1] // LANES):
        xc = x[:, c * LANES:(c + 1) * LANES]
        sq = xc * xc
        hi = sq.astype(_BF16)
        lo = (sq - hi.astype(_F32)).astype(_BF16)
        ssq = (jnp.dot(hi, bd, preferred_element_type=_F32)
               + jnp.dot(lo, bd, preferred_element_type=_F32))
        outs.append(xc * lax.rsqrt(ssq * (1.0 / HEAD_DIM) + NORM_EPS))
    y = outs[0] if len(outs) == 1 else jnp.concatenate(outs, axis=1)
    return y * g


def _mod_kernel(c_ref, w_ref, b_ref, o_ref):
    c = c_ref[...]
    a = c / (1.0 + jnp.exp(-c))
    o_ref[0] = jnp.dot(a, w_ref[0], preferred_element_type=_F32,
                       precision=lax.Precision.HIGHEST) + b_ref[0]


def _modulation(cc, w_mod, b_mod):
    depth, d, cols = w_mod.shape
    tn = cols // 4
    return pl.pallas_call(
        _mod_kernel,
        out_shape=jax.ShapeDtypeStruct((depth, 8, cols), _F32),
        grid=(depth, cols // tn),
        in_specs=[
            pl.BlockSpec((8, d), lambda l, j: (0, 0)),
            pl.BlockSpec((1, d, tn), lambda l, j: (l, 0, j)),
            pl.BlockSpec((1, 1, tn), lambda l, j: (l, 0, j)),
        ],
        out_specs=pl.BlockSpec((1, 8, tn), lambda l, j: (l, 0, j)),
        compiler_params=_params("arbitrary", "arbitrary"),
        name="modulation",
    )(cc, w_mod, b_mod.reshape(depth, 1, cols))


def _inproj_kernel(x_ref, mod_ref, gattn_ref, win_ref, gcq_ref, gckv_ref, wuq_ref, wukv_ref,
                   gqn_ref, gkn_ref, rope_ref, bd_ref,
                   qa_ref, ka_ref, va_ref, qb_ref, kb_ref, vb_ref, qw_ref, kw_ref, vw_ref,
                   gates_ref, *, d, scale_a, scale_h):
    x = x_ref[...]
    sh1 = mod_ref[0, :, 0:d]
    sc1 = mod_ref[0, :, d:2 * d]
    h = (_rms(x, gattn_ref[...]) * (1.0 + sc1) + sh1).astype(_BF16)

    def proj(off, width):
        return jnp.dot(h, win_ref[:, off:off + width], preferred_element_type=_F32)

    rope_a = tuple(rope_ref[:, i * LANES:(i + 1) * LANES] for i in range(3))
    rope_h = tuple(rope_ref[:, i * LANES:(i + 1) * LANES] for i in range(3, 6))
    bd = bd_ref[...]

    cq = _rms(proj(OFF_CQ, MLA_Q_RANK), gcq_ref[...]).astype(_BF16)
    qa = jnp.dot(cq, wuq_ref[...], preferred_element_type=_F32)
    qa_ref[...] = (_rope(qa, rope_a, MLA_ROPE // 2) * (scale_a * LOG2E)).astype(_BF16)
    ckv = _rms(proj(OFF_CKV, MLA_KV_RANK), gckv_ref[...]).astype(_BF16)
    kva = jnp.dot(ckv, wukv_ref[...], preferred_element_type=_F32)
    kr = _rope(proj(OFF_KR, LANES), rope_a, MLA_ROPE // 2)
    ka = kva[:, :HEADS * LANES] + jnp.concatenate([kr] * HEADS, axis=1)
    ka_ref[...] = ka.astype(_BF16)
    va_ref[...] = kva[:, HEADS * LANES:].astype(_BF16)

    qb = _head_rms(proj(OFF_QB, HEADS * HEAD_DIM), bd, gqn_ref[...])
    qb_ref[...] = (_rope(qb, rope_h, HEAD_DIM // 2) * (scale_h * LOG2E)).astype(_BF16)
    kb = _head_rms(proj(OFF_KB, 2 * LANES), bd, gkn_ref[...])
    kb_ref[...] = _rope(kb, rope_h, HEAD_DIM // 2).astype(_BF16)
    vb_ref[...] = proj(OFF_VB, 2 * LANES).astype(_BF16)

    qw_ref[...] = (_rope(proj(OFF_QW, HEADS * HEAD_DIM), rope_h, HEAD_DIM // 2) * scale_h).astype(_BF16)
    kw_ref[...] = _rope(proj(OFF_KW, 2 * LANES), rope_h, HEAD_DIM // 2).astype(_BF16)
    vw_ref[...] = proj(OFF_VW, 2 * LANES).astype(_BF16)

    for k in range(3):
        g = proj(OFF_GATES + k * d, d)
        gates_ref[:, k * d:(k + 1) * d] = (1.0 / (1.0 + jnp.exp(-g))).astype(_BF16)


def _inproj(xs, mod_l, gattn, win, gcq, gckv, wuq, wukv, gqn_t, gkn_t, rope, bd, *, geo):
    n, d = xs.shape
    tm = PROJ_TILE
    lat_tiles, tiles_per_batch, batch = geo
    row = lambda t: (t, 0)

    def mod_idx(t):
        return (jnp.where(t < lat_tiles, t // tiles_per_batch, batch), 0, 0)

    def rope_idx(t):
        return (jnp.where(t < lat_tiles, t % tiles_per_batch, tiles_per_batch), 0)

    widths = (HEADS * LANES, HEADS * LANES, HEADS * HEAD_DIM, HEADS * HEAD_DIM, 2 * LANES, 2 * LANES,
              HEADS * HEAD_DIM, 2 * LANES, 2 * LANES, 3 * d)
    return pl.pallas_call(
        functools.partial(_inproj_kernel, d=d, scale_a=(MLA_NOPE + MLA_ROPE) ** -0.5,
                          scale_h=HEAD_DIM ** -0.5),
        out_shape=[jax.ShapeDtypeStruct((n, w), _BF16) for w in widths],
        grid=(n // tm,),
        in_specs=[
            pl.BlockSpec((tm, d), row),
            pl.BlockSpec((1, 1, N_MOD * d), mod_idx),
            _full(gattn.shape), _full(win.shape), _full(gcq.shape), _full(gckv.shape),
            _full(wuq.shape), _full(wukv.shape), _full(gqn_t.shape), _full(gkn_t.shape),
            pl.BlockSpec((tm, 6 * LANES), rope_idx),
            _full(bd.shape),
        ],
        out_specs=[pl.BlockSpec((tm, w), row) for w in widths],
        compiler_params=_params("arbitrary"),
        name="inproj",
    )(xs, mod_l, gattn, win, gcq, gckv, wuq, wukv, gqn_t, gkn_t, rope, bd)


def _attn_kernel(*refs, nseg, wide, has_sink, band, nq, base2, has_into):
    q_ref = refs[0]
    k_refs = refs[1:1 + nseg]
    v_refs = refs[1 + nseg:1 + 2 * nseg]
    sink_ref = refs[1 + 2 * nseg] if has_sink else None
    o_ref = refs[-1]
    pair = pl.program_id(1)
    qi = pl.program_id(2)
    tq = q_ref.shape[0]
    lane = lax.broadcasted_iota(jnp.int32, (tq, LANES), 1)

    masks = [None] * nseg
    if band:
        r = lax.broadcasted_iota(jnp.int32, (tq, WINDOW), 0)
        c = lax.broadcasted_iota(jnp.int32, (tq, WINDOW), 1)
        masks[0] = c >= r + jnp.where(qi > 0, 0, WINDOW)
        masks[2] = c <= r - (tq - WINDOW) - jnp.where(qi < nq - 1, 0, tq)
        r = lax.broadcasted_iota(jnp.int32, (tq, tq), 0)
        c = lax.broadcasted_iota(jnp.int32, (tq, tq), 1)
        masks[1] = jnp.abs(r - c) <= WINDOW

    for pp in range(1 if wide else HEAD_PAIRS):
        qcols = slice(pp * LANES, (pp + 1) * LANES)
        kvcols = slice((pp // 2) * LANES, (pp // 2 + 1) * LANES)
        outs = []
        for hh in range(2):
            if wide:
                q = q_ref[:, hh * LANES:(hh + 1) * LANES]
                ks = [k[:, hh * LANES:(hh + 1) * LANES] for k in k_refs]
                vs = [v[...] for v in v_refs]
                head = 2 * pair + hh
            else:
                own = (lane[:1] < HEAD_DIM) if hh == 0 else (lane[:1] >= HEAD_DIM)
                q = q_ref[:, qcols] * own.astype(_F32).astype(_BF16)
                ks = [k[:, kvcols] for k in k_refs]
                vs = [v[:, kvcols] for v in v_refs]
                head = 2 * pp + hh
            ss = [lax.dot_general(q, k, _NT, preferred_element_type=_F32) for k in ks]
            ss = [s if m is None else jnp.where(m, s, NEG_INF) for s, m in zip(ss, masks)]
            m = functools.reduce(jnp.maximum, [jnp.max(s, axis=-1, keepdims=True) for s in ss])
            if has_sink:
                sink = sink_ref[pl.ds(head, 1), 0:1]
                m = jnp.maximum(m, sink)
            ps = [(jnp.exp2 if base2 else jnp.exp)(s - m) for s in ss]
            l = functools.reduce(jnp.add, [jnp.sum(p, axis=-1, keepdims=True) for p in ps])
            if has_sink:
                l = l + jnp.exp(sink - m)
            o = functools.reduce(jnp.add, [jnp.dot(p.astype(_BF16), v, preferred_element_type=_F32)
                                           for p, v in zip(ps, vs)])
            outs.append(o / l)
        o_ref[:, qcols] = jnp.where(lane < HEAD_DIM, outs[0], outs[1]).astype(o_ref.dtype)


def _flash_kernel(q_ref, k_ref, kc_ref, v_ref, vc_ref, o_ref, *, wide, chunk):
    tq = q_ref.shape[0]
    lane = lax.broadcasted_iota(jnp.int32, (1, LANES), 1)
    qs, kcols = [], []
    for hh in range(2):
        if wide:
            qs.append(q_ref[:, hh * LANES:(hh + 1) * LANES])
            kcols.append(slice(hh * LANES, (hh + 1) * LANES))
        else:
            own = (lane < HEAD_DIM) if hh == 0 else (lane >= HEAD_DIM)
            qs.append(q_ref[...] * own.astype(_F32).astype(_BF16))
            kcols.append(slice(0, LANES))
    m = [jnp.full((tq, 1), -jnp.inf, _F32) for _ in range(2)]
    l = [jnp.zeros((tq, 1), _F32) for _ in range(2)]
    acc = [jnp.zeros((tq, LANES), _F32) for _ in range(2)]
    pieces = [(k_ref, v_ref, r0, chunk) for r0 in range(0, k_ref.shape[0], chunk)]
    pieces.append((kc_ref, vc_ref, 0, kc_ref.shape[0]))
    for kr, vr, r0, rows in pieces:
        for hh in range(2):
            s = lax.dot_general(qs[hh], kr[r0:r0 + rows, kcols[hh]], _NT,
                                preferred_element_type=_F32)
            m_new = jnp.maximum(m[hh], jnp.max(s, axis=-1, keepdims=True))
            alpha = jnp.exp2(m[hh] - m_new)
            p = jnp.exp2(s - m_new)
            l[hh] = alpha * l[hh] + jnp.sum(p, axis=-1, keepdims=True)
            acc[hh] = alpha * acc[hh] + jnp.dot(p.astype(_BF16), vr[r0:r0 + rows, :],
                                                preferred_element_type=_F32)
            m[hh] = m_new
    o_ref[...] = jnp.where(lane < HEAD_DIM, acc[0] / l[0], acc[1] / l[1]).astype(o_ref.dtype)


def _flash_attention(q, k, v, *, wide, batch, seq, n_ctx, out_rows):
    tq = FLASH_Q_TILE
    nq = seq // tq
    qw = 2 * LANES if wide else LANES
    kdiv = 1 if wide else 2
    ctx_blk0 = batch * seq // n_ctx
    return pl.pallas_call(
        functools.partial(_flash_kernel, wide=wide, chunk=FLASH_K_CHUNK),
        out_shape=jax.ShapeDtypeStruct((out_rows, HEADS * HEAD_DIM), _BF16),
        grid=(batch, HEAD_PAIRS, nq),
        in_specs=[
            pl.BlockSpec((tq, qw), lambda b, p, i: (b * nq + i, p)),
            pl.BlockSpec((seq, qw), lambda b, p, i: (b, p // kdiv)),
            pl.BlockSpec((n_ctx, qw), lambda b, p, i: (ctx_blk0 + b, p // kdiv)),
            pl.BlockSpec((seq, LANES), lambda b, p, i: (b, p // kdiv)),
            pl.BlockSpec((n_ctx, LANES), lambda b, p, i: (ctx_blk0 + b, p // kdiv)),
        ],
        out_specs=pl.BlockSpec((tq, LANES), lambda b, p, i: (b * nq + i, p)),
        compiler_params=_params("arbitrary", "arbitrary", "arbitrary"),
        name="flash_attention",
    )(q, k, k, v, v)


def _attention(q, k, v, sink, *, wide, segs, nq, q_block0, batch, tq, band=False, base2=False,
               out_rows=None, into=None):
    pairs = HEAD_PAIRS if wide else 1
    qw, kw, vw, ow = ((2 * LANES, 2 * LANES, LANES, LANES) if wide
                      else (HEADS * HEAD_DIM, KV_HEADS * LANES, KV_HEADS * LANES, HEADS * HEAD_DIM))
    in_specs = [pl.BlockSpec((tq, qw), lambda b, p, i: (q_block0 + b * nq + i, p))]
    for width in (kw, vw):
        for rows, fn in segs:
            in_specs.append(pl.BlockSpec((rows, width), functools.partial(
                lambda b, p, i, fn: (fn(b, i), p), fn=fn)))
    args = [q] + [k] * len(segs) + [v] * len(segs)
    if sink is not None:
        in_specs.append(_full(sink.shape))
        args.append(sink)
    aliases = {}
    out0 = 0
    if into is not None:
        in_specs.append(pl.BlockSpec(memory_space=pl.ANY))
        args.append(into)
        aliases = {len(args) - 1: 0}
        out_rows = into.shape[0]
        out0 = q_block0
    return pl.pallas_call(
        functools.partial(_attn_kernel, nseg=len(segs), wide=wide, has_sink=sink is not None,
                          band=band, nq=nq, base2=base2, has_into=into is not None),
        out_shape=jax.ShapeDtypeStruct((out_rows or batch * nq * tq, HEADS * HEAD_DIM), _BF16),
        grid=(batch, pairs, nq),
        in_specs=in_specs,
        out_specs=pl.BlockSpec((tq, ow), lambda b, p, i: (out0 + b * nq + i, p)),
        input_output_aliases=aliases,
        compiler_params=_params("arbitrary", "arbitrary", "arbitrary"),
        name="attention",
    )(*args)


def _merge_kernel(x_ref, mod_ref, oa_ref, ob_ref, ow_ref, gates_ref, woa_ref, wob_ref, wow_ref,
                  wo_ref, gffn_ref, wpq_ref, subk_ref,
                  x1_ref, h2_ref, s1_ref, s2_ref, *, d):
    gt1 = mod_ref[0, :, 2 * d:3 * d]
    sh2 = mod_ref[0, :, 3 * d:4 * d]
    sc2 = mod_ref[0, :, 4 * d:5 * d]
    m = None
    for k, (o_ref, w_ref) in enumerate(((oa_ref, woa_ref), (ob_ref, wob_ref), (ow_ref, wow_ref))):
        t = gates_ref[:, k * d:(k + 1) * d].astype(_F32) * jnp.dot(
            o_ref[...], w_ref[...], preferred_element_type=_F32)
        m = t if m is None else m + t
    y = jnp.dot(m.astype(_BF16), wo_ref[...], preferred_element_type=_F32)
    x1 = x_ref[...] + gt1 * y
    x1_ref[...] = x1
    h2 = (_rms(x1, gffn_ref[...]) * (1.0 + sc2) + sh2).astype(_BF16)
    h2_ref[...] = h2
    q = jnp.dot(h2, wpq_ref[...], preferred_element_type=_F32).astype(_BF16)
    for g in range(2 * PEER_HEADS):
        s = lax.dot_general(subk_ref[g], q[:, g * PEER_HALF:(g + 1) * PEER_HALF], _NT,
                            preferred_element_type=_F32)
        if g % 2 == 0:
            s1_ref[g // 2] = s
        else:
            s2_ref[g // 2] = s


def _merge(xs, mod_l, oa, ob, ow, gates, woa, wob, wow, wo, gffn, wpq, subk, *, n_rows, geo):
    d = xs.shape[1]
    tm = PROJ_TILE
    lat_tiles, tiles_per_batch, batch = geo
    row = lambda t: (t, 0)

    def mod_idx(t):
        return (jnp.where(t < lat_tiles, t // tiles_per_batch, batch), 0, 0)

    return pl.pallas_call(
        functools.partial(_merge_kernel, d=d),
        out_shape=[
            jax.ShapeDtypeStruct((n_rows, d), _F32),
            jax.ShapeDtypeStruct((n_rows, d), _BF16),
            jax.ShapeDtypeStruct((PEER_HEADS, PEER_KEYS, n_rows), _F32),
            jax.ShapeDtypeStruct((PEER_HEADS, PEER_KEYS, n_rows), _F32),
        ],
        grid=(n_rows // tm,),
        in_specs=[
            pl.BlockSpec((tm, d), row),
            pl.BlockSpec((1, 1, N_MOD * d), mod_idx),
            pl.BlockSpec((tm, HEADS * HEAD_DIM), row),
            pl.BlockSpec((tm, HEADS * HEAD_DIM), row),
            pl.BlockSpec((tm, HEADS * HEAD_DIM), row),
            pl.BlockSpec((tm, 3 * d), row),
            _full(woa.shape), _full(wob.shape), _full(wow.shape), _full(wo.shape),
            _full(gffn.shape), _full(wpq.shape), _full(subk.shape),
        ],
        out_specs=[
            pl.BlockSpec((tm, d), row),
            pl.BlockSpec((tm, d), row),
            pl.BlockSpec((PEER_HEADS, PEER_KEYS, tm), lambda t: (0, 0, t)),
            pl.BlockSpec((PEER_HEADS, PEER_KEYS, tm), lambda t: (0, 0, t)),
        ],
        compiler_params=_params("arbitrary"),
        name="merge",
    )(xs, mod_l, oa, ob, ow, gates, woa, wob, wow, wo, gffn, wpq, subk)


NO_RANK = 64.0


def _top_values(w, k, want_rank=False):
    vals = []
    rank = jnp.full(w.shape, NO_RANK, _F32) if want_rank else None
    for r in range(k):
        m = jnp.max(w, axis=0, keepdims=True)
        vals.append(m)
        hit = w == m
        if want_rank:
            rank = jnp.where(hit, float(r), rank)
        w = jnp.where(hit, -jnp.inf, w)
    return (vals, rank) if want_rank else vals


def _stack_rows(vals, n_rows):
    rows = lax.broadcasted_iota(jnp.int32, (n_rows, vals[0].shape[1]), 0)
    out = jnp.full(rows.shape, -jnp.inf, _F32)
    for k, v in enumerate(vals):
        out = jnp.where(rows == k, v, out)
    return out


def _topk_kernel(s1_ref, s2_ref, e1_ref, cnt_ref, e2_ref, rk_ref):
    n_top = PEER_TOPK + 1

    def head(h, carry):
        s1 = s1_ref[h]
        s2 = s2_ref[h]
        t1 = _top_values(s1, n_top)
        t2, rank2 = _top_values(s2, n_top, want_rank=True)
        t2c = _stack_rows(t2, 24)
        cand = [t1[0] + t2c]
        cand += [t1[a] + t2c[:8] for a in range(1, 8)]
        cand += [_stack_rows(t1[8:], 16) + t2[0]]
        best = _top_values(jnp.concatenate(cand, axis=0), n_top)
        z = functools.reduce(jnp.add, [jnp.exp(v - best[0]) for v in best[:PEER_TOPK]])
        tau = 0.5 * (best[PEER_TOPK - 1] + best[PEER_TOPK])
        need = tau - s1
        cnt = jnp.zeros(s1.shape, _F32)
        for b in range(PEER_TOPK):
            cnt = jnp.where(t2[b] >= need, float(b + 1), cnt)
        e1_ref[h] = jnp.exp(s1 - t1[0]) / z
        cnt_ref[h] = cnt
        e2_ref[h] = pltpu.bitcast(jnp.exp(s2 - t2[0]).astype(_BF16), jnp.uint32)
        rk_ref[h] = pltpu.bitcast(rank2.astype(_BF16), jnp.uint32)
        return carry

    lax.fori_loop(0, PEER_HEADS, head, 0)


def _topk(s1t, s2t):
    n = s1t.shape[2]
    tk = TOPK_TOKENS
    blk = pl.BlockSpec((PEER_HEADS, PEER_KEYS, tk), lambda t: (0, 0, t))
    packed = pl.BlockSpec((PEER_HEADS, PEER_KEYS // 2, tk), lambda t: (0, 0, t))
    return pl.pallas_call(
        _topk_kernel,
        out_shape=[jax.ShapeDtypeStruct(s1t.shape, _F32)] * 2
        + [jax.ShapeDtypeStruct((PEER_HEADS, PEER_KEYS // 2, n), jnp.uint32)] * 2,
        grid=(n // tk,),
        in_specs=[blk, blk],
        out_specs=[blk, blk, packed, packed],
        compiler_params=_params("arbitrary"),
        name="peer_topk",
    )(s1t, s2t)


def _gate_rows(a_ref, cnt_ref, e1_ref, e2_ref, rk_ref, w_ref, first_keys, lane_tiles):
    shape = (PEER_CHUNK, LANES)
    for ii in first_keys:
        for lt in lane_tiles:
            ls = slice(lt * LANES, (lt + 1) * LANES)
            cntb = [jnp.broadcast_to(cnt_ref[h, ii:ii + 1, ls], shape).astype(_BF16)
                    for h in range(PEER_HEADS)]
            e1b = [jnp.broadcast_to(e1_ref[h, ii:ii + 1, ls], shape).astype(_BF16)
                   for h in range(PEER_HEADS)]
            for jc in range(PEER_KEYS // PEER_CHUNK):
                rows = slice(jc * PEER_CHUNK, (jc + 1) * PEER_CHUNK)
                words = slice(rows.start // 2, rows.stop // 2)
                g = None
                for h in range(PEER_HEADS):
                    e2 = pltpu.bitcast(e2_ref[h, words, ls], _BF16)
                    rk = pltpu.bitcast(rk_ref[h, words, ls], _BF16)
                    t = e1b[h] * jnp.where(rk < cntb[h], e2, jnp.zeros_like(e2))
                    g = t if g is None else g + t
                erows = slice(ii * PEER_KEYS + rows.start, ii * PEER_KEYS + rows.stop)
                a = a_ref[erows, ls]
                act = (0.5 * a) * (1.0 + lax.erf(a * (1.0 / math.sqrt(2.0))))
                w_ref[erows, ls] = g * act.astype(_BF16)


def _peer_kernel(h2n_ref, x1_ref, mod_ref, u0_ref, un_ref, vt_ref, cnt_ref, e1_ref, e2_ref, rk_ref,
                 o_ref, acc_ref, a_ref, w_ref, *, d, n_steps):
    k = pl.program_id(1)

    @pl.when(jnp.logical_and(pl.program_id(0) == 0, k == 0))
    def _():
        a_ref[...] = lax.dot_general(u0_ref[...], h2n_ref[...], _NT, preferred_element_type=_F32)

    @pl.when(k == 0)
    def _():
        acc_ref[...] = jnp.zeros_like(acc_ref)

    half = a_ref.shape[1] // 2
    for hv in range(2):
        lanes = slice(hv * half, (hv + 1) * half)
        _gate_rows(a_ref, cnt_ref, e1_ref, e2_ref, rk_ref, w_ref, range(PEER_ROWS),
                   range(hv * half // LANES, (hv + 1) * half // LANES))
        acc_ref[:, lanes] += jnp.dot(vt_ref[...], w_ref[:, lanes], preferred_element_type=_F32)
        a_ref[:, lanes] = lax.dot_general(un_ref[...], h2n_ref[lanes, :], _NT,
                                          preferred_element_type=_F32)

    @pl.when(k == n_steps - 1)
    def _():
        gt2 = mod_ref[0, :, 5 * d:6 * d]
        o_ref[...] = x1_ref[...] + gt2 * acc_ref[...].T


def _peer(h2, x1, mod_l, u, vt, e1t, cntt, e2t, rkt, *, geo):
    n, d = h2.shape
    tn = PEER_TOKENS
    te = PEER_ROWS * PEER_KEYS
    n_steps = u.shape[0] // te
    n_tok = n // tn
    lat_tiles, tiles_per_batch, batch = geo
    row = lambda t, k: (t, 0)

    def mod_idx(t, k):
        return (jnp.where(t < lat_tiles, t // tiles_per_batch, batch), 0, 0)

    rows_blk = pl.BlockSpec((PEER_HEADS, PEER_ROWS, tn), lambda t, k: (0, k, t))
    return pl.pallas_call(
        functools.partial(_peer_kernel, d=d, n_steps=n_steps),
        out_shape=jax.ShapeDtypeStruct((n, d), _F32),
        grid=(n // tn, n_steps),
        in_specs=[
            pl.BlockSpec((tn, d), lambda t, k: (jnp.minimum(t + (k + 1) // n_steps, n_tok - 1), 0)),
            pl.BlockSpec((tn, d), row),
            pl.BlockSpec((1, 1, N_MOD * d), mod_idx),
            pl.BlockSpec((te, d), lambda t, k: (0, 0)),
            pl.BlockSpec((te, d), lambda t, k: ((k + 1) % n_steps, 0)),
            pl.BlockSpec((d, te), lambda t, k: (0, k)),
            rows_blk,
            rows_blk,
            pl.BlockSpec((PEER_HEADS, PEER_KEYS // 2, tn), lambda t, k: (0, 0, t)),
            pl.BlockSpec((PEER_HEADS, PEER_KEYS // 2, tn), lambda t, k: (0, 0, t)),
        ],
        out_specs=pl.BlockSpec((tn, d), row),
        scratch_shapes=[
            pltpu.VMEM((d, tn), _F32),
            pltpu.VMEM((te, tn), _F32),
            pltpu.VMEM((te, tn), _BF16),
        ],
        compiler_params=_params("arbitrary", "arbitrary"),
        name="peer_experts",
    )(h2, x1, mod_l, u, u, vt, cntt, e1t, e2t, rkt)


def _final_kernel(x_ref, g_ref, o_ref):
    o_ref[...] = _rms(x_ref[...], g_ref[...])


def _final_norm(xs, g, n_rows):
    d = xs.shape[1]
    tm = PROJ_TILE
    return pl.pallas_call(
        _final_kernel,
        out_shape=jax.ShapeDtypeStruct((n_rows, d), _F32),
        grid=(n_rows // tm,),
        in_specs=[pl.BlockSpec((tm, d), lambda t: (t, 0)), _full(g.shape)],
        out_specs=pl.BlockSpec((tm, d), lambda t: (t, 0)),
        compiler_params=_params("arbitrary"),
        name="final_norm",
    )(xs, g)


def _rope_tables(seq):
    rows = seq // GRID_W
    r = jnp.repeat(jnp.arange(rows, dtype=_F32), GRID_W)
    col = jnp.tile(jnp.arange(GRID_W, dtype=_F32), rows)

    def cos_sin(rot_dim):
        n_freq = rot_dim // 4
        inv = ROPE_BASE ** (-jnp.arange(n_freq, dtype=_F32) / n_freq)
        ang = jnp.concatenate([r[:, None] * inv, col[:, None] * inv], axis=-1)
        return jnp.cos(ang), jnp.sin(ang)

    ca, sa = cos_sin(MLA_ROPE)
    one = jnp.ones((seq, MLA_NOPE), _F32)
    zero = jnp.zeros_like(one)
    z16 = jnp.zeros_like(ca)
    pad1 = jnp.ones((seq, LANES - MLA_NOPE - MLA_ROPE), _F32)
    pad0 = jnp.zeros_like(pad1)
    tab_a = [jnp.concatenate([one, ca, ca, pad1], axis=1),
             jnp.concatenate([zero, -sa, z16, pad0], axis=1),
             jnp.concatenate([zero, z16, sa, pad0], axis=1)]
    ch, sh = cos_sin(HEAD_DIM)
    z32 = jnp.zeros_like(ch)
    tab_h = [jnp.concatenate([ch, ch, ch, ch], axis=1),
             jnp.concatenate([-sh, z32, -sh, z32], axis=1),
             jnp.concatenate([z32, sh, z32, sh], axis=1)]
    lat = jnp.concatenate(tab_a + tab_h, axis=1)
    ident = jnp.concatenate([jnp.ones((PROJ_TILE, LANES), _F32),
                             jnp.zeros((PROJ_TILE, 2 * LANES), _F32)] * 2, axis=1)
    return jnp.concatenate([lat, ident], axis=0)


def _pack_w_in(w_in, d):
    splits = np.cumsum([MLA_Q_RANK, MLA_KV_RANK, MLA_ROPE, 512, 128, 128, 512, 128, 128])
    cq, ckv, kr, qb, kb, vb, qw, kw, vw, gates = jnp.split(w_in, splits, axis=-1)
    lead = w_in.shape[:-1]
    kr_pad = jnp.concatenate([jnp.zeros(lead + (MLA_NOPE,), w_in.dtype), kr,
                              jnp.zeros(lead + (LANES - MLA_NOPE - MLA_ROPE,), w_in.dtype)], axis=-1)

    def dup(w):
        g0, g1 = w[..., :HEAD_DIM], w[..., HEAD_DIM:]
        return jnp.concatenate([g0, g0, g1, g1], axis=-1)

    return jnp.concatenate([cq, ckv, kr_pad, qb, dup(kb), dup(vb), qw, dup(kw), dup(vw), gates],
                           axis=-1).astype(_BF16)


def _pack_w_uq(w_uq):
    depth, rank, _ = w_uq.shape
    w = w_uq.reshape(depth, rank, HEADS, MLA_NOPE + MLA_ROPE)
    w = jnp.pad(w, ((0, 0), (0, 0), (0, 0), (0, LANES - MLA_NOPE - MLA_ROPE)))
    return w.reshape(depth, rank, HEADS * LANES).astype(_BF16)


def _pack_w_ukv(w_ukv):
    depth, rank, _ = w_ukv.shape
    w = w_ukv.reshape(depth, rank, HEADS, MLA_NOPE + HEAD_DIM)
    k = jnp.pad(w[..., :MLA_NOPE], ((0, 0), (0, 0), (0, 0), (0, LANES - MLA_NOPE)))
    v = w[..., MLA_NOPE:]
    return jnp.concatenate([k.reshape(depth, rank, HEADS * LANES),
                            v.reshape(depth, rank, HEADS * HEAD_DIM)], axis=-1).astype(_BF16)


def kernel(x, c, ctx, c_ctx, w_mod, b_mod, g_attn, g_ffn, w_in, g_cq, g_ckv, w_uq, w_ukv,
           g_qn, g_kn, sink, w_oa, w_ob, w_ow, w_o, w_pq, sub_keys, peer_u, peer_v, g_final):
    batch, seq, d = x.shape
    n_ctx = ctx.shape[1]
    depth = w_mod.shape[0]
    assert n_ctx == ROW_TILE and seq % PEER_TOKENS == 0 and (batch * n_ctx) % PEER_TOKENS == 0
    assert batch < 8 and seq % GRID_W == 0 and seq >= 2 * WIN_Q_TILE and seq % WIN_Q_TILE == 0
    assert seq % FLASH_Q_TILE == 0 and seq % FLASH_K_CHUNK == 0 and seq % PROJ_TILE == 0
    n_lat = batch * seq
    n_all = n_lat + batch * n_ctx
    ctx_blk0 = n_lat // ROW_TILE

    xs = jnp.concatenate([x.reshape(n_lat, d), ctx.reshape(batch * n_ctx, d)], axis=0)
    cc = jnp.concatenate([c, c_ctx[None], jnp.zeros((8 - batch - 1, d), _F32)], axis=0)
    mod = _modulation(cc, w_mod, b_mod).reshape(depth, 8, 1, N_MOD * d)

    rope = _rope_tables(seq)
    eye = np.kron(np.eye(2, dtype=np.float32), np.ones((HEAD_DIM, HEAD_DIM), np.float32))
    bd = jnp.asarray(eye, _BF16)
    win = _pack_w_in(w_in, d)
    wuq = _pack_w_uq(w_uq)
    wukv = _pack_w_ukv(w_ukv)
    gqn_t = jnp.tile(g_qn, (1, HEADS))[:, None, :]
    gkn_t = jnp.tile(g_kn, (1, 2 * LANES // HEAD_DIM))[:, None, :]
    sink_t = jnp.broadcast_to(sink[:, :, None], (depth, HEADS, LANES))
    woa, wob, wow, wo, wpq = (w.astype(_BF16) for w in (w_oa, w_ob, w_ow, w_o, w_pq))
    subk = sub_keys.reshape(depth, 2 * PEER_HEADS, PEER_KEYS, PEER_HALF).astype(_BF16)
    u_bf = peer_u.astype(_BF16)
    vt_bf = jnp.swapaxes(peer_v, 1, 2).astype(_BF16)

    geo_row = (n_lat // PROJ_TILE, seq // PROJ_TILE, batch)
    geo_peer = (n_lat // PEER_TOKENS, seq // PEER_TOKENS, batch)
    lat_seg = (seq, lambda b, i: b)
    ctx_seg = (n_ctx, lambda b, i: ctx_blk0 + b)
    nqw = seq // WIN_Q_TILE
    per_tile = WIN_Q_TILE // WINDOW
    edge_blocks = seq // WINDOW
    band_segs = [
        (WINDOW, lambda b, i: b * edge_blocks + jnp.maximum(per_tile * i - 1, 0)),
        (WIN_Q_TILE, lambda b, i: b * nqw + i),
        (WINDOW, lambda b, i: b * edge_blocks + jnp.minimum(per_tile * (i + 1), edge_blocks - 1)),
        ctx_seg,
    ]

    for l in range(depth):
        last = l == depth - 1
        qa, ka, va, qb, kb, vb, qw, kw, vw, gates = _inproj(
            xs, mod[l], g_attn[l][None], win[l], g_cq[l][None], g_ckv[l][None], wuq[l], wukv[l],
            gqn_t[l], gkn_t[l], rope, bd, geo=geo_row)
        dense = dict(batch=batch, seq=seq, n_ctx=n_ctx)
        n_rows = n_lat if last else n_all
        o_a = _flash_attention(qa, ka, va, wide=True, out_rows=n_rows, **dense)
        o_b = _flash_attention(qb, kb, vb, wide=False, out_rows=n_rows, **dense)
        o_w = _attention(qw, kw, vw, sink_t[l], wide=False, segs=band_segs, band=True,
                         nq=nqw, q_block0=0, batch=batch, tq=WIN_Q_TILE, out_rows=n_rows)
        if not last:
            cq = dict(nq=1, q_block0=ctx_blk0, batch=batch, segs=[ctx_seg], tq=n_ctx)
            o_a = _attention(qa, ka, va, None, wide=True, base2=True, into=o_a, **cq)
            o_b = _attention(qb, kb, vb, None, wide=False, base2=True, into=o_b, **cq)
            o_w = _attention(qw, kw, vw, sink_t[l], wide=False, into=o_w, **cq)
        x1, h2, s1t, s2t = _merge(xs, mod[l], o_a, o_b, o_w, gates, woa[l], wob[l], wow[l], wo[l],
                                  g_ffn[l][None], wpq[l], subk[l], n_rows=n_rows, geo=geo_row)
        e1t, cntt, e2t, rkt = _topk(s1t, s2t)
        xs = _peer(h2, x1, mod[l], u_bf[l], vt_bf[l], e1t, cntt, e2t, rkt, geo=geo_peer)
    out = _final_norm(xs, g_final[None], n_lat)
    return out.reshape(batch, seq, d)
```

```python
import functools
import math

import jax
import jax.numpy as jnp
import numpy as np
from jax import lax
from jax.experimental import pallas as pl
from jax.experimental.pallas import tpu as pltpu

GRID_W = 64
ROPE_BASE = 10000.0
NORM_EPS = 1e-6
NEG_INF = -1e30
LOG2E = math.log2(math.e)
WINDOW = 128
N_MOD = 6

HEADS = 8
HEAD_PAIRS = HEADS // 2
MLA_Q_RANK = 384
MLA_KV_RANK = 256
MLA_NOPE = 64
MLA_ROPE = 32
HEAD_DIM = 64
KV_HEADS = 2
LANES = 128

PEER_HEADS = 8
PEER_KEYS = 128
PEER_HALF = 128
PEER_TOPK = 16

ROW_TILE = 256
PROJ_TILE = 512
WIN_Q_TILE = 256
FLASH_Q_TILE = 1024
FLASH_K_CHUNK = 4096
PEER_TOKENS = 512
PEER_ROWS = 16
PEER_CHUNK = 32
PEER_A_PIECE = 512
TOPK_TOKENS = 256
VMEM_LIMIT = 56 * 1024 * 1024

OFF_CQ = 0
OFF_CKV = OFF_CQ + MLA_Q_RANK
OFF_KR = OFF_CKV + MLA_KV_RANK
OFF_QB = OFF_KR + LANES
OFF_KB = OFF_QB + HEADS * HEAD_DIM
OFF_VB = OFF_KB + 2 * LANES
OFF_QW = OFF_VB + 2 * LANES
OFF_KW = OFF_QW + HEADS * HEAD_DIM
OFF_VW = OFF_KW + 2 * LANES
OFF_GATES = OFF_VW + 2 * LANES

_NT = (((1,), (1,)), ((), ()))
_F32 = jnp.float32
_BF16 = jnp.bfloat16


def _params(*semantics):
    return pltpu.CompilerParams(dimension_semantics=semantics, vmem_limit_bytes=VMEM_LIMIT)


def _full(shape):
    return pl.BlockSpec(shape, lambda *_: (0,) * len(shape), pipeline_mode=pl.Buffered(1))


def _rms(x, g):
    return x * lax.rsqrt(jnp.mean(x * x, axis=-1, keepdims=True) + NORM_EPS) * g


def _rope(x, tabs, shift):
    cos, sin_lo, sin_hi = tabs
    outs = []
    for c in range(x.shape[1] // LANES):
        xc = x[:, c * LANES:(c + 1) * LANES]
        up = pltpu.roll(xc, LANES - shift, 1)
        dn = pltpu.roll(xc, shift, 1)
        outs.append(xc * cos + up * sin_lo + dn * sin_hi)
    return outs[0] if len(outs) == 1 else jnp.concatenate(outs, axis=1)


def _head_rms(x, bd, g):
    outs = []
    for c in range(x.shape[1] // LANES):
        xc = x[:, c * LANES:(c + 1) * LANES]
        sq = xc * xc
        hi = sq.astype(_BF16)
        lo = (sq - hi.astype(_F32)).astype(_BF16)
        ssq = (jnp.dot(hi, bd, preferred_element_type=_F32)
               + jnp.dot(lo, bd, preferred_element_type=_F32))
        outs.append(xc * lax.rsqrt(ssq * (1.0 / HEAD_DIM) + NORM_EPS))
    y = outs[0] if len(outs) == 1 else jnp.concatenate(outs, axis=1)
    return y * g


def _mod_kernel(c_ref, w_ref, b_ref, o_ref):
    c = c_ref[...]
    a = c / (1.0 + jnp.exp(-c))
    o_ref[0] = jnp.dot(a, w_ref[0], preferred_element_type=_F32,
                       precision=lax.Precision.HIGHEST) + b_ref[0]


def _modulation(cc, w_mod, b_mod):
    depth, d, cols = w_mod.shape
    tn = cols // 4
    return pl.pallas_call(
        _mod_kernel,
        out_shape=jax.ShapeDtypeStruct((depth, 8, cols), _F32),
        grid=(depth, cols // tn),
        in_specs=[
            pl.BlockSpec((8, d), lambda l, j: (0, 0)),
            pl.BlockSpec((1, d, tn), lambda l, j: (l, 0, j)),
            pl.BlockSpec((1, 1, tn), lambda l, j: (l, 0, j)),
        ],
        out_specs=pl.BlockSpec((1, 8, tn), lambda l, j: (l, 0, j)),
        compiler_params=_params("arbitrary", "arbitrary"),
        name="modulation",
    )(cc, w_mod, b_mod.reshape(depth, 1, cols))


def _inproj_kernel(x_ref, mod_ref, gattn_ref, win_ref, gcq_ref, gckv_ref, wuq_ref, wukv_ref,
                   gqn_ref, gkn_ref, rope_ref, bd_ref,
                   qa_ref, ka_ref, va_ref, qb_ref, kb_ref, vb_ref, qw_ref, kw_ref, vw_ref,
                   gates_ref, *, d, scale_a, scale_h):
    x = x_ref[...]
    sh1 = mod_ref[0, :, 0:d]
    sc1 = mod_ref[0, :, d:2 * d]
    h = (_rms(x, gattn_ref[...]) * (1.0 + sc1) + sh1).astype(_BF16)

    def proj(off, width):
        return jnp.dot(h, win_ref[:, off:off + width], preferred_element_type=_F32)

    rope_a = tuple(rope_ref[:, i * LANES:(i + 1) * LANES] for i in range(3))
    rope_h = tuple(rope_ref[:, i * LANES:(i + 1) * LANES] for i in range(3, 6))
    bd = bd_ref[...]

    cq = _rms(proj(OFF_CQ, MLA_Q_RANK), gcq_ref[...]).astype(_BF16)
    qa = jnp.dot(cq, wuq_ref[...], preferred_element_type=_F32)
    qa_ref[...] = (_rope(qa, rope_a, MLA_ROPE // 2) * (scale_a * LOG2E)).astype(_BF16)
    ckv = _rms(proj(OFF_CKV, MLA_KV_RANK), gckv_ref[...]).astype(_BF16)
    kva = jnp.dot(ckv, wukv_ref[...], preferred_element_type=_F32)
    kr = _rope(proj(OFF_KR, LANES), rope_a, MLA_ROPE // 2)
    ka = kva[:, :HEADS * LANES] + jnp.concatenate([kr] * HEADS, axis=1)
    ka_ref[...] = ka.astype(_BF16)
    va_ref[...] = kva[:, HEADS * LANES:].astype(_BF16)

    qb = _head_rms(proj(OFF_QB, HEADS * HEAD_DIM), bd, gqn_ref[...])
    qb_ref[...] = (_rope(qb, rope_h, HEAD_DIM // 2) * (scale_h * LOG2E)).astype(_BF16)
    kb = _head_rms(proj(OFF_KB, 2 * LANES), bd, gkn_ref[...])
    kb_ref[...] = _rope(kb, rope_h, HEAD_DIM // 2).astype(_BF16)
    vb_ref[...] = proj(OFF_VB, 2 * LANES).astype(_BF16)

    qw_ref[...] = (_rope(proj(OFF_QW, HEADS * HEAD_DIM), rope_h, HEAD_DIM // 2) * scale_h).astype(_BF16)
    kw_ref[...] = _rope(proj(OFF_KW, 2 * LANES), rope_h, HEAD_DIM // 2).astype(_BF16)
    vw_ref[...] = proj(OFF_VW, 2 * LANES).astype(_BF16)

    for k in range(3):
        g = proj(OFF_GATES + k * d, d)
        gates_ref[:, k * d:(k + 1) * d] = (1.0 / (1.0 + jnp.exp(-g))).astype(_BF16)


def _inproj(xs, mod_l, gattn, win, gcq, gckv, wuq, wukv, gqn_t, gkn_t, rope, bd, *, geo):
    n, d = xs.shape
    tm = PROJ_TILE
    lat_tiles, tiles_per_batch, batch = geo
    row = lambda t: (t, 0)

    def mod_idx(t):
        return (jnp.where(t < lat_tiles, t // tiles_per_batch, batch), 0, 0)

    def rope_idx(t):
        return (jnp.where(t < lat_tiles, t % tiles_per_batch, tiles_per_batch), 0)

    widths = (HEADS * LANES, HEADS * LANES, HEADS * HEAD_DIM, HEADS * HEAD_DIM, 2 * LANES, 2 * LANES,
              HEADS * HEAD_DIM, 2 * LANES, 2 * LANES, 3 * d)
    return pl.pallas_call(
        functools.partial(_inproj_kernel, d=d, scale_a=(MLA_NOPE + MLA_ROPE) ** -0.5,
                          scale_h=HEAD_DIM ** -0.5),
        out_shape=[jax.ShapeDtypeStruct((n, w), _BF16) for w in widths],
        grid=(n // tm,),
        in_specs=[
            pl.BlockSpec((tm, d), row),
            pl.BlockSpec((1, 1, N_MOD * d), mod_idx),
            _full(gattn.shape), _full(win.shape), _full(gcq.shape), _full(gckv.shape),
            _full(wuq.shape), _full(wukv.shape), _full(gqn_t.shape), _full(gkn_t.shape),
            pl.BlockSpec((tm, 6 * LANES), rope_idx),
            _full(bd.shape),
        ],
        out_specs=[pl.BlockSpec((tm, w), row) for w in widths],
        compiler_params=_params("arbitrary"),
        name="inproj",
    )(xs, mod_l, gattn, win, gcq, gckv, wuq, wukv, gqn_t, gkn_t, rope, bd)


def _attn_kernel(*refs, nseg, wide, has_sink, band, nq, base2, has_into):
    q_ref = refs[0]
    k_refs = refs[1:1 + nseg]
    v_refs = refs[1 + nseg:1 + 2 * nseg]
    sink_ref = refs[1 + 2 * nseg] if has_sink else None
    o_ref = refs[-1]
    pair = pl.program_id(1)
    qi = pl.program_id(2)
    tq = q_ref.shape[0]
    lane = lax.broadcasted_iota(jnp.int32, (tq, LANES), 1)

    masks = [None] * nseg
    if band:
        r = lax.broadcasted_iota(jnp.int32, (tq, WINDOW), 0)
        c = lax.broadcasted_iota(jnp.int32, (tq, WINDOW), 1)
        masks[0] = c >= r + jnp.where(qi > 0, 0, WINDOW)
        masks[2] = c <= r - (tq - WINDOW) - jnp.where(qi < nq - 1, 0, tq)
        r = lax.broadcasted_iota(jnp.int32, (tq, tq), 0)
        c = lax.broadcasted_iota(jnp.int32, (tq, tq), 1)
        masks[1] = jnp.abs(r - c) <= WINDOW

    for pp in range(1 if wide else HEAD_PAIRS):
        qcols = slice(pp * LANES, (pp + 1) * LANES)
        kvcols = slice((pp // 2) * LANES, (pp // 2 + 1) * LANES)
        outs = []
        for hh in range(2):
            if wide:
                q = q_ref[:, hh * LANES:(hh + 1) * LANES]
                ks = [k[:, hh * LANES:(hh + 1) * LANES] for k in k_refs]
                vs = [v[...] for v in v_refs]
                head = 2 * pair + hh
            else:
                own = (lane[:1] < HEAD_DIM) if hh == 0 else (lane[:1] >= HEAD_DIM)
                q = q_ref[:, qcols] * own.astype(_F32).astype(_BF16)
                ks = [k[:, kvcols] for k in k_refs]
                vs = [v[:, kvcols] for v in v_refs]
                head = 2 * pp + hh
            ss = [lax.dot_general(q, k, _NT, preferred_element_type=_F32) for k in ks]
            ss = [s if m is None else jnp.where(m, s, NEG_INF) for s, m in zip(ss, masks)]
            m = functools.reduce(jnp.maximum, [jnp.max(s, axis=-1, keepdims=True) for s in ss])
            if has_sink:
                sink = sink_ref[pl.ds(head, 1), 0:1]
                m = jnp.maximum(m, sink)
            ps = [(jnp.exp2 if base2 else jnp.exp)(s - m) for s in ss]
            l = functools.reduce(jnp.add, [jnp.sum(p, axis=-1, keepdims=True) for p in ps])
            if has_sink:
                l = l + jnp.exp(sink - m)
            o = functools.reduce(jnp.add, [jnp.dot(p.astype(_BF16), v, preferred_element_type=_F32)
                                           for p, v in zip(ps, vs)])
            outs.append(o / l)
        o_ref[:, qcols] = jnp.where(lane < HEAD_DIM, outs[0], outs[1]).astype(o_ref.dtype)


def _flash_kernel(q_ref, k_ref, kc_ref, v_ref, vc_ref, o_ref, *, wide, chunk):
    tq = q_ref.shape[0]
    lane = lax.broadcasted_iota(jnp.int32, (1, LANES), 1)
    qs, kcols = [], []
    for hh in range(2):
        if wide:
            qs.append(q_ref[:, hh * LANES:(hh + 1) * LANES])
            kcols.append(slice(hh * LANES, (hh + 1) * LANES))
        else:
            own = (lane < HEAD_DIM) if hh == 0 else (lane >= HEAD_DIM)
            qs.append(q_ref[...] * own.astype(_F32).astype(_BF16))
            kcols.append(slice(0, LANES))
    m = [jnp.full((tq, 1), -jnp.inf, _F32) for _ in range(2)]
    l = [jnp.zeros((tq, 1), _F32) for _ in range(2)]
    acc = [jnp.zeros((tq, LANES), _F32) for _ in range(2)]
    pieces = [(k_ref, v_ref, r0, chunk) for r0 in range(0, k_ref.shape[0], chunk)]
    pieces.append((kc_ref, vc_ref, 0, kc_ref.shape[0]))
    for kr, vr, r0, rows in pieces:
        for hh in range(2):
            s = lax.dot_general(qs[hh], kr[r0:r0 + rows, kcols[hh]], _NT,
                                preferred_element_type=_F32)
            m_new = jnp.maximum(m[hh], jnp.max(s, axis=-1, keepdims=True))
            alpha = jnp.exp2(m[hh] - m_new)
            p = jnp.exp2(s - m_new)
            l[hh] = alpha * l[hh] + jnp.sum(p, axis=-1, keepdims=True)
            acc[hh] = alpha * acc[hh] + jnp.dot(p.astype(_BF16), vr[r0:r0 + rows, :],
                                                preferred_element_type=_F32)
            m[hh] = m_new
    o_ref[...] = jnp.where(lane < HEAD_DIM, acc[0] / l[0], acc[1] / l[1]).astype(o_ref.dtype)


def _flash_attention(q, k, v, *, wide, batch, seq, n_ctx, out_rows):
    tq = FLASH_Q_TILE
    nq = seq // tq
    qw = 2 * LANES if wide else LANES
    kdiv = 1 if wide else 2
    ctx_blk0 = batch * seq // n_ctx
    return pl.pallas_call(
        functools.partial(_flash_kernel, wide=wide, chunk=FLASH_K_CHUNK),
        out_shape=jax.ShapeDtypeStruct((out_rows, HEADS * HEAD_DIM), _BF16),
        grid=(batch, HEAD_PAIRS, nq),
        in_specs=[
            pl.BlockSpec((tq, qw), lambda b, p, i: (b * nq + i, p)),
            pl.BlockSpec((seq, qw), lambda b, p, i: (b, p // kdiv)),
            pl.BlockSpec((n_ctx, qw), lambda b, p, i: (ctx_blk0 + b, p // kdiv)),
            pl.BlockSpec((seq, LANES), lambda b, p, i: (b, p // kdiv)),
            pl.BlockSpec((n_ctx, LANES), lambda b, p, i: (ctx_blk0 + b, p // kdiv)),
        ],
        out_specs=pl.BlockSpec((tq, LANES), lambda b, p, i: (b * nq + i, p)),
        compiler_params=_params("arbitrary", "arbitrary", "arbitrary"),
        name="flash_attention",
    )(q, k, k, v, v)


def _attention(q, k, v, sink, *, wide, segs, nq, q_block0, batch, tq, band=False, base2=False,
               out_rows=None, into=None):
    pairs = HEAD_PAIRS if wide else 1
    qw, kw, vw, ow = ((2 * LANES, 2 * LANES, LANES, LANES) if wide
                      else (HEADS * HEAD_DIM, KV_HEADS * LANES, KV_HEADS * LANES, HEADS * HEAD_DIM))
    in_specs = [pl.BlockSpec((tq, qw), lambda b, p, i: (q_block0 + b * nq + i, p))]
    for width in (kw, vw):
        for rows, fn in segs:
            in_specs.append(pl.BlockSpec((rows, width), functools.partial(
                lambda b, p, i, fn: (fn(b, i), p), fn=fn)))
    args = [q] + [k] * len(segs) + [v] * len(segs)
    if sink is not None:
        in_specs.append(_full(sink.shape))
        args.append(sink)
    aliases = {}
    out0 = 0
    if into is not None:
        in_specs.append(pl.BlockSpec(memory_space=pl.ANY))
        args.append(into)
        aliases = {len(args) - 1: 0}
        out_rows = into.shape[0]
        out0 = q_block0
    return pl.pallas_call(
        functools.partial(_attn_kernel, nseg=len(segs), wide=wide, has_sink=sink is not None,
                          band=band, nq=nq, base2=base2, has_into=into is not None),
        out_shape=jax.ShapeDtypeStruct((out_rows or batch * nq * tq, HEADS * HEAD_DIM), _BF16),
        grid=(batch, pairs, nq),
        in_specs=in_specs,
        out_specs=pl.BlockSpec((tq, ow), lambda b, p, i: (out0 + b * nq + i, p)),
        input_output_aliases=aliases,
        compiler_params=_params("arbitrary", "arbitrary", "arbitrary"),
        name="attention",
    )(*args)


def _merge_kernel(x_ref, mod_ref, oa_ref, ob_ref, ow_ref, gates_ref, woa_ref, wob_ref, wow_ref,
                  wo_ref, gffn_ref, wpq_ref, subk_ref,
                  x1_ref, h2_ref, s1_ref, s2_ref, *, d):
    gt1 = mod_ref[0, :, 2 * d:3 * d]
    sh2 = mod_ref[0, :, 3 * d:4 * d]
    sc2 = mod_ref[0, :, 4 * d:5 * d]
    m = None
    for k, (o_ref, w_ref) in enumerate(((oa_ref, woa_ref), (ob_ref, wob_ref), (ow_ref, wow_ref))):
        t = gates_ref[:, k * d:(k + 1) * d].astype(_F32) * jnp.dot(
            o_ref[...], w_ref[...], preferred_element_type=_F32)
        m = t if m is None else m + t
    y = jnp.dot(m.astype(_BF16), wo_ref[...], preferred_element_type=_F32)
    x1 = x_ref[...] + gt1 * y
    x1_ref[...] = x1
    h2 = (_rms(x1, gffn_ref[...]) * (1.0 + sc2) + sh2).astype(_BF16)
    h2_ref[...] = h2
    q = jnp.dot(h2, wpq_ref[...], preferred_element_type=_F32).astype(_BF16)
    for g in range(2 * PEER_HEADS):
        s = lax.dot_general(subk_ref[g], q[:, g * PEER_HALF:(g + 1) * PEER_HALF], _NT,
                            preferred_element_type=_F32)
        if g % 2 == 0:
            s1_ref[g // 2] = s
        else:
            s2_ref[g // 2] = s


def _merge(xs, mod_l, oa, ob, ow, gates, woa, wob, wow, wo, gffn, wpq, subk, *, n_rows, geo):
    d = xs.shape[1]
    tm = PROJ_TILE
    lat_tiles, tiles_per_batch, batch = geo
    row = lambda t: (t, 0)

    def mod_idx(t):
        return (jnp.where(t < lat_tiles, t // tiles_per_batch, batch), 0, 0)

    return pl.pallas_call(
        functools.partial(_merge_kernel, d=d),
        out_shape=[
            jax.ShapeDtypeStruct((n_rows, d), _F32),
            jax.ShapeDtypeStruct((n_rows, d), _BF16),
            jax.ShapeDtypeStruct((PEER_HEADS, PEER_KEYS, n_rows), _F32),
            jax.ShapeDtypeStruct((PEER_HEADS, PEER_KEYS, n_rows), _F32),
        ],
        grid=(n_rows // tm,),
        in_specs=[
            pl.BlockSpec((tm, d), row),
            pl.BlockSpec((1, 1, N_MOD * d), mod_idx),
            pl.BlockSpec((tm, HEADS * HEAD_DIM), row),
            pl.BlockSpec((tm, HEADS * HEAD_DIM), row),
            pl.BlockSpec((tm, HEADS * HEAD_DIM), row),
            pl.BlockSpec((tm, 3 * d), row),
            _full(woa.shape), _full(wob.shape), _full(wow.shape), _full(wo.shape),
            _full(gffn.shape), _full(wpq.shape), _full(subk.shape),
        ],
        out_specs=[
            pl.BlockSpec((tm, d), row),
            pl.BlockSpec((tm, d), row),
            pl.BlockSpec((PEER_HEADS, PEER_KEYS, tm), lambda t: (0, 0, t)),
            pl.BlockSpec((PEER_HEADS, PEER_KEYS, tm), lambda t: (0, 0, t)),
        ],
        compiler_params=_params("arbitrary"),
        name="merge",
    )(xs, mod_l, oa, ob, ow, gates, woa, wob, wow, wo, gffn, wpq, subk)


NO_RANK = 64.0


def _top_values(w, k, want_rank=False):
    vals = []
    rank = jnp.full(w.shape, NO_RANK, _F32) if want_rank else None
    for r in range(k):
        m = jnp.max(w, axis=0, keepdims=True)
        vals.append(m)
        hit = w == m
        if want_rank:
            rank = jnp.where(hit, float(r), rank)
        w = jnp.where(hit, -jnp.inf, w)
    return (vals, rank) if want_rank else vals


def _stack_rows(vals, n_rows):
    rows = lax.broadcasted_iota(jnp.int32, (n_rows, vals[0].shape[1]), 0)
    out = jnp.full(rows.shape, -jnp.inf, _F32)
    for k, v in enumerate(vals):
        out = jnp.where(rows == k, v, out)
    return out


def _topk_kernel(s1_ref, s2_ref, e1_ref, cnt_ref, e2_ref, rk_ref):
    n_top = PEER_TOPK + 1

    def head(h, carry):
        s1 = s1_ref[h]
        s2 = s2_ref[h]
        t1 = _top_values(s1, n_top)
        t2, rank2 = _top_values(s2, n_top, want_rank=True)
        t2c = _stack_rows(t2, 24)
        cand = [t1[0] + t2c]
        cand += [t1[a] + t2c[:8] for a in range(1, 8)]
        cand += [_stack_rows(t1[8:], 16) + t2[0]]
        best = _top_values(jnp.concatenate(cand, axis=0), n_top)
        z = functools.reduce(jnp.add, [jnp.exp(v - best[0]) for v in best[:PEER_TOPK]])
        tau = 0.5 * (best[PEER_TOPK - 1] + best[PEER_TOPK])
        need = tau - s1
        cnt = jnp.zeros(s1.shape, _F32)
        for b in range(PEER_TOPK):
            cnt = jnp.where(t2[b] >= need, float(b + 1), cnt)
        e1_ref[h] = jnp.exp(s1 - t1[0]) / z
        cnt_ref[h] = cnt
        e2_ref[h] = pltpu.bitcast(jnp.exp(s2 - t2[0]).astype(_BF16), jnp.uint32)
        rk_ref[h] = pltpu.bitcast(rank2.astype(_BF16), jnp.uint32)
        return carry

    lax.fori_loop(0, PEER_HEADS, head, 0)


def _topk(s1t, s2t):
    n = s1t.shape[2]
    tk = TOPK_TOKENS
    blk = pl.BlockSpec((PEER_HEADS, PEER_KEYS, tk), lambda t: (0, 0, t))
    packed = pl.BlockSpec((PEER_HEADS, PEER_KEYS // 2, tk), lambda t: (0, 0, t))
    return pl.pallas_call(
        _topk_kernel,
        out_shape=[jax.ShapeDtypeStruct(s1t.shape, _F32)] * 2
        + [jax.ShapeDtypeStruct((PEER_HEADS, PEER_KEYS // 2, n), jnp.uint32)] * 2,
        grid=(n // tk,),
        in_specs=[blk, blk],
        out_specs=[blk, blk, packed, packed],
        compiler_params=_params("arbitrary"),
        name="peer_topk",
    )(s1t, s2t)


def _gate_rows(a_ref, cnt_ref, e1_ref, e2_ref, rk_ref, w_ref, first_keys, lane_tiles):
    shape = (PEER_CHUNK, LANES)
    for ii in first_keys:
        for lt in lane_tiles:
            ls = slice(lt * LANES, (lt + 1) * LANES)
            cntb = [jnp.broadcast_to(cnt_ref[h, ii:ii + 1, ls], shape).astype(_BF16)
                    for h in range(PEER_HEADS)]
            e1b = [jnp.broadcast_to(e1_ref[h, ii:ii + 1, ls], shape).astype(_BF16)
                   for h in range(PEER_HEADS)]
            for jc in range(PEER_KEYS // PEER_CHUNK):
                rows = slice(jc * PEER_CHUNK, (jc + 1) * PEER_CHUNK)
                words = slice(rows.start // 2, rows.stop // 2)
                g = None
                for h in range(PEER_HEADS):
                    e2 = pltpu.bitcast(e2_ref[h, words, ls], _BF16)
                    rk = pltpu.bitcast(rk_ref[h, words, ls], _BF16)
                    t = e1b[h] * jnp.where(rk < cntb[h], e2, jnp.zeros_like(e2))
                    g = t if g is None else g + t
                erows = slice(ii * PEER_KEYS + rows.start, ii * PEER_KEYS + rows.stop)
                a = a_ref[erows, ls]
                act = (0.5 * a) * (1.0 + lax.erf(a * (1.0 / math.sqrt(2.0))))
                w_ref[erows, ls] = g * act.astype(_BF16)


def _peer_kernel(h2n_ref, x1_ref, mod_ref, u0_ref, un_ref, vt_ref, cnt_ref, e1_ref, e2_ref, rk_ref,
                 o_ref, acc_ref, a_ref, w_ref, *, d, n_steps):
    k = pl.program_id(1)

    @pl.when(jnp.logical_and(pl.program_id(0) == 0, k == 0))
    def _():
        a_ref[...] = lax.dot_general(u0_ref[...], h2n_ref[...], _NT, preferred_element_type=_F32)

    @pl.when(k == 0)
    def _():
        acc_ref[...] = jnp.zeros_like(acc_ref)

    half = a_ref.shape[1] // 2
    for hv in range(2):
        lanes = slice(hv * half, (hv + 1) * half)
        _gate_rows(a_ref, cnt_ref, e1_ref, e2_ref, rk_ref, w_ref, range(PEER_ROWS),
                   range(hv * half // LANES, (hv + 1) * half // LANES))
        acc_ref[:, lanes] += jnp.dot(vt_ref[...], w_ref[:, lanes], preferred_element_type=_F32)
        a_ref[:, lanes] = lax.dot_general(un_ref[...], h2n_ref[lanes, :], _NT,
                                          preferred_element_type=_F32)

    @pl.when(k == n_steps - 1)
    def _():
        gt2 = mod_ref[0, :, 5 * d:6 * d]
        o_ref[...] = x1_ref[...] + gt2 * acc_ref[...].T


def _peer(h2, x1, mod_l, u, vt, e1t, cntt, e2t, rkt, *, geo):
    n, d = h2.shape
    tn = PEER_TOKENS
    te = PEER_ROWS * PEER_KEYS
    n_steps = u.shape[0] // te
    n_tok = n // tn
    lat_tiles, tiles_per_batch, batch = geo
    row = lambda t, k: (t, 0)

    def mod_idx(t, k):
        return (jnp.where(t < lat_tiles, t // tiles_per_batch, batch), 0, 0)

    rows_blk = pl.BlockSpec((PEER_HEADS, PEER_ROWS, tn), lambda t, k: (0, k, t))
    return pl.pallas_call(
        functools.partial(_peer_kernel, d=d, n_steps=n_steps),
        out_shape=jax.ShapeDtypeStruct((n, d), _F32),
        grid=(n // tn, n_steps),
        in_specs=[
            pl.BlockSpec((tn, d), lambda t, k: (jnp.minimum(t + (k + 1) // n_steps, n_tok - 1), 0)),
            pl.BlockSpec((tn, d), row),
            pl.BlockSpec((1, 1, N_MOD * d), mod_idx),
            pl.BlockSpec((te, d), lambda t, k: (0, 0)),
            pl.BlockSpec((te, d), lambda t, k: ((k + 1) % n_steps, 0)),
            pl.BlockSpec((d, te), lambda t, k: (0, k)),
            rows_blk,
            rows_blk,
            pl.BlockSpec((PEER_HEADS, PEER_KEYS // 2, tn), lambda t, k: (0, 0, t)),
            pl.BlockSpec((PEER_HEADS, PEER_KEYS // 2, tn), lambda t, k: (0, 0, t)),
        ],
        out_specs=pl.BlockSpec((tn, d), row),
        scratch_shapes=[
            pltpu.VMEM((d, tn), _F32),
            pltpu.VMEM((te, tn), _F32),
            pltpu.VMEM((te, tn), _BF16),
        ],
        compiler_params=_params("arbitrary", "arbitrary"),
        name="peer_experts",
    )(h2, x1, mod_l, u, u, vt, cntt, e1t, e2t, rkt)


def _final_kernel(x_ref, g_ref, o_ref):
    o_ref[...] = _rms(x_ref[...], g_ref[...])


def _final_norm(xs, g, n_rows):
    d = xs.shape[1]
    tm = PROJ_TILE
    return pl.pallas_call(
        _final_kernel,
        out_shape=jax.ShapeDtypeStruct((n_rows, d), _F32),
        grid=(n_rows // tm,),
        in_specs=[pl.BlockSpec((tm, d), lambda t: (t, 0)), _full(g.shape)],
        out_specs=pl.BlockSpec((tm, d), lambda t: (t, 0)),
        compiler_params=_params("arbitrary"),
        name="final_norm",
    )(xs, g)


def _rope_tables(seq):
    rows = seq // GRID_W
    r = jnp.repeat(jnp.arange(rows, dtype=_F32), GRID_W)
    col = jnp.tile(jnp.arange(GRID_W, dtype=_F32), rows)

    def cos_sin(rot_dim):
        n_freq = rot_dim // 4
        inv = ROPE_BASE ** (-jnp.arange(n_freq, dtype=_F32) / n_freq)
        ang = jnp.concatenate([r[:, None] * inv, col[:, None] * inv], axis=-1)
        return jnp.cos(ang), jnp.sin(ang)

    ca, sa = cos_sin(MLA_ROPE)
    one = jnp.ones((seq, MLA_NOPE), _F32)
    zero = jnp.zeros_like(one)
    z16 = jnp.zeros_like(ca)
    pad1 = jnp.ones((seq, LANES - MLA_NOPE - MLA_ROPE), _F32)
    pad0 = jnp.zeros_like(pad1)
    tab_a = [jnp.concatenate([one, ca, ca, pad1], axis=1),
             jnp.concatenate([zero, -sa, z16, pad0], axis=1),
             jnp.concatenate([zero, z16, sa, pad0], axis=1)]
    ch, sh = cos_sin(HEAD_DIM)
    z32 = jnp.zeros_like(ch)
    tab_h = [jnp.concatenate([ch, ch, ch, ch], axis=1),
             jnp.concatenate([-sh, z32, -sh, z32], axis=1),
             jnp.concatenate([z32, sh, z32, sh], axis=1)]
    lat = jnp.concatenate(tab_a + tab_h, axis=1)
    ident = jnp.concatenate([jnp.ones((PROJ_TILE, LANES), _F32),
                             jnp.zeros((PROJ_TILE, 2 * LANES), _F32)] * 2, axis=1)
    return jnp.concatenate([lat, ident], axis=0)


def _pack_w_in(w_in, d):
    splits = np.cumsum([MLA_Q_RANK, MLA_KV_RANK, MLA_ROPE, 512, 128, 128, 512, 128, 128])
    cq, ckv, kr, qb, kb, vb, qw, kw, vw, gates = jnp.split(w_in, splits, axis=-1)
    lead = w_in.shape[:-1]
    kr_pad = jnp.concatenate([jnp.zeros(lead + (MLA_NOPE,), w_in.dtype), kr,
                              jnp.zeros(lead + (LANES - MLA_NOPE - MLA_ROPE,), w_in.dtype)], axis=-1)

    def dup(w):
        g0, g1 = w[..., :HEAD_DIM], w[..., HEAD_DIM:]
        return jnp.concatenate([g0, g0, g1, g1], axis=-1)

    return jnp.concatenate([cq, ckv, kr_pad, qb, dup(kb), dup(vb), qw, dup(kw), dup(vw), gates],
                           axis=-1).astype(_BF16)


def _pack_w_uq(w_uq):
    depth, rank, _ = w_uq.shape
    w = w_uq.reshape(depth, rank, HEADS, MLA_NOPE + MLA_ROPE)
    w = jnp.pad(w, ((0, 0), (0, 0), (0, 0), (0, LANES - MLA_NOPE - MLA_ROPE)))
    return w.reshape(depth, rank, HEADS * LANES).astype(_BF16)


def _pack_w_ukv(w_ukv):
    depth, rank, _ = w_ukv.shape
    w = w_ukv.reshape(depth, rank, HEADS, MLA_NOPE + HEAD_DIM)
    k = jnp.pad(w[..., :MLA_NOPE], ((0, 0), (0, 0), (0, 0), (0, LANES - MLA_NOPE)))
    v = w[..., MLA_NOPE:]
    return jnp.concatenate([k.reshape(depth, rank, HEADS * LANES),
                            v.reshape(depth, rank, HEADS * HEAD_DIM)], axis=-1).astype(_BF16)


def kernel(x, c, ctx, c_ctx, w_mod, b_mod, g_attn, g_ffn, w_in, g_cq, g_ckv, w_uq, w_ukv,
           g_qn, g_kn, sink, w_oa, w_ob, w_ow, w_o, w_pq, sub_keys, peer_u, peer_v, g_final):
    batch, seq, d = x.shape
    n_ctx = ctx.shape[1]
    depth = w_mod.shape[0]
    assert n_ctx == ROW_TILE and seq % PEER_TOKENS == 0 and (batch * n_ctx) % PEER_TOKENS == 0
    assert batch < 8 and seq % GRID_W == 0 and seq >= 2 * WIN_Q_TILE and seq % WIN_Q_TILE == 0
    assert seq % FLASH_Q_TILE == 0 and seq % FLASH_K_CHUNK == 0 and seq % PROJ_TILE == 0
    n_lat = batch * seq
    n_all = n_lat + batch * n_ctx
    ctx_blk0 = n_lat // ROW_TILE

    xs = jnp.concatenate([x.reshape(n_lat, d), ctx.reshape(batch * n_ctx, d)], axis=0)
    cc = jnp.concatenate([c, c_ctx[None], jnp.zeros((8 - batch - 1, d), _F32)], axis=0)
    mod = _modulation(cc, w_mod, b_mod).reshape(depth, 8, 1, N_MOD * d)

    rope = _rope_tables(seq)
    eye = np.kron(np.eye(2, dtype=np.float32), np.ones((HEAD_DIM, HEAD_DIM), np.float32))
    bd = jnp.asarray(eye, _BF16)
    win = _pack_w_in(w_in, d)
    wuq = _pack_w_uq(w_uq)
    wukv = _pack_w_ukv(w_ukv)
    gqn_t = jnp.tile(g_qn, (1, HEADS))[:, None, :]
    gkn_t = jnp.tile(g_kn, (1, 2 * LANES // HEAD_DIM))[:, None, :]
    sink_t = jnp.broadcast_to(sink[:, :, None], (depth, HEADS, LANES))
    woa, wob, wow, wo, wpq = (w.astype(_BF16) for w in (w_oa, w_ob, w_ow, w_o, w_pq))
    subk = sub_keys.reshape(depth, 2 * PEER_HEADS, PEER_KEYS, PEER_HALF).astype(_BF16)
    u_bf = peer_u.astype(_BF16)
    vt_bf = jnp.swapaxes(peer_v, 1, 2).astype(_BF16)

    geo_row = (n_lat // PROJ_TILE, seq // PROJ_TILE, batch)
    geo_peer = (n_lat // PEER_TOKENS, seq // PEER_TOKENS, batch)
    lat_seg = (seq, lambda b, i: b)
    ctx_seg = (n_ctx, lambda b, i: ctx_blk0 + b)
    nqw = seq // WIN_Q_TILE
    per_tile = WIN_Q_TILE // WINDOW
    edge_blocks = seq // WINDOW
    band_segs = [
        (WINDOW, lambda b, i: b * edge_blocks + jnp.maximum(per_tile * i - 1, 0)),
        (WIN_Q_TILE, lambda b, i: b * nqw + i),
        (WINDOW, lambda b, i: b * edge_blocks + jnp.minimum(per_tile * (i + 1), edge_blocks - 1)),
        ctx_seg,
    ]

    for l in range(depth):
        last = l == depth - 1
        qa, ka, va, qb, kb, vb, qw, kw, vw, gates = _inproj(
            xs, mod[l], g_attn[l][None], win[l], g_cq[l][None], g_ckv[l][None], wuq[l], wukv[l],
            gqn_t[l], gkn_t[l], rope, bd, geo=geo_row)
        dense = dict(batch=batch, seq=seq, n_ctx=n_ctx)
        n_rows = n_lat if last else n_all
        o_a = _flash_attention(qa, ka, va, wide=True, out_rows=n_rows, **dense)
        o_b = _flash_attention(qb, kb, vb, wide=False, out_rows=n_rows, **dense)
        o_w = _attention(qw, kw, vw, sink_t[l], wide=False, segs=band_segs, band=True,
                         nq=nqw, q_block0=0, batch=batch, tq=WIN_Q_TILE, out_rows=n_rows)
        if not last:
            cq = dict(nq=1, q_block0=ctx_blk0, batch=batch, segs=[ctx_seg], tq=n_ctx)
            o_a = _attention(qa, ka, va, None, wide=True, base2=True, into=o_a, **cq)
            o_b = _attention(qb, kb, vb, None, wide=False, base2=True, into=o_b, **cq)
            o_w = _attention(qw, kw, vw, sink_t[l], wide=False, into=o_w, **cq)
        x1, h2, s1t, s2t = _merge(xs, mod[l], o_a, o_b, o_w, gates, woa[l], wob[l], wow[l], wo[l],
                                  g_ffn[l][None], wpq[l], subk[l], n_rows=n_rows, geo=geo_row)
        e1t, cntt, e2t, rkt = _topk(s1t, s2t)
        xs = _peer(h2, x1, mod[l], u_bf[l], vt_bf[l], e1t, cntt, e2t, rkt, geo=geo_peer)
    out = _final_norm(xs, g_final[None], n_lat)
    return out.reshape(batch, seq, d)
```

```python
import functools
import math

import jax
import jax.numpy as jnp
import numpy as np
from jax import lax
from jax.experimental import pallas as pl
from jax.experimental.pallas import tpu as pltpu

GRID_W = 64
ROPE_BASE = 10000.0
NORM_EPS = 1e-6
NEG_INF = -1e30
LOG2E = math.log2(math.e)
WINDOW = 128
N_MOD = 6

HEADS = 8
HEAD_PAIRS = HEADS // 2
MLA_Q_RANK = 384
MLA_KV_RANK = 256
MLA_NOPE = 64
MLA_ROPE = 32
HEAD_DIM = 64
KV_HEADS = 2
LANES = 128

PEER_HEADS = 8
PEER_KEYS = 128
PEER_HALF = 128
PEER_TOPK = 16

ROW_TILE = 256
PROJ_TILE = 512
WIN_Q_TILE = 256
FLASH_Q_TILE = 1024
FLASH_K_CHUNK = 4096
PEER_TOKENS = 512
PEER_ROWS = 16
PEER_CHUNK = 32
PEER_A_PIECE = 512
TOPK_TOKENS = 256
VMEM_LIMIT = 56 * 1024 * 1024

OFF_CQ = 0
OFF_CKV = OFF_CQ + MLA_Q_RANK
OFF_KR = OFF_CKV + MLA_KV_RANK
OFF_QB = OFF_KR + LANES
OFF_KB = OFF_QB + HEADS * HEAD_DIM
OFF_VB = OFF_KB + 2 * LANES
OFF_QW = OFF_VB + 2 * LANES
OFF_KW = OFF_QW + HEADS * HEAD_DIM
OFF_VW = OFF_KW + 2 * LANES
OFF_GATES = OFF_VW + 2 * LANES

_NT = (((1,), (1,)), ((), ()))
_F32 = jnp.float32
_BF16 = jnp.bfloat16


def _params(*semantics):
    return pltpu.CompilerParams(dimension_semantics=semantics, vmem_limit_bytes=VMEM_LIMIT)


def _full(shape):
    return pl.BlockSpec(shape, lambda *_: (0,) * len(shape), pipeline_mode=pl.Buffered(1))


def _rms(x, g):
    return x * lax.rsqrt(jnp.mean(x * x, axis=-1, keepdims=True) + NORM_EPS) * g


def _rope(x, tabs, shift):
    cos, sin_lo, sin_hi = tabs
    outs = []
    for c in range(x.shape[1] // LANES):
        xc = x[:, c * LANES:(c + 1) * LANES]
        up = pltpu.roll(xc, LANES - shift, 1)
        dn = pltpu.roll(xc, shift, 1)
        outs.append(xc * cos + up * sin_lo + dn * sin_hi)
    return outs[0] if len(outs) == 1 else jnp.concatenate(outs, axis=1)


def _head_rms(x, bd, g):
    outs = []
    for c in range(x.shape[1] // LANES):
        xc = x[:, c * LANES:(c + 1) * LANES]
        sq = xc * xc
        hi = sq.astype(_BF16)
        lo = (sq - hi.astype(_F32)).astype(_BF16)
        ssq = (jnp.dot(hi, bd, preferred_element_type=_F32)
               + jnp.dot(lo, bd, preferred_element_type=_F32))
        outs.append(xc * lax.rsqrt(ssq * (1.0 / HEAD_DIM) + NORM_EPS))
    y = outs[0] if len(outs) == 1 else jnp.concatenate(outs, axis=1)
    return y * g


def _mod_kernel(c_ref, w_ref, b_ref, o_ref):
    c = c_ref[...]
    a = c / (1.0 + jnp.exp(-c))
    o_ref[0] = jnp.dot(a, w_ref[0], preferred_element_type=_F32,
                       precision=lax.Precision.HIGHEST) + b_ref[0]


def _modulation(cc, w_mod, b_mod):
    depth, d, cols = w_mod.shape
    tn = cols // 4
    return pl.pallas_call(
        _mod_kernel,
        out_shape=jax.ShapeDtypeStruct((depth, 8, cols), _F32),
        grid=(depth, cols // tn),
        in_specs=[
            pl.BlockSpec((8, d), lambda l, j: (0, 0)),
            pl.BlockSpec((1, d, tn), lambda l, j: (l, 0, j)),
            pl.BlockSpec((1, 1, tn), lambda l, j: (l, 0, j)),
        ],
        out_specs=pl.BlockSpec((1, 8, tn), lambda l, j: (l, 0, j)),
        compiler_params=_params("arbitrary", "arbitrary"),
        name="modulation",
    )(cc, w_mod, b_mod.reshape(depth, 1, cols))


def _inproj_kernel(x_ref, mod_ref, gattn_ref, win_ref, gcq_ref, gckv_ref, wuq_ref, wukv_ref,
                   gqn_ref, gkn_ref, rope_ref, bd_ref,
                   qa_ref, ka_ref, va_ref, qb_ref, kb_ref, vb_ref, qw_ref, kw_ref, vw_ref,
                   gates_ref, *, d, scale_a, scale_h):
    x = x_ref[...]
    sh1 = mod_ref[0, :, 0:d]
    sc1 = mod_ref[0, :, d:2 * d]
    h = (_rms(x, gattn_ref[...]) * (1.0 + sc1) + sh1).astype(_BF16)

    def proj(off, width):
        return jnp.dot(h, win_ref[:, off:off + width], preferred_element_type=_F32)

    rope_a = tuple(rope_ref[:, i * LANES:(i + 1) * LANES] for i in range(3))
    rope_h = tuple(rope_ref[:, i * LANES:(i + 1) * LANES] for i in range(3, 6))
    bd = bd_ref[...]

    cq = _rms(proj(OFF_CQ, MLA_Q_RANK), gcq_ref[...]).astype(_BF16)
    qa = jnp.dot(cq, wuq_ref[...], preferred_element_type=_F32)
    qa_ref[...] = (_rope(qa, rope_a, MLA_ROPE // 2) * (scale_a * LOG2E)).astype(_BF16)
    ckv = _rms(proj(OFF_CKV, MLA_KV_RANK), gckv_ref[...]).astype(_BF16)
    kva = jnp.dot(ckv, wukv_ref[...], preferred_element_type=_F32)
    kr = _rope(proj(OFF_KR, LANES), rope_a, MLA_ROPE // 2)
    ka = kva[:, :HEADS * LANES] + jnp.concatenate([kr] * HEADS, axis=1)
    ka_ref[...] = ka.astype(_BF16)
    va_ref[...] = kva[:, HEADS * LANES:].astype(_BF16)

    qb = _head_rms(proj(OFF_QB, HEADS * HEAD_DIM), bd, gqn_ref[...])
    qb_ref[...] = (_rope(qb, rope_h, HEAD_DIM // 2) * (scale_h * LOG2E)).astype(_BF16)
    kb = _head_rms(proj(OFF_KB, 2 * LANES), bd, gkn_ref[...])
    kb_ref[...] = _rope(kb, rope_h, HEAD_DIM // 2).astype(_BF16)
    vb_ref[...] = proj(OFF_VB, 2 * LANES).astype(_BF16)

    qw_ref[...] = (_rope(proj(OFF_QW, HEADS * HEAD_DIM), rope_h, HEAD_DIM // 2) * scale_h).astype(_BF16)
    kw_ref[...] = _rope(proj(OFF_KW, 2 * LANES), rope_h, HEAD_DIM // 2).astype(_BF16)
    vw_ref[...] = proj(OFF_VW, 2 * LANES).astype(_BF16)

    for k in range(3):
        g = proj(OFF_GATES + k * d, d)
        gates_ref[:, k * d:(k + 1) * d] = (1.0 / (1.0 + jnp.exp(-g))).astype(_BF16)


def _inproj(xs, mod_l, gattn, win, gcq, gckv, wuq, wukv, gqn_t, gkn_t, rope, bd, *, geo):
    n, d = xs.shape
    tm = PROJ_TILE
    lat_tiles, tiles_per_batch, batch = geo
    row = lambda t: (t, 0)

    def mod_idx(t):
        return (jnp.where(t < lat_tiles, t // tiles_per_batch, batch), 0, 0)

    def rope_idx(t):
        return (jnp.where(t < lat_tiles, t % tiles_per_batch, tiles_per_batch), 0)

    widths = (HEADS * LANES, HEADS * LANES, HEADS * HEAD_DIM, HEADS * HEAD_DIM, 2 * LANES, 2 * LANES,
              HEADS * HEAD_DIM, 2 * LANES, 2 * LANES, 3 * d)
    return pl.pallas_call(
        functools.partial(_inproj_kernel, d=d, scale_a=(MLA_NOPE + MLA_ROPE) ** -0.5,
                          scale_h=HEAD_DIM ** -0.5),
        out_shape=[jax.ShapeDtypeStruct((n, w), _BF16) for w in widths],
        grid=(n // tm,),
        in_specs=[
            pl.BlockSpec((tm, d), row),
            pl.BlockSpec((1, 1, N_MOD * d), mod_idx),
            _full(gattn.shape), _full(win.shape), _full(gcq.shape), _full(gckv.shape),
            _full(wuq.shape), _full(wukv.shape), _full(gqn_t.shape), _full(gkn_t.shape),
            pl.BlockSpec((tm, 6 * LANES), rope_idx),
            _full(bd.shape),
        ],
        out_specs=[pl.BlockSpec((tm, w), row) for w in widths],
        compiler_params=_params("arbitrary"),
        name="inproj",
    )(xs, mod_l, gattn, win, gcq, gckv, wuq, wukv, gqn_t, gkn_t, rope, bd)


def _attn_kernel(*refs, nseg, wide, has_sink, band, nq, base2):
    q_ref = refs[0]
    k_refs = refs[1:1 + nseg]
    v_refs = refs[1 + nseg:1 + 2 * nseg]
    sink_ref = refs[1 + 2 * nseg] if has_sink else None
    o_ref = refs[-1]
    pair = pl.program_id(1)
    qi = pl.program_id(2)
    tq = q_ref.shape[0]
    lane = lax.broadcasted_iota(jnp.int32, (tq, LANES), 1)

    masks = [None] * nseg
    if band:
        r = lax.broadcasted_iota(jnp.int32, (tq, WINDOW), 0)
        c = lax.broadcasted_iota(jnp.int32, (tq, WINDOW), 1)
        masks[0] = c >= r + jnp.where(qi > 0, 0, WINDOW)
        masks[2] = c <= r - (tq - WINDOW) - jnp.where(qi < nq - 1, 0, tq)
        r = lax.broadcasted_iota(jnp.int32, (tq, tq), 0)
        c = lax.broadcasted_iota(jnp.int32, (tq, tq), 1)
        masks[1] = jnp.abs(r - c) <= WINDOW

    for pp in range(1 if wide else HEAD_PAIRS):
        qcols = slice(pp * LANES, (pp + 1) * LANES)
        kvcols = slice((pp // 2) * LANES, (pp // 2 + 1) * LANES)
        outs = []
        for hh in range(2):
            if wide:
                q = q_ref[:, hh * LANES:(hh + 1) * LANES]
                ks = [k[:, hh * LANES:(hh + 1) * LANES] for k in k_refs]
                vs = [v[...] for v in v_refs]
                head = 2 * pair + hh
            else:
                own = (lane[:1] < HEAD_DIM) if hh == 0 else (lane[:1] >= HEAD_DIM)
                q = q_ref[:, qcols] * own.astype(_F32).astype(_BF16)
                ks = [k[:, kvcols] for k in k_refs]
                vs = [v[:, kvcols] for v in v_refs]
                head = 2 * pp + hh
            ss = [lax.dot_general(q, k, _NT, preferred_element_type=_F32) for k in ks]
            ss = [s if m is None else jnp.where(m, s, NEG_INF) for s, m in zip(ss, masks)]
            m = functools.reduce(jnp.maximum, [jnp.max(s, axis=-1, keepdims=True) for s in ss])
            if has_sink:
                sink = sink_ref[pl.ds(head, 1), 0:1]
                m = jnp.maximum(m, sink)
            ps = [(jnp.exp2 if base2 else jnp.exp)(s - m) for s in ss]
            l = functools.reduce(jnp.add, [jnp.sum(p, axis=-1, keepdims=True) for p in ps])
            if has_sink:
                l = l + jnp.exp(sink - m)
            o = functools.reduce(jnp.add, [jnp.dot(p.astype(_BF16), v, preferred_element_type=_F32)
                                           for p, v in zip(ps, vs)])
            outs.append(o / l)
        o_ref[:, qcols] = jnp.where(lane < HEAD_DIM, outs[0], outs[1]).astype(o_ref.dtype)


def _flash_kernel(q_ref, k_ref, kc_ref, v_ref, vc_ref, o_ref, *, wide, chunk):
    tq = q_ref.shape[0]
    lane = lax.broadcasted_iota(jnp.int32, (1, LANES), 1)
    qs, kcols = [], []
    for hh in range(2):
        if wide:
            qs.append(q_ref[:, hh * LANES:(hh + 1) * LANES])
            kcols.append(slice(hh * LANES, (hh + 1) * LANES))
        else:
            own = (lane < HEAD_DIM) if hh == 0 else (lane >= HEAD_DIM)
            qs.append(q_ref[...] * own.astype(_F32).astype(_BF16))
            kcols.append(slice(0, LANES))
    m = [jnp.full((tq, 1), -jnp.inf, _F32) for _ in range(2)]
    l = [jnp.zeros((tq, 1), _F32) for _ in range(2)]
    acc = [jnp.zeros((tq, LANES), _F32) for _ in range(2)]
    pieces = [(k_ref, v_ref, r0, chunk) for r0 in range(0, k_ref.shape[0], chunk)]
    pieces.append((kc_ref, vc_ref, 0, kc_ref.shape[0]))
    for kr, vr, r0, rows in pieces:
        for hh in range(2):
            s = lax.dot_general(qs[hh], kr[r0:r0 + rows, kcols[hh]], _NT,
                                preferred_element_type=_F32)
            m_new = jnp.maximum(m[hh], jnp.max(s, axis=-1, keepdims=True))
            alpha = jnp.exp2(m[hh] - m_new)
            p = jnp.exp2(s - m_new)
            l[hh] = alpha * l[hh] + jnp.sum(p, axis=-1, keepdims=True)
            acc[hh] = alpha * acc[hh] + jnp.dot(p.astype(_BF16), vr[r0:r0 + rows, :],
                                                preferred_element_type=_F32)
            m[hh] = m_new
    o_ref[...] = jnp.where(lane < HEAD_DIM, acc[0] / l[0], acc[1] / l[1]).astype(o_ref.dtype)


def _flash_attention(q, k, v, *, wide, batch, seq, n_ctx):
    tq = FLASH_Q_TILE
    nq = seq // tq
    qw = 2 * LANES if wide else LANES
    kdiv = 1 if wide else 2
    ctx_blk0 = batch * seq // n_ctx
    return pl.pallas_call(
        functools.partial(_flash_kernel, wide=wide, chunk=FLASH_K_CHUNK),
        out_shape=jax.ShapeDtypeStruct((batch * seq, HEADS * HEAD_DIM), _BF16),
        grid=(batch, HEAD_PAIRS, nq),
        in_specs=[
            pl.BlockSpec((tq, qw), lambda b, p, i: (b * nq + i, p)),
            pl.BlockSpec((seq, qw), lambda b, p, i: (b, p // kdiv)),
            pl.BlockSpec((n_ctx, qw), lambda b, p, i: (ctx_blk0 + b, p // kdiv)),
            pl.BlockSpec((seq, LANES), lambda b, p, i: (b, p // kdiv)),
            pl.BlockSpec((n_ctx, LANES), lambda b, p, i: (ctx_blk0 + b, p // kdiv)),
        ],
        out_specs=pl.BlockSpec((tq, LANES), lambda b, p, i: (b * nq + i, p)),
        compiler_params=_params("arbitrary", "arbitrary", "arbitrary"),
        name="flash_attention",
    )(q, k, k, v, v)


def _attention(q, k, v, sink, *, wide, segs, nq, q_block0, batch, tq, band=False, base2=False):
    pairs = HEAD_PAIRS if wide else 1
    qw, kw, vw, ow = ((2 * LANES, 2 * LANES, LANES, LANES) if wide
                      else (HEADS * HEAD_DIM, KV_HEADS * LANES, KV_HEADS * LANES, HEADS * HEAD_DIM))
    in_specs = [pl.BlockSpec((tq, qw), lambda b, p, i: (q_block0 + b * nq + i, p))]
    for width in (kw, vw):
        for rows, fn in segs:
            in_specs.append(pl.BlockSpec((rows, width), functools.partial(
                lambda b, p, i, fn: (fn(b, i), p), fn=fn)))
    args = [q] + [k] * len(segs) + [v] * len(segs)
    if sink is not None:
        in_specs.append(_full(sink.shape))
        args.append(sink)
    return pl.pallas_call(
        functools.partial(_attn_kernel, nseg=len(segs), wide=wide, has_sink=sink is not None,
                          band=band, nq=nq, base2=base2),
        out_shape=jax.ShapeDtypeStruct((batch * nq * tq, HEADS * HEAD_DIM), _BF16),
        grid=(batch, pairs, nq),
        in_specs=in_specs,
        out_specs=pl.BlockSpec((tq, ow), lambda b, p, i: (b * nq + i, p)),
        compiler_params=_params("arbitrary", "arbitrary", "arbitrary"),
        name="attention",
    )(*args)


def _merge_kernel(*refs, d, lat_tiles, has_ctx):
    n_o = 6 if has_ctx else 3
    x_ref, mod_ref = refs[:2]
    o_refs = refs[2:2 + n_o]
    (gates_ref, woa_ref, wob_ref, wow_ref, wo_ref, gffn_ref, wpq_ref, subk_ref,
     x1_ref, h2_ref, s1_ref, s2_ref) = refs[2 + n_o:14 + n_o]
    if has_ctx:
        o_scr = refs[14 + n_o]
        tile = pl.program_id(0)

        @pl.when(tile < lat_tiles)
        def _():
            for k in range(3):
                o_scr[k] = o_refs[k][...]

        @pl.when(tile >= lat_tiles)
        def _():
            for k in range(3):
                o_scr[k] = o_refs[3 + k][...]

        outs = [o_scr[k] for k in range(3)]
    else:
        outs = [o_refs[k][...] for k in range(3)]
    gt1 = mod_ref[0, :, 2 * d:3 * d]
    sh2 = mod_ref[0, :, 3 * d:4 * d]
    sc2 = mod_ref[0, :, 4 * d:5 * d]
    m = None
    for k, w_ref in enumerate((woa_ref, wob_ref, wow_ref)):
        t = gates_ref[:, k * d:(k + 1) * d].astype(_F32) * jnp.dot(
            outs[k], w_ref[...], preferred_element_type=_F32)
        m = t if m is None else m + t
    y = jnp.dot(m.astype(_BF16), wo_ref[...], preferred_element_type=_F32)
    x1 = x_ref[...] + gt1 * y
    x1_ref[...] = x1
    h2 = (_rms(x1, gffn_ref[...]) * (1.0 + sc2) + sh2).astype(_BF16)
    h2_ref[...] = h2
    q = jnp.dot(h2, wpq_ref[...], preferred_element_type=_F32).astype(_BF16)
    for g in range(2 * PEER_HEADS):
        s = lax.dot_general(subk_ref[g], q[:, g * PEER_HALF:(g + 1) * PEER_HALF], _NT,
                            preferred_element_type=_F32)
        if g % 2 == 0:
            s1_ref[g // 2] = s
        else:
            s2_ref[g // 2] = s


def _merge(xs, mod_l, o_lat, o_ctx, gates, woa, wob, wow, wo, gffn, wpq, subk, *, n_rows, geo):
    d = xs.shape[1]
    tm = PROJ_TILE
    lat_tiles, tiles_per_batch, batch = geo
    row = lambda t: (t, 0)
    ow = HEADS * HEAD_DIM

    def mod_idx(t):
        return (jnp.where(t < lat_tiles, t // tiles_per_batch, batch), 0, 0)

    o_specs = [pl.BlockSpec((tm, ow), lambda t: (jnp.minimum(t, lat_tiles - 1), 0))] * 3
    o_args = list(o_lat)
    scratch = []
    if o_ctx is not None:
        o_specs += [pl.BlockSpec((tm, ow), lambda t: (jnp.maximum(t - lat_tiles, 0), 0))] * 3
        o_args += list(o_ctx)
        scratch = [pltpu.VMEM((3, tm, ow), _BF16)]
    return pl.pallas_call(
        functools.partial(_merge_kernel, d=d, lat_tiles=lat_tiles, has_ctx=o_ctx is not None),
        out_shape=[
            jax.ShapeDtypeStruct((n_rows, d), _F32),
            jax.ShapeDtypeStruct((n_rows, d), _BF16),
            jax.ShapeDtypeStruct((PEER_HEADS, PEER_KEYS, n_rows), _F32),
            jax.ShapeDtypeStruct((PEER_HEADS, PEER_KEYS, n_rows), _F32),
        ],
        grid=(n_rows // tm,),
        in_specs=[
            pl.BlockSpec((tm, d), row),
            pl.BlockSpec((1, 1, N_MOD * d), mod_idx),
            *o_specs,
            pl.BlockSpec((tm, 3 * d), row),
            _full(woa.shape), _full(wob.shape), _full(wow.shape), _full(wo.shape),
            _full(gffn.shape), _full(wpq.shape), _full(subk.shape),
        ],
        out_specs=[
            pl.BlockSpec((tm, d), row),
            pl.BlockSpec((tm, d), row),
            pl.BlockSpec((PEER_HEADS, PEER_KEYS, tm), lambda t: (0, 0, t)),
            pl.BlockSpec((PEER_HEADS, PEER_KEYS, tm), lambda t: (0, 0, t)),
        ],
        scratch_shapes=scratch,
        compiler_params=_params("arbitrary"),
        name="merge",
    )(xs, mod_l, *o_args, gates, woa, wob, wow, wo, gffn, wpq, subk)


NO_RANK = 64.0


def _top_values(w, k, want_rank=False):
    vals = []
    rank = jnp.full(w.shape, NO_RANK, _F32) if want_rank else None
    for r in range(k):
        m = jnp.max(w, axis=0, keepdims=True)
        vals.append(m)
        hit = w == m
        if want_rank:
            rank = jnp.where(hit, float(r), rank)
        w = jnp.where(hit, -jnp.inf, w)
    return (vals, rank) if want_rank else vals


def _stack_rows(vals, n_rows):
    rows = lax.broadcasted_iota(jnp.int32, (n_rows, vals[0].shape[1]), 0)
    out = jnp.full(rows.shape, -jnp.inf, _F32)
    for k, v in enumerate(vals):
        out = jnp.where(rows == k, v, out)
    return out


def _topk_kernel(s1_ref, s2_ref, e1_ref, cnt_ref, e2_ref, rk_ref):
    n_top = PEER_TOPK + 1

    def head(h, carry):
        s1 = s1_ref[h]
        s2 = s2_ref[h]
        t1 = _top_values(s1, n_top)
        t2, rank2 = _top_values(s2, n_top, want_rank=True)
        t2c = _stack_rows(t2, 24)
        cand = [t1[0] + t2c]
        cand += [t1[a] + t2c[:8] for a in range(1, 8)]
        cand += [_stack_rows(t1[8:], 16) + t2[0]]
        best = _top_values(jnp.concatenate(cand, axis=0), n_top)
        z = functools.reduce(jnp.add, [jnp.exp(v - best[0]) for v in best[:PEER_TOPK]])
        tau = 0.5 * (best[PEER_TOPK - 1] + best[PEER_TOPK])
        need = tau - s1
        cnt = jnp.zeros(s1.shape, _F32)
        for b in range(PEER_TOPK):
            cnt = jnp.where(t2[b] >= need, float(b + 1), cnt)
        e1_ref[h] = jnp.exp(s1 - t1[0]) / z
        cnt_ref[h] = cnt
        e2_ref[h] = pltpu.bitcast(jnp.exp(s2 - t2[0]).astype(_BF16), jnp.uint32)
        rk_ref[h] = pltpu.bitcast(rank2.astype(_BF16), jnp.uint32)
        return carry

    lax.fori_loop(0, PEER_HEADS, head, 0)


def _topk(s1t, s2t):
    n = s1t.shape[2]
    tk = TOPK_TOKENS
    blk = pl.BlockSpec((PEER_HEADS, PEER_KEYS, tk), lambda t: (0, 0, t))
    packed = pl.BlockSpec((PEER_HEADS, PEER_KEYS // 2, tk), lambda t: (0, 0, t))
    return pl.pallas_call(
        _topk_kernel,
        out_shape=[jax.ShapeDtypeStruct(s1t.shape, _F32)] * 2
        + [jax.ShapeDtypeStruct((PEER_HEADS, PEER_KEYS // 2, n), jnp.uint32)] * 2,
        grid=(n // tk,),
        in_specs=[blk, blk],
        out_specs=[blk, blk, packed, packed],
        compiler_params=_params("arbitrary"),
        name="peer_topk",
    )(s1t, s2t)


def _gate_rows(a_ref, cnt_ref, e1_ref, e2_ref, rk_ref, w_ref, first_keys, lane_tiles):
    shape = (PEER_CHUNK, LANES)
    for ii in first_keys:
        for lt in lane_tiles:
            ls = slice(lt * LANES, (lt + 1) * LANES)
            cntb = [jnp.broadcast_to(cnt_ref[h, ii:ii + 1, ls], shape).astype(_BF16)
                    for h in range(PEER_HEADS)]
            e1b = [jnp.broadcast_to(e1_ref[h, ii:ii + 1, ls], shape).astype(_BF16)
                   for h in range(PEER_HEADS)]
            for jc in range(PEER_KEYS // PEER_CHUNK):
                rows = slice(jc * PEER_CHUNK, (jc + 1) * PEER_CHUNK)
                words = slice(rows.start // 2, rows.stop // 2)
                g = None
                for h in range(PEER_HEADS):
                    e2 = pltpu.bitcast(e2_ref[h, words, ls], _BF16)
                    rk = pltpu.bitcast(rk_ref[h, words, ls], _BF16)
                    t = e1b[h] * jnp.where(rk < cntb[h], e2, jnp.zeros_like(e2))
                    g = t if g is None else g + t
                erows = slice(ii * PEER_KEYS + rows.start, ii * PEER_KEYS + rows.stop)
                a = a_ref[erows, ls]
                act = (0.5 * a) * (1.0 + lax.erf(a * (1.0 / math.sqrt(2.0))))
                w_ref[erows, ls] = g * act.astype(_BF16)


def _peer_kernel(h2n_ref, x1_ref, mod_ref, u0_ref, un_ref, vt_ref, cnt_ref, e1_ref, e2_ref, rk_ref,
                 o_ref, acc_ref, a_ref, w_ref, *, d, n_steps):
    k = pl.program_id(1)

    @pl.when(jnp.logical_and(pl.program_id(0) == 0, k == 0))
    def _():
        a_ref[...] = lax.dot_general(u0_ref[...], h2n_ref[...], _NT, preferred_element_type=_F32)

    @pl.when(k == 0)
    def _():
        acc_ref[...] = jnp.zeros_like(acc_ref)

    half = a_ref.shape[1] // 2
    for hv in range(2):
        lanes = slice(hv * half, (hv + 1) * half)
        _gate_rows(a_ref, cnt_ref, e1_ref, e2_ref, rk_ref, w_ref, range(PEER_ROWS),
                   range(hv * half // LANES, (hv + 1) * half // LANES))
        acc_ref[:, lanes] += jnp.dot(vt_ref[...], w_ref[:, lanes], preferred_element_type=_F32)
        a_ref[:, lanes] = lax.dot_general(un_ref[...], h2n_ref[lanes, :], _NT,
                                          preferred_element_type=_F32)

    @pl.when(k == n_steps - 1)
    def _():
        gt2 = mod_ref[0, :, 5 * d:6 * d]
        o_ref[...] = x1_ref[...] + gt2 * acc_ref[...].T


def _peer(h2, x1, mod_l, u, vt, e1t, cntt, e2t, rkt, *, geo):
    n, d = h2.shape
    tn = PEER_TOKENS
    te = PEER_ROWS * PEER_KEYS
    n_steps = u.shape[0] // te
    n_tok = n // tn
    lat_tiles, tiles_per_batch, batch = geo
    row = lambda t, k: (t, 0)

    def mod_idx(t, k):
        return (jnp.where(t < lat_tiles, t // tiles_per_batch, batch), 0, 0)

    rows_blk = pl.BlockSpec((PEER_HEADS, PEER_ROWS, tn), lambda t, k: (0, k, t))
    return pl.pallas_call(
        functools.partial(_peer_kernel, d=d, n_steps=n_steps),
        out_shape=jax.ShapeDtypeStruct((n, d), _F32),
        grid=(n // tn, n_steps),
        in_specs=[
            pl.BlockSpec((tn, d), lambda t, k: (jnp.minimum(t + (k + 1) // n_steps, n_tok - 1), 0)),
            pl.BlockSpec((tn, d), row),
            pl.BlockSpec((1, 1, N_MOD * d), mod_idx),
            pl.BlockSpec((te, d), lambda t, k: (0, 0)),
            pl.BlockSpec((te, d), lambda t, k: ((k + 1) % n_steps, 0)),
            pl.BlockSpec((d, te), lambda t, k: (0, k)),
            rows_blk,
            rows_blk,
            pl.BlockSpec((PEER_HEADS, PEER_KEYS // 2, tn), lambda t, k: (0, 0, t)),
            pl.BlockSpec((PEER_HEADS, PEER_KEYS // 2, tn), lambda t, k: (0, 0, t)),
        ],
        out_specs=pl.BlockSpec((tn, d), row),
        scratch_shapes=[
            pltpu.VMEM((d, tn), _F32),
            pltpu.VMEM((te, tn), _F32),
            pltpu.VMEM((te, tn), _BF16),
        ],
        compiler_params=_params("arbitrary", "arbitrary"),
        name="peer_experts",
    )(h2, x1, mod_l, u, u, vt, cntt, e1t, e2t, rkt)


def _final_kernel(x_ref, g_ref, o_ref):
    o_ref[...] = _rms(x_ref[...], g_ref[...])


def _final_norm(xs, g, n_rows):
    d = xs.shape[1]
    tm = PROJ_TILE
    return pl.pallas_call(
        _final_kernel,
        out_shape=jax.ShapeDtypeStruct((n_rows, d), _F32),
        grid=(n_rows // tm,),
        in_specs=[pl.BlockSpec((tm, d), lambda t: (t, 0)), _full(g.shape)],
        out_specs=pl.BlockSpec((tm, d), lambda t: (t, 0)),
        compiler_params=_params("arbitrary"),
        name="final_norm",
    )(xs, g)


def _rope_tables(seq):
    rows = seq // GRID_W
    r = jnp.repeat(jnp.arange(rows, dtype=_F32), GRID_W)
    col = jnp.tile(jnp.arange(GRID_W, dtype=_F32), rows)

    def cos_sin(rot_dim):
        n_freq = rot_dim // 4
        inv = ROPE_BASE ** (-jnp.arange(n_freq, dtype=_F32) / n_freq)
        ang = jnp.concatenate([r[:, None] * inv, col[:, None] * inv], axis=-1)
        return jnp.cos(ang), jnp.sin(ang)

    ca, sa = cos_sin(MLA_ROPE)
    one = jnp.ones((seq, MLA_NOPE), _F32)
    zero = jnp.zeros_like(one)
    z16 = jnp.zeros_like(ca)
    pad1 = jnp.ones((seq, LANES - MLA_NOPE - MLA_ROPE), _F32)
    pad0 = jnp.zeros_like(pad1)
    tab_a = [jnp.concatenate([one, ca, ca, pad1], axis=1),
             jnp.concatenate([zero, -sa, z16, pad0], axis=1),
             jnp.concatenate([zero, z16, sa, pad0], axis=1)]
    ch, sh = cos_sin(HEAD_DIM)
    z32 = jnp.zeros_like(ch)
    tab_h = [jnp.concatenate([ch, ch, ch, ch], axis=1),
             jnp.concatenate([-sh, z32, -sh, z32], axis=1),
             jnp.concatenate([z32, sh, z32, sh], axis=1)]
    lat = jnp.concatenate(tab_a + tab_h, axis=1)
    ident = jnp.concatenate([jnp.ones((PROJ_TILE, LANES), _F32),
                             jnp.zeros((PROJ_TILE, 2 * LANES), _F32)] * 2, axis=1)
    return jnp.concatenate([lat, ident], axis=0)


def _pack_w_in(w_in, d):
    splits = np.cumsum([MLA_Q_RANK, MLA_KV_RANK, MLA_ROPE, 512, 128, 128, 512, 128, 128])
    cq, ckv, kr, qb, kb, vb, qw, kw, vw, gates = jnp.split(w_in, splits, axis=-1)
    lead = w_in.shape[:-1]
    kr_pad = jnp.concatenate([jnp.zeros(lead + (MLA_NOPE,), w_in.dtype), kr,
                              jnp.zeros(lead + (LANES - MLA_NOPE - MLA_ROPE,), w_in.dtype)], axis=-1)

    def dup(w):
        g0, g1 = w[..., :HEAD_DIM], w[..., HEAD_DIM:]
        return jnp.concatenate([g0, g0, g1, g1], axis=-1)

    return jnp.concatenate([cq, ckv, kr_pad, qb, dup(kb), dup(vb), qw, dup(kw), dup(vw), gates],
                           axis=-1).astype(_BF16)


def _pack_w_uq(w_uq):
    depth, rank, _ = w_uq.shape
    w = w_uq.reshape(depth, rank, HEADS, MLA_NOPE + MLA_ROPE)
    w = jnp.pad(w, ((0, 0), (0, 0), (0, 0), (0, LANES - MLA_NOPE - MLA_ROPE)))
    return w.reshape(depth, rank, HEADS * LANES).astype(_BF16)


def _pack_w_ukv(w_ukv):
    depth, rank, _ = w_ukv.shape
    w = w_ukv.reshape(depth, rank, HEADS, MLA_NOPE + HEAD_DIM)
    k = jnp.pad(w[..., :MLA_NOPE], ((0, 0), (0, 0), (0, 0), (0, LANES - MLA_NOPE)))
    v = w[..., MLA_NOPE:]
    return jnp.concatenate([k.reshape(depth, rank, HEADS * LANES),
                            v.reshape(depth, rank, HEADS * HEAD_DIM)], axis=-1).astype(_BF16)


def kernel(x, c, ctx, c_ctx, w_mod, b_mod, g_attn, g_ffn, w_in, g_cq, g_ckv, w_uq, w_ukv,
           g_qn, g_kn, sink, w_oa, w_ob, w_ow, w_o, w_pq, sub_keys, peer_u, peer_v, g_final):
    batch, seq, d = x.shape
    n_ctx = ctx.shape[1]
    depth = w_mod.shape[0]
    assert n_ctx == ROW_TILE and seq % PEER_TOKENS == 0 and (batch * n_ctx) % PEER_TOKENS == 0
    assert batch < 8 and seq % GRID_W == 0 and seq >= 2 * WIN_Q_TILE and seq % WIN_Q_TILE == 0
    assert seq % FLASH_Q_TILE == 0 and seq % FLASH_K_CHUNK == 0 and seq % PROJ_TILE == 0
    n_lat = batch * seq
    n_all = n_lat + batch * n_ctx
    ctx_blk0 = n_lat // ROW_TILE

    xs = jnp.concatenate([x.reshape(n_lat, d), ctx.reshape(batch * n_ctx, d)], axis=0)
    cc = jnp.concatenate([c, c_ctx[None], jnp.zeros((8 - batch - 1, d), _F32)], axis=0)
    mod = _modulation(cc, w_mod, b_mod).reshape(depth, 8, 1, N_MOD * d)

    rope = _rope_tables(seq)
    eye = np.kron(np.eye(2, dtype=np.float32), np.ones((HEAD_DIM, HEAD_DIM), np.float32))
    bd = jnp.asarray(eye, _BF16)
    win = _pack_w_in(w_in, d)
    wuq = _pack_w_uq(w_uq)
    wukv = _pack_w_ukv(w_ukv)
    gqn_t = jnp.tile(g_qn, (1, HEADS))[:, None, :]
    gkn_t = jnp.tile(g_kn, (1, 2 * LANES // HEAD_DIM))[:, None, :]
    sink_t = jnp.broadcast_to(sink[:, :, None], (depth, HEADS, LANES))
    woa, wob, wow, wo, wpq = (w.astype(_BF16) for w in (w_oa, w_ob, w_ow, w_o, w_pq))
    subk = sub_keys.reshape(depth, 2 * PEER_HEADS, PEER_KEYS, PEER_HALF).astype(_BF16)
    u_bf = peer_u.astype(_BF16)
    vt_bf = jnp.swapaxes(peer_v, 1, 2).astype(_BF16)

    geo_row = (n_lat // PROJ_TILE, seq // PROJ_TILE, batch)
    geo_peer = (n_lat // PEER_TOKENS, seq // PEER_TOKENS, batch)
    lat_seg = (seq, lambda b, i: b)
    ctx_seg = (n_ctx, lambda b, i: ctx_blk0 + b)
    nqw = seq // WIN_Q_TILE
    per_tile = WIN_Q_TILE // WINDOW
    edge_blocks = seq // WINDOW
    band_segs = [
        (WINDOW, lambda b, i: b * edge_blocks + jnp.maximum(per_tile * i - 1, 0)),
        (WIN_Q_TILE, lambda b, i: b * nqw + i),
        (WINDOW, lambda b, i: b * edge_blocks + jnp.minimum(per_tile * (i + 1), edge_blocks - 1)),
        ctx_seg,
    ]

    for l in range(depth):
        last = l == depth - 1
        qa, ka, va, qb, kb, vb, qw, kw, vw, gates = _inproj(
            xs, mod[l], g_attn[l][None], win[l], g_cq[l][None], g_ckv[l][None], wuq[l], wukv[l],
            gqn_t[l], gkn_t[l], rope, bd, geo=geo_row)
        dense = dict(batch=batch, seq=seq, n_ctx=n_ctx)
        n_rows = n_lat if last else n_all
        o_lat = (_flash_attention(qa, ka, va, wide=True, **dense),
                 _flash_attention(qb, kb, vb, wide=False, **dense),
                 _attention(qw, kw, vw, sink_t[l], wide=False, segs=band_segs, band=True,
                            nq=nqw, q_block0=0, batch=batch, tq=WIN_Q_TILE))
        o_ctx = None
        if not last:
            cq = dict(nq=1, q_block0=ctx_blk0, batch=batch, segs=[ctx_seg], tq=n_ctx)
            o_ctx = (_attention(qa, ka, va, None, wide=True, base2=True, **cq),
                     _attention(qb, kb, vb, None, wide=False, base2=True, **cq),
                     _attention(qw, kw, vw, sink_t[l], wide=False, **cq))
        x1, h2, s1t, s2t = _merge(xs, mod[l], o_lat, o_ctx, gates, woa[l], wob[l], wow[l], wo[l],
                                  g_ffn[l][None], wpq[l], subk[l], n_rows=n_rows, geo=geo_row)
        e1t, cntt, e2t, rkt = _topk(s1t, s2t)
        xs = _peer(h2, x1, mod[l], u_bf[l], vt_bf[l], e1t, cntt, e2t, rkt, geo=geo_peer)
    out = _final_norm(xs, g_final[None], n_lat)
    return out.reshape(batch, seq, d)
```

```python
import functools
import math

import jax
import jax.numpy as jnp
import numpy as np
from jax import lax
from jax.experimental import pallas as pl
from jax.experimental.pallas import tpu as pltpu

GRID_W = 64
ROPE_BASE = 10000.0
NORM_EPS = 1e-6
NEG_INF = -1e30
LOG2E = math.log2(math.e)
WINDOW = 128
N_MOD = 6

HEADS = 8
HEAD_PAIRS = HEADS // 2
MLA_Q_RANK = 384
MLA_KV_RANK = 256
MLA_NOPE = 64
MLA_ROPE = 32
HEAD_DIM = 64
KV_HEADS = 2
LANES = 128

PEER_HEADS = 8
PEER_KEYS = 128
PEER_HALF = 128
PEER_TOPK = 16

ROW_TILE = 256
PROJ_TILE = 512
WIN_Q_TILE = 256
FLASH_Q_TILE = 1024
FLASH_K_CHUNK = 4096
PEER_TOKENS = 512
PEER_ROWS = 16
PEER_CHUNK = 32
PEER_A_PIECE = 512
TOPK_TOKENS = 512
VMEM_LIMIT = 56 * 1024 * 1024

OFF_CQ = 0
OFF_CKV = OFF_CQ + MLA_Q_RANK
OFF_KR = OFF_CKV + MLA_KV_RANK
OFF_QB = OFF_KR + LANES
OFF_KB = OFF_QB + HEADS * HEAD_DIM
OFF_VB = OFF_KB + 2 * LANES
OFF_QW = OFF_VB + 2 * LANES
OFF_KW = OFF_QW + HEADS * HEAD_DIM
OFF_VW = OFF_KW + 2 * LANES
OFF_GATES = OFF_VW + 2 * LANES

_NT = (((1,), (1,)), ((), ()))
_F32 = jnp.float32
_BF16 = jnp.bfloat16


def _params(*semantics):
    return pltpu.CompilerParams(dimension_semantics=semantics, vmem_limit_bytes=VMEM_LIMIT)


def _full(shape):
    return pl.BlockSpec(shape, lambda *_: (0,) * len(shape), pipeline_mode=pl.Buffered(1))


def _rms(x, g):
    return x * lax.rsqrt(jnp.mean(x * x, axis=-1, keepdims=True) + NORM_EPS) * g


def _rope(x, tabs, shift):
    cos, sin_lo, sin_hi = tabs
    outs = []
    for c in range(x.shape[1] // LANES):
        xc = x[:, c * LANES:(c + 1) * LANES]
        up = pltpu.roll(xc, LANES - shift, 1)
        dn = pltpu.roll(xc, shift, 1)
        outs.append(xc * cos + up * sin_lo + dn * sin_hi)
    return outs[0] if len(outs) == 1 else jnp.concatenate(outs, axis=1)


def _head_rms(x, bd, g):
    outs = []
    for c in range(x.shape[1] // LANES):
        xc = x[:, c * LANES:(c + 1) * LANES]
        sq = xc * xc
        hi = sq.astype(_BF16)
        lo = (sq - hi.astype(_F32)).astype(_BF16)
        ssq = (jnp.dot(hi, bd, preferred_element_type=_F32)
               + jnp.dot(lo, bd, preferred_element_type=_F32))
        outs.append(xc * lax.rsqrt(ssq * (1.0 / HEAD_DIM) + NORM_EPS))
    y = outs[0] if len(outs) == 1 else jnp.concatenate(outs, axis=1)
    return y * g


def _mod_kernel(c_ref, w_ref, b_ref, o_ref):
    c = c_ref[...]
    a = c / (1.0 + jnp.exp(-c))
    o_ref[0] = jnp.dot(a, w_ref[0], preferred_element_type=_F32,
                       precision=lax.Precision.HIGHEST) + b_ref[0]


def _modulation(cc, w_mod, b_mod):
    depth, d, cols = w_mod.shape
    tn = cols // 4
    return pl.pallas_call(
        _mod_kernel,
        out_shape=jax.ShapeDtypeStruct((depth, 8, cols), _F32),
        grid=(depth, cols // tn),
        in_specs=[
            pl.BlockSpec((8, d), lambda l, j: (0, 0)),
            pl.BlockSpec((1, d, tn), lambda l, j: (l, 0, j)),
            pl.BlockSpec((1, 1, tn), lambda l, j: (l, 0, j)),
        ],
        out_specs=pl.BlockSpec((1, 8, tn), lambda l, j: (l, 0, j)),
        compiler_params=_params("arbitrary", "arbitrary"),
        name="modulation",
    )(cc, w_mod, b_mod.reshape(depth, 1, cols))


def _inproj_kernel(x_ref, mod_ref, gattn_ref, win_ref, gcq_ref, gckv_ref, wuq_ref, wukv_ref,
                   gqn_ref, gkn_ref, rope_ref, bd_ref,
                   qa_ref, ka_ref, va_ref, qb_ref, kb_ref, vb_ref, qw_ref, kw_ref, vw_ref,
                   gates_ref, *, d, scale_a, scale_h):
    x = x_ref[...]
    sh1 = mod_ref[0, :, 0:d]
    sc1 = mod_ref[0, :, d:2 * d]
    h = (_rms(x, gattn_ref[...]) * (1.0 + sc1) + sh1).astype(_BF16)

    def proj(off, width):
        return jnp.dot(h, win_ref[:, off:off + width], preferred_element_type=_F32)

    rope_a = tuple(rope_ref[:, i * LANES:(i + 1) * LANES] for i in range(3))
    rope_h = tuple(rope_ref[:, i * LANES:(i + 1) * LANES] for i in range(3, 6))
    bd = bd_ref[...]

    cq = _rms(proj(OFF_CQ, MLA_Q_RANK), gcq_ref[...]).astype(_BF16)
    qa = jnp.dot(cq, wuq_ref[...], preferred_element_type=_F32)
    qa_ref[...] = (_rope(qa, rope_a, MLA_ROPE // 2) * (scale_a * LOG2E)).astype(_BF16)
    ckv = _rms(proj(OFF_CKV, MLA_KV_RANK), gckv_ref[...]).astype(_BF16)
    kva = jnp.dot(ckv, wukv_ref[...], preferred_element_type=_F32)
    kr = _rope(proj(OFF_KR, LANES), rope_a, MLA_ROPE // 2)
    ka = kva[:, :HEADS * LANES] + jnp.concatenate([kr] * HEADS, axis=1)
    ka_ref[...] = ka.astype(_BF16)
    va_ref[...] = kva[:, HEADS * LANES:].astype(_BF16)

    qb = _head_rms(proj(OFF_QB, HEADS * HEAD_DIM), bd, gqn_ref[...])
    qb_ref[...] = (_rope(qb, rope_h, HEAD_DIM // 2) * (scale_h * LOG2E)).astype(_BF16)
    kb = _head_rms(proj(OFF_KB, 2 * LANES), bd, gkn_ref[...])
    kb_ref[...] = _rope(kb, rope_h, HEAD_DIM // 2).astype(_BF16)
    vb_ref[...] = proj(OFF_VB, 2 * LANES).astype(_BF16)

    qw_ref[...] = (_rope(proj(OFF_QW, HEADS * HEAD_DIM), rope_h, HEAD_DIM // 2) * scale_h).astype(_BF16)
    kw_ref[...] = _rope(proj(OFF_KW, 2 * LANES), rope_h, HEAD_DIM // 2).astype(_BF16)
    vw_ref[...] = proj(OFF_VW, 2 * LANES).astype(_BF16)

    for k in range(3):
        g = proj(OFF_GATES + k * d, d)
        gates_ref[:, k * d:(k + 1) * d] = (1.0 / (1.0 + jnp.exp(-g))).astype(_BF16)


def _inproj(xs, mod_l, gattn, win, gcq, gckv, wuq, wukv, gqn_t, gkn_t, rope, bd, *, geo):
    n, d = xs.shape
    tm = PROJ_TILE
    lat_tiles, tiles_per_batch, batch = geo
    row = lambda t: (t, 0)

    def mod_idx(t):
        return (jnp.where(t < lat_tiles, t // tiles_per_batch, batch), 0, 0)

    def rope_idx(t):
        return (jnp.where(t < lat_tiles, t % tiles_per_batch, tiles_per_batch), 0)

    widths = (HEADS * LANES, HEADS * LANES, HEADS * HEAD_DIM, HEADS * HEAD_DIM, 2 * LANES, 2 * LANES,
              HEADS * HEAD_DIM, 2 * LANES, 2 * LANES, 3 * d)
    return pl.pallas_call(
        functools.partial(_inproj_kernel, d=d, scale_a=(MLA_NOPE + MLA_ROPE) ** -0.5,
                          scale_h=HEAD_DIM ** -0.5),
        out_shape=[jax.ShapeDtypeStruct((n, w), _BF16) for w in widths],
        grid=(n // tm,),
        in_specs=[
            pl.BlockSpec((tm, d), row),
            pl.BlockSpec((1, 1, N_MOD * d), mod_idx),
            _full(gattn.shape), _full(win.shape), _full(gcq.shape), _full(gckv.shape),
            _full(wuq.shape), _full(wukv.shape), _full(gqn_t.shape), _full(gkn_t.shape),
            pl.BlockSpec((tm, 6 * LANES), rope_idx),
            _full(bd.shape),
        ],
        out_specs=[pl.BlockSpec((tm, w), row) for w in widths],
        compiler_params=_params("arbitrary"),
        name="inproj",
    )(xs, mod_l, gattn, win, gcq, gckv, wuq, wukv, gqn_t, gkn_t, rope, bd)


def _attn_kernel(*refs, nseg, wide, has_sink, band, nq, base2):
    q_ref = refs[0]
    k_refs = refs[1:1 + nseg]
    v_refs = refs[1 + nseg:1 + 2 * nseg]
    sink_ref = refs[1 + 2 * nseg] if has_sink else None
    o_ref = refs[-1]
    pair = pl.program_id(1)
    qi = pl.program_id(2)
    tq = q_ref.shape[0]
    lane = lax.broadcasted_iota(jnp.int32, (tq, LANES), 1)

    masks = [None] * nseg
    if band:
        r = lax.broadcasted_iota(jnp.int32, (tq, WINDOW), 0)
        c = lax.broadcasted_iota(jnp.int32, (tq, WINDOW), 1)
        masks[0] = c >= r + jnp.where(qi > 0, 0, WINDOW)
        masks[2] = c <= r - (tq - WINDOW) - jnp.where(qi < nq - 1, 0, tq)
        r = lax.broadcasted_iota(jnp.int32, (tq, tq), 0)
        c = lax.broadcasted_iota(jnp.int32, (tq, tq), 1)
        masks[1] = jnp.abs(r - c) <= WINDOW

    for pp in range(1 if wide else HEAD_PAIRS):
        qcols = slice(pp * LANES, (pp + 1) * LANES)
        kvcols = slice((pp // 2) * LANES, (pp // 2 + 1) * LANES)
        outs = []
        for hh in range(2):
            if wide:
                q = q_ref[:, hh * LANES:(hh + 1) * LANES]
                ks = [k[:, hh * LANES:(hh + 1) * LANES] for k in k_refs]
                vs = [v[...] for v in v_refs]
                head = 2 * pair + hh
            else:
                own = (lane[:1] < HEAD_DIM) if hh == 0 else (lane[:1] >= HEAD_DIM)
                q = q_ref[:, qcols] * own.astype(_F32).astype(_BF16)
                ks = [k[:, kvcols] for k in k_refs]
                vs = [v[:, kvcols] for v in v_refs]
                head = 2 * pp + hh
            ss = [lax.dot_general(q, k, _NT, preferred_element_type=_F32) for k in ks]
            ss = [s if m is None else jnp.where(m, s, NEG_INF) for s, m in zip(ss, masks)]
            m = functools.reduce(jnp.maximum, [jnp.max(s, axis=-1, keepdims=True) for s in ss])
            if has_sink:
                sink = sink_ref[pl.ds(head, 1), 0:1]
                m = jnp.maximum(m, sink)
            ps = [(jnp.exp2 if base2 else jnp.exp)(s - m) for s in ss]
            l = functools.reduce(jnp.add, [jnp.sum(p, axis=-1, keepdims=True) for p in ps])
            if has_sink:
                l = l + jnp.exp(sink - m)
            o = functools.reduce(jnp.add, [jnp.dot(p.astype(_BF16), v, preferred_element_type=_F32)
                                           for p, v in zip(ps, vs)])
            outs.append(o / l)
        o_ref[:, qcols] = jnp.where(lane < HEAD_DIM, outs[0], outs[1]).astype(o_ref.dtype)


def _flash_kernel(q_ref, k_ref, kc_ref, v_ref, vc_ref, o_ref, *, wide, chunk):
    tq = q_ref.shape[0]
    lane = lax.broadcasted_iota(jnp.int32, (1, LANES), 1)
    qs, kcols = [], []
    for hh in range(2):
        if wide:
            qs.append(q_ref[:, hh * LANES:(hh + 1) * LANES])
            kcols.append(slice(hh * LANES, (hh + 1) * LANES))
        else:
            own = (lane < HEAD_DIM) if hh == 0 else (lane >= HEAD_DIM)
            qs.append(q_ref[...] * own.astype(_F32).astype(_BF16))
            kcols.append(slice(0, LANES))
    m = [jnp.full((tq, 1), -jnp.inf, _F32) for _ in range(2)]
    l = [jnp.zeros((tq, 1), _F32) for _ in range(2)]
    acc = [jnp.zeros((tq, LANES), _F32) for _ in range(2)]
    pieces = [(k_ref, v_ref, r0, chunk) for r0 in range(0, k_ref.shape[0], chunk)]
    pieces.append((kc_ref, vc_ref, 0, kc_ref.shape[0]))
    for kr, vr, r0, rows in pieces:
        for hh in range(2):
            s = lax.dot_general(qs[hh], kr[r0:r0 + rows, kcols[hh]], _NT,
                                preferred_element_type=_F32)
            m_new = jnp.maximum(m[hh], jnp.max(s, axis=-1, keepdims=True))
            alpha = jnp.exp2(m[hh] - m_new)
            p = jnp.exp2(s - m_new)
            l[hh] = alpha * l[hh] + jnp.sum(p, axis=-1, keepdims=True)
            acc[hh] = alpha * acc[hh] + jnp.dot(p.astype(_BF16), vr[r0:r0 + rows, :],
                                                preferred_element_type=_F32)
            m[hh] = m_new
    o_ref[...] = jnp.where(lane < HEAD_DIM, acc[0] / l[0], acc[1] / l[1]).astype(o_ref.dtype)


def _flash_attention(q, k, v, *, wide, batch, seq, n_ctx):
    tq = FLASH_Q_TILE
    nq = seq // tq
    qw = 2 * LANES if wide else LANES
    kdiv = 1 if wide else 2
    ctx_blk0 = batch * seq // n_ctx
    return pl.pallas_call(
        functools.partial(_flash_kernel, wide=wide, chunk=min(FLASH_K_CHUNK, seq)),
        out_shape=jax.ShapeDtypeStruct((batch * seq, HEADS * HEAD_DIM), _BF16),
        grid=(batch, HEAD_PAIRS, nq),
        in_specs=[
            pl.BlockSpec((tq, qw), lambda b, p, i: (b * nq + i, p)),
            pl.BlockSpec((seq, qw), lambda b, p, i: (b, p // kdiv)),
            pl.BlockSpec((n_ctx, qw), lambda b, p, i: (ctx_blk0 + b, p // kdiv)),
            pl.BlockSpec((seq, LANES), lambda b, p, i: (b, p // kdiv)),
            pl.BlockSpec((n_ctx, LANES), lambda b, p, i: (ctx_blk0 + b, p // kdiv)),
        ],
        out_specs=pl.BlockSpec((tq, LANES), lambda b, p, i: (b * nq + i, p)),
        compiler_params=_params("arbitrary", "arbitrary", "arbitrary"),
        name="flash_attention",
    )(q, k, k, v, v)


def _attention(q, k, v, sink, *, wide, segs, nq, q_block0, batch, tq, band=False, base2=False):
    pairs = HEAD_PAIRS if wide else 1
    qw, kw, vw, ow = ((2 * LANES, 2 * LANES, LANES, LANES) if wide
                      else (HEADS * HEAD_DIM, KV_HEADS * LANES, KV_HEADS * LANES, HEADS * HEAD_DIM))
    in_specs = [pl.BlockSpec((tq, qw), lambda b, p, i: (q_block0 + b * nq + i, p))]
    for width in (kw, vw):
        for rows, fn in segs:
            in_specs.append(pl.BlockSpec((rows, width), functools.partial(
                lambda b, p, i, fn: (fn(b, i), p), fn=fn)))
    args = [q] + [k] * len(segs) + [v] * len(segs)
    if sink is not None:
        in_specs.append(_full(sink.shape))
        args.append(sink)
    return pl.pallas_call(
        functools.partial(_attn_kernel, nseg=len(segs), wide=wide, has_sink=sink is not None,
                          band=band, nq=nq, base2=base2),
        out_shape=jax.ShapeDtypeStruct((batch * nq * tq, HEADS * HEAD_DIM), _BF16),
        grid=(batch, pairs, nq),
        in_specs=in_specs,
        out_specs=pl.BlockSpec((tq, ow), lambda b, p, i: (b * nq + i, p)),
        compiler_params=_params("arbitrary", "arbitrary", "arbitrary"),
        name="attention",
    )(*args)


def _merge_kernel(*refs, d, lat_tiles, has_ctx):
    n_o = 6 if has_ctx else 3
    x_ref, mod_ref = refs[:2]
    o_refs = refs[2:2 + n_o]
    (gates_ref, woa_ref, wob_ref, wow_ref, wo_ref, gffn_ref, wpq_ref, subk_ref,
     x1_ref, h2_ref, s1_ref, s2_ref) = refs[2 + n_o:14 + n_o]
    if has_ctx:
        o_scr = refs[14 + n_o]
        tile = pl.program_id(0)

        @pl.when(tile < lat_tiles)
        def _():
            for k in range(3):
                o_scr[k] = o_refs[k][...]

        @pl.when(tile >= lat_tiles)
        def _():
            for k in range(3):
                o_scr[k] = o_refs[3 + k][...]

        outs = [o_scr[k] for k in range(3)]
    else:
        outs = [o_refs[k][...] for k in range(3)]
    gt1 = mod_ref[0, :, 2 * d:3 * d]
    sh2 = mod_ref[0, :, 3 * d:4 * d]
    sc2 = mod_ref[0, :, 4 * d:5 * d]
    m = None
    for k, w_ref in enumerate((woa_ref, wob_ref, wow_ref)):
        t = gates_ref[:, k * d:(k + 1) * d].astype(_F32) * jnp.dot(
            outs[k], w_ref[...], preferred_element_type=_F32)
        m = t if m is None else m + t
    y = jnp.dot(m.astype(_BF16), wo_ref[...], preferred_element_type=_F32)
    x1 = x_ref[...] + gt1 * y
    x1_ref[...] = x1
    h2 = (_rms(x1, gffn_ref[...]) * (1.0 + sc2) + sh2).astype(_BF16)
    h2_ref[...] = h2
    q = jnp.dot(h2, wpq_ref[...], preferred_element_type=_F32).astype(_BF16)
    for g in range(2 * PEER_HEADS):
        s = lax.dot_general(subk_ref[g], q[:, g * PEER_HALF:(g + 1) * PEER_HALF], _NT,
                            preferred_element_type=_F32)
        if g % 2 == 0:
            s1_ref[g // 2] = s
        else:
            s2_ref[g // 2] = s


def _merge(xs, mod_l, o_lat, o_ctx, gates, woa, wob, wow, wo, gffn, wpq, subk, *, n_rows, geo):
    d = xs.shape[1]
    tm = PROJ_TILE
    lat_tiles, tiles_per_batch, batch = geo
    row = lambda t: (t, 0)
    ow = HEADS * HEAD_DIM

    def mod_idx(t):
        return (jnp.where(t < lat_tiles, t // tiles_per_batch, batch), 0, 0)

    o_specs = [pl.BlockSpec((tm, ow), lambda t: (jnp.minimum(t, lat_tiles - 1), 0))] * 3
    o_args = list(o_lat)
    scratch = []
    if o_ctx is not None:
        o_specs += [pl.BlockSpec((tm, ow), lambda t: (jnp.maximum(t - lat_tiles, 0), 0))] * 3
        o_args += list(o_ctx)
        scratch = [pltpu.VMEM((3, tm, ow), _BF16)]
    return pl.pallas_call(
        functools.partial(_merge_kernel, d=d, lat_tiles=lat_tiles, has_ctx=o_ctx is not None),
        out_shape=[
            jax.ShapeDtypeStruct((n_rows, d), _F32),
            jax.ShapeDtypeStruct((n_rows, d), _BF16),
            jax.ShapeDtypeStruct((PEER_HEADS, PEER_KEYS, n_rows), _F32),
            jax.ShapeDtypeStruct((PEER_HEADS, PEER_KEYS, n_rows), _F32),
        ],
        grid=(n_rows // tm,),
        in_specs=[
            pl.BlockSpec((tm, d), row),
            pl.BlockSpec((1, 1, N_MOD * d), mod_idx),
            *o_specs,
            pl.BlockSpec((tm, 3 * d), row),
            _full(woa.shape), _full(wob.shape), _full(wow.shape), _full(wo.shape),
            _full(gffn.shape), _full(wpq.shape), _full(subk.shape),
        ],
        out_specs=[
            pl.BlockSpec((tm, d), row),
            pl.BlockSpec((tm, d), row),
            pl.BlockSpec((PEER_HEADS, PEER_KEYS, tm), lambda t: (0, 0, t)),
            pl.BlockSpec((PEER_HEADS, PEER_KEYS, tm), lambda t: (0, 0, t)),
        ],
        scratch_shapes=scratch,
        compiler_params=_params("arbitrary"),
        name="merge",
    )(xs, mod_l, *o_args, gates, woa, wob, wow, wo, gffn, wpq, subk)


NO_RANK = 64.0


def _top_values(w, k, want_rank=False):
    vals = []
    rank = jnp.full(w.shape, NO_RANK, _F32) if want_rank else None
    for r in range(k):
        m = jnp.max(w, axis=0, keepdims=True)
        vals.append(m)
        hit = w == m
        if want_rank:
            rank = jnp.where(hit, float(r), rank)
        w = jnp.where(hit, -jnp.inf, w)
    return (vals, rank) if want_rank else vals


def _stack_rows(vals, n_rows):
    rows = lax.broadcasted_iota(jnp.int32, (n_rows, vals[0].shape[1]), 0)
    out = jnp.full(rows.shape, -jnp.inf, _F32)
    for k, v in enumerate(vals):
        out = jnp.where(rows == k, v, out)
    return out


def _topk_kernel(s1_ref, s2_ref, e1_ref, cnt_ref, e2_ref, rk_ref):
    n_top = PEER_TOPK + 1

    def head(h, carry):
        s1 = s1_ref[h]
        s2 = s2_ref[h]
        t1 = _top_values(s1, n_top)
        t2, rank2 = _top_values(s2, n_top, want_rank=True)
        t2c = _stack_rows(t2, 24)
        cand = [t1[0] + t2c]
        cand += [t1[a] + t2c[:8] for a in range(1, 8)]
        cand += [_stack_rows(t1[8:], 16) + t2[0]]
        best = _top_values(jnp.concatenate(cand, axis=0), n_top)
        z = functools.reduce(jnp.add, [jnp.exp(v - best[0]) for v in best[:PEER_TOPK]])
        tau = 0.5 * (best[PEER_TOPK - 1] + best[PEER_TOPK])
        need = tau - s1
        cnt = jnp.zeros(s1.shape, _F32)
        for b in range(PEER_TOPK):
            cnt = jnp.where(t2[b] >= need, float(b + 1), cnt)
        e1_ref[h] = jnp.exp(s1 - t1[0]) / z
        cnt_ref[h] = cnt
        e2_ref[h] = pltpu.bitcast(jnp.exp(s2 - t2[0]).astype(_BF16), jnp.uint32)
        rk_ref[h] = pltpu.bitcast(rank2.astype(_BF16), jnp.uint32)
        return carry

    lax.fori_loop(0, PEER_HEADS, head, 0)


def _topk(s1t, s2t):
    n = s1t.shape[2]
    tk = TOPK_TOKENS
    blk = pl.BlockSpec((PEER_HEADS, PEER_KEYS, tk), lambda t: (0, 0, t))
    packed = pl.BlockSpec((PEER_HEADS, PEER_KEYS // 2, tk), lambda t: (0, 0, t))
    return pl.pallas_call(
        _topk_kernel,
        out_shape=[jax.ShapeDtypeStruct(s1t.shape, _F32)] * 2
        + [jax.ShapeDtypeStruct((PEER_HEADS, PEER_KEYS // 2, n), jnp.uint32)] * 2,
        grid=(n // tk,),
        in_specs=[blk, blk],
        out_specs=[blk, blk, packed, packed],
        compiler_params=_params("arbitrary"),
        name="peer_topk",
    )(s1t, s2t)


def _gate_rows(a_ref, cnt_ref, e1_ref, e2_ref, rk_ref, w_ref, first_keys, lane_tiles):
    shape = (PEER_CHUNK, LANES)
    for ii in first_keys:
        for lt in lane_tiles:
            ls = slice(lt * LANES, (lt + 1) * LANES)
            cntb = [jnp.broadcast_to(cnt_ref[h, ii:ii + 1, ls], shape).astype(_BF16)
                    for h in range(PEER_HEADS)]
            e1b = [jnp.broadcast_to(e1_ref[h, ii:ii + 1, ls], shape).astype(_BF16)
                   for h in range(PEER_HEADS)]
            for jc in range(PEER_KEYS // PEER_CHUNK):
                rows = slice(jc * PEER_CHUNK, (jc + 1) * PEER_CHUNK)
                words = slice(rows.start // 2, rows.stop // 2)
                g = None
                for h in range(PEER_HEADS):
                    e2 = pltpu.bitcast(e2_ref[h, words, ls], _BF16)
                    rk = pltpu.bitcast(rk_ref[h, words, ls], _BF16)
                    t = e1b[h] * jnp.where(rk < cntb[h], e2, jnp.zeros_like(e2))
                    g = t if g is None else g + t
                erows = slice(ii * PEER_KEYS + rows.start, ii * PEER_KEYS + rows.stop)
                a = a_ref[erows, ls]
                act = (0.5 * a) * (1.0 + lax.erf(a * (1.0 / math.sqrt(2.0))))
                w_ref[erows, ls] = g * act.astype(_BF16)


def _peer_kernel(h2n_ref, x1_ref, mod_ref, u0_ref, un_ref, vt_ref, cnt_ref, e1_ref, e2_ref, rk_ref,
                 gfin_ref, o_ref, acc_ref, a_ref, w_ref, *, d, n_steps, final):
    k = pl.program_id(1)

    @pl.when(jnp.logical_and(pl.program_id(0) == 0, k == 0))
    def _():
        a_ref[...] = lax.dot_general(u0_ref[...], h2n_ref[...], _NT, preferred_element_type=_F32)

    @pl.when(k == 0)
    def _():
        acc_ref[...] = jnp.zeros_like(acc_ref)

    half = a_ref.shape[1] // 2
    for hv in range(2):
        lanes = slice(hv * half, (hv + 1) * half)
        _gate_rows(a_ref, cnt_ref, e1_ref, e2_ref, rk_ref, w_ref, range(PEER_ROWS),
                   range(hv * half // LANES, (hv + 1) * half // LANES))
        acc_ref[:, lanes] += jnp.dot(vt_ref[...], w_ref[:, lanes], preferred_element_type=_F32)
        a_ref[:, lanes] = lax.dot_general(un_ref[...], h2n_ref[lanes, :], _NT,
                                          preferred_element_type=_F32)

    @pl.when(k == n_steps - 1)
    def _():
        gt2 = mod_ref[0, :, 5 * d:6 * d]
        x2 = x1_ref[...] + gt2 * acc_ref[...].T
        o_ref[...] = _rms(x2, gfin_ref[...]) if final else x2


def _peer(h2, x1, mod_l, u, vt, e1t, cntt, e2t, rkt, gfin, *, geo, final):
    n, d = h2.shape
    tn = PEER_TOKENS
    te = PEER_ROWS * PEER_KEYS
    n_steps = u.shape[0] // te
    n_tok = n // tn
    lat_tiles, tiles_per_batch, batch = geo
    row = lambda t, k: (t, 0)

    def mod_idx(t, k):
        return (jnp.where(t < lat_tiles, t // tiles_per_batch, batch), 0, 0)

    rows_blk = pl.BlockSpec((PEER_HEADS, PEER_ROWS, tn), lambda t, k: (0, k, t))
    return pl.pallas_call(
        functools.partial(_peer_kernel, d=d, n_steps=n_steps, final=final),
        out_shape=jax.ShapeDtypeStruct((n, d), _F32),
        grid=(n // tn, n_steps),
        in_specs=[
            pl.BlockSpec((tn, d), lambda t, k: (jnp.minimum(t + (k + 1) // n_steps, n_tok - 1), 0)),
            pl.BlockSpec((tn, d), row),
            pl.BlockSpec((1, 1, N_MOD * d), mod_idx),
            pl.BlockSpec((te, d), lambda t, k: (0, 0)),
            pl.BlockSpec((te, d), lambda t, k: ((k + 1) % n_steps, 0)),
            pl.BlockSpec((d, te), lambda t, k: (0, k)),
            rows_blk,
            rows_blk,
            pl.BlockSpec((PEER_HEADS, PEER_KEYS // 2, tn), lambda t, k: (0, 0, t)),
            pl.BlockSpec((PEER_HEADS, PEER_KEYS // 2, tn), lambda t, k: (0, 0, t)),
            _full(gfin.shape),
        ],
        out_specs=pl.BlockSpec((tn, d), row),
        scratch_shapes=[
            pltpu.VMEM((d, tn), _F32),
            pltpu.VMEM((te, tn), _F32),
            pltpu.VMEM((te, tn), _BF16),
        ],
        compiler_params=_params("arbitrary", "arbitrary"),
        name="peer_experts",
    )(h2, x1, mod_l, u, u, vt, cntt, e1t, e2t, rkt, gfin)


def _rope_tables(seq):
    rows = seq // GRID_W
    r = jnp.repeat(jnp.arange(rows, dtype=_F32), GRID_W)
    col = jnp.tile(jnp.arange(GRID_W, dtype=_F32), rows)

    def cos_sin(rot_dim):
        n_freq = rot_dim // 4
        inv = ROPE_BASE ** (-jnp.arange(n_freq, dtype=_F32) / n_freq)
        ang = jnp.concatenate([r[:, None] * inv, col[:, None] * inv], axis=-1)
        return jnp.cos(ang), jnp.sin(ang)

    ca, sa = cos_sin(MLA_ROPE)
    one = jnp.ones((seq, MLA_NOPE), _F32)
    zero = jnp.zeros_like(one)
    z16 = jnp.zeros_like(ca)
    pad1 = jnp.ones((seq, LANES - MLA_NOPE - MLA_ROPE), _F32)
    pad0 = jnp.zeros_like(pad1)
    tab_a = [jnp.concatenate([one, ca, ca, pad1], axis=1),
             jnp.concatenate([zero, -sa, z16, pad0], axis=1),
             jnp.concatenate([zero, z16, sa, pad0], axis=1)]
    ch, sh = cos_sin(HEAD_DIM)
    z32 = jnp.zeros_like(ch)
    tab_h = [jnp.concatenate([ch, ch, ch, ch], axis=1),
             jnp.concatenate([-sh, z32, -sh, z32], axis=1),
             jnp.concatenate([z32, sh, z32, sh], axis=1)]
    lat = jnp.concatenate(tab_a + tab_h, axis=1)
    ident = jnp.concatenate([jnp.ones((PROJ_TILE, LANES), _F32),
                             jnp.zeros((PROJ_TILE, 2 * LANES), _F32)] * 2, axis=1)
    return jnp.concatenate([lat, ident], axis=0)


def _pack_w_in(w_in, d):
    splits = np.cumsum([MLA_Q_RANK, MLA_KV_RANK, MLA_ROPE, 512, 128, 128, 512, 128, 128])
    cq, ckv, kr, qb, kb, vb, qw, kw, vw, gates = jnp.split(w_in, splits, axis=-1)
    lead = w_in.shape[:-1]
    kr_pad = jnp.concatenate([jnp.zeros(lead + (MLA_NOPE,), w_in.dtype), kr,
                              jnp.zeros(lead + (LANES - MLA_NOPE - MLA_ROPE,), w_in.dtype)], axis=-1)

    def dup(w):
        g0, g1 = w[..., :HEAD_DIM], w[..., HEAD_DIM:]
        return jnp.concatenate([g0, g0, g1, g1], axis=-1)

    return jnp.concatenate([cq, ckv, kr_pad, qb, dup(kb), dup(vb), qw, dup(kw), dup(vw), gates],
                           axis=-1).astype(_BF16)


def _pack_w_uq(w_uq):
    depth, rank, _ = w_uq.shape
    w = w_uq.reshape(depth, rank, HEADS, MLA_NOPE + MLA_ROPE)
    w = jnp.pad(w, ((0, 0), (0, 0), (0, 0), (0, LANES - MLA_NOPE - MLA_ROPE)))
    return w.reshape(depth, rank, HEADS * LANES).astype(_BF16)


def _pack_w_ukv(w_ukv):
    depth, rank, _ = w_ukv.shape
    w = w_ukv.reshape(depth, rank, HEADS, MLA_NOPE + HEAD_DIM)
    k = jnp.pad(w[..., :MLA_NOPE], ((0, 0), (0, 0), (0, 0), (0, LANES - MLA_NOPE)))
    v = w[..., MLA_NOPE:]
    return jnp.concatenate([k.reshape(depth, rank, HEADS * LANES),
                            v.reshape(depth, rank, HEADS * HEAD_DIM)], axis=-1).astype(_BF16)


def kernel(x, c, ctx, c_ctx, w_mod, b_mod, g_attn, g_ffn, w_in, g_cq, g_ckv, w_uq, w_ukv,
           g_qn, g_kn, sink, w_oa, w_ob, w_ow, w_o, w_pq, sub_keys, peer_u, peer_v, g_final):
    batch, seq, d = x.shape
    n_ctx = ctx.shape[1]
    depth = w_mod.shape[0]
    assert n_ctx == ROW_TILE and seq % PEER_TOKENS == 0 and (batch * n_ctx) % PEER_TOKENS == 0
    assert batch < 8 and seq % GRID_W == 0 and seq >= 2 * WIN_Q_TILE and seq % WIN_Q_TILE == 0
    assert seq % FLASH_Q_TILE == 0 and seq % min(FLASH_K_CHUNK, seq) == 0 and seq % PROJ_TILE == 0
    n_lat = batch * seq
    n_all = n_lat + batch * n_ctx
    ctx_blk0 = n_lat // ROW_TILE

    xs = jnp.concatenate([x.reshape(n_lat, d), ctx.reshape(batch * n_ctx, d)], axis=0)
    cc = jnp.concatenate([c, c_ctx[None], jnp.zeros((8 - batch - 1, d), _F32)], axis=0)
    mod = _modulation(cc, w_mod, b_mod).reshape(depth, 8, 1, N_MOD * d)

    rope = _rope_tables(seq)
    eye = np.kron(np.eye(2, dtype=np.float32), np.ones((HEAD_DIM, HEAD_DIM), np.float32))
    bd = jnp.asarray(eye, _BF16)
    win = _pack_w_in(w_in, d)
    wuq = _pack_w_uq(w_uq)
    wukv = _pack_w_ukv(w_ukv)
    gqn_t = jnp.tile(g_qn, (1, HEADS))[:, None, :]
    gkn_t = jnp.tile(g_kn, (1, 2 * LANES // HEAD_DIM))[:, None, :]
    sink_t = jnp.broadcast_to(sink[:, :, None], (depth, HEADS, LANES))
    woa, wob, wow, wo, wpq = (w.astype(_BF16) for w in (w_oa, w_ob, w_ow, w_o, w_pq))
    subk = sub_keys.reshape(depth, 2 * PEER_HEADS, PEER_KEYS, PEER_HALF).astype(_BF16)
    u_bf = peer_u.astype(_BF16)
    vt_bf = jnp.swapaxes(peer_v, 1, 2).astype(_BF16)

    geo_row = (n_lat // PROJ_TILE, seq // PROJ_TILE, batch)
    geo_peer = (n_lat // PEER_TOKENS, seq // PEER_TOKENS, batch)
    lat_seg = (seq, lambda b, i: b)
    ctx_seg = (n_ctx, lambda b, i: ctx_blk0 + b)
    nqw = seq // WIN_Q_TILE
    per_tile = WIN_Q_TILE // WINDOW
    edge_blocks = seq // WINDOW
    band_segs = [
        (WINDOW, lambda b, i: b * edge_blocks + jnp.maximum(per_tile * i - 1, 0)),
        (WIN_Q_TILE, lambda b, i: b * nqw + i),
        (WINDOW, lambda b, i: b * edge_blocks + jnp.minimum(per_tile * (i + 1), edge_blocks - 1)),
        ctx_seg,
    ]

    for l in range(depth):
        last = l == depth - 1
        qa, ka, va, qb, kb, vb, qw, kw, vw, gates = _inproj(
            xs, mod[l], g_attn[l][None], win[l], g_cq[l][None], g_ckv[l][None], wuq[l], wukv[l],
            gqn_t[l], gkn_t[l], rope, bd, geo=geo_row)
        dense = dict(batch=batch, seq=seq, n_ctx=n_ctx)
        n_rows = n_lat if last else n_all
        o_lat = (_flash_attention(qa, ka, va, wide=True, **dense),
                 _flash_attention(qb, kb, vb, wide=False, **dense),
                 _attention(qw, kw, vw, sink_t[l], wide=False, segs=band_segs, band=True,
                            nq=nqw, q_block0=0, batch=batch, tq=WIN_Q_TILE))
        o_ctx = None
        if not last:
            cq = dict(nq=1, q_block0=ctx_blk0, batch=batch, segs=[ctx_seg], tq=n_ctx)
            o_ctx = (_attention(qa, ka, va, None, wide=True, base2=True, **cq),
                     _attention(qb, kb, vb, None, wide=False, base2=True, **cq),
                     _attention(qw, kw, vw, sink_t[l], wide=False, **cq))
        x1, h2, s1t, s2t = _merge(xs, mod[l], o_lat, o_ctx, gates, woa[l], wob[l], wow[l], wo[l],
                                  g_ffn[l][None], wpq[l], subk[l], n_rows=n_rows, geo=geo_row)
        e1t, cntt, e2t, rkt = _topk(s1t, s2t)
        xs = _peer(h2, x1, mod[l], u_bf[l], vt_bf[l], e1t, cntt, e2t, rkt, g_final[None],
                   geo=geo_peer, final=last)
    return xs.reshape(batch, seq, d)
```

```python
import functools
import math

import jax
import jax.numpy as jnp
import numpy as np
from jax import lax
from jax.experimental import pallas as pl
from jax.experimental.pallas import tpu as pltpu

GRID_W = 64
ROPE_BASE = 10000.0
NORM_EPS = 1e-6
NEG_INF = -1e30
LOG2E = math.log2(math.e)
WINDOW = 128
N_MOD = 6

HEADS = 8
HEAD_PAIRS = HEADS // 2
MLA_Q_RANK = 384
MLA_KV_RANK = 256
MLA_NOPE = 64
MLA_ROPE = 32
HEAD_DIM = 64
KV_HEADS = 2
LANES = 128

PEER_HEADS = 8
PEER_KEYS = 128
PEER_HALF = 128
PEER_TOPK = 16

ROW_TILE = 256
PROJ_TILE = 512
WIN_Q_TILE = 256
FLASH_Q_TILE = 1024
FLASH_K_CHUNK = 4096
PEER_TOKENS = 512
PEER_ROWS = 16
PEER_CHUNK = 32
PEER_A_PIECE = 512
TOPK_TOKENS = 512
VMEM_LIMIT = 56 * 1024 * 1024

OFF_CQ = 0
OFF_CKV = OFF_CQ + MLA_Q_RANK
OFF_KR = OFF_CKV + MLA_KV_RANK
OFF_QB = OFF_KR + LANES
OFF_KB = OFF_QB + HEADS * HEAD_DIM
OFF_VB = OFF_KB + 2 * LANES
OFF_QW = OFF_VB + 2 * LANES
OFF_KW = OFF_QW + HEADS * HEAD_DIM
OFF_VW = OFF_KW + 2 * LANES
OFF_GATES = OFF_VW + 2 * LANES

_NT = (((1,), (1,)), ((), ()))
_F32 = jnp.float32
_BF16 = jnp.bfloat16


def _params(*semantics):
    return pltpu.CompilerParams(dimension_semantics=semantics, vmem_limit_bytes=VMEM_LIMIT)


def _full(shape):
    return pl.BlockSpec(shape, lambda *_: (0,) * len(shape), pipeline_mode=pl.Buffered(1))


def _rms(x, g):
    return x * lax.rsqrt(jnp.mean(x * x, axis=-1, keepdims=True) + NORM_EPS) * g


def _rope(x, tabs, shift):
    cos, sin_lo, sin_hi = tabs
    outs = []
    for c in range(x.shape[1] // LANES):
        xc = x[:, c * LANES:(c + 1) * LANES]
        up = pltpu.roll(xc, LANES - shift, 1)
        dn = pltpu.roll(xc, shift, 1)
        outs.append(xc * cos + up * sin_lo + dn * sin_hi)
    return outs[0] if len(outs) == 1 else jnp.concatenate(outs, axis=1)


def _head_rms(x, bd, g):
    outs = []
    for c in range(x.shape[1] // LANES):
        xc = x[:, c * LANES:(c + 1) * LANES]
        sq = xc * xc
        hi = sq.astype(_BF16)
        lo = (sq - hi.astype(_F32)).astype(_BF16)
        ssq = (jnp.dot(hi, bd, preferred_element_type=_F32)
               + jnp.dot(lo, bd, preferred_element_type=_F32))
        outs.append(xc * lax.rsqrt(ssq * (1.0 / HEAD_DIM) + NORM_EPS))
    y = outs[0] if len(outs) == 1 else jnp.concatenate(outs, axis=1)
    return y * g


def _mod_kernel(c_ref, w_ref, b_ref, o_ref):
    c = c_ref[...]
    a = c / (1.0 + jnp.exp(-c))
    o_ref[0] = jnp.dot(a, w_ref[0], preferred_element_type=_F32,
                       precision=lax.Precision.HIGHEST) + b_ref[0]


def _modulation(cc, w_mod, b_mod):
    depth, d, cols = w_mod.shape
    tn = cols // 4
    return pl.pallas_call(
        _mod_kernel,
        out_shape=jax.ShapeDtypeStruct((depth, 8, cols), _F32),
        grid=(depth, cols // tn),
        in_specs=[
            pl.BlockSpec((8, d), lambda l, j: (0, 0)),
            pl.BlockSpec((1, d, tn), lambda l, j: (l, 0, j)),
            pl.BlockSpec((1, 1, tn), lambda l, j: (l, 0, j)),
        ],
        out_specs=pl.BlockSpec((1, 8, tn), lambda l, j: (l, 0, j)),
        compiler_params=_params("arbitrary", "arbitrary"),
        name="modulation",
    )(cc, w_mod, b_mod.reshape(depth, 1, cols))


def _inproj_kernel(x_ref, mod_ref, gattn_ref, win_ref, gcq_ref, gckv_ref, wuq_ref, wukv_ref,
                   gqn_ref, gkn_ref, rope_ref, bd_ref,
                   qa_ref, ka_ref, va_ref, qb_ref, kb_ref, vb_ref, qw_ref, kw_ref, vw_ref,
                   gates_ref, *, d, scale_a, scale_h):
    x = x_ref[...]
    sh1 = mod_ref[0, :, 0:d]
    sc1 = mod_ref[0, :, d:2 * d]
    h = (_rms(x, gattn_ref[...]) * (1.0 + sc1) + sh1).astype(_BF16)

    def proj(off, width):
        return jnp.dot(h, win_ref[:, off:off + width], preferred_element_type=_F32)

    rope_a = tuple(rope_ref[:, i * LANES:(i + 1) * LANES] for i in range(3))
    rope_h = tuple(rope_ref[:, i * LANES:(i + 1) * LANES] for i in range(3, 6))
    bd = bd_ref[...]

    cq = _rms(proj(OFF_CQ, MLA_Q_RANK), gcq_ref[...]).astype(_BF16)
    qa = jnp.dot(cq, wuq_ref[...], preferred_element_type=_F32)
    qa_ref[...] = (_rope(qa, rope_a, MLA_ROPE // 2) * (scale_a * LOG2E)).astype(_BF16)
    ckv = _rms(proj(OFF_CKV, MLA_KV_RANK), gckv_ref[...]).astype(_BF16)
    kva = jnp.dot(ckv, wukv_ref[...], preferred_element_type=_F32)
    kr = _rope(proj(OFF_KR, LANES), rope_a, MLA_ROPE // 2)
    ka = kva[:, :HEADS * LANES] + jnp.concatenate([kr] * HEADS, axis=1)
    ka_ref[...] = ka.astype(_BF16)
    va_ref[...] = kva[:, HEADS * LANES:].astype(_BF16)

    qb = _head_rms(proj(OFF_QB, HEADS * HEAD_DIM), bd, gqn_ref[...])
    qb_ref[...] = (_rope(qb, rope_h, HEAD_DIM // 2) * (scale_h * LOG2E)).astype(_BF16)
    kb = _head_rms(proj(OFF_KB, 2 * LANES), bd, gkn_ref[...])
    kb_ref[...] = _rope(kb, rope_h, HEAD_DIM // 2).astype(_BF16)
    vb_ref[...] = proj(OFF_VB, 2 * LANES).astype(_BF16)

    qw_ref[...] = (_rope(proj(OFF_QW, HEADS * HEAD_DIM), rope_h, HEAD_DIM // 2) * scale_h).astype(_BF16)
    kw_ref[...] = _rope(proj(OFF_KW, 2 * LANES), rope_h, HEAD_DIM // 2).astype(_BF16)
    vw_ref[...] = proj(OFF_VW, 2 * LANES).astype(_BF16)

    for k in range(3):
        g = proj(OFF_GATES + k * d, d)
        gates_ref[:, k * d:(k + 1) * d] = (1.0 / (1.0 + jnp.exp(-g))).astype(_BF16)


def _inproj(xs, mod_l, gattn, win, gcq, gckv, wuq, wukv, gqn_t, gkn_t, rope, bd, *, geo):
    n, d = xs.shape
    tm = PROJ_TILE
    lat_tiles, tiles_per_batch, batch = geo
    row = lambda t: (t, 0)

    def mod_idx(t):
        return (jnp.where(t < lat_tiles, t // tiles_per_batch, batch), 0, 0)

    def rope_idx(t):
        return (jnp.where(t < lat_tiles, t % tiles_per_batch, tiles_per_batch), 0)

    widths = (HEADS * LANES, HEADS * LANES, HEADS * HEAD_DIM, HEADS * HEAD_DIM, 2 * LANES, 2 * LANES,
              HEADS * HEAD_DIM, 2 * LANES, 2 * LANES, 3 * d)
    return pl.pallas_call(
        functools.partial(_inproj_kernel, d=d, scale_a=(MLA_NOPE + MLA_ROPE) ** -0.5,
                          scale_h=HEAD_DIM ** -0.5),
        out_shape=[jax.ShapeDtypeStruct((n, w), _BF16) for w in widths],
        grid=(n // tm,),
        in_specs=[
            pl.BlockSpec((tm, d), row),
            pl.BlockSpec((1, 1, N_MOD * d), mod_idx),
            _full(gattn.shape), _full(win.shape), _full(gcq.shape), _full(gckv.shape),
            _full(wuq.shape), _full(wukv.shape), _full(gqn_t.shape), _full(gkn_t.shape),
            pl.BlockSpec((tm, 6 * LANES), rope_idx),
            _full(bd.shape),
        ],
        out_specs=[pl.BlockSpec((tm, w), row) for w in widths],
        compiler_params=_params("arbitrary"),
        name="inproj",
    )(xs, mod_l, gattn, win, gcq, gckv, wuq, wukv, gqn_t, gkn_t, rope, bd)


def _attn_kernel(*refs, nseg, wide, has_sink, band, nq, base2):
    q_ref = refs[0]
    k_refs = refs[1:1 + nseg]
    v_refs = refs[1 + nseg:1 + 2 * nseg]
    sink_ref = refs[1 + 2 * nseg] if has_sink else None
    o_ref = refs[-1]
    pair = pl.program_id(1)
    qi = pl.program_id(2)
    tq = q_ref.shape[0]
    lane = lax.broadcasted_iota(jnp.int32, (tq, LANES), 1)

    masks = [None] * nseg
    if band:
        r = lax.broadcasted_iota(jnp.int32, (tq, WINDOW), 0)
        c = lax.broadcasted_iota(jnp.int32, (tq, WINDOW), 1)
        masks[0] = c >= r + jnp.where(qi > 0, 0, WINDOW)
        masks[2] = c <= r - (tq - WINDOW) - jnp.where(qi < nq - 1, 0, tq)
        r = lax.broadcasted_iota(jnp.int32, (tq, tq), 0)
        c = lax.broadcasted_iota(jnp.int32, (tq, tq), 1)
        masks[1] = jnp.abs(r - c) <= WINDOW

    for pp in range(1 if wide else HEAD_PAIRS):
        qcols = slice(pp * LANES, (pp + 1) * LANES)
        kvcols = slice((pp // 2) * LANES, (pp // 2 + 1) * LANES)
        outs = []
        for hh in range(2):
            if wide:
                q = q_ref[:, hh * LANES:(hh + 1) * LANES]
                ks = [k[:, hh * LANES:(hh + 1) * LANES] for k in k_refs]
                vs = [v[...] for v in v_refs]
                head = 2 * pair + hh
            else:
                own = (lane[:1] < HEAD_DIM) if hh == 0 else (lane[:1] >= HEAD_DIM)
                q = q_ref[:, qcols] * own.astype(_F32).astype(_BF16)
                ks = [k[:, kvcols] for k in k_refs]
                vs = [v[:, kvcols] for v in v_refs]
                head = 2 * pp + hh
            ss = [lax.dot_general(q, k, _NT, preferred_element_type=_F32) for k in ks]
            ss = [s if m is None else jnp.where(m, s, NEG_INF) for s, m in zip(ss, masks)]
            m = functools.reduce(jnp.maximum, [jnp.max(s, axis=-1, keepdims=True) for s in ss])
            if has_sink:
                sink = sink_ref[pl.ds(head, 1), 0:1]
                m = jnp.maximum(m, sink)
            ps = [(jnp.exp2 if base2 else jnp.exp)(s - m) for s in ss]
            l = functools.reduce(jnp.add, [jnp.sum(p, axis=-1, keepdims=True) for p in ps])
            if has_sink:
                l = l + jnp.exp(sink - m)
            o = functools.reduce(jnp.add, [jnp.dot(p.astype(_BF16), v, preferred_element_type=_F32)
                                           for p, v in zip(ps, vs)])
            outs.append(o / l)
        o_ref[:, qcols] = jnp.where(lane < HEAD_DIM, outs[0], outs[1]).astype(o_ref.dtype)


def _flash_kernel(q_ref, k_ref, kc_ref, v_ref, vc_ref, o_ref, *, wide, chunk):
    tq = q_ref.shape[0]
    lane = lax.broadcasted_iota(jnp.int32, (1, LANES), 1)
    qs, kcols = [], []
    for hh in range(2):
        if wide:
            qs.append(q_ref[:, hh * LANES:(hh + 1) * LANES])
            kcols.append(slice(hh * LANES, (hh + 1) * LANES))
        else:
            own = (lane < HEAD_DIM) if hh == 0 else (lane >= HEAD_DIM)
            qs.append(q_ref[...] * own.astype(_F32).astype(_BF16))
            kcols.append(slice(0, LANES))
    m = [jnp.full((tq, 1), -jnp.inf, _F32) for _ in range(2)]
    l = [jnp.zeros((tq, 1), _F32) for _ in range(2)]
    acc = [jnp.zeros((tq, LANES), _F32) for _ in range(2)]
    pieces = [(k_ref, v_ref, r0, chunk) for r0 in range(0, k_ref.shape[0], chunk)]
    pieces.append((kc_ref, vc_ref, 0, kc_ref.shape[0]))
    for kr, vr, r0, rows in pieces:
        for hh in range(2):
            s = lax.dot_general(qs[hh], kr[r0:r0 + rows, kcols[hh]], _NT,
                                preferred_element_type=_F32)
            m_new = jnp.maximum(m[hh], jnp.max(s, axis=-1, keepdims=True))
            alpha = jnp.exp2(m[hh] - m_new)
            p = jnp.exp2(s - m_new)
            l[hh] = alpha * l[hh] + jnp.sum(p, axis=-1, keepdims=True)
            acc[hh] = alpha * acc[hh] + jnp.dot(p.astype(_BF16), vr[r0:r0 + rows, :],
                                                preferred_element_type=_F32)
            m[hh] = m_new
    o_ref[...] = jnp.where(lane < HEAD_DIM, acc[0] / l[0], acc[1] / l[1]).astype(o_ref.dtype)


def _flash_attention(q, k, v, *, wide, batch, seq, n_ctx):
    tq = FLASH_Q_TILE
    nq = seq // tq
    qw = 2 * LANES if wide else LANES
    kdiv = 1 if wide else 2
    ctx_blk0 = batch * seq // n_ctx
    return pl.pallas_call(
        functools.partial(_flash_kernel, wide=wide, chunk=min(FLASH_K_CHUNK, seq)),
        out_shape=jax.ShapeDtypeStruct((batch * seq, HEADS * HEAD_DIM), _BF16),
        grid=(batch, HEAD_PAIRS, nq),
        in_specs=[
            pl.BlockSpec((tq, qw), lambda b, p, i: (b * nq + i, p)),
            pl.BlockSpec((seq, qw), lambda b, p, i: (b, p // kdiv)),
            pl.BlockSpec((n_ctx, qw), lambda b, p, i: (ctx_blk0 + b, p // kdiv)),
            pl.BlockSpec((seq, LANES), lambda b, p, i: (b, p // kdiv)),
            pl.BlockSpec((n_ctx, LANES), lambda b, p, i: (ctx_blk0 + b, p // kdiv)),
        ],
        out_specs=pl.BlockSpec((tq, LANES), lambda b, p, i: (b * nq + i, p)),
        compiler_params=_params("arbitrary", "arbitrary", "arbitrary"),
        name="flash_attention",
    )(q, k, k, v, v)


def _attention(q, k, v, sink, *, wide, segs, nq, q_block0, batch, tq, band=False, base2=False):
    pairs = HEAD_PAIRS if wide else 1
    qw, kw, vw, ow = ((2 * LANES, 2 * LANES, LANES, LANES) if wide
                      else (HEADS * HEAD_DIM, KV_HEADS * LANES, KV_HEADS * LANES, HEADS * HEAD_DIM))
    in_specs = [pl.BlockSpec((tq, qw), lambda b, p, i: (q_block0 + b * nq + i, p))]
    for width in (kw, vw):
        for rows, fn in segs:
            in_specs.append(pl.BlockSpec((rows, width), functools.partial(
                lambda b, p, i, fn: (fn(b, i), p), fn=fn)))
    args = [q] + [k] * len(segs) + [v] * len(segs)
    if sink is not None:
        in_specs.append(_full(sink.shape))
        args.append(sink)
    return pl.pallas_call(
        functools.partial(_attn_kernel, nseg=len(segs), wide=wide, has_sink=sink is not None,
                          band=band, nq=nq, base2=base2),
        out_shape=jax.ShapeDtypeStruct((batch * nq * tq, HEADS * HEAD_DIM), _BF16),
        grid=(batch, pairs, nq),
        in_specs=in_specs,
        out_specs=pl.BlockSpec((tq, ow), lambda b, p, i: (b * nq + i, p)),
        compiler_params=_params("arbitrary", "arbitrary", "arbitrary"),
        name="attention",
    )(*args)


def _merge_kernel(*refs, d, lat_tiles, has_ctx):
    n_o = 6 if has_ctx else 3
    x_ref, mod_ref = refs[:2]
    o_refs = refs[2:2 + n_o]
    (gates_ref, woa_ref, wob_ref, wow_ref, wo_ref, gffn_ref, wpq_ref, subk_ref,
     x1_ref, h2_ref, s1_ref, s2_ref) = refs[2 + n_o:14 + n_o]
    if has_ctx:
        o_scr = refs[14 + n_o]
        tile = pl.program_id(0)

        @pl.when(tile < lat_tiles)
        def _():
            for k in range(3):
                o_scr[k] = o_refs[k][...]

        @pl.when(tile >= lat_tiles)
        def _():
            for k in range(3):
                o_scr[k] = o_refs[3 + k][...]

        outs = [o_scr[k] for k in range(3)]
    else:
        outs = [o_refs[k][...] for k in range(3)]
    gt1 = mod_ref[0, :, 2 * d:3 * d]
    sh2 = mod_ref[0, :, 3 * d:4 * d]
    sc2 = mod_ref[0, :, 4 * d:5 * d]
    m = None
    for k, w_ref in enumerate((woa_ref, wob_ref, wow_ref)):
        t = gates_ref[:, k * d:(k + 1) * d].astype(_F32) * jnp.dot(
            outs[k], w_ref[...], preferred_element_type=_F32)
        m = t if m is None else m + t
    y = jnp.dot(m.astype(_BF16), wo_ref[...], preferred_element_type=_F32)
    x1 = x_ref[...] + gt1 * y
    x1_ref[...] = x1
    h2 = (_rms(x1, gffn_ref[...]) * (1.0 + sc2) + sh2).astype(_BF16)
    h2_ref[...] = h2
    q = jnp.dot(h2, wpq_ref[...], preferred_element_type=_F32).astype(_BF16)
    for g in range(2 * PEER_HEADS):
        s = lax.dot_general(subk_ref[g], q[:, g * PEER_HALF:(g + 1) * PEER_HALF], _NT,
                            preferred_element_type=_F32)
        if g % 2 == 0:
            s1_ref[g // 2] = s
        else:
            s2_ref[g // 2] = s


def _merge(xs, mod_l, o_lat, o_ctx, gates, woa, wob, wow, wo, gffn, wpq, subk, *, n_rows, geo):
    d = xs.shape[1]
    tm = PROJ_TILE
    lat_tiles, tiles_per_batch, batch = geo
    row = lambda t: (t, 0)
    ow = HEADS * HEAD_DIM

    def mod_idx(t):
        return (jnp.where(t < lat_tiles, t // tiles_per_batch, batch), 0, 0)

    o_specs = [pl.BlockSpec((tm, ow), lambda t: (jnp.minimum(t, lat_tiles - 1), 0))] * 3
    o_args = list(o_lat)
    scratch = []
    if o_ctx is not None:
        o_specs += [pl.BlockSpec((tm, ow), lambda t: (jnp.maximum(t - lat_tiles, 0), 0))] * 3
        o_args += list(o_ctx)
        scratch = [pltpu.VMEM((3, tm, ow), _BF16)]
    return pl.pallas_call(
        functools.partial(_merge_kernel, d=d, lat_tiles=lat_tiles, has_ctx=o_ctx is not None),
        out_shape=[
            jax.ShapeDtypeStruct((n_rows, d), _F32),
            jax.ShapeDtypeStruct((n_rows, d), _BF16),
            jax.ShapeDtypeStruct((PEER_HEADS, PEER_KEYS, n_rows), _F32),
            jax.ShapeDtypeStruct((PEER_HEADS, PEER_KEYS, n_rows), _F32),
        ],
        grid=(n_rows // tm,),
        in_specs=[
            pl.BlockSpec((tm, d), row),
            pl.BlockSpec((1, 1, N_MOD * d), mod_idx),
            *o_specs,
            pl.BlockSpec((tm, 3 * d), row),
            _full(woa.shape), _full(wob.shape), _full(wow.shape), _full(wo.shape),
            _full(gffn.shape), _full(wpq.shape), _full(subk.shape),
        ],
        out_specs=[
            pl.BlockSpec((tm, d), row),
            pl.BlockSpec((tm, d), row),
            pl.BlockSpec((PEER_HEADS, PEER_KEYS, tm), lambda t: (0, 0, t)),
            pl.BlockSpec((PEER_HEADS, PEER_KEYS, tm), lambda t: (0, 0, t)),
        ],
        scratch_shapes=scratch,
        compiler_params=_params("arbitrary"),
        name="merge",
    )(xs, mod_l, *o_args, gates, woa, wob, wow, wo, gffn, wpq, subk)


NO_RANK = 64.0


def _top_values(w, k, want_rank=False):
    vals = []
    rank = jnp.full(w.shape, NO_RANK, _F32) if want_rank else None
    for r in range(k):
        m = jnp.max(w, axis=0, keepdims=True)
        vals.append(m)
        hit = w == m
        if want_rank:
            rank = jnp.where(hit, float(r), rank)
        w = jnp.where(hit, -jnp.inf, w)
    return (vals, rank) if want_rank else vals


def _stack_rows(vals, n_rows):
    rows = lax.broadcasted_iota(jnp.int32, (n_rows, vals[0].shape[1]), 0)
    out = jnp.full(rows.shape, -jnp.inf, _F32)
    for k, v in enumerate(vals):
        out = jnp.where(rows == k, v, out)
    return out


def _topk_kernel(s1_ref, s2_ref, e1_ref, cnt_ref, e2_ref, rk_ref):
    n_top = PEER_TOPK + 1

    def head(h, carry):
        s1 = s1_ref[h]
        s2 = s2_ref[h]
        t1 = _top_values(s1, n_top)
        t2, rank2 = _top_values(s2, n_top, want_rank=True)
        t2c = _stack_rows(t2, 24)
        cand = [t1[0] + t2c]
        cand += [t1[a] + t2c[:8] for a in range(1, 8)]
        cand += [_stack_rows(t1[8:], 16) + t2[0]]
        best = _top_values(jnp.concatenate(cand, axis=0), n_top)
        z = functools.reduce(jnp.add, [jnp.exp(v - best[0]) for v in best[:PEER_TOPK]])
        tau = 0.5 * (best[PEER_TOPK - 1] + best[PEER_TOPK])
        need = tau - s1
        cnt = jnp.zeros(s1.shape, _F32)
        for b in range(PEER_TOPK):
            cnt = jnp.where(t2[b] >= need, float(b + 1), cnt)
        e1_ref[h] = jnp.exp(s1 - t1[0]) / z
        cnt_ref[h] = cnt
        e2_ref[h] = pltpu.bitcast(jnp.exp(s2 - t2[0]).astype(_BF16), jnp.uint32)
        rk_ref[h] = pltpu.bitcast(rank2.astype(_BF16), jnp.uint32)
        return carry

    lax.fori_loop(0, PEER_HEADS, head, 0)


def _topk(s1t, s2t):
    n = s1t.shape[2]
    tk = TOPK_TOKENS
    blk = pl.BlockSpec((PEER_HEADS, PEER_KEYS, tk), lambda t: (0, 0, t))
    packed = pl.BlockSpec((PEER_HEADS, PEER_KEYS // 2, tk), lambda t: (0, 0, t))
    return pl.pallas_call(
        _topk_kernel,
        out_shape=[jax.ShapeDtypeStruct(s1t.shape, _F32)] * 2
        + [jax.ShapeDtypeStruct((PEER_HEADS, PEER_KEYS // 2, n), jnp.uint32)] * 2,
        grid=(n // tk,),
        in_specs=[blk, blk],
        out_specs=[blk, blk, packed, packed],
        compiler_params=_params("arbitrary"),
        name="peer_topk",
    )(s1t, s2t)


def _gate_rows(a_ref, cnt_ref, e1_ref, e2_ref, rk_ref, w_ref, first_keys, lane_tiles):
    shape = (PEER_CHUNK, LANES)
    for ii in first_keys:
        for lt in lane_tiles:
            ls = slice(lt * LANES, (lt + 1) * LANES)
            cntb = [jnp.broadcast_to(cnt_ref[h, ii:ii + 1, ls], shape).astype(_BF16)
                    for h in range(PEER_HEADS)]
            e1b = [jnp.broadcast_to(e1_ref[h, ii:ii + 1, ls], shape).astype(_BF16)
                   for h in range(PEER_HEADS)]
            for jc in range(PEER_KEYS // PEER_CHUNK):
                rows = slice(jc * PEER_CHUNK, (jc + 1) * PEER_CHUNK)
                words = slice(rows.start // 2, rows.stop // 2)
                g = None
                for h in range(PEER_HEADS):
                    e2 = pltpu.bitcast(e2_ref[h, words, ls], _BF16)
                    rk = pltpu.bitcast(rk_ref[h, words, ls], _BF16)
                    t = e1b[h] * jnp.where(rk < cntb[h], e2, jnp.zeros_like(e2))
                    g = t if g is None else g + t
                erows = slice(ii * PEER_KEYS + rows.start, ii * PEER_KEYS + rows.stop)
                a = a_ref[erows, ls]
                act = (0.5 * a) * (1.0 + lax.erf(a * (1.0 / math.sqrt(2.0))))
                w_ref[erows, ls] = g * act.astype(_BF16)


def _peer_kernel(h2n_ref, x1_ref, mod_ref, u0_ref, un_ref, vt_ref, cnt_ref, e1_ref, e2_ref, rk_ref,
                 gfin_ref, o_ref, acc_ref, a_ref, w_ref, *, d, n_steps, final):
    k = pl.program_id(1)

    @pl.when(jnp.logical_and(pl.program_id(0) == 0, k == 0))
    def _():
        a_ref[...] = lax.dot_general(u0_ref[...], h2n_ref[...], _NT, preferred_element_type=_F32)

    @pl.when(k == 0)
    def _():
        acc_ref[...] = jnp.zeros_like(acc_ref)

    half = a_ref.shape[1] // 2
    for hv in range(2):
        lanes = slice(hv * half, (hv + 1) * half)
        _gate_rows(a_ref, cnt_ref, e1_ref, e2_ref, rk_ref, w_ref, range(PEER_ROWS),
                   range(hv * half // LANES, (hv + 1) * half // LANES))
        acc_ref[:, lanes] += jnp.dot(vt_ref[...], w_ref[:, lanes], preferred_element_type=_F32)
        a_ref[:, lanes] = lax.dot_general(un_ref[...], h2n_ref[lanes, :], _NT,
                                          preferred_element_type=_F32)

    @pl.when(k == n_steps - 1)
    def _():
        gt2 = mod_ref[0, :, 5 * d:6 * d]
        x2 = x1_ref[...] + gt2 * acc_ref[...].T
        o_ref[...] = _rms(x2, gfin_ref[...]) if final else x2


def _peer(h2, x1, mod_l, u, vt, e1t, cntt, e2t, rkt, gfin, *, layer, geo, final):
    n, d = h2.shape
    tn = PEER_TOKENS
    te = PEER_ROWS * PEER_KEYS
    n_steps = u.shape[1] // te
    n_tok = n // tn
    lat_tiles, tiles_per_batch, batch = geo
    row = lambda t, k: (t, 0)

    def mod_idx(t, k):
        return (jnp.where(t < lat_tiles, t // tiles_per_batch, batch), 0, 0)

    rows_blk = pl.BlockSpec((PEER_HEADS, PEER_ROWS, tn), lambda t, k: (0, k, t))
    return pl.pallas_call(
        functools.partial(_peer_kernel, d=d, n_steps=n_steps, final=final),
        out_shape=jax.ShapeDtypeStruct((n, d), _F32),
        grid=(n // tn, n_steps),
        in_specs=[
            pl.BlockSpec((tn, d), lambda t, k: (jnp.minimum(t + (k + 1) // n_steps, n_tok - 1), 0)),
            pl.BlockSpec((tn, d), row),
            pl.BlockSpec((1, 1, N_MOD * d), mod_idx),
            pl.BlockSpec((None, te, d), lambda t, k: (layer, 0, 0)),
            pl.BlockSpec((None, te, d), lambda t, k: (layer, (k + 1) % n_steps, 0)),
            pl.BlockSpec((None, d, te), lambda t, k: (layer, 0, k)),
            rows_blk,
            rows_blk,
            pl.BlockSpec((PEER_HEADS, PEER_KEYS // 2, tn), lambda t, k: (0, 0, t)),
            pl.BlockSpec((PEER_HEADS, PEER_KEYS // 2, tn), lambda t, k: (0, 0, t)),
            _full(gfin.shape),
        ],
        out_specs=pl.BlockSpec((tn, d), row),
        scratch_shapes=[
            pltpu.VMEM((d, tn), _F32),
            pltpu.VMEM((te, tn), _F32),
            pltpu.VMEM((te, tn), _BF16),
        ],
        compiler_params=_params("arbitrary", "arbitrary"),
        name="peer_experts",
    )(h2, x1, mod_l, u, u, vt, cntt, e1t, e2t, rkt, gfin)


def _rope_tables(seq):
    rows = seq // GRID_W
    r = jnp.repeat(jnp.arange(rows, dtype=_F32), GRID_W)
    col = jnp.tile(jnp.arange(GRID_W, dtype=_F32), rows)

    def cos_sin(rot_dim):
        n_freq = rot_dim // 4
        inv = ROPE_BASE ** (-jnp.arange(n_freq, dtype=_F32) / n_freq)
        ang = jnp.concatenate([r[:, None] * inv, col[:, None] * inv], axis=-1)
        return jnp.cos(ang), jnp.sin(ang)

    ca, sa = cos_sin(MLA_ROPE)
    one = jnp.ones((seq, MLA_NOPE), _F32)
    zero = jnp.zeros_like(one)
    z16 = jnp.zeros_like(ca)
    pad1 = jnp.ones((seq, LANES - MLA_NOPE - MLA_ROPE), _F32)
    pad0 = jnp.zeros_like(pad1)
    tab_a = [jnp.concatenate([one, ca, ca, pad1], axis=1),
             jnp.concatenate([zero, -sa, z16, pad0], axis=1),
             jnp.concatenate([zero, z16, sa, pad0], axis=1)]
    ch, sh = cos_sin(HEAD_DIM)
    z32 = jnp.zeros_like(ch)
    tab_h = [jnp.concatenate([ch, ch, ch, ch], axis=1),
             jnp.concatenate([-sh, z32, -sh, z32], axis=1),
             jnp.concatenate([z32, sh, z32, sh], axis=1)]
    lat = jnp.concatenate(tab_a + tab_h, axis=1)
    ident = jnp.concatenate([jnp.ones((PROJ_TILE, LANES), _F32),
                             jnp.zeros((PROJ_TILE, 2 * LANES), _F32)] * 2, axis=1)
    return jnp.concatenate([lat, ident], axis=0)


def _pack_w_in(w_in, d):
    splits = np.cumsum([MLA_Q_RANK, MLA_KV_RANK, MLA_ROPE, 512, 128, 128, 512, 128, 128])
    cq, ckv, kr, qb, kb, vb, qw, kw, vw, gates = jnp.split(w_in, splits, axis=-1)
    lead = w_in.shape[:-1]
    kr_pad = jnp.concatenate([jnp.zeros(lead + (MLA_NOPE,), w_in.dtype), kr,
                              jnp.zeros(lead + (LANES - MLA_NOPE - MLA_ROPE,), w_in.dtype)], axis=-1)

    def dup(w):
        g0, g1 = w[..., :HEAD_DIM], w[..., HEAD_DIM:]
        return jnp.concatenate([g0, g0, g1, g1], axis=-1)

    return jnp.concatenate([cq, ckv, kr_pad, qb, dup(kb), dup(vb), qw, dup(kw), dup(vw), gates],
                           axis=-1).astype(_BF16)


def _pack_w_uq(w_uq):
    depth, rank, _ = w_uq.shape
    w = w_uq.reshape(depth, rank, HEADS, MLA_NOPE + MLA_ROPE)
    w = jnp.pad(w, ((0, 0), (0, 0), (0, 0), (0, LANES - MLA_NOPE - MLA_ROPE)))
    return w.reshape(depth, rank, HEADS * LANES).astype(_BF16)


def _pack_w_ukv(w_ukv):
    depth, rank, _ = w_ukv.shape
    w = w_ukv.reshape(depth, rank, HEADS, MLA_NOPE + HEAD_DIM)
    k = jnp.pad(w[..., :MLA_NOPE], ((0, 0), (0, 0), (0, 0), (0, LANES - MLA_NOPE)))
    v = w[..., MLA_NOPE:]
    return jnp.concatenate([k.reshape(depth, rank, HEADS * LANES),
                            v.reshape(depth, rank, HEADS * HEAD_DIM)], axis=-1).astype(_BF16)


def kernel(x, c, ctx, c_ctx, w_mod, b_mod, g_attn, g_ffn, w_in, g_cq, g_ckv, w_uq, w_ukv,
           g_qn, g_kn, sink, w_oa, w_ob, w_ow, w_o, w_pq, sub_keys, peer_u, peer_v, g_final):
    batch, seq, d = x.shape
    n_ctx = ctx.shape[1]
    depth = w_mod.shape[0]
    assert n_ctx == ROW_TILE and seq % PEER_TOKENS == 0 and (batch * n_ctx) % PEER_TOKENS == 0
    assert batch < 8 and seq % GRID_W == 0 and seq >= 2 * WIN_Q_TILE and seq % WIN_Q_TILE == 0
    assert seq % FLASH_Q_TILE == 0 and seq % min(FLASH_K_CHUNK, seq) == 0 and seq % PROJ_TILE == 0
    n_lat = batch * seq
    n_all = n_lat + batch * n_ctx
    ctx_blk0 = n_lat // ROW_TILE

    xs = jnp.concatenate([x.reshape(n_lat, d), ctx.reshape(batch * n_ctx, d)], axis=0)
    cc = jnp.concatenate([c, c_ctx[None], jnp.zeros((8 - batch - 1, d), _F32)], axis=0)
    mod = _modulation(cc, w_mod, b_mod).reshape(depth, 8, 1, N_MOD * d)

    rope = _rope_tables(seq)
    eye = np.kron(np.eye(2, dtype=np.float32), np.ones((HEAD_DIM, HEAD_DIM), np.float32))
    bd = jnp.asarray(eye, _BF16)
    win = _pack_w_in(w_in, d)
    wuq = _pack_w_uq(w_uq)
    wukv = _pack_w_ukv(w_ukv)
    gqn_t = jnp.tile(g_qn, (1, HEADS))[:, None, :]
    gkn_t = jnp.tile(g_kn, (1, 2 * LANES // HEAD_DIM))[:, None, :]
    sink_t = jnp.broadcast_to(sink[:, :, None], (depth, HEADS, LANES))
    woa, wob, wow, wo, wpq = (w.astype(_BF16) for w in (w_oa, w_ob, w_ow, w_o, w_pq))
    subk = sub_keys.reshape(depth, 2 * PEER_HEADS, PEER_KEYS, PEER_HALF).astype(_BF16)
    u_bf = peer_u.astype(_BF16)
    vt_bf = jnp.swapaxes(peer_v, 1, 2).astype(_BF16)

    geo_row = (n_lat // PROJ_TILE, seq // PROJ_TILE, batch)
    geo_peer = (n_lat // PEER_TOKENS, seq // PEER_TOKENS, batch)
    lat_seg = (seq, lambda b, i: b)
    ctx_seg = (n_ctx, lambda b, i: ctx_blk0 + b)
    nqw = seq // WIN_Q_TILE
    per_tile = WIN_Q_TILE // WINDOW
    edge_blocks = seq // WINDOW
    band_segs = [
        (WINDOW, lambda b, i: b * edge_blocks + jnp.maximum(per_tile * i - 1, 0)),
        (WIN_Q_TILE, lambda b, i: b * nqw + i),
        (WINDOW, lambda b, i: b * edge_blocks + jnp.minimum(per_tile * (i + 1), edge_blocks - 1)),
        ctx_seg,
    ]

    for l in range(depth):
        last = l == depth - 1
        qa, ka, va, qb, kb, vb, qw, kw, vw, gates = _inproj(
            xs, mod[l], g_attn[l][None], win[l], g_cq[l][None], g_ckv[l][None], wuq[l], wukv[l],
            gqn_t[l], gkn_t[l], rope, bd, geo=geo_row)
        dense = dict(batch=batch, seq=seq, n_ctx=n_ctx)
        n_rows = n_lat if last else n_all
        o_lat = (_flash_attention(qa, ka, va, wide=True, **dense),
                 _flash_attention(qb, kb, vb, wide=False, **dense),
                 _attention(qw, kw, vw, sink_t[l], wide=False, segs=band_segs, band=True,
                            nq=nqw, q_block0=0, batch=batch, tq=WIN_Q_TILE))
        o_ctx = None
        if not last:
            cq = dict(nq=1, q_block0=ctx_blk0, batch=batch, segs=[ctx_seg], tq=n_ctx)
            o_ctx = (_attention(qa, ka, va, None, wide=True, base2=True, **cq),
                     _attention(qb, kb, vb, None, wide=False, base2=True, **cq),
                     _attention(qw, kw, vw, sink_t[l], wide=False, **cq))
        x1, h2, s1t, s2t = _merge(xs, mod[l], o_lat, o_ctx, gates, woa[l], wob[l], wow[l], wo[l],
                                  g_ffn[l][None], wpq[l], subk[l], n_rows=n_rows, geo=geo_row)
        e1t, cntt, e2t, rkt = _topk(s1t, s2t)
        xs = _peer(h2, x1, mod[l], u_bf, vt_bf, e1t, cntt, e2t, rkt, g_final[None],
                   layer=l, geo=geo_peer, final=last)
    return xs.reshape(batch, seq, d)
```

```python
import functools
import math

import jax
import jax.numpy as jnp
import numpy as np
from jax import lax
from jax.experimental import pallas as pl
from jax.experimental.pallas import tpu as pltpu

GRID_W = 64
ROPE_BASE = 10000.0
NORM_EPS = 1e-6
NEG_INF = -1e30
LOG2E = math.log2(math.e)
WINDOW = 128
N_MOD = 6

HEADS = 8
HEAD_PAIRS = HEADS // 2
MLA_Q_RANK = 384
MLA_KV_RANK = 256
MLA_NOPE = 64
MLA_ROPE = 32
HEAD_DIM = 64
KV_HEADS = 2
LANES = 128

PEER_HEADS = 8
PEER_KEYS = 128
PEER_HALF = 128
PEER_TOPK = 16

ROW_TILE = 256
PROJ_TILE = 512
WIN_Q_TILE = 256
FLASH_Q_TILE = 1024
FLASH_K_CHUNK = 4096
PEER_TOKENS = 512
PEER_ROWS = 16
PEER_CHUNK = 32
PEER_A_PIECE = 512
TOPK_TOKENS = 512
VMEM_LIMIT = 56 * 1024 * 1024

OFF_CQ = 0
OFF_CKV = OFF_CQ + MLA_Q_RANK
OFF_KR = OFF_CKV + MLA_KV_RANK
OFF_QB = OFF_KR + LANES
OFF_KB = OFF_QB + HEADS * HEAD_DIM
OFF_VB = OFF_KB + 2 * LANES
OFF_QW = OFF_VB + 2 * LANES
OFF_KW = OFF_QW + HEADS * HEAD_DIM
OFF_VW = OFF_KW + 2 * LANES
OFF_GATES = OFF_VW + 2 * LANES

_NT = (((1,), (1,)), ((), ()))
_F32 = jnp.float32
_BF16 = jnp.bfloat16


def _params(*semantics):
    return pltpu.CompilerParams(dimension_semantics=semantics, vmem_limit_bytes=VMEM_LIMIT)


def _full(shape):
    return pl.BlockSpec(shape, lambda *_: (0,) * len(shape), pipeline_mode=pl.Buffered(1))


def _rms(x, g):
    return x * lax.rsqrt(jnp.mean(x * x, axis=-1, keepdims=True) + NORM_EPS) * g


def _rope(x, tabs, shift):
    cos, sin_lo, sin_hi = tabs
    outs = []
    for c in range(x.shape[1] // LANES):
        xc = x[:, c * LANES:(c + 1) * LANES]
        up = pltpu.roll(xc, LANES - shift, 1)
        dn = pltpu.roll(xc, shift, 1)
        outs.append(xc * cos + up * sin_lo + dn * sin_hi)
    return outs[0] if len(outs) == 1 else jnp.concatenate(outs, axis=1)


def _head_rms(x, bd, g):
    outs = []
    for c in range(x.shape[1] // LANES):
        xc = x[:, c * LANES:(c + 1) * LANES]
        sq = xc * xc
        hi = sq.astype(_BF16)
        lo = (sq - hi.astype(_F32)).astype(_BF16)
        ssq = (jnp.dot(hi, bd, preferred_element_type=_F32)
               + jnp.dot(lo, bd, preferred_element_type=_F32))
        outs.append(xc * lax.rsqrt(ssq * (1.0 / HEAD_DIM) + NORM_EPS))
    y = outs[0] if len(outs) == 1 else jnp.concatenate(outs, axis=1)
    return y * g


def _mod_kernel(c_ref, w_ref, b_ref, o_ref):
    c = c_ref[...]
    a = c / (1.0 + jnp.exp(-c))
    o_ref[0] = jnp.dot(a, w_ref[0], preferred_element_type=_F32,
                       precision=lax.Precision.HIGHEST) + b_ref[0]


def _modulation(cc, w_mod, b_mod):
    depth, d, cols = w_mod.shape
    tn = cols // 4
    return pl.pallas_call(
        _mod_kernel,
        out_shape=jax.ShapeDtypeStruct((depth, 8, cols), _F32),
        grid=(depth, cols // tn),
        in_specs=[
            pl.BlockSpec((8, d), lambda l, j: (0, 0)),
            pl.BlockSpec((1, d, tn), lambda l, j: (l, 0, j)),
            pl.BlockSpec((1, 1, tn), lambda l, j: (l, 0, j)),
        ],
        out_specs=pl.BlockSpec((1, 8, tn), lambda l, j: (l, 0, j)),
        compiler_params=_params("arbitrary", "arbitrary"),
        name="modulation",
    )(cc, w_mod, b_mod.reshape(depth, 1, cols))


def _inproj_kernel(x_ref, mod_ref, gattn_ref, win_ref, gcq_ref, gckv_ref, wuq_ref, wukv_ref,
                   gqn_ref, gkn_ref, rope_ref, bd_ref,
                   qa_ref, ka_ref, va_ref, qb_ref, kb_ref, vb_ref, qw_ref, kw_ref, vw_ref,
                   gates_ref, *, d, scale_a, scale_h):
    x = x_ref[...]
    sh1 = mod_ref[0, :, 0:d]
    sc1 = mod_ref[0, :, d:2 * d]
    h = (_rms(x, gattn_ref[...]) * (1.0 + sc1) + sh1).astype(_BF16)

    def proj(off, width):
        return jnp.dot(h, win_ref[:, off:off + width], preferred_element_type=_F32)

    rope_a = tuple(rope_ref[:, i * LANES:(i + 1) * LANES] for i in range(3))
    rope_h = tuple(rope_ref[:, i * LANES:(i + 1) * LANES] for i in range(3, 6))
    bd = bd_ref[...]

    cq = _rms(proj(OFF_CQ, MLA_Q_RANK), gcq_ref[...]).astype(_BF16)
    qa = jnp.dot(cq, wuq_ref[...], preferred_element_type=_F32)
    qa_ref[...] = (_rope(qa, rope_a, MLA_ROPE // 2) * (scale_a * LOG2E)).astype(_BF16)
    ckv = _rms(proj(OFF_CKV, MLA_KV_RANK), gckv_ref[...]).astype(_BF16)
    kva = jnp.dot(ckv, wukv_ref[...], preferred_element_type=_F32)
    kr = _rope(proj(OFF_KR, LANES), rope_a, MLA_ROPE // 2)
    ka = kva[:, :HEADS * LANES] + jnp.concatenate([kr] * HEADS, axis=1)
    ka_ref[...] = ka.astype(_BF16)
    va_ref[...] = kva[:, HEADS * LANES:].astype(_BF16)

    qb = _head_rms(proj(OFF_QB, HEADS * HEAD_DIM), bd, gqn_ref[...])
    qb_ref[...] = (_rope(qb, rope_h, HEAD_DIM // 2) * (scale_h * LOG2E)).astype(_BF16)
    kb = _head_rms(proj(OFF_KB, 2 * LANES), bd, gkn_ref[...])
    kb_ref[...] = _rope(kb, rope_h, HEAD_DIM // 2).astype(_BF16)
    vb_ref[...] = proj(OFF_VB, 2 * LANES).astype(_BF16)

    qw_ref[...] = (_rope(proj(OFF_QW, HEADS * HEAD_DIM), rope_h, HEAD_DIM // 2) * scale_h).astype(_BF16)
    kw_ref[...] = _rope(proj(OFF_KW, 2 * LANES), rope_h, HEAD_DIM // 2).astype(_BF16)
    vw_ref[...] = proj(OFF_VW, 2 * LANES).astype(_BF16)

    for k in range(3):
        g = proj(OFF_GATES + k * d, d)
        gates_ref[:, k * d:(k + 1) * d] = (1.0 / (1.0 + jnp.exp(-g))).astype(_BF16)


def _inproj(xs, mod_l, gattn, win, gcq, gckv, wuq, wukv, gqn_t, gkn_t, rope, bd, *, layer, geo):
    n, d = xs.shape
    tm = PROJ_TILE
    lat_tiles, tiles_per_batch, batch = geo
    row = lambda t: (t, 0)

    def mod_idx(t):
        return (jnp.where(t < lat_tiles, t // tiles_per_batch, batch), 0, 0)

    def rope_idx(t):
        return (jnp.where(t < lat_tiles, t % tiles_per_batch, tiles_per_batch), 0)

    widths = (HEADS * LANES, HEADS * LANES, HEADS * HEAD_DIM, HEADS * HEAD_DIM, 2 * LANES, 2 * LANES,
              HEADS * HEAD_DIM, 2 * LANES, 2 * LANES, 3 * d)
    return pl.pallas_call(
        functools.partial(_inproj_kernel, d=d, scale_a=(MLA_NOPE + MLA_ROPE) ** -0.5,
                          scale_h=HEAD_DIM ** -0.5),
        out_shape=[jax.ShapeDtypeStruct((n, w), _BF16) for w in widths],
        grid=(n // tm,),
        in_specs=[
            pl.BlockSpec((tm, d), row),
            pl.BlockSpec((1, 1, N_MOD * d), mod_idx),
            _full(gattn.shape),
            pl.BlockSpec((None,) + win.shape[1:], lambda t: (layer, 0, 0),
                         pipeline_mode=pl.Buffered(1)),
            _full(gcq.shape), _full(gckv.shape),
            _full(wuq.shape), _full(wukv.shape), _full(gqn_t.shape), _full(gkn_t.shape),
            pl.BlockSpec((tm, 6 * LANES), rope_idx),
            _full(bd.shape),
        ],
        out_specs=[pl.BlockSpec((tm, w), row) for w in widths],
        compiler_params=_params("arbitrary"),
        name="inproj",
    )(xs, mod_l, gattn, win, gcq, gckv, wuq, wukv, gqn_t, gkn_t, rope, bd)


def _attn_kernel(*refs, nseg, wide, has_sink, band, nq, base2):
    q_ref = refs[0]
    k_refs = refs[1:1 + nseg]
    v_refs = refs[1 + nseg:1 + 2 * nseg]
    sink_ref = refs[1 + 2 * nseg] if has_sink else None
    o_ref = refs[-1]
    pair = pl.program_id(1)
    qi = pl.program_id(2)
    tq = q_ref.shape[0]
    lane = lax.broadcasted_iota(jnp.int32, (tq, LANES), 1)

    masks = [None] * nseg
    if band:
        r = lax.broadcasted_iota(jnp.int32, (tq, WINDOW), 0)
        c = lax.broadcasted_iota(jnp.int32, (tq, WINDOW), 1)
        masks[0] = c >= r + jnp.where(qi > 0, 0, WINDOW)
        masks[2] = c <= r - (tq - WINDOW) - jnp.where(qi < nq - 1, 0, tq)
        r = lax.broadcasted_iota(jnp.int32, (tq, tq), 0)
        c = lax.broadcasted_iota(jnp.int32, (tq, tq), 1)
        masks[1] = jnp.abs(r - c) <= WINDOW

    for pp in range(1 if wide else HEAD_PAIRS):
        qcols = slice(pp * LANES, (pp + 1) * LANES)
        kvcols = slice((pp // 2) * LANES, (pp // 2 + 1) * LANES)
        outs = []
        for hh in range(2):
            if wide:
                q = q_ref[:, hh * LANES:(hh + 1) * LANES]
                ks = [k[:, hh * LANES:(hh + 1) * LANES] for k in k_refs]
                vs = [v[...] for v in v_refs]
                head = 2 * pair + hh
            else:
                own = (lane[:1] < HEAD_DIM) if hh == 0 else (lane[:1] >= HEAD_DIM)
                q = q_ref[:, qcols] * own.astype(_F32).astype(_BF16)
                ks = [k[:, kvcols] for k in k_refs]
                vs = [v[:, kvcols] for v in v_refs]
                head = 2 * pp + hh
            ss = [lax.dot_general(q, k, _NT, preferred_element_type=_F32) for k in ks]
            ss = [s if m is None else jnp.where(m, s, NEG_INF) for s, m in zip(ss, masks)]
            m = functools.reduce(jnp.maximum, [jnp.max(s, axis=-1, keepdims=True) for s in ss])
            if has_sink:
                sink = sink_ref[pl.ds(head, 1), 0:1]
                m = jnp.maximum(m, sink)
            ps = [(jnp.exp2 if base2 else jnp.exp)(s - m) for s in ss]
            l = functools.reduce(jnp.add, [jnp.sum(p, axis=-1, keepdims=True) for p in ps])
            if has_sink:
                l = l + jnp.exp(sink - m)
            o = functools.reduce(jnp.add, [jnp.dot(p.astype(_BF16), v, preferred_element_type=_F32)
                                           for p, v in zip(ps, vs)])
            outs.append(o / l)
        o_ref[:, qcols] = jnp.where(lane < HEAD_DIM, outs[0], outs[1]).astype(o_ref.dtype)


def _flash_kernel(q_ref, k_ref, kc_ref, v_ref, vc_ref, o_ref, *, wide, chunk):
    tq = q_ref.shape[0]
    lane = lax.broadcasted_iota(jnp.int32, (1, LANES), 1)
    qs, kcols = [], []
    for hh in range(2):
        if wide:
            qs.append(q_ref[:, hh * LANES:(hh + 1) * LANES])
            kcols.append(slice(hh * LANES, (hh + 1) * LANES))
        else:
            own = (lane < HEAD_DIM) if hh == 0 else (lane >= HEAD_DIM)
            qs.append(q_ref[...] * own.astype(_F32).astype(_BF16))
            kcols.append(slice(0, LANES))
    m = [jnp.full((tq, 1), -jnp.inf, _F32) for _ in range(2)]
    l = [jnp.zeros((tq, 1), _F32) for _ in range(2)]
    acc = [jnp.zeros((tq, LANES), _F32) for _ in range(2)]
    pieces = [(k_ref, v_ref, r0, chunk) for r0 in range(0, k_ref.shape[0], chunk)]
    pieces.append((kc_ref, vc_ref, 0, kc_ref.shape[0]))
    for kr, vr, r0, rows in pieces:
        for hh in range(2):
            s = lax.dot_general(qs[hh], kr[r0:r0 + rows, kcols[hh]], _NT,
                                preferred_element_type=_F32)
            m_new = jnp.maximum(m[hh], jnp.max(s, axis=-1, keepdims=True))
            alpha = jnp.exp2(m[hh] - m_new)
            p = jnp.exp2(s - m_new)
            l[hh] = alpha * l[hh] + jnp.sum(p, axis=-1, keepdims=True)
            acc[hh] = alpha * acc[hh] + jnp.dot(p.astype(_BF16), vr[r0:r0 + rows, :],
                                                preferred_element_type=_F32)
            m[hh] = m_new
    o_ref[...] = jnp.where(lane < HEAD_DIM, acc[0] / l[0], acc[1] / l[1]).astype(o_ref.dtype)


def _flash_attention(q, k, v, *, wide, batch, seq, n_ctx):
    tq = FLASH_Q_TILE
    nq = seq // tq
    qw = 2 * LANES if wide else LANES
    kdiv = 1 if wide else 2
    ctx_blk0 = batch * seq // n_ctx
    return pl.pallas_call(
        functools.partial(_flash_kernel, wide=wide, chunk=min(FLASH_K_CHUNK, seq)),
        out_shape=jax.ShapeDtypeStruct((batch * seq, HEADS * HEAD_DIM), _BF16),
        grid=(batch, HEAD_PAIRS, nq),
        in_specs=[
            pl.BlockSpec((tq, qw), lambda b, p, i: (b * nq + i, p)),
            pl.BlockSpec((seq, qw), lambda b, p, i: (b, p // kdiv)),
            pl.BlockSpec((n_ctx, qw), lambda b, p, i: (ctx_blk0 + b, p // kdiv)),
            pl.BlockSpec((seq, LANES), lambda b, p, i: (b, p // kdiv)),
            pl.BlockSpec((n_ctx, LANES), lambda b, p, i: (ctx_blk0 + b, p // kdiv)),
        ],
        out_specs=pl.BlockSpec((tq, LANES), lambda b, p, i: (b * nq + i, p)),
        compiler_params=_params("arbitrary", "arbitrary", "arbitrary"),
        name="flash_attention",
    )(q, k, k, v, v)


def _attention(q, k, v, sink, *, wide, segs, nq, q_block0, batch, tq, band=False, base2=False):
    pairs = HEAD_PAIRS if wide else 1
    qw, kw, vw, ow = ((2 * LANES, 2 * LANES, LANES, LANES) if wide
                      else (HEADS * HEAD_DIM, KV_HEADS * LANES, KV_HEADS * LANES, HEADS * HEAD_DIM))
    in_specs = [pl.BlockSpec((tq, qw), lambda b, p, i: (q_block0 + b * nq + i, p))]
    for width in (kw, vw):
        for rows, fn in segs:
            in_specs.append(pl.BlockSpec((rows, width), functools.partial(
                lambda b, p, i, fn: (fn(b, i), p), fn=fn)))
    args = [q] + [k] * len(segs) + [v] * len(segs)
    if sink is not None:
        in_specs.append(_full(sink.shape))
        args.append(sink)
    return pl.pallas_call(
        functools.partial(_attn_kernel, nseg=len(segs), wide=wide, has_sink=sink is not None,
                          band=band, nq=nq, base2=base2),
        out_shape=jax.ShapeDtypeStruct((batch * nq * tq, HEADS * HEAD_DIM), _BF16),
        grid=(batch, pairs, nq),
        in_specs=in_specs,
        out_specs=pl.BlockSpec((tq, ow), lambda b, p, i: (b * nq + i, p)),
        compiler_params=_params("arbitrary", "arbitrary", "arbitrary"),
        name="attention",
    )(*args)


def _merge_kernel(*refs, d, lat_tiles, has_ctx):
    n_o = 6 if has_ctx else 3
    x_ref, mod_ref = refs[:2]
    o_refs = refs[2:2 + n_o]
    (gates_ref, woa_ref, wob_ref, wow_ref, wo_ref, gffn_ref, wpq_ref, subk_ref,
     x1_ref, h2_ref, s1_ref, s2_ref) = refs[2 + n_o:14 + n_o]
    if has_ctx:
        o_scr = refs[14 + n_o]
        tile = pl.program_id(0)

        @pl.when(tile < lat_tiles)
        def _():
            for k in range(3):
                o_scr[k] = o_refs[k][...]

        @pl.when(tile >= lat_tiles)
        def _():
            for k in range(3):
                o_scr[k] = o_refs[3 + k][...]

        outs = [o_scr[k] for k in range(3)]
    else:
        outs = [o_refs[k][...] for k in range(3)]
    gt1 = mod_ref[0, :, 2 * d:3 * d]
    sh2 = mod_ref[0, :, 3 * d:4 * d]
    sc2 = mod_ref[0, :, 4 * d:5 * d]
    m = None
    for k, w_ref in enumerate((woa_ref, wob_ref, wow_ref)):
        t = gates_ref[:, k * d:(k + 1) * d].astype(_F32) * jnp.dot(
            outs[k], w_ref[...], preferred_element_type=_F32)
        m = t if m is None else m + t
    y = jnp.dot(m.astype(_BF16), wo_ref[...], preferred_element_type=_F32)
    x1 = x_ref[...] + gt1 * y
    x1_ref[...] = x1
    h2 = (_rms(x1, gffn_ref[...]) * (1.0 + sc2) + sh2).astype(_BF16)
    h2_ref[...] = h2
    q = jnp.dot(h2, wpq_ref[...], preferred_element_type=_F32).astype(_BF16)
    for g in range(2 * PEER_HEADS):
        s = lax.dot_general(subk_ref[g], q[:, g * PEER_HALF:(g + 1) * PEER_HALF], _NT,
                            preferred_element_type=_F32)
        if g % 2 == 0:
            s1_ref[g // 2] = s
        else:
            s2_ref[g // 2] = s


def _merge(xs, mod_l, o_lat, o_ctx, gates, woa, wob, wow, wo, gffn, wpq, subk, *, n_rows, geo):
    d = xs.shape[1]
    tm = PROJ_TILE
    lat_tiles, tiles_per_batch, batch = geo
    row = lambda t: (t, 0)
    ow = HEADS * HEAD_DIM

    def mod_idx(t):
        return (jnp.where(t < lat_tiles, t // tiles_per_batch, batch), 0, 0)

    o_specs = [pl.BlockSpec((tm, ow), lambda t: (jnp.minimum(t, lat_tiles - 1), 0))] * 3
    o_args = list(o_lat)
    scratch = []
    if o_ctx is not None:
        o_specs += [pl.BlockSpec((tm, ow), lambda t: (jnp.maximum(t - lat_tiles, 0), 0))] * 3
        o_args += list(o_ctx)
        scratch = [pltpu.VMEM((3, tm, ow), _BF16)]
    return pl.pallas_call(
        functools.partial(_merge_kernel, d=d, lat_tiles=lat_tiles, has_ctx=o_ctx is not None),
        out_shape=[
            jax.ShapeDtypeStruct((n_rows, d), _F32),
            jax.ShapeDtypeStruct((n_rows, d), _BF16),
            jax.ShapeDtypeStruct((PEER_HEADS, PEER_KEYS, n_rows), _F32),
            jax.ShapeDtypeStruct((PEER_HEADS, PEER_KEYS, n_rows), _F32),
        ],
        grid=(n_rows // tm,),
        in_specs=[
            pl.BlockSpec((tm, d), row),
            pl.BlockSpec((1, 1, N_MOD * d), mod_idx),
            *o_specs,
            pl.BlockSpec((tm, 3 * d), row),
            _full(woa.shape), _full(wob.shape), _full(wow.shape), _full(wo.shape),
            _full(gffn.shape), _full(wpq.shape), _full(subk.shape),
        ],
        out_specs=[
            pl.BlockSpec((tm, d), row),
            pl.BlockSpec((tm, d), row),
            pl.BlockSpec((PEER_HEADS, PEER_KEYS, tm), lambda t: (0, 0, t)),
            pl.BlockSpec((PEER_HEADS, PEER_KEYS, tm), lambda t: (0, 0, t)),
        ],
        scratch_shapes=scratch,
        compiler_params=_params("arbitrary"),
        name="merge",
    )(xs, mod_l, *o_args, gates, woa, wob, wow, wo, gffn, wpq, subk)


NO_RANK = 64.0


def _top_values(w, k, want_rank=False):
    vals = []
    rank = jnp.full(w.shape, NO_RANK, _F32) if want_rank else None
    for r in range(k):
        m = jnp.max(w, axis=0, keepdims=True)
        vals.append(m)
        hit = w == m
        if want_rank:
            rank = jnp.where(hit, float(r), rank)
        w = jnp.where(hit, -jnp.inf, w)
    return (vals, rank) if want_rank else vals


def _stack_rows(vals, n_rows):
    rows = lax.broadcasted_iota(jnp.int32, (n_rows, vals[0].shape[1]), 0)
    out = jnp.full(rows.shape, -jnp.inf, _F32)
    for k, v in enumerate(vals):
        out = jnp.where(rows == k, v, out)
    return out


def _topk_kernel(s1_ref, s2_ref, e1_ref, cnt_ref, e2_ref, rk_ref):
    n_top = PEER_TOPK + 1

    def head(h, carry):
        s1 = s1_ref[h]
        s2 = s2_ref[h]
        t1 = _top_values(s1, n_top)
        t2, rank2 = _top_values(s2, n_top, want_rank=True)
        t2c = _stack_rows(t2, 24)
        cand = [t1[0] + t2c]
        cand += [t1[a] + t2c[:8] for a in range(1, 8)]
        cand += [_stack_rows(t1[8:], 16) + t2[0]]
        best = _top_values(jnp.concatenate(cand, axis=0), n_top)
        z = functools.reduce(jnp.add, [jnp.exp(v - best[0]) for v in best[:PEER_TOPK]])
        tau = 0.5 * (best[PEER_TOPK - 1] + best[PEER_TOPK])
        need = tau - s1
        cnt = jnp.zeros(s1.shape, _F32)
        for b in range(PEER_TOPK):
            cnt = jnp.where(t2[b] >= need, float(b + 1), cnt)
        e1_ref[h] = jnp.exp(s1 - t1[0]) / z
        cnt_ref[h] = cnt
        e2_ref[h] = pltpu.bitcast(jnp.exp(s2 - t2[0]).astype(_BF16), jnp.uint32)
        rk_ref[h] = pltpu.bitcast(rank2.astype(_BF16), jnp.uint32)
        return carry

    lax.fori_loop(0, PEER_HEADS, head, 0)


def _topk(s1t, s2t):
    n = s1t.shape[2]
    tk = TOPK_TOKENS
    blk = pl.BlockSpec((PEER_HEADS, PEER_KEYS, tk), lambda t: (0, 0, t))
    packed = pl.BlockSpec((PEER_HEADS, PEER_KEYS // 2, tk), lambda t: (0, 0, t))
    return pl.pallas_call(
        _topk_kernel,
        out_shape=[jax.ShapeDtypeStruct(s1t.shape, _F32)] * 2
        + [jax.ShapeDtypeStruct((PEER_HEADS, PEER_KEYS // 2, n), jnp.uint32)] * 2,
        grid=(n // tk,),
        in_specs=[blk, blk],
        out_specs=[blk, blk, packed, packed],
        compiler_params=_params("arbitrary"),
        name="peer_topk",
    )(s1t, s2t)


def _gate_rows(a_ref, cnt_ref, e1_ref, e2_ref, rk_ref, w_ref, first_keys, lane_tiles):
    shape = (PEER_CHUNK, LANES)
    for ii in first_keys:
        for lt in lane_tiles:
            ls = slice(lt * LANES, (lt + 1) * LANES)
            cntb = [jnp.broadcast_to(cnt_ref[h, ii:ii + 1, ls], shape).astype(_BF16)
                    for h in range(PEER_HEADS)]
            e1b = [jnp.broadcast_to(e1_ref[h, ii:ii + 1, ls], shape).astype(_BF16)
                   for h in range(PEER_HEADS)]
            for jc in range(PEER_KEYS // PEER_CHUNK):
                rows = slice(jc * PEER_CHUNK, (jc + 1) * PEER_CHUNK)
                words = slice(rows.start // 2, rows.stop // 2)
                g = None
                for h in range(PEER_HEADS):
                    e2 = pltpu.bitcast(e2_ref[h, words, ls], _BF16)
                    rk = pltpu.bitcast(rk_ref[h, words, ls], _BF16)
                    t = e1b[h] * jnp.where(rk < cntb[h], e2, jnp.zeros_like(e2))
                    g = t if g is None else g + t
                erows = slice(ii * PEER_KEYS + rows.start, ii * PEER_KEYS + rows.stop)
                a = a_ref[erows, ls]
                act = (0.5 * a) * (1.0 + lax.erf(a * (1.0 / math.sqrt(2.0))))
                w_ref[erows, ls] = g * act.astype(_BF16)


def _peer_kernel(h2n_ref, x1_ref, mod_ref, u0_ref, un_ref, vt_ref, cnt_ref, e1_ref, e2_ref, rk_ref,
                 gfin_ref, o_ref, acc_ref, a_ref, w_ref, *, d, n_steps, final):
    k = pl.program_id(1)

    @pl.when(jnp.logical_and(pl.program_id(0) == 0, k == 0))
    def _():
        a_ref[...] = lax.dot_general(u0_ref[...], h2n_ref[...], _NT, preferred_element_type=_F32)

    @pl.when(k == 0)
    def _():
        acc_ref[...] = jnp.zeros_like(acc_ref)

    half = a_ref.shape[1] // 2
    for hv in range(2):
        lanes = slice(hv * half, (hv + 1) * half)
        _gate_rows(a_ref, cnt_ref, e1_ref, e2_ref, rk_ref, w_ref, range(PEER_ROWS),
                   range(hv * half // LANES, (hv + 1) * half // LANES))
        acc_ref[:, lanes] += jnp.dot(vt_ref[...], w_ref[:, lanes], preferred_element_type=_F32)
        a_ref[:, lanes] = lax.dot_general(un_ref[...], h2n_ref[lanes, :], _NT,
                                          preferred_element_type=_F32)

    @pl.when(k == n_steps - 1)
    def _():
        gt2 = mod_ref[0, :, 5 * d:6 * d]
        x2 = x1_ref[...] + gt2 * acc_ref[...].T
        o_ref[...] = _rms(x2, gfin_ref[...]) if final else x2


def _peer(h2, x1, mod_l, u, vt, e1t, cntt, e2t, rkt, gfin, *, layer, geo, final):
    n, d = h2.shape
    tn = PEER_TOKENS
    te = PEER_ROWS * PEER_KEYS
    n_steps = u.shape[1] // te
    n_tok = n // tn
    lat_tiles, tiles_per_batch, batch = geo
    row = lambda t, k: (t, 0)

    def mod_idx(t, k):
        return (jnp.where(t < lat_tiles, t // tiles_per_batch, batch), 0, 0)

    rows_blk = pl.BlockSpec((PEER_HEADS, PEER_ROWS, tn), lambda t, k: (0, k, t))
    return pl.pallas_call(
        functools.partial(_peer_kernel, d=d, n_steps=n_steps, final=final),
        out_shape=jax.ShapeDtypeStruct((n, d), _F32),
        grid=(n // tn, n_steps),
        in_specs=[
            pl.BlockSpec((tn, d), lambda t, k: (jnp.minimum(t + (k + 1) // n_steps, n_tok - 1), 0)),
            pl.BlockSpec((tn, d), row),
            pl.BlockSpec((1, 1, N_MOD * d), mod_idx),
            pl.BlockSpec((None, te, d), lambda t, k: (layer, 0, 0)),
            pl.BlockSpec((None, te, d), lambda t, k: (layer, (k + 1) % n_steps, 0)),
            pl.BlockSpec((None, d, te), lambda t, k: (layer, 0, k)),
            rows_blk,
            rows_blk,
            pl.BlockSpec((PEER_HEADS, PEER_KEYS // 2, tn), lambda t, k: (0, 0, t)),
            pl.BlockSpec((PEER_HEADS, PEER_KEYS // 2, tn), lambda t, k: (0, 0, t)),
            _full(gfin.shape),
        ],
        out_specs=pl.BlockSpec((tn, d), row),
        scratch_shapes=[
            pltpu.VMEM((d, tn), _F32),
            pltpu.VMEM((te, tn), _F32),
            pltpu.VMEM((te, tn), _BF16),
        ],
        compiler_params=_params("arbitrary", "arbitrary"),
        name="peer_experts",
    )(h2, x1, mod_l, u, u, vt, cntt, e1t, e2t, rkt, gfin)


def _rope_tables(seq):
    rows = seq // GRID_W
    r = jnp.repeat(jnp.arange(rows, dtype=_F32), GRID_W)
    col = jnp.tile(jnp.arange(GRID_W, dtype=_F32), rows)

    def cos_sin(rot_dim):
        n_freq = rot_dim // 4
        inv = ROPE_BASE ** (-jnp.arange(n_freq, dtype=_F32) / n_freq)
        ang = jnp.concatenate([r[:, None] * inv, col[:, None] * inv], axis=-1)
        return jnp.cos(ang), jnp.sin(ang)

    ca, sa = cos_sin(MLA_ROPE)
    one = jnp.ones((seq, MLA_NOPE), _F32)
    zero = jnp.zeros_like(one)
    z16 = jnp.zeros_like(ca)
    pad1 = jnp.ones((seq, LANES - MLA_NOPE - MLA_ROPE), _F32)
    pad0 = jnp.zeros_like(pad1)
    tab_a = [jnp.concatenate([one, ca, ca, pad1], axis=1),
             jnp.concatenate([zero, -sa, z16, pad0], axis=1),
             jnp.concatenate([zero, z16, sa, pad0], axis=1)]
    ch, sh = cos_sin(HEAD_DIM)
    z32 = jnp.zeros_like(ch)
    tab_h = [jnp.concatenate([ch, ch, ch, ch], axis=1),
             jnp.concatenate([-sh, z32, -sh, z32], axis=1),
             jnp.concatenate([z32, sh, z32, sh], axis=1)]
    lat = jnp.concatenate(tab_a + tab_h, axis=1)
    ident = jnp.concatenate([jnp.ones((PROJ_TILE, LANES), _F32),
                             jnp.zeros((PROJ_TILE, 2 * LANES), _F32)] * 2, axis=1)
    return jnp.concatenate([lat, ident], axis=0)


def _pack_w_in(w_in, d):
    splits = np.cumsum([MLA_Q_RANK, MLA_KV_RANK, MLA_ROPE, 512, 128, 128, 512, 128, 128])
    cq, ckv, kr, qb, kb, vb, qw, kw, vw, gates = jnp.split(w_in, splits, axis=-1)
    lead = w_in.shape[:-1]
    kr_pad = jnp.concatenate([jnp.zeros(lead + (MLA_NOPE,), w_in.dtype), kr,
                              jnp.zeros(lead + (LANES - MLA_NOPE - MLA_ROPE,), w_in.dtype)], axis=-1)

    def dup(w):
        g0, g1 = w[..., :HEAD_DIM], w[..., HEAD_DIM:]
        return jnp.concatenate([g0, g0, g1, g1], axis=-1)

    return jnp.concatenate([cq, ckv, kr_pad, qb, dup(kb), dup(vb), qw, dup(kw), dup(vw), gates],
                           axis=-1).astype(_BF16)


def _pack_w_uq(w_uq):
    depth, rank, _ = w_uq.shape
    w = w_uq.reshape(depth, rank, HEADS, MLA_NOPE + MLA_ROPE)
    w = jnp.pad(w, ((0, 0), (0, 0), (0, 0), (0, LANES - MLA_NOPE - MLA_ROPE)))
    return w.reshape(depth, rank, HEADS * LANES).astype(_BF16)


def _pack_w_ukv(w_ukv):
    depth, rank, _ = w_ukv.shape
    w = w_ukv.reshape(depth, rank, HEADS, MLA_NOPE + HEAD_DIM)
    k = jnp.pad(w[..., :MLA_NOPE], ((0, 0), (0, 0), (0, 0), (0, LANES - MLA_NOPE)))
    v = w[..., MLA_NOPE:]
    return jnp.concatenate([k.reshape(depth, rank, HEADS * LANES),
                            v.reshape(depth, rank, HEADS * HEAD_DIM)], axis=-1).astype(_BF16)


def kernel(x, c, ctx, c_ctx, w_mod, b_mod, g_attn, g_ffn, w_in, g_cq, g_ckv, w_uq, w_ukv,
           g_qn, g_kn, sink, w_oa, w_ob, w_ow, w_o, w_pq, sub_keys, peer_u, peer_v, g_final):
    batch, seq, d = x.shape
    n_ctx = ctx.shape[1]
    depth = w_mod.shape[0]
    assert n_ctx == ROW_TILE and seq % PEER_TOKENS == 0 and (batch * n_ctx) % PEER_TOKENS == 0
    assert batch < 8 and seq % GRID_W == 0 and seq >= 2 * WIN_Q_TILE and seq % WIN_Q_TILE == 0
    assert seq % FLASH_Q_TILE == 0 and seq % min(FLASH_K_CHUNK, seq) == 0 and seq % PROJ_TILE == 0
    n_lat = batch * seq
    n_all = n_lat + batch * n_ctx
    ctx_blk0 = n_lat // ROW_TILE

    xs = jnp.concatenate([x.reshape(n_lat, d), ctx.reshape(batch * n_ctx, d)], axis=0)
    cc = jnp.concatenate([c, c_ctx[None], jnp.zeros((8 - batch - 1, d), _F32)], axis=0)
    mod = _modulation(cc, w_mod, b_mod).reshape(depth, 8, 1, N_MOD * d)

    rope = _rope_tables(seq)
    eye = np.kron(np.eye(2, dtype=np.float32), np.ones((HEAD_DIM, HEAD_DIM), np.float32))
    bd = jnp.asarray(eye, _BF16)
    win = _pack_w_in(w_in, d)
    wuq = _pack_w_uq(w_uq)
    wukv = _pack_w_ukv(w_ukv)
    gqn_t = jnp.tile(g_qn, (1, HEADS))[:, None, :]
    gkn_t = jnp.tile(g_kn, (1, 2 * LANES // HEAD_DIM))[:, None, :]
    sink_t = jnp.broadcast_to(sink[:, :, None], (depth, HEADS, LANES))
    woa, wob, wow, wo, wpq = (w.astype(_BF16) for w in (w_oa, w_ob, w_ow, w_o, w_pq))
    subk = sub_keys.reshape(depth, 2 * PEER_HEADS, PEER_KEYS, PEER_HALF).astype(_BF16)
    u_bf = peer_u.astype(_BF16)
    vt_bf = jnp.swapaxes(peer_v, 1, 2).astype(_BF16)

    geo_row = (n_lat // PROJ_TILE, seq // PROJ_TILE, batch)
    geo_peer = (n_lat // PEER_TOKENS, seq // PEER_TOKENS, batch)
    lat_seg = (seq, lambda b, i: b)
    ctx_seg = (n_ctx, lambda b, i: ctx_blk0 + b)
    nqw = seq // WIN_Q_TILE
    per_tile = WIN_Q_TILE // WINDOW
    edge_blocks = seq // WINDOW
    band_segs = [
        (WINDOW, lambda b, i: b * edge_blocks + jnp.maximum(per_tile * i - 1, 0)),
        (WIN_Q_TILE, lambda b, i: b * nqw + i),
        (WINDOW, lambda b, i: b * edge_blocks + jnp.minimum(per_tile * (i + 1), edge_blocks - 1)),
        ctx_seg,
    ]

    for l in range(depth):
        last = l == depth - 1
        qa, ka, va, qb, kb, vb, qw, kw, vw, gates = _inproj(
            xs, mod[l], g_attn[l][None], win, g_cq[l][None], g_ckv[l][None], wuq[l], wukv[l],
            gqn_t[l], gkn_t[l], rope, bd, layer=l, geo=geo_row)
        dense = dict(batch=batch, seq=seq, n_ctx=n_ctx)
        n_rows = n_lat if last else n_all
        o_lat = (_flash_attention(qa, ka, va, wide=True, **dense),
                 _flash_attention(qb, kb, vb, wide=False, **dense),
                 _attention(qw, kw, vw, sink_t[l], wide=False, segs=band_segs, band=True,
                            nq=nqw, q_block0=0, batch=batch, tq=WIN_Q_TILE))
        o_ctx = None
        if not last:
            cq = dict(nq=1, q_block0=ctx_blk0, batch=batch, segs=[ctx_seg], tq=n_ctx)
            o_ctx = (_attention(qa, ka, va, None, wide=True, base2=True, **cq),
                     _attention(qb, kb, vb, None, wide=False, base2=True, **cq),
                     _attention(qw, kw, vw, sink_t[l], wide=False, **cq))
        x1, h2, s1t, s2t = _merge(xs, mod[l], o_lat, o_ctx, gates, woa[l], wob[l], wow[l], wo[l],
                                  g_ffn[l][None], wpq[l], subk[l], n_rows=n_rows, geo=geo_row)
        e1t, cntt, e2t, rkt = _topk(s1t, s2t)
        xs = _peer(h2, x1, mod[l], u_bf, vt_bf, e1t, cntt, e2t, rkt, g_final[None],
                   layer=l, geo=geo_peer, final=last)
    return xs.reshape(batch, seq, d)
```
